```python
import math
import jax
import jax.numpy as jnp
from jax import lax
import numpy as np

D_MODEL = 1024
BATCH = 2
SEQ = 8192
DEPTH = 4
DEC_BATCH = 32
DEC_SEQ = 8
PAST_LEN = 8192
PAGE_SIZE = 128

EPS = 1e-6
A_HEADS = 8
A_GROUPS = 2
A_HEAD_DIM = 64
CMP_LEN = 32
CMP_STRIDE = 16
SLC_LEN = 64
SLC_TOPK = 16
WINDOW = 512
Q_BLOCK = 128
B_HEADS = 4
B_DK = 128
B_DV = 128
ROPE_BASE = 10000.0
CHUNK = 64
C_HEADS = 4
C_HEAD_DIM = 64
D_HEADS = 4
D_EXPAND = 128
D_HEAD_V = 128
F_FLOOR = 1e-6
N_MEM = 256
X_HEADS = 4
X_HEAD_DIM = D_MODEL // X_HEADS
D_FF = 4 * D_MODEL

A_Q = A_HEADS * A_HEAD_DIM
A_KV = A_GROUPS * A_HEAD_DIM
B_QK = B_HEADS * B_DK
B_VW = B_HEADS * B_DV
C_W = C_HEADS * 2 * C_HEAD_DIM
D_QF = D_HEADS * D_EXPAND
D_IV = D_HEADS * D_HEAD_V
A_SPLITS = (A_Q, A_KV, A_KV, A_KV, A_KV, A_KV, A_KV, 3 * A_HEADS, B_QK, B_QK, B_VW, B_VW)
C_SPLITS = (C_W, C_W, C_W, D_QF, D_QF, D_IV, D_IV)
IN_A = sum(A_SPLITS)
OUT_A = A_Q + B_VW
IN_C = sum(C_SPLITS)
OUT_C = C_W + D_IV
BIG = 1e9
NEG = -1e30

kernel_name = 'hybrid_nsa_retention_diffattn_hgrn2_step'


def rms_norm(x, g):
    xf = x.astype(jnp.float32)
    y = xf * lax.rsqrt(jnp.mean(xf * xf, axis=-1, keepdims=True) + EPS)
    return (y * g.astype(jnp.float32)).astype(x.dtype)


def head_rms(x):
    xf = x.astype(jnp.float32)
    return xf * lax.rsqrt(jnp.mean(xf * xf, axis=-1, keepdims=True) + EPS)


def masked_softmax(s, mask):
    s = jnp.where(mask, s.astype(jnp.float32), NEG)
    e = jnp.where(mask, jnp.exp(s - jnp.max(s, axis=-1, keepdims=True)), 0.0)
    den = jnp.sum(e, axis=-1, keepdims=True)
    return e / jnp.where(den > 0.0, den, 1.0)


def rope(x, pos):
    half = x.shape[-1] // 2
    freqs = ROPE_BASE ** (-jnp.arange(half, dtype=jnp.float32) / half)
    ang = pos.astype(jnp.float32)[:, None] * freqs[None, :]
    cos, sin = jnp.cos(ang)[None, :, None, :], jnp.sin(ang)[None, :, None, :]
    xf = x.astype(jnp.float32)
    x1, x2 = xf[..., :half], xf[..., half:]
    return jnp.concatenate([x1 * cos - x2 * sin, x2 * cos + x1 * sin], axis=-1)


def split_cols(h, w, widths):
    z = jnp.einsum('btd,de->bte', h, w)
    return jnp.split(z, [int(c) for c in np.cumsum(widths)[:-1]], axis=-1)


def gather_pages(cache, layer, page_table):
    rows = cache[layer, page_table]
    return rows.reshape((page_table.shape[0], page_table.shape[1] * cache.shape[2]) + cache.shape[3:])


def chunked_gated_recurrence(q, k, v, log_f, s0):
    B, T, H, K = q.shape
    V = v.shape[-1]
    c = CHUNK if T % CHUNK == 0 else T
    n = T // c

    def to_chunks(a):
        return a.astype(jnp.float32).reshape(B, n, c, H, a.shape[-1]).transpose(1, 0, 3, 2, 4)

    causal = jnp.tril(jnp.ones((c, c), dtype=bool))[:, :, None]

    def step(S, inp):
        qi, ki, vi, gi = inp
        b = jnp.cumsum(gi, axis=2)
        rel = b[:, :, :, None, :] - b[:, :, None, :, :]
        decay = jnp.where(causal, jnp.exp(jnp.where(causal, rel, 0.0)), 0.0)
        att = jnp.sum(qi[:, :, :, None, :] * ki[:, :, None, :, :] * decay, axis=-1)
        o = jnp.einsum('bhts,bhsv->bhtv', att, vi) + jnp.einsum('bhtk,bhkv->bhtv', qi * jnp.exp(b), S)
        b_end = b[:, :, -1:, :]
        S = jnp.exp(b_end)[:, :, 0, :, None] * S + jnp.einsum('bhsk,bhsv->bhkv', ki * jnp.exp(b_end - b), vi)
        return S, o

    s_fin, o = lax.scan(step, s0.astype(jnp.float32), (to_chunks(q), to_chunks(k), to_chunks(v), to_chunks(log_f)))
    return o.transpose(1, 0, 3, 2, 4).reshape(B, T, H, V), s_fin


def retention(rq, rk, rv, rg, pos, s0):
    q = rope(rq, pos)
    k = rope(rk, pos) * B_DK ** -0.5
    log_gamma = jnp.log1p(-jnp.exp2(-5.0 - jnp.arange(B_HEADS, dtype=jnp.float32)))
    log_f = jnp.broadcast_to(log_gamma[:, None], q.shape)
    o, s = chunked_gated_recurrence(q, k, rv, log_f, s0)
    o = head_rms(o) * jax.nn.silu(rg.astype(jnp.float32))
    return o.reshape(o.shape[0], o.shape[1], B_VW), s


def hgrn2(hq, hf, hi, hg, lb, s0):
    zf = hf.astype(jnp.float32)
    lb = lb.reshape(D_HEADS, D_EXPAND)
    f = lb + (1.0 - lb) * jax.nn.sigmoid(zf)
    log_f = jnp.log(jnp.maximum(f, F_FLOOR))
    k = 1.0 - f
    o, s = chunked_gated_recurrence(jax.nn.silu(hq.astype(jnp.float32)), k, hi, log_f, s0)
    o = head_rms(o) * jax.nn.silu(hg.astype(jnp.float32))
    return o.reshape(o.shape[0], o.shape[1], D_IV), s


def compress(k, pe, w):
    B, Tk, G, d = k.shape
    n_cmp = (Tk - CMP_LEN) // CMP_STRIDE + 1
    r = CMP_LEN // CMP_STRIDE
    n_chunk = n_cmp + r - 1
    kc = k[:, :n_chunk * CMP_STRIDE].reshape(B, n_chunk, CMP_STRIDE, G, d)
    blocks = jnp.concatenate([kc[:, j:j + n_cmp] for j in range(r)], axis=2)
    blocks = blocks + pe[None, None, :, None, :]
    return jnp.einsum('bnlgd,lde->bnge', blocks, w.reshape(CMP_LEN, d, d))


def slc_blocks(k):
    B, Tk, G, d = k.shape
    n_slc = -(-Tk // SLC_LEN)
    k = jnp.pad(k, ((0, 0), (0, n_slc * SLC_LEN - Tk), (0, 0), (0, 0)))
    return k.reshape(B, n_slc, SLC_LEN, G, d).transpose(0, 3, 1, 2, 4)


def nsa_queries(q, gates, pos_q, kc, vc, kb, vb, kw, vw, pos_w):
    B, Tq, H, d = q.shape
    G, hg = A_GROUPS, A_HEADS // A_GROUPS
    qg = q.reshape(B, Tq, G, hg, d)
    n_cmp = kc.shape[1]
    cmp_end = jnp.arange(n_cmp) * CMP_STRIDE + CMP_LEN - 1
    p_cmp = masked_softmax(jnp.einsum('bqghd,bngd->bghqn', qg, kc), (cmp_end[None, :] <= pos_q[:, None])[None, None, None])
    o_cmp = jnp.einsum('bghqn,bngd->bqghd', p_cmp, vc)
    n_slc = kb.shape[2]
    p_grp = jnp.sum(p_cmp, axis=2)
    left = CMP_LEN // CMP_STRIDE - 1
    per = SLC_LEN // CMP_STRIDE
    p_pad = jnp.pad(p_grp, ((0, 0), (0, 0), (0, 0), (left, per * n_slc - n_cmp)))
    p_slc = sum(p_pad[..., r:r + per * (n_slc - 1) + 1:per] for r in range(per + left))
    blk_q = pos_q // SLC_LEN
    j = jnp.arange(n_slc)
    valid = j[None, :] <= blk_q[:, None]
    forced = valid & ((j[None, :] == 0) | (j[None, :] >= blk_q[:, None] - 1))
    score = jnp.where(forced, BIG, jnp.where(valid, p_slc, -BIG))
    k_top = min(SLC_TOPK, n_slc)
    _, idx = lax.top_k(score, k_top)
    bi = jnp.arange(B)[:, None, None, None]
    gi = jnp.arange(G)[None, :, None, None]
    ks = kb[bi, gi, idx]
    vs = vb[bi, gi, idx]
    tok = idx[..., None] * SLC_LEN + jnp.arange(SLC_LEN)
    s = jnp.einsum('bqghd,bgqnjd->bghqnj', qg, ks).reshape(B, G, hg, Tq, k_top * SLC_LEN)
    p = masked_softmax(s, (tok <= pos_q[None, None, :, None, None]).reshape(B, G, 1, Tq, k_top * SLC_LEN))
    o_slc = jnp.einsum('bghqm,bgqmd->bqghd', p, vs.reshape(B, G, Tq, k_top * SLC_LEN, d))
    dpos = pos_q[:, None] - pos_w[None, :]
    win_mask = (dpos >= 0) & (dpos < WINDOW) & (pos_w[None, :] >= 0)
    p = masked_softmax(jnp.einsum('bqghd,bkgd->bghqk', qg, kw), win_mask[None, None, None])
    o_win = jnp.einsum('bghqk,bkgd->bqghd', p, vw)
    gt = gates.reshape(B, Tq, G, hg, 3).astype(jnp.float32)
    o = gt[..., 0:1] * o_cmp + gt[..., 1:2] * o_slc + gt[..., 2:3] * o_win
    return o.reshape(B, Tq, A_Q)


def even_inputs(h, w_in):
    B, T, _ = h.shape
    q, kc, vc, ks, vs, kw, vw, gt, rq, rk, rv, rg = split_cols(h, w_in, A_SPLITS)

    def kv(k_, v_):
        return jnp.stack([k_.reshape(B, T, A_GROUPS, A_HEAD_DIM), v_.reshape(B, T, A_GROUPS, A_HEAD_DIM)], axis=2)

    q = q.reshape(B, T, A_HEADS, A_HEAD_DIM) * A_HEAD_DIM ** -0.5
    gates = jax.nn.sigmoid(gt.reshape(B, T, A_HEADS, 3))
    ret = tuple(a.reshape(B, T, B_HEADS, -1) for a in (rq, rk, rv, rg))
    return q, gates, kv(kc, vc), kv(ks, vs), kv(kw, vw), ret


def even_prompt(h, pe, w_cmp, w_in, w_out):
    B, T, _ = h.shape
    q, gates, kv_cmp, kv_slc, kv_win, (rq, rk, rv, rg) = even_inputs(h, w_in)
    kc, vc = compress(kv_cmp[:, :, 0], pe[0], w_cmp[0]), compress(kv_cmp[:, :, 1], pe[1], w_cmp[1])
    kb, vb = slc_blocks(kv_slc[:, :, 0]), slc_blocks(kv_slc[:, :, 1])
    kw = jnp.pad(kv_win, ((0, 0), (WINDOW, 0), (0, 0), (0, 0), (0, 0)))
    qb = min(Q_BLOCK, T)
    band = WINDOW + qb

    def block(qs):
        wb = lax.dynamic_slice_in_dim(kw, qs, band, axis=1)
        return nsa_queries(lax.dynamic_slice_in_dim(q, qs, qb, axis=1), lax.dynamic_slice_in_dim(gates, qs, qb, axis=1),
                           qs + jnp.arange(qb), kc, vc, kb, vb, wb[:, :, 0], wb[:, :, 1], qs - WINDOW + jnp.arange(band))

    o_a = lax.map(block, jnp.arange(T // qb) * qb)
    o_a = o_a.transpose(1, 0, 2, 3).reshape(B, T, A_Q)
    o_b, s_ret = retention(rq, rk, rv, rg, jnp.arange(T), jnp.zeros((B, B_HEADS, B_DK, B_DV), jnp.float32))
    y = jnp.concatenate([o_a.astype(h.dtype), o_b.astype(h.dtype)], axis=-1) @ w_out
    return y, (kv_cmp, kv_slc, kv_win[:, T - min(WINDOW, T):], s_ret)


def even_sample(h, hist_cmp, hist_slc, win_buf, s0, pe, w_cmp, w_in, w_out):
    B, T, _ = h.shape
    q, gates, kv_cmp, kv_slc, kv_win, (rq, rk, rv, rg) = even_inputs(h, w_in)
    pos_q = PAST_LEN + jnp.arange(T)
    full_cmp = jnp.concatenate([hist_cmp, kv_cmp.astype(hist_cmp.dtype)], axis=1)
    full_slc = jnp.concatenate([hist_slc, kv_slc.astype(hist_slc.dtype)], axis=1)
    band = jnp.concatenate([win_buf, kv_win.astype(win_buf.dtype)], axis=1)
    n_buf = win_buf.shape[1]
    kc, vc = compress(full_cmp[:, :, 0], pe[0], w_cmp[0]), compress(full_cmp[:, :, 1], pe[1], w_cmp[1])
    kb, vb = slc_blocks(full_slc[:, :, 0]), slc_blocks(full_slc[:, :, 1])
    o_a = nsa_queries(q, gates, pos_q, kc, vc, kb, vb, band[:, :, 0], band[:, :, 1], PAST_LEN - n_buf + jnp.arange(n_buf + T))
    o_b, s_ret = retention(rq, rk, rv, rg, pos_q, s0)
    y = jnp.concatenate([o_a.astype(h.dtype), o_b.astype(h.dtype)], axis=-1) @ w_out
    return y, (kv_cmp, kv_slc, band[:, T:], s_ret.astype(s0.dtype))


def diff_lambda(lam_p, layer):
    lam_init = 0.8 - 0.6 * math.exp(-0.3 * layer)
    lp = lam_p.astype(jnp.float32)
    lam = jnp.exp(jnp.sum(lp[0] * lp[1])) - jnp.exp(jnp.sum(lp[2] * lp[3])) + lam_init
    return lam, lam_init


def diff_attend(q, k, v, pos_q, pos_k, lam):
    s = jnp.einsum('bqhmd,bkhmd->bhmqk', q, k)
    p = masked_softmax(s, (pos_k[None, :] <= pos_q[:, None])[None, None, None])
    a = p[:, :, 0] - lam * p[:, :, 1]
    return jnp.einsum('bhqk,bkhv->bqhv', a, v)


def odd_inputs(h, w_in):
    B, T, _ = h.shape
    q, k, v, hq, hf, hi, hg = split_cols(h, w_in, C_SPLITS)
    q = q.reshape(B, T, C_HEADS, 2, C_HEAD_DIM) * C_HEAD_DIM ** -0.5
    kv = jnp.stack([k.reshape(B, T, C_HEADS, 2 * C_HEAD_DIM), v.reshape(B, T, C_HEADS, 2 * C_HEAD_DIM)], axis=2)
    rec = tuple(a.reshape(B, T, D_HEADS, -1) for a in (hq, hf, hi, hg))
    return q, kv, rec


def odd_prompt(h, lam, lam_init, lb, w_in, w_out):
    B, T, _ = h.shape
    q, kv, (hq, hf, hi, hg) = odd_inputs(h, w_in)
    k = kv[:, :, 0].reshape(B, T, C_HEADS, 2, C_HEAD_DIM)
    v = kv[:, :, 1]
    pos = jnp.arange(T)
    qb = min(Q_BLOCK, T)

    def block(qs):
        return diff_attend(lax.dynamic_slice_in_dim(q, qs, qb, axis=1), k, v, qs + jnp.arange(qb), pos, lam)

    o_c = lax.map(block, jnp.arange(T // qb) * qb)
    o_c = o_c.transpose(1, 0, 2, 3, 4).reshape(B, T, C_HEADS, 2 * C_HEAD_DIM)
    o_c = (head_rms(o_c) * (1.0 - lam_init)).reshape(B, T, C_W)
    o_d, s_h = hgrn2(hq, hf, hi, hg, lb, jnp.zeros((B, D_HEADS, D_EXPAND, D_HEAD_V), jnp.float32))
    y = jnp.concatenate([o_c.astype(h.dtype), o_d.astype(h.dtype)], axis=-1) @ w_out
    return y, (kv, s_h)


def odd_sample(h, hist_kv, s0, lam, lam_init, lb, w_in, w_out):
    B, T, _ = h.shape
    q, kv, (hq, hf, hi, hg) = odd_inputs(h, w_in)
    full = jnp.concatenate([hist_kv, kv.astype(hist_kv.dtype)], axis=1)
    Tk = full.shape[1]
    k = full[:, :, 0].reshape(B, Tk, C_HEADS, 2, C_HEAD_DIM)
    o_c = diff_attend(q, k, full[:, :, 1], PAST_LEN + jnp.arange(T), jnp.arange(Tk), lam)
    o_c = (head_rms(o_c) * (1.0 - lam_init)).reshape(B, T, C_W)
    o_d, s_h = hgrn2(hq, hf, hi, hg, lb, s0)
    y = jnp.concatenate([o_c.astype(h.dtype), o_d.astype(h.dtype)], axis=-1) @ w_out
    return y, (kv, s_h.astype(s0.dtype))


def mem_kv(mem, w_kv):
    B = mem.shape[0]
    return (mem @ w_kv).reshape(B, N_MEM, 2, X_HEADS, X_HEAD_DIM)


def cross_attend(h, kv, w_q, w_o):
    B, T, _ = h.shape
    q = (h @ w_q).reshape(B, T, X_HEADS, X_HEAD_DIM) * X_HEAD_DIM ** -0.5
    p = jax.nn.softmax(jnp.einsum('bthd,bmhd->bhtm', q, kv[:, :, 0]).astype(jnp.float32), axis=-1)
    o = jnp.einsum('bhtm,bmhd->bthd', p, kv[:, :, 1]).reshape(B, T, D_MODEL)
    return o.astype(h.dtype) @ w_o


def sq_relu_mlp(h, w_up, w_down):
    return jnp.square(jax.nn.relu(h @ w_up)) @ w_down


def setup_inputs(seed: int = 0) -> dict:
    key = jax.random.key(seed)
    ks = iter(jax.random.split(key, 32))
    n_even, n_odd = (DEPTH + 1) // 2, DEPTH // 2
    n_pages = PAST_LEN // PAGE_SIZE
    n_used = DEC_BATCH * n_pages
    n_pool = n_used + max(1, n_used // 4)
    win_buf = min(WINDOW, PAST_LEN)

    def nrm(shape, scale):
        return scale * jax.random.normal(next(ks), shape, jnp.float32)

    page_table = jax.random.permutation(next(ks), n_pool)[:n_used].reshape(DEC_BATCH, n_pages).astype(jnp.int32)
    return {
        'x_prompt': nrm((BATCH, SEQ, D_MODEL), 1.0),
        'x_sample': nrm((DEC_BATCH, DEC_SEQ, D_MODEL), 1.0),
        'cache_nsa_cmp_kv': nrm((n_even, n_pool, PAGE_SIZE, 2, A_GROUPS, A_HEAD_DIM), 1.0),
        'cache_nsa_slc_kv': nrm((n_even, n_pool, PAGE_SIZE, 2, A_GROUPS, A_HEAD_DIM), 1.0),
        'cache_nsa_win_kv': nrm((n_even, DEC_BATCH, win_buf, 2, A_GROUPS, A_HEAD_DIM), 1.0),
        'state_ret': nrm((n_even, DEC_BATCH, B_HEADS, B_DK, B_DV), 0.5),
        'cache_diff_kv': nrm((n_odd, n_pool, PAGE_SIZE, 2, C_HEADS, 2 * C_HEAD_DIM), 1.0),
        'state_hgrn': nrm((n_odd, DEC_BATCH, D_HEADS, D_EXPAND, D_HEAD_V), 0.5),
        'cache_mem_kv': nrm((DEPTH, DEC_BATCH, N_MEM, 2, X_HEADS, X_HEAD_DIM), 1.0),
        'page_table': page_table,
        'mem_prompt': nrm((BATCH, N_MEM, D_MODEL), 1.0),
        'norm_g': 1.0 + nrm((DEPTH, 6, D_MODEL), 0.05),
        'w_in_a': nrm((n_even, D_MODEL, IN_A), D_MODEL ** -0.5),
        'cmp_pos': nrm((n_even, 2, CMP_LEN, A_HEAD_DIM), 0.5),
        'cmp_w': nrm((n_even, 2, CMP_LEN * A_HEAD_DIM, A_HEAD_DIM), (CMP_LEN * A_HEAD_DIM) ** -0.5),
        'w_out_a': nrm((n_even, OUT_A, D_MODEL), OUT_A ** -0.5),
        'w_in_c': nrm((n_odd, D_MODEL, IN_C), D_MODEL ** -0.5),
        'diff_lam': nrm((n_odd, 4, C_HEAD_DIM), 0.1),
        'hgrn_lb': nrm((n_odd, D_QF), 0.5),
        'w_out_c': nrm((n_odd, OUT_C, D_MODEL), OUT_C ** -0.5),
        'w_xq': nrm((DEPTH, D_MODEL, D_MODEL), D_MODEL ** -0.5),
        'w_xkv': nrm((DEPTH, D_MODEL, 2 * D_MODEL), D_MODEL ** -0.5),
        'w_xo': nrm((DEPTH, D_MODEL, D_MODEL), D_MODEL ** -0.5),
        'w_up': nrm((DEPTH, D_MODEL, D_FF), D_MODEL ** -0.5),
        'w_down': nrm((DEPTH, D_FF, D_MODEL), D_FF ** -0.5),
    }


def reference(x_prompt, x_sample, cache_nsa_cmp_kv, cache_nsa_slc_kv, cache_nsa_win_kv, state_ret, cache_diff_kv,
              state_hgrn, cache_mem_kv, page_table, mem_prompt, norm_g, w_in_a, cmp_pos, cmp_w, w_out_a, w_in_c,
              diff_lam, hgrn_lb, w_out_c, w_xq, w_xkv, w_xo, w_up, w_down):
    lbs = jax.nn.softmax(hgrn_lb.astype(jnp.float32), axis=0)
    lbs = jnp.cumsum(lbs, axis=0) - lbs[0]
    xp, xs = x_prompt, x_sample
    cmp_p, cmp_s, slc_p, slc_s, win_p, win_s, ret_p, ret_s = [], [], [], [], [], [], [], []
    dkv_p, dkv_s, hg_p, hg_s, mem_p = [], [], [], [], []
    for layer in range(DEPTH):
        g = norm_g[layer]
        hp, hs = rms_norm(xp, g[0]), rms_norm(xs, g[0])
        if layer % 2 == 0:
            a = layer // 2
            yp, (c1, s1, w1, r1) = even_prompt(hp, cmp_pos[a], cmp_w[a], w_in_a[a], w_out_a[a])
            ys, (c2, s2, w2, r2) = even_sample(hs, gather_pages(cache_nsa_cmp_kv, a, page_table),
                                               gather_pages(cache_nsa_slc_kv, a, page_table), cache_nsa_win_kv[a],
                                               state_ret[a], cmp_pos[a], cmp_w[a], w_in_a[a], w_out_a[a])
            cmp_p.append(c1); cmp_s.append(c2); slc_p.append(s1); slc_s.append(s2)
            win_p.append(w1); win_s.append(w2); ret_p.append(r1); ret_s.append(r2)
        else:
            o = layer // 2
            lam, lam_init = diff_lambda(diff_lam[o], layer)
            yp, (k1, h1) = odd_prompt(hp, lam, lam_init, lbs[o], w_in_c[o], w_out_c[o])
            ys, (k2, h2) = odd_sample(hs, gather_pages(cache_diff_kv, o, page_table), state_hgrn[o], lam, lam_init,
                                      lbs[o], w_in_c[o], w_out_c[o])
            dkv_p.append(k1); dkv_s.append(k2); hg_p.append(h1); hg_s.append(h2)
        xp = xp + rms_norm(yp, g[1])
        xs = xs + rms_norm(ys, g[1])
        kv_mem = mem_kv(mem_prompt, w_xkv[layer])
        mem_p.append(kv_mem)
        xp = xp + rms_norm(cross_attend(rms_norm(xp, g[2]), kv_mem, w_xq[layer], w_xo[layer]), g[3])
        xs = xs + rms_norm(cross_attend(rms_norm(xs, g[2]), cache_mem_kv[layer], w_xq[layer], w_xo[layer]), g[3])
        xp = xp + rms_norm(sq_relu_mlp(rms_norm(xp, g[4]), w_up[layer], w_down[layer]), g[5])
        xs = xs + rms_norm(sq_relu_mlp(rms_norm(xs, g[4]), w_up[layer], w_down[layer]), g[5])
    return (xp, xs, jnp.stack(cmp_p), jnp.stack(cmp_s), jnp.stack(slc_p), jnp.stack(slc_s), jnp.stack(win_p),
            jnp.stack(win_s), jnp.stack(ret_p), jnp.stack(ret_s), jnp.stack(dkv_p), jnp.stack(dkv_s), jnp.stack(hg_p),
            jnp.stack(hg_s), jnp.stack(mem_p))
```

```python
import functools
import math

import numpy as np
import jax
import jax.numpy as jnp
from jax import lax
from jax.experimental import pallas as pl
from jax.experimental.pallas import tpu as pltpu

F32 = jnp.float32
BF16 = jnp.bfloat16

EPS = 1e-6
NEG = -1e30
BIG = 1e9
A_HEADS, A_GROUPS, A_HEAD_DIM = 8, 2, 64
CMP_LEN, CMP_STRIDE, SLC_LEN, SLC_TOPK, WINDOW = 32, 16, 64, 16, 512
B_HEADS, B_DK = 4, 128
ROPE_BASE = 10000.0
CHUNK = 64
C_HEADS, C_HEAD_DIM = 4, 64
D_HEADS = 4
F_FLOOR = 1e-6
X_HEADS = 4
PAGE_SIZE = 128
SUB = 8

VMEM_LIMIT = 56 * 1024 * 1024

NT = (((1,), (1,)), ((), ()))
TN = (((0,), (0,)), ((), ()))


def _cparams(n_grid):
    return pltpu.CompilerParams(dimension_semantics=("arbitrary",) * n_grid, vmem_limit_bytes=VMEM_LIMIT)


def _pick(n, cands):
    for c in cands:
        if n % c == 0:
            return c
    return n


def _split3(x):
    hi = x.astype(BF16)
    r = x - hi.astype(F32)
    mid = r.astype(BF16)
    lo = (r - mid.astype(F32)).astype(BF16)
    return hi, mid, lo


def _dot_x01(x, m01):
    return sum(jnp.dot(p, m01, preferred_element_type=F32) for p in _split3(x))


def _dot_01x(m01, x):
    return sum(jnp.dot(m01, p, preferred_element_type=F32) for p in _split3(x))


def _dot3_nt(a, b):
    ah = a.astype(BF16)
    al = (a - ah.astype(F32)).astype(BF16)
    bh = b.astype(BF16)
    bl = (b - bh.astype(F32)).astype(BF16)
    d = lambda x, y: lax.dot_general(x, y, NT, preferred_element_type=F32)
    return d(ah, bh) + d(ah, bl) + d(al, bh)


def _sigmoid(x):
    return 1.0 / (1.0 + jnp.exp(-x))


def _rms(x, g):
    return x * lax.rsqrt(jnp.mean(x * x, axis=-1, keepdims=True) + EPS) * g


def _rms_matmul_kernel(x_ref, g_ref, w_ref, o_ref, *, norm):
    x = x_ref[...]
    if norm:
        x = _rms(x, g_ref[...])
    o_ref[...] = jnp.dot(x.astype(BF16), w_ref[...], preferred_element_type=F32).astype(o_ref.dtype)


def rms_matmul(x, g, w, out_dtype, norm=True):
    R, D = x.shape
    N = w.shape[1]
    tm = _pick(R, (512, 256, 128, 64, 32, 16, 8))
    tn = _pick(N, (1792, 1152, 1024, 896, 768, 640, 512, 384, 256, 128))
    return pl.pallas_call(
        functools.partial(_rms_matmul_kernel, norm=norm),
        out_shape=jax.ShapeDtypeStruct((R, N), out_dtype),
        grid=(R // tm, N // tn),
        in_specs=[pl.BlockSpec((tm, D), lambda i, j: (i, 0)),
                  pl.BlockSpec((1, D), lambda i, j: (0, 0)),
                  pl.BlockSpec((D, tn), lambda i, j: (0, j))],
        out_specs=pl.BlockSpec((tm, tn), lambda i, j: (i, j)),
        compiler_params=_cparams(2), name="rms_matmul")(x, g, w)


def _mlp_kernel(x_ref, g4_ref, g5_ref, wu_ref, wd_ref, o_ref, xn_scr, acc_scr):
    j = pl.program_id(1)

    @pl.when(j == 0)
    def _():
        xn_scr[...] = _rms(x_ref[...], g4_ref[...]).astype(BF16)
        acc_scr[...] = jnp.zeros_like(acc_scr)

    h = jnp.dot(xn_scr[...], wu_ref[...], preferred_element_type=F32)
    h = jnp.square(jnp.maximum(h, 0.0))
    acc_scr[...] += jnp.dot(h.astype(BF16), wd_ref[...], preferred_element_type=F32)

    @pl.when(j == pl.num_programs(1) - 1)
    def _():
        o_ref[...] = x_ref[...] + _rms(acc_scr[...], g5_ref[...])


def mlp(x, g4, g5, w_up, w_down):
    R, D = x.shape
    F = w_up.shape[1]
    tm = _pick(R, (512, 256, 128, 64, 32, 16, 8))
    tf = _pick(F, (1024, 512, 256, 128))
    return pl.pallas_call(
        _mlp_kernel,
        out_shape=jax.ShapeDtypeStruct((R, D), F32),
        grid=(R // tm, F // tf),
        in_specs=[pl.BlockSpec((tm, D), lambda i, j: (i, 0)),
                  pl.BlockSpec((1, D), lambda i, j: (0, 0)),
                  pl.BlockSpec((1, D), lambda i, j: (0, 0)),
                  pl.BlockSpec((D, tf), lambda i, j: (0, j)),
                  pl.BlockSpec((tf, D), lambda i, j: (j, 0))],
        out_specs=pl.BlockSpec((tm, D), lambda i, j: (i, 0)),
        scratch_shapes=[pltpu.VMEM((tm, D), BF16), pltpu.VMEM((tm, D), F32)],
        compiler_params=_cparams(2), name="mlp")(x, g4, g5, w_up, w_down)


def _proj_res_kernel(a_ref, w_ref, g_ref, x_ref, o_ref):
    y = jnp.dot(a_ref[...], w_ref[...], preferred_element_type=F32)
    o_ref[...] = x_ref[...] + _rms(y, g_ref[...])


def proj_res(a, w, g, x):
    R, D = x.shape
    K = a.shape[1]
    tm = _pick(R, (512, 256, 128, 64, 32, 16, 8))
    return pl.pallas_call(
        _proj_res_kernel,
        out_shape=jax.ShapeDtypeStruct((R, D), F32),
        grid=(R // tm,),
        in_specs=[pl.BlockSpec((tm, K), lambda i: (i, 0)),
                  pl.BlockSpec((K, D), lambda i: (0, 0)),
                  pl.BlockSpec((1, D), lambda i: (0, 0)),
                  pl.BlockSpec((tm, D), lambda i: (i, 0))],
        out_specs=pl.BlockSpec((tm, D), lambda i: (i, 0)),
        compiler_params=_cparams(1), name="proj_res")(a, w, g, x)


def _even_out_kernel(oc_ref, os_ref, ow_ref, gt_ref, ob_ref, e_ref, wa_ref, wb_ref, g_ref, x_ref, o_ref):
    gates = _sigmoid(gt_ref[...])
    ge = _dot_x01(gates, e_ref[...])
    aq = oc_ref.shape[1]
    oa = ge[:, :aq] * oc_ref[...] + ge[:, aq:2 * aq] * os_ref[...] + ge[:, 2 * aq:] * ow_ref[...]
    y = jnp.dot(oa.astype(BF16), wa_ref[...], preferred_element_type=F32)
    y += jnp.dot(ob_ref[...], wb_ref[...], preferred_element_type=F32)
    o_ref[...] = x_ref[...] + _rms(y, g_ref[...])


def even_out(o_cmp, o_slc, o_win, z, gt_blk, o_b, e01, wa, wb, g, x):
    R, D = x.shape
    aq = o_cmp.shape[1]
    tm = _pick(R, (256, 128, 64, 32, 16, 8))
    row = lambda i: (i, 0)
    fix = lambda i: (0, 0)
    return pl.pallas_call(
        _even_out_kernel,
        out_shape=jax.ShapeDtypeStruct((R, D), F32),
        grid=(R // tm,),
        in_specs=[pl.BlockSpec((tm, aq), row), pl.BlockSpec((tm, aq), row), pl.BlockSpec((tm, aq), row),
                  pl.BlockSpec((tm, 128), lambda i: (i, gt_blk)),
                  pl.BlockSpec((tm, o_b.shape[1]), row),
                  pl.BlockSpec(e01.shape, fix), pl.BlockSpec(wa.shape, fix), pl.BlockSpec(wb.shape, fix),
                  pl.BlockSpec((1, D), fix), pl.BlockSpec((tm, D), row)],
        out_specs=pl.BlockSpec((tm, D), row),
        compiler_params=_cparams(1), name="even_out")(o_cmp, o_slc, o_win, z, o_b, e01, wa, wb, g, x)


def _odd_out_kernel(oc_ref, od_ref, lam_ref, wc_ref, wd_ref, g_ref, x_ref, o_ref, *, lam_init, heads, dv):
    lp = lam_ref[...]
    lam = (jnp.exp(jnp.sum(lp[0:1] * lp[1:2], axis=-1, keepdims=True))
           - jnp.exp(jnp.sum(lp[2:3] * lp[3:4], axis=-1, keepdims=True)) + lam_init)
    parts = []
    for h in range(heads):
        o1 = oc_ref[:, (2 * h) * dv:(2 * h + 1) * dv]
        o2 = oc_ref[:, (2 * h + 1) * dv:(2 * h + 2) * dv]
        o = o1 - lam * o2
        o = o * lax.rsqrt(jnp.mean(o * o, axis=-1, keepdims=True) + EPS) * (1.0 - lam_init)
        parts.append(o.astype(BF16))
    oc = jnp.concatenate(parts, axis=-1)
    y = jnp.dot(oc, wc_ref[...], preferred_element_type=F32)
    y += jnp.dot(od_ref[...], wd_ref[...], preferred_element_type=F32)
    o_ref[...] = x_ref[...] + _rms(y, g_ref[...])


def odd_out(o_diff, o_d, lam_p, lam_init, wc, wd, g, x):
    R, D = x.shape
    tm = _pick(R, (256, 128, 64, 32, 16, 8))
    row = lambda i: (i, 0)
    fix = lambda i: (0, 0)
    return pl.pallas_call(
        functools.partial(_odd_out_kernel, lam_init=lam_init, heads=C_HEADS, dv=2 * C_HEAD_DIM),
        out_shape=jax.ShapeDtypeStruct((R, D), F32),
        grid=(R // tm,),
        in_specs=[pl.BlockSpec((tm, o_diff.shape[1]), row), pl.BlockSpec((tm, o_d.shape[1]), row),
                  pl.BlockSpec(lam_p.shape, fix), pl.BlockSpec(wc.shape, fix), pl.BlockSpec(wd.shape, fix),
                  pl.BlockSpec((1, D), fix), pl.BlockSpec((tm, D), row)],
        out_specs=pl.BlockSpec((tm, D), row),
        compiler_params=_cparams(1), name="odd_out")(o_diff, o_d, lam_p, wc, wd, g, x)


def _xattn_kernel(q_ref, kv_ref, o_ref, *, heads, scale):
    dm = q_ref.shape[1]
    hd = dm // heads
    for h in range(heads):
        qh = q_ref[:, h * hd:(h + 1) * hd]
        kh = kv_ref[0, :, h * hd:(h + 1) * hd].astype(BF16)
        vh = kv_ref[0, :, dm + h * hd:dm + (h + 1) * hd].astype(BF16)
        s = lax.dot_general(qh, kh, NT, preferred_element_type=F32) * scale
        e = jnp.exp(s - jnp.max(s, axis=-1, keepdims=True))
        p = e / jnp.sum(e, axis=-1, keepdims=True)
        o = jnp.dot(p.astype(BF16), vh, preferred_element_type=F32)
        o_ref[:, h * hd:(h + 1) * hd] = o.astype(o_ref.dtype)


def xattn(q, kv, rows_per_batch):
    R, D = q.shape
    nb, n_mem, _ = kv.shape
    tq = _pick(rows_per_batch, (512, 256, 128, 64, 32, 16, 8))
    per = rows_per_batch // tq
    return pl.pallas_call(
        functools.partial(_xattn_kernel, heads=X_HEADS, scale=(D // X_HEADS) ** -0.5),
        out_shape=jax.ShapeDtypeStruct((R, D), BF16),
        grid=(R // tq,),
        in_specs=[pl.BlockSpec((tq, D), lambda i: (i, 0)),
                  pl.BlockSpec((1, n_mem, 2 * D), lambda i: (i // per, 0, 0))],
        out_specs=pl.BlockSpec((tq, D), lambda i: (i, 0)),
        compiler_params=_cparams(1), name="xattn")(q, kv)


def _compress_kernel(x_ref, w_ref, pe_ref, o_ref):
    x = x_ref[0].astype(BF16)
    ab = jnp.dot(x, w_ref[...], preferred_element_type=F32)
    cab = jnp.dot(pe_ref[...].astype(BF16), w_ref[...], preferred_element_type=F32)
    half = ab.shape[1] // 2
    a, b = ab[:, :half], ab[:, half:]
    const = cab[0:1, :half] + cab[1:2, half:]
    n = a.shape[0]
    o_ref[0] = a + pltpu.roll(b, n - 1, 0) + const


def compress(xc, wbig, pe2):
    nb, n_chunk, kdim = xc.shape
    cout = wbig.shape[1] // 2
    return pl.pallas_call(
        _compress_kernel,
        out_shape=jax.ShapeDtypeStruct((nb, n_chunk, cout), F32),
        grid=(nb,),
        in_specs=[pl.BlockSpec((1, n_chunk, kdim), lambda b: (b, 0, 0)),
                  pl.BlockSpec(wbig.shape, lambda b: (0, 0)),
                  pl.BlockSpec(pe2.shape, lambda b: (0, 0))],
        out_specs=pl.BlockSpec((1, n_chunk, cout), lambda b: (b, 0, 0)),
        compiler_params=_cparams(1), name="compress")(xc, wbig, pe2)


def _cmp_kernel(q_ref, kv_ref, a_ref, o_ref, sel_ref, *, tq, q_off, n_slc, nbp):
    G, hg, d = A_GROUPS, A_HEADS // A_GROUPS, A_HEAD_DIM
    qi = pl.program_id(1)
    ncmp = kv_ref.shape[1]
    pos_q = q_off + qi * tq + lax.broadcasted_iota(jnp.int32, (tq, 1), 0)
    cmp_end = lax.broadcasted_iota(jnp.int32, (1, ncmp), 1) * CMP_STRIDE + (CMP_LEN - 1)
    mask3 = (cmp_end <= pos_q)[None]
    jb = lax.broadcasted_iota(jnp.int32, (1, nbp), 1)
    blk_q = lax.shift_right_logical(pos_q, int(math.log2(SLC_LEN)))
    valid = jb <= blk_q
    forced = valid & ((jb == 0) | (jb >= blk_q - 1))
    k_top = min(SLC_TOPK, n_slc)
    for g in range(G):
        kc = kv_ref[0, :, g * d:(g + 1) * d].astype(BF16)
        vc = kv_ref[0, :, (G + g) * d:(G + g + 1) * d].astype(BF16)
        qs = jnp.concatenate([q_ref[:, (g * hg + h) * d:(g * hg + h + 1) * d] for h in range(hg)], axis=0)
        qs = (qs * d ** -0.5).astype(BF16)
        s = lax.dot_general(qs, kc, NT, preferred_element_type=F32).reshape(hg, tq, ncmp)
        s = jnp.where(mask3, s, NEG)
        e = jnp.where(mask3, jnp.exp(s - jnp.max(s, axis=-1, keepdims=True)), 0.0)
        den = jnp.sum(e, axis=-1, keepdims=True)
        p = e / jnp.where(den > 0.0, den, 1.0)
        o = jnp.dot(p.reshape(hg * tq, ncmp).astype(BF16), vc, preferred_element_type=F32)
        for h in range(hg):
            o_ref[:, (g * hg + h) * d:(g * hg + h + 1) * d] = o[h * tq:(h + 1) * tq]
        p_slc = _dot_x01(jnp.sum(p, axis=0), a_ref[...])
        score = jnp.where(forced, BIG, jnp.where(valid, p_slc, -BIG))
        score = jnp.where(jb < n_slc, score, -jnp.inf)
        sel = jnp.zeros((tq, nbp), F32)
        for _ in range(k_top):
            mx = jnp.max(score, axis=-1, keepdims=True)
            jm = jnp.min(jnp.where(score == mx, jb, nbp), axis=-1, keepdims=True)
            hit = jb == jm
            sel = jnp.where(hit, 1.0, sel)
            score = jnp.where(hit, -jnp.inf, score)
        sel_ref[:, g * nbp:(g + 1) * nbp] = jnp.where(valid, sel, 0.0).astype(BF16)


def cmp_attend(z, q_blk, kvc, a01, nb, Tq, q_off, n_slc):
    aq = A_HEADS * A_HEAD_DIM
    nbp = a01.shape[1]
    tq = _pick(Tq, (128, 64, 32, 16, 8))
    nq = Tq // tq
    ncmp = kvc.shape[1]
    return pl.pallas_call(
        functools.partial(_cmp_kernel, tq=tq, q_off=q_off, n_slc=n_slc, nbp=nbp),
        out_shape=(jax.ShapeDtypeStruct((nb * Tq, aq), F32),
                   jax.ShapeDtypeStruct((nb * Tq, A_GROUPS * nbp), BF16)),
        grid=(nb, nq),
        in_specs=[pl.BlockSpec((tq, aq), lambda b, i: (b * nq + i, q_blk)),
                  pl.BlockSpec((1, ncmp, kvc.shape[2]), lambda b, i: (b, 0, 0)),
                  pl.BlockSpec(a01.shape, lambda b, i: (0, 0))],
        out_specs=(pl.BlockSpec((tq, aq), lambda b, i: (b * nq + i, 0)),
                   pl.BlockSpec((tq, A_GROUPS * nbp), lambda b, i: (b * nq + i, 0))),
        compiler_params=_cparams(2), name="cmp_attend")(z, kvc, a01)


class _FlashCfg:
    def __init__(self, **kw):
        self.__dict__.update(kw)


def _tile_range(c, qi):
    q_lo = c.q_off + qi * c.tq
    last = (q_lo + c.tq - 1 - c.k_off) // c.tk
    if c.mode == "window":
        first = (q_lo - (WINDOW - 1) - c.k_off) // c.tk
    else:
        first = 0 * qi
    return first, last


def _flash_kernel(*refs, c):
    if c.sel:
        q_ref, k_ref, v_ref, sel_ref, o_ref, qs_scr, m_scr, l_scr, acc_scr = refs
    else:
        q_ref, k_ref, v_ref, o_ref, qs_scr, m_scr, l_scr, acc_scr = refs
        sel_ref = None
    qi, j = pl.program_id(1), pl.program_id(2)
    tq, tk, G, hg, dk, dv = c.tq, c.tk, c.G, c.hg, c.dk, c.dv

    @pl.when(j == 0)
    def _():
        m_scr[...] = jnp.full(m_scr.shape, NEG, F32)
        l_scr[...] = jnp.zeros(l_scr.shape, F32)
        acc_scr[...] = jnp.zeros(acc_scr.shape, F32)
        for g in range(G):
            qs = jnp.concatenate([q_ref[:, (g * hg + h) * dk:(g * hg + h + 1) * dk] for h in range(hg)], axis=0)
            qs_scr[g] = (qs * dk ** -0.5).astype(BF16)

    first, last = _tile_range(c, qi)
    jabs = first + j

    @pl.when((jabs >= 0) & (jabs <= jnp.minimum(last, c.nk - 1)))
    def _():
        pos_q = c.q_off + qi * tq + lax.broadcasted_iota(jnp.int32, (tq, 1), 0)
        tok = jabs * tk + lax.broadcasted_iota(jnp.int32, (1, tk), 1)
        dpos = pos_q - (c.k_off + tok)
        mask = dpos >= 0
        if c.mode == "window":
            mask = mask & (dpos < WINDOW)
        if c.sel:
            jb = lax.broadcasted_iota(jnp.int32, (c.nbp, 1), 0)
            e01 = jnp.where(jb == lax.shift_right_logical(tok, int(math.log2(SLC_LEN))), 1.0, 0.0).astype(BF16)
        for g in range(G):
            kg = k_ref[:, g * dk:(g + 1) * dk].astype(BF16)
            vi = g // c.g_per_v
            vg = v_ref[:, vi * dv:(vi + 1) * dv].astype(BF16)
            s = lax.dot_general(qs_scr[g], kg, NT, preferred_element_type=F32).reshape(hg, tq, tk)
            mk = mask
            if c.sel:
                st = jnp.dot(sel_ref[:, g * c.nbp:(g + 1) * c.nbp], e01, preferred_element_type=F32)
                mk = mk & (st > 0.5)
            mk = mk[None]
            s = jnp.where(mk, s, NEG)
            m_old = m_scr[g]
            m_new = jnp.maximum(m_old, jnp.max(s, axis=-1, keepdims=True))
            alpha = jnp.exp(m_old - m_new)
            p = jnp.where(mk, jnp.exp(s - m_new), 0.0)
            l_scr[g] = alpha * l_scr[g] + jnp.sum(p, axis=-1, keepdims=True)
            pv = jnp.dot(p.reshape(hg * tq, tk).astype(BF16), vg, preferred_element_type=F32)
            acc_scr[g] = alpha * acc_scr[g] + pv.reshape(hg, tq, dv)
            m_scr[g] = m_new

    @pl.when(j == pl.num_programs(2) - 1)
    def _():
        for g in range(G):
            l = l_scr[g]
            o = acc_scr[g] / jnp.where(l > 0.0, l, 1.0)
            for h in range(hg):
                o_ref[:, (g * hg + h) * dv:(g * hg + h + 1) * dv] = o[h].astype(o_ref.dtype)


def flash(q2d, q_blk, k2d, k_blk, v2d, v_blk, *, nb, Tq, Tk, tq, tk, G, hg, dk, dv, g_per_v, mode,
          q_off, k_off, sel=None, nbp=0):
    nq, nk = Tq // tq, Tk // tk
    assert Tq % tq == 0 and Tk % tk == 0
    c = _FlashCfg(tq=tq, tk=tk, G=G, hg=hg, dk=dk, dv=dv, g_per_v=g_per_v, mode=mode, q_off=q_off, k_off=k_off,
                  nk=nk, sel=sel is not None, nbp=nbp)
    steps = max(min(_tile_range(c, i)[1], nk - 1) - _tile_range(c, i)[0] + 1 for i in range(nq))
    qw, kw, vw = G * hg * dk, G * dk, (G // g_per_v) * dv

    def kv_map(blk):
        def f(b, i, j):
            first, last = _tile_range(c, i)
            return (b * nk + jnp.clip(first + j, 0, jnp.minimum(last, nk - 1)), blk)
        return f

    in_specs = [pl.BlockSpec((tq, qw), lambda b, i, j: (b * nq + i, q_blk)),
                pl.BlockSpec((tk, kw), kv_map(k_blk)),
                pl.BlockSpec((tk, vw), kv_map(v_blk))]
    args = [q2d, k2d, v2d]
    if sel is not None:
        in_specs.append(pl.BlockSpec((tq, G * nbp), lambda b, i, j: (b * nq + i, 0)))
        args.append(sel)
    return pl.pallas_call(
        functools.partial(_flash_kernel, c=c),
        out_shape=jax.ShapeDtypeStruct((nb * Tq, G * hg * dv), F32),
        grid=(nb, nq, steps),
        in_specs=in_specs,
        out_specs=pl.BlockSpec((tq, G * hg * dv), lambda b, i, j: (b * nq + i, 0)),
        scratch_shapes=[pltpu.VMEM((G, hg * tq, dk), BF16), pltpu.VMEM((G, hg, tq, 1), F32),
                        pltpu.VMEM((G, hg, tq, 1), F32), pltpu.VMEM((G, hg, tq, dv), F32)],
        compiler_params=_cparams(3), name="flash_" + mode + ("_sel" if sel is not None else ""))(*args)


def _seg_mask(C):
    nblk = C // SUB
    nseg = max(SUB * nblk * (nblk - 1) // 2, SUB)
    pm = np.zeros((C, nseg), np.float32)
    for i in range(1, nblk):
        off = SUB * i * (i - 1) // 2
        pm[SUB * i:SUB * (i + 1), off:off + SUB * i] = 1.0
    return pm


def _rec_kernel(*refs, mode, C, H, pos_off, has_s0, layer):
    refs = list(refs)
    a_ref, b_ref, v_ref, gate_ref = refs[:4]
    refs = refs[4:]
    aux_ref = refs.pop(0)
    pm_ref = refs.pop(0)
    s0_ref = refs.pop(0) if has_s0 else None
    o_ref, st_ref, st_scr = refs
    c_id = pl.program_id(1)
    K = 128
    nblk = C // SUB

    @pl.when(c_id == 0)
    def _():
        if has_s0:
            st_scr[...] = s0_ref[0]
        else:
            st_scr[...] = jnp.zeros(st_scr.shape, F32)

    a_all, b_all, v_all, gate_all = a_ref[...], b_ref[...], v_ref[...], gate_ref[...]
    rows = lax.broadcasted_iota(jnp.int32, (C, 1), 0)
    if mode == "ret":
        pos = (pos_off + c_id * C + rows).astype(F32)
        ang = pos * aux_ref[0:1, :]
        cos, sin_s = jnp.cos(ang), jnp.sin(ang) * aux_ref[1:2, :]

        def rope(x):
            return x * cos + pltpu.roll(x, K // 2, 1) * sin_s

        qs = [rope(a_all[:, h * K:(h + 1) * K]) for h in range(H)]
        ks = [rope(b_all[:, h * K:(h + 1) * K]) * B_DK ** -0.5 for h in range(H)]
        gs = [jnp.full((C, K), math.log1p(-2.0 ** (-5.0 - h)), F32) for h in range(H)]
    else:
        x = aux_ref[...]
        ex = jnp.exp(x - jnp.max(x, axis=0, keepdims=True))
        sm = ex / jnp.sum(ex, axis=0, keepdims=True)
        lb_all = jnp.zeros((1, H * K), F32)
        for i in range(1, layer + 1):
            lb_all = lb_all + sm[i:i + 1]
        qs, ks, gs = [], [], []
        for h in range(H):
            ah, zf, lb = a_all[:, h * K:(h + 1) * K], b_all[:, h * K:(h + 1) * K], lb_all[:, h * K:(h + 1) * K]
            f = lb + (1.0 - lb) * _sigmoid(zf)
            qs.append(ah * _sigmoid(ah))
            ks.append(1.0 - f)
            gs.append(jnp.log(jnp.maximum(f, F_FLOOR)))

    ltri = jnp.where(rows >= lax.broadcasted_iota(jnp.int32, (1, C), 1), 1.0, 0.0).astype(BF16)
    bs_all = _dot_01x(ltri, jnp.concatenate(gs, axis=-1))
    srow = lax.broadcasted_iota(jnp.int32, (SUB, 1), 0)

    for h in range(H):
        q, k, bsum = qs[h], ks[h], bs_all[:, h * K:(h + 1) * K]
        v = v_all[:, h * K:(h + 1) * K]
        st = st_scr[h]
        o = lax.dot_general((q * jnp.exp(bsum)).astype(BF16), st.astype(BF16), NT, preferred_element_type=F32)
        if nblk > 1:
            rho = jnp.concatenate(
                [jnp.zeros((SUB, K), F32)] +
                [jnp.broadcast_to(bsum[SUB * i - 1:SUB * i], (SUB, K)) for i in range(1, nblk)], axis=0)
            qt = q * jnp.exp(bsum - rho)
            kt = jnp.concatenate(
                [k[:SUB * i] * jnp.exp(bsum[SUB * i - 1:SUB * i] - bsum[:SUB * i]) for i in range(1, nblk)], axis=0)
            vcat = jnp.concatenate([v[:SUB * i] for i in range(1, nblk)], axis=0)
            p = _dot3_nt(qt, kt) * pm_ref[...]
            o += jnp.dot(p.astype(BF16), vcat.astype(BF16), preferred_element_type=F32)
        diag = []
        for i in range(nblk):
            sl = slice(SUB * i, SUB * (i + 1))
            q8, k8, b8, v8 = q[sl], k[sl], bsum[sl], v[sl]
            od = jnp.zeros((SUB, K), F32)
            for s in range(SUB):
                causal = srow >= s
                dec = jnp.exp(jnp.where(causal, b8 - b8[s:s + 1], 0.0))
                att = jnp.sum(q8 * k8[s:s + 1] * dec, axis=-1, keepdims=True)
                od += jnp.where(causal, att, 0.0) * v8[s:s + 1]
            diag.append(od)
        o += jnp.concatenate(diag, axis=0) if nblk > 1 else diag[0]
        bend = bsum[C - 1:C]
        kst = k * jnp.exp(bend - bsum)
        st_scr[h] = st * jnp.exp(bend) + lax.dot_general(v.astype(BF16), kst.astype(BF16), TN,
                                                         preferred_element_type=F32)
        gate = gate_all[:, h * K:(h + 1) * K]
        o = o * lax.rsqrt(jnp.mean(o * o, axis=-1, keepdims=True) + EPS) * (gate * _sigmoid(gate))
        o_ref[:, h * K:(h + 1) * K] = o.astype(o_ref.dtype)

    @pl.when(c_id == pl.num_programs(1) - 1)
    def _():
        st_ref[0] = st_scr[...]


def recurrence(z, blks, aux, *, mode, nb, T, pos_off, s0t=None, layer=0):
    H, K = 4, 128
    C = CHUNK if T % CHUNK == 0 else T
    nc = T // C
    pm = _seg_mask(C)
    nseg = pm.shape[1]
    row = lambda blk: (lambda b, c: (b * nc + c, blk))
    in_specs = [pl.BlockSpec((C, H * K), row(blks[0])), pl.BlockSpec((C, H * K), row(blks[1])),
                pl.BlockSpec((C, H * K), row(blks[2])), pl.BlockSpec((C, H * K), row(blks[3])),
                pl.BlockSpec(aux.shape, lambda b, c: (0, 0)),
                pl.BlockSpec((C, nseg), lambda b, c: (0, 0))]
    args = [z, z, z, z, aux, jnp.asarray(pm)]
    if s0t is not None:
        in_specs.append(pl.BlockSpec((1, H, K, K), lambda b, c: (b, 0, 0, 0)))
        args.append(s0t)
    return pl.pallas_call(
        functools.partial(_rec_kernel, mode=mode, C=C, H=H, pos_off=pos_off, has_s0=s0t is not None, layer=layer),
        out_shape=(jax.ShapeDtypeStruct((nb * T, H * K), BF16), jax.ShapeDtypeStruct((nb, H, K, K), F32)),
        grid=(nb, nc),
        in_specs=in_specs,
        out_specs=(pl.BlockSpec((C, H * K), lambda b, c: (b * nc + c, 0)),
                   pl.BlockSpec((1, H, K, K), lambda b, c: (b, 0, 0, 0))),
        scratch_shapes=[pltpu.VMEM((H, K, K), F32)],
        compiler_params=_cparams(2), name="recurrence_" + mode)(*args)


def _gather_kernel(pt_ref, cache_ref, *refs, n_pages):
    o_ref = refs[-1]
    p = pl.program_id(1)

    @pl.when(p < n_pages)
    def _():
        o_ref[0, 0] = cache_ref[0, 0]

    if len(refs) == 2:
        @pl.when(p >= n_pages)
        def _():
            o_ref[0, 0] = refs[0][0]


def gather_pages(cache, layer, page_table, tail=None):
    nb, n_pages = page_table.shape
    _, _, ps, C = cache.shape
    n_out = n_pages + (tail is not None)
    in_specs = [pl.BlockSpec((1, 1, ps, C), lambda b, p, pt: (layer, pt[b * n_pages + jnp.minimum(p, n_pages - 1)], 0, 0))]
    args = [page_table.reshape(-1), cache]
    if tail is not None:
        in_specs.append(pl.BlockSpec((1, ps, C), lambda b, p, pt: (b, 0, 0)))
        args.append(tail)
    grid_spec = pltpu.PrefetchScalarGridSpec(
        num_scalar_prefetch=1, grid=(nb, n_out), in_specs=in_specs,
        out_specs=pl.BlockSpec((1, 1, ps, C), lambda b, p, pt: (b, p, 0, 0)))
    return pl.pallas_call(
        functools.partial(_gather_kernel, n_pages=n_pages),
        out_shape=jax.ShapeDtypeStruct((nb, n_out, ps, C), F32),
        grid_spec=grid_spec,
        compiler_params=_cparams(2), name="gather_pages")(*args)


def _even_w_in(w):
    aq, akv = A_HEADS * A_HEAD_DIM, A_GROUPS * A_HEAD_DIM
    splits = np.cumsum([aq] + [akv] * 6 + [3 * A_HEADS] + [512] * 4)[:-1]
    q, kc, vc, ks, vs, kw, vw, gt, rq, rk, rv, rg = jnp.split(w, [int(s) for s in splits], axis=1)
    gt = jnp.pad(gt, ((0, 0), (0, 128 - gt.shape[1])))
    return jnp.concatenate([q, rq, rk, rv, rg, kc, vc, ks, vs, kw, vw, gt], axis=1).astype(BF16)


EV_Q, EV_RQ, EV_RK, EV_RV, EV_RG = 0, 1, 2, 3, 4
EV_KS, EV_VS, EV_KW, EV_VW, EV_GT = 22, 23, 24, 25, 26
EV_CMP, EV_SLC, EV_WIN = 2560, 2816, 3072


def _compress_w(w_cmp, pe):
    G, d = A_GROUPS, A_HEAD_DIM
    r = CMP_LEN // CMP_STRIDE
    assert r == 2
    w = w_cmp.reshape(2, r, CMP_STRIDE, d, d)
    eye_kv = jnp.eye(2, dtype=F32)
    eye_g = jnp.eye(G, dtype=F32)
    big = jnp.einsum("khlde,kq,gp->lqpdhkge", w, eye_kv, eye_g)
    big = big.reshape(CMP_STRIDE * 2 * G * d, r * 2 * G * d).astype(BF16)
    pe_r = pe.reshape(2, r, CMP_STRIDE, d)
    rows = jnp.broadcast_to(pe_r.transpose(1, 2, 0, 3)[:, :, :, None, :], (r, CMP_STRIDE, 2, G, d))
    rows = rows.reshape(r, CMP_STRIDE * 2 * G * d)
    return big, jnp.pad(rows, ((0, 8 - r), (0, 0)))


def _slc_sum_matrix(n_rows, n_cmp, nbp):
    a = np.zeros((n_rows, nbp), np.float32)
    per, left = SLC_LEN // CMP_STRIDE, CMP_LEN // CMP_STRIDE - 1
    for j in range(nbp):
        for n in range(per * j - left, per * j + per):
            if 0 <= n < n_cmp:
                a[n, j] = 1.0
    return jnp.asarray(a, BF16)


def _gate_expand():
    e = np.zeros((128, 3 * A_HEADS * A_HEAD_DIM), np.float32)
    for h in range(A_HEADS):
        for i in range(3):
            e[3 * h + i, i * A_HEADS * A_HEAD_DIM + h * A_HEAD_DIM:i * A_HEADS * A_HEAD_DIM + (h + 1) * A_HEAD_DIM] = 1.0
    return jnp.asarray(e, BF16)


def _rope_aux():
    half = B_DK // 2
    freqs = ROPE_BASE ** (-jnp.arange(half, dtype=F32) / half)
    sign = jnp.concatenate([-jnp.ones((half,), F32), jnp.ones((half,), F32)])
    return jnp.stack([jnp.concatenate([freqs, freqs]), sign])


def _nsa(z, nb, T, q_off, kvc, n_slc, slc_src, win_src):
    n_cmp_pad = kvc.shape[1]
    nbp = -(-n_slc // 128) * 128
    a01 = _slc_sum_matrix(n_cmp_pad, n_cmp_pad - 1, nbp)
    o_cmp, sel = cmp_attend(z, EV_Q, kvc, a01, nb, T, q_off, n_slc)
    tq = _pick(T, (128, 64, 32, 16, 8))
    common = dict(nb=nb, Tq=T, tq=tq, G=A_GROUPS, hg=A_HEADS // A_GROUPS, dk=A_HEAD_DIM, dv=A_HEAD_DIM,
                  g_per_v=1, q_off=q_off)
    arr, kb, vb, Tk, tk, k_off = slc_src
    o_slc = flash(z, EV_Q, arr, kb, arr, vb, Tk=Tk, tk=tk, mode="causal", k_off=k_off, sel=sel, nbp=nbp, **common)
    arr, kb, vb, Tk, tk, k_off = win_src
    o_win = flash(z, EV_Q, arr, kb, arr, vb, Tk=Tk, tk=tk, mode="window", k_off=k_off, **common)
    return o_cmp, o_slc, o_win


def _even_layer(xp, xs, p, a, dims, caches):
    B, T, nbs, Ts, past = dims
    g = p["norm_g"]
    w_in = _even_w_in(p["w_in_a"][a])
    wbig, pe2 = _compress_w(p["cmp_w"][a], p["cmp_pos"][a])
    w_out = p["w_out_a"][a].astype(BF16)
    aq = A_HEADS * A_HEAD_DIM
    wa, wb = w_out[:aq], w_out[aq:]
    e01 = _gate_expand()
    aux = _rope_aux()
    akv2 = 2 * A_GROUPS * A_HEAD_DIM

    zp = rms_matmul(xp, g[0:1], w_in, F32)
    kv_cmp_p = zp[:, EV_CMP:EV_CMP + akv2]
    kv_slc_p = zp[:, EV_SLC:EV_SLC + akv2]
    kv_win_p = zp[:, EV_WIN:EV_WIN + akv2]
    kvc_p = compress(kv_cmp_p.reshape(B, T // CMP_STRIDE, CMP_STRIDE * akv2), wbig, pe2)
    tk = _pick(T, (512, 256, 128))
    tw = _pick(T, (128,))
    o_cmp, o_slc, o_win = _nsa(zp, B, T, 0, kvc_p, -(-T // SLC_LEN),
                               (zp, EV_KS, EV_VS, T, tk, 0), (zp, EV_KW, EV_VW, T, tw, 0))
    ob_p, st_p = recurrence(zp, (EV_RQ, EV_RK, EV_RV, EV_RG), aux, mode="ret", nb=B, T=T, pos_off=0)
    xp = even_out(o_cmp, o_slc, o_win, zp, EV_GT, ob_p, e01, wa, wb, g[1:2], xp)

    cache_cmp, cache_slc, win_buf, s0, page_table = caches
    zs = rms_matmul(xs, g[0:1], w_in, F32)
    kv_cmp_s = zs[:, EV_CMP:EV_CMP + akv2]
    kv_slc_s = zs[:, EV_SLC:EV_SLC + akv2]
    kv_win_s = zs[:, EV_WIN:EV_WIN + akv2]
    n_pages = page_table.shape[1]
    assert Ts < CMP_STRIDE and past % SLC_LEN == 0 and Ts <= PAGE_SIZE
    hist_cmp = gather_pages(cache_cmp, a, page_table)
    kvc_s = compress(hist_cmp.reshape(nbs, past // CMP_STRIDE, CMP_STRIDE * akv2), wbig, pe2)
    tail = jnp.pad(kv_slc_s.reshape(nbs, Ts, akv2), ((0, 0), (0, PAGE_SIZE - Ts), (0, 0)))
    full_slc = gather_pages(cache_slc, a, page_table, tail).reshape(nbs * (past + PAGE_SIZE), akv2)
    band = jnp.concatenate([win_buf, kv_win_s.reshape(nbs, Ts, akv2)], axis=1)
    n_buf = win_buf.shape[1]
    tks = _pick(past + PAGE_SIZE, (640, 512, 384, 256, 128))
    o_cmp, o_slc, o_win = _nsa(zs, nbs, Ts, past, kvc_s, -(-(past + Ts) // SLC_LEN),
                               (full_slc, 0, 1, past + PAGE_SIZE, tks, 0),
                               (band.reshape(nbs * (n_buf + Ts), akv2), 0, 1, n_buf + Ts, n_buf + Ts, past - n_buf))
    ob_s, st_s = recurrence(zs, (EV_RQ, EV_RK, EV_RV, EV_RG), aux, mode="ret", nb=nbs, T=Ts, pos_off=past,
                            s0t=jnp.swapaxes(s0, -1, -2))
    xs = even_out(o_cmp, o_slc, o_win, zs, EV_GT, ob_s, e01, wa, wb, g[1:2], xs)

    shp = lambda x, n, t: x.reshape(n, t, 2, A_GROUPS, A_HEAD_DIM)
    wkeep = min(WINDOW, T)
    outs = (shp(kv_cmp_p, B, T), shp(kv_cmp_s, nbs, Ts), shp(kv_slc_p, B, T), shp(kv_slc_s, nbs, Ts),
            shp(kv_win_p, B, T)[:, T - wkeep:], shp(band[:, Ts:], nbs, n_buf),
            jnp.swapaxes(st_p, -1, -2), jnp.swapaxes(st_s, -1, -2).astype(s0.dtype))
    return xp, xs, outs


def _odd_layer(xp, xs, p, o, layer, dims, caches):
    B, T, nbs, Ts, past = dims
    g = p["norm_g"]
    w_in = p["w_in_c"][o].astype(BF16)
    w_out = p["w_out_c"][o].astype(BF16)
    cw = C_HEADS * 2 * C_HEAD_DIM
    wc, wd = w_out[:cw], w_out[cw:]
    lam_init = 0.8 - 0.6 * math.exp(-0.3 * layer)
    lam_p = p["diff_lam"][o]
    lb_logits = p["hgrn_lb"]
    dcommon = dict(G=2 * C_HEADS, hg=1, dk=C_HEAD_DIM, dv=2 * C_HEAD_DIM, g_per_v=2, mode="causal", k_off=0)

    zp = rms_matmul(xp, g[0:1], w_in, F32)
    tq = _pick(T, (256, 128, 64, 32, 16, 8))
    tk = _pick(T, (512, 256, 128))
    o_diff = flash(zp, 0, zp, 1, zp, 2, nb=B, Tq=T, Tk=T, tq=tq, tk=tk, q_off=0, **dcommon)
    od_p, st_p = recurrence(zp, (3, 4, 5, 6), lb_logits, mode="hgrn", nb=B, T=T, pos_off=0, layer=o)
    xp = odd_out(o_diff, od_p, lam_p, lam_init, wc, wd, g[1:2], xp)

    cache_diff, s0, page_table = caches
    zs = rms_matmul(xs, g[0:1], w_in, F32)
    kv_s = zs[:, cw:3 * cw]
    tail = jnp.pad(kv_s.reshape(nbs, Ts, 2 * cw), ((0, 0), (0, PAGE_SIZE - Ts), (0, 0)))
    full = gather_pages(cache_diff, o, page_table, tail).reshape(nbs * (past + PAGE_SIZE), 2 * cw)
    tks = _pick(past + PAGE_SIZE, (640, 512, 384, 256, 128))
    o_diff = flash(zs, 0, full, 0, full, 1, nb=nbs, Tq=Ts, Tk=past + PAGE_SIZE, tq=Ts, tk=tks, q_off=past, **dcommon)
    od_s, st_s = recurrence(zs, (3, 4, 5, 6), lb_logits, mode="hgrn", nb=nbs, T=Ts, pos_off=past,
                            s0t=jnp.swapaxes(s0, -1, -2), layer=o)
    xs = odd_out(o_diff, od_s, lam_p, lam_init, wc, wd, g[1:2], xs)

    shp = lambda x, n, t: x.reshape(n, t, 2, C_HEADS, 2 * C_HEAD_DIM)
    outs = (shp(zp[:, cw:3 * cw], B, T), shp(kv_s, nbs, Ts),
            jnp.swapaxes(st_p, -1, -2), jnp.swapaxes(st_s, -1, -2).astype(s0.dtype))
    return xp, xs, outs


def _tail_layers(xp, xs, p, layer, dims, mem_prompt, cache_mem):
    B, T, nbs, Ts, _ = dims
    g = p["norm_g"][layer]
    D = xp.shape[1]
    w_q = p["w_xq"][layer].astype(BF16)
    w_kv = p["w_xkv"][layer].astype(BF16)
    w_o = p["w_xo"][layer].astype(BF16)
    w_up = p["w_up"][layer].astype(BF16)
    w_down = p["w_down"][layer].astype(BF16)
    n_mem = mem_prompt.shape[1]
    kv_mem = rms_matmul(mem_prompt.reshape(B * n_mem, D), g[0:1], w_kv, F32, norm=False).reshape(B, n_mem, 2 * D)
    qp = rms_matmul(xp, g[2:3], w_q, BF16)
    xp = proj_res(xattn(qp, kv_mem, T), w_o, g[3:4], xp)
    qs = rms_matmul(xs, g[2:3], w_q, BF16)
    xs = proj_res(xattn(qs, cache_mem.reshape(nbs, n_mem, 2 * D), Ts), w_o, g[3:4], xs)
    xp = mlp(xp, g[4:5], g[5:6], w_up, w_down)
    xs = mlp(xs, g[4:5], g[5:6], w_up, w_down)
    return xp, xs, kv_mem.reshape(B, n_mem, 2, X_HEADS, D // X_HEADS)


def kernel(x_prompt, x_sample, cache_nsa_cmp_kv, cache_nsa_slc_kv, cache_nsa_win_kv, state_ret, cache_diff_kv, state_hgrn, cache_mem_kv, page_table, mem_prompt, norm_g, w_in_a, cmp_pos, cmp_w, w_out_a, w_in_c, diff_lam, hgrn_lb, w_out_c, w_xq, w_xkv, w_xo, w_up, w_down):
    B, T, D = x_prompt.shape
    nbs, Ts, _ = x_sample.shape
    depth = norm_g.shape[0]
    n_pages = page_table.shape[1]
    past = n_pages * PAGE_SIZE
    dims = (B, T, nbs, Ts, past)
    p = dict(w_in_a=w_in_a, cmp_pos=cmp_pos, cmp_w=cmp_w, w_out_a=w_out_a, w_in_c=w_in_c, diff_lam=diff_lam,
             hgrn_lb=hgrn_lb.astype(F32), w_out_c=w_out_c, w_xq=w_xq, w_xkv=w_xkv, w_xo=w_xo, w_up=w_up, w_down=w_down)
    xp = x_prompt.reshape(B * T, D)
    xs = x_sample.reshape(nbs * Ts, D)
    flat4 = lambda c: c.reshape(c.shape[0], c.shape[1], c.shape[2], -1)
    c_cmp, c_slc, c_diff = flat4(cache_nsa_cmp_kv), flat4(cache_nsa_slc_kv), flat4(cache_diff_kv)
    win_all = cache_nsa_win_kv.reshape(cache_nsa_win_kv.shape[0], nbs, cache_nsa_win_kv.shape[2], -1)
    ev, od, mem = [], [], []
    for layer in range(depth):
        pl_ = dict(p, norm_g=norm_g[layer])
        if layer % 2 == 0:
            a = layer // 2
            xp, xs, outs = _even_layer(xp, xs, pl_, a, dims, (c_cmp, c_slc, win_all[a], state_ret[a], page_table))
            ev.append(outs)
        else:
            o = layer // 2
            xp, xs, outs = _odd_layer(xp, xs, pl_, o, layer, dims, (c_diff, state_hgrn[o], page_table))
            od.append(outs)
        xp, xs, kvm = _tail_layers(xp, xs, dict(p, norm_g=norm_g), layer, dims, mem_prompt, cache_mem_kv[layer])
        mem.append(kvm)
    stack = lambda lst, i: jnp.stack([t[i] for t in lst])
    return (xp.reshape(B, T, D), xs.reshape(nbs, Ts, D),
            stack(ev, 0), stack(ev, 1), stack(ev, 2), stack(ev, 3), stack(ev, 4), stack(ev, 5), stack(ev, 6), stack(ev, 7),
            stack(od, 0), stack(od, 1), stack(od, 2), stack(od, 3), jnp.stack(mem))
```

```python
import functools
import math

import numpy as np
import jax
import jax.numpy as jnp
from jax import lax
from jax.experimental import pallas as pl
from jax.experimental.pallas import tpu as pltpu

F32 = jnp.float32
BF16 = jnp.bfloat16

EPS = 1e-6
NEG = -1e30
BIG = 1e9
A_HEADS, A_GROUPS, A_HEAD_DIM = 8, 2, 64
CMP_LEN, CMP_STRIDE, SLC_LEN, SLC_TOPK, WINDOW = 32, 16, 64, 16, 512
B_HEADS, B_DK = 4, 128
ROPE_BASE = 10000.0
CHUNK = 64
C_HEADS, C_HEAD_DIM = 4, 64
D_HEADS = 4
F_FLOOR = 1e-6
X_HEADS = 4
PAGE_SIZE = 128
SUB = 8

VMEM_LIMIT = 56 * 1024 * 1024

NT = (((1,), (1,)), ((), ()))
TN = (((0,), (0,)), ((), ()))


def _cparams(n_grid):
    return pltpu.CompilerParams(dimension_semantics=("arbitrary",) * n_grid, vmem_limit_bytes=VMEM_LIMIT)


def _pick(n, cands):
    for c in cands:
        if n % c == 0:
            return c
    return n


def _split3(x):
    hi = x.astype(BF16)
    r = x - hi.astype(F32)
    mid = r.astype(BF16)
    lo = (r - mid.astype(F32)).astype(BF16)
    return hi, mid, lo


def _dot_x01(x, m01):
    return sum(jnp.dot(p, m01, preferred_element_type=F32) for p in _split3(x))


def _dot_01x(m01, x):
    return sum(jnp.dot(m01, p, preferred_element_type=F32) for p in _split3(x))


def _dot3_nt(a, b):
    ah = a.astype(BF16)
    al = (a - ah.astype(F32)).astype(BF16)
    bh = b.astype(BF16)
    bl = (b - bh.astype(F32)).astype(BF16)
    d = lambda x, y: lax.dot_general(x, y, NT, preferred_element_type=F32)
    return d(ah, bh) + d(ah, bl) + d(al, bh)


def _sigmoid(x):
    return 1.0 / (1.0 + jnp.exp(-x))


def _rms(x, g):
    return x * lax.rsqrt(jnp.mean(x * x, axis=-1, keepdims=True) + EPS) * g


def _rms_matmul_kernel(x_ref, g_ref, w_ref, o_ref, *, norm):
    x = x_ref[...]
    if norm:
        x = _rms(x, g_ref[...])
    o_ref[...] = jnp.dot(x.astype(BF16), w_ref[...], preferred_element_type=F32).astype(o_ref.dtype)


def rms_matmul(x, g, w, out_dtype, norm=True):
    R, D = x.shape
    N = w.shape[1]
    tm = _pick(R, (512, 256, 128, 64, 32, 16, 8))
    tn = _pick(N, (1792, 1152, 1024, 896, 768, 640, 512, 384, 256, 128))
    return pl.pallas_call(
        functools.partial(_rms_matmul_kernel, norm=norm),
        out_shape=jax.ShapeDtypeStruct((R, N), out_dtype),
        grid=(R // tm, N // tn),
        in_specs=[pl.BlockSpec((tm, D), lambda i, j: (i, 0)),
                  pl.BlockSpec((1, D), lambda i, j: (0, 0)),
                  pl.BlockSpec((D, tn), lambda i, j: (0, j))],
        out_specs=pl.BlockSpec((tm, tn), lambda i, j: (i, j)),
        compiler_params=_cparams(2), name="rms_matmul")(x, g, w)


def _mlp_kernel(x_ref, g4_ref, g5_ref, wu_ref, wd_ref, o_ref, xn_scr, acc_scr):
    j = pl.program_id(1)

    @pl.when(j == 0)
    def _():
        xn_scr[...] = _rms(x_ref[...], g4_ref[...]).astype(BF16)
        acc_scr[...] = jnp.zeros_like(acc_scr)

    h = jnp.dot(xn_scr[...], wu_ref[...], preferred_element_type=F32)
    h = jnp.square(jnp.maximum(h, 0.0))
    acc_scr[...] += jnp.dot(h.astype(BF16), wd_ref[...], preferred_element_type=F32)

    @pl.when(j == pl.num_programs(1) - 1)
    def _():
        o_ref[...] = x_ref[...] + _rms(acc_scr[...], g5_ref[...])


def mlp(x, g4, g5, w_up, w_down):
    R, D = x.shape
    F = w_up.shape[1]
    tm = _pick(R, (512, 256, 128, 64, 32, 16, 8))
    tf = _pick(F, (1024, 512, 256, 128))
    return pl.pallas_call(
        _mlp_kernel,
        out_shape=jax.ShapeDtypeStruct((R, D), F32),
        grid=(R // tm, F // tf),
        in_specs=[pl.BlockSpec((tm, D), lambda i, j: (i, 0)),
                  pl.BlockSpec((1, D), lambda i, j: (0, 0)),
                  pl.BlockSpec((1, D), lambda i, j: (0, 0)),
                  pl.BlockSpec((D, tf), lambda i, j: (0, j)),
                  pl.BlockSpec((tf, D), lambda i, j: (j, 0))],
        out_specs=pl.BlockSpec((tm, D), lambda i, j: (i, 0)),
        scratch_shapes=[pltpu.VMEM((tm, D), BF16), pltpu.VMEM((tm, D), F32)],
        compiler_params=_cparams(2), name="mlp")(x, g4, g5, w_up, w_down)


def _proj_res_kernel(a_ref, w_ref, g_ref, x_ref, o_ref):
    y = jnp.dot(a_ref[...], w_ref[...], preferred_element_type=F32)
    o_ref[...] = x_ref[...] + _rms(y, g_ref[...])


def proj_res(a, w, g, x):
    R, D = x.shape
    K = a.shape[1]
    tm = _pick(R, (512, 256, 128, 64, 32, 16, 8))
    return pl.pallas_call(
        _proj_res_kernel,
        out_shape=jax.ShapeDtypeStruct((R, D), F32),
        grid=(R // tm,),
        in_specs=[pl.BlockSpec((tm, K), lambda i: (i, 0)),
                  pl.BlockSpec((K, D), lambda i: (0, 0)),
                  pl.BlockSpec((1, D), lambda i: (0, 0)),
                  pl.BlockSpec((tm, D), lambda i: (i, 0))],
        out_specs=pl.BlockSpec((tm, D), lambda i: (i, 0)),
        compiler_params=_cparams(1), name="proj_res")(a, w, g, x)


def _even_out_kernel(oc_ref, os_ref, ow_ref, gt_ref, ob_ref, e_ref, wa_ref, wb_ref, g_ref, x_ref, o_ref):
    gates = _sigmoid(gt_ref[...])
    ge = _dot_x01(gates, e_ref[...])
    aq = oc_ref.shape[1]
    oa = ge[:, :aq] * oc_ref[...] + ge[:, aq:2 * aq] * os_ref[...] + ge[:, 2 * aq:] * ow_ref[...]
    y = jnp.dot(oa.astype(BF16), wa_ref[...], preferred_element_type=F32)
    y += jnp.dot(ob_ref[...], wb_ref[...], preferred_element_type=F32)
    o_ref[...] = x_ref[...] + _rms(y, g_ref[...])


def even_out(o_cmp, o_slc, o_win, z, gt_blk, o_b, e01, wa, wb, g, x):
    R, D = x.shape
    aq = o_cmp.shape[1]
    tm = _pick(R, (256, 128, 64, 32, 16, 8))
    row = lambda i: (i, 0)
    fix = lambda i: (0, 0)
    return pl.pallas_call(
        _even_out_kernel,
        out_shape=jax.ShapeDtypeStruct((R, D), F32),
        grid=(R // tm,),
        in_specs=[pl.BlockSpec((tm, aq), row), pl.BlockSpec((tm, aq), row), pl.BlockSpec((tm, aq), row),
                  pl.BlockSpec((tm, 128), lambda i: (i, gt_blk)),
                  pl.BlockSpec((tm, o_b.shape[1]), row),
                  pl.BlockSpec(e01.shape, fix), pl.BlockSpec(wa.shape, fix), pl.BlockSpec(wb.shape, fix),
                  pl.BlockSpec((1, D), fix), pl.BlockSpec((tm, D), row)],
        out_specs=pl.BlockSpec((tm, D), row),
        compiler_params=_cparams(1), name="even_out")(o_cmp, o_slc, o_win, z, o_b, e01, wa, wb, g, x)


def _odd_out_kernel(oc_ref, od_ref, lam_ref, wc_ref, wd_ref, g_ref, x_ref, o_ref, *, lam_init, heads, dv):
    lp = lam_ref[...]
    lam = (jnp.exp(jnp.sum(lp[0:1] * lp[1:2], axis=-1, keepdims=True))
           - jnp.exp(jnp.sum(lp[2:3] * lp[3:4], axis=-1, keepdims=True)) + lam_init)
    parts = []
    for h in range(heads):
        o1 = oc_ref[:, (2 * h) * dv:(2 * h + 1) * dv]
        o2 = oc_ref[:, (2 * h + 1) * dv:(2 * h + 2) * dv]
        o = o1 - lam * o2
        o = o * lax.rsqrt(jnp.mean(o * o, axis=-1, keepdims=True) + EPS) * (1.0 - lam_init)
        parts.append(o.astype(BF16))
    oc = jnp.concatenate(parts, axis=-1)
    y = jnp.dot(oc, wc_ref[...], preferred_element_type=F32)
    y += jnp.dot(od_ref[...], wd_ref[...], preferred_element_type=F32)
    o_ref[...] = x_ref[...] + _rms(y, g_ref[...])


def odd_out(o_diff, o_d, lam_p, lam_init, wc, wd, g, x):
    R, D = x.shape
    tm = _pick(R, (256, 128, 64, 32, 16, 8))
    row = lambda i: (i, 0)
    fix = lambda i: (0, 0)
    return pl.pallas_call(
        functools.partial(_odd_out_kernel, lam_init=lam_init, heads=C_HEADS, dv=2 * C_HEAD_DIM),
        out_shape=jax.ShapeDtypeStruct((R, D), F32),
        grid=(R // tm,),
        in_specs=[pl.BlockSpec((tm, o_diff.shape[1]), row), pl.BlockSpec((tm, o_d.shape[1]), row),
                  pl.BlockSpec(lam_p.shape, fix), pl.BlockSpec(wc.shape, fix), pl.BlockSpec(wd.shape, fix),
                  pl.BlockSpec((1, D), fix), pl.BlockSpec((tm, D), row)],
        out_specs=pl.BlockSpec((tm, D), row),
        compiler_params=_cparams(1), name="odd_out")(o_diff, o_d, lam_p, wc, wd, g, x)


def _xattn_kernel(q_ref, kv_ref, o_ref, *, heads, scale, rows):
    dm = q_ref.shape[1]
    hd = dm // heads
    for h in range(heads):
        qh = q_ref[:, h * hd:(h + 1) * hd]
        if rows:
            nh = hd // 128
            per_tok = 2 * nh * heads
            n_mem = kv_ref.shape[2] // per_tok
            row = lambda slot: jnp.concatenate(
                [kv_ref[0, 0, pl.ds((slot * nh + i) * heads + h, n_mem, stride=per_tok), :] for i in range(nh)],
                axis=-1).astype(BF16)
            kh, vh = row(0), row(1)
        else:
            kh = kv_ref[0, :, h * hd:(h + 1) * hd].astype(BF16)
            vh = kv_ref[0, :, dm + h * hd:dm + (h + 1) * hd].astype(BF16)
        s = lax.dot_general(qh, kh, NT, preferred_element_type=F32) * scale
        e = jnp.exp(s - jnp.max(s, axis=-1, keepdims=True))
        p = e / jnp.sum(e, axis=-1, keepdims=True)
        o = jnp.dot(p.astype(BF16), vh, preferred_element_type=F32)
        o_ref[:, h * hd:(h + 1) * hd] = o.astype(o_ref.dtype)


def xattn(q, kv, rows_per_batch, layer=None):
    R, D = q.shape
    tq = _pick(rows_per_batch, (512, 256, 128, 64, 32, 16, 8))
    per = rows_per_batch // tq
    if layer is None:
        kv_spec = pl.BlockSpec((1,) + kv.shape[1:], lambda i: (i // per, 0, 0))
    else:
        kv_spec = pl.BlockSpec((1, 1) + kv.shape[2:], lambda i: (layer, i // per, 0, 0))
    return pl.pallas_call(
        functools.partial(_xattn_kernel, heads=X_HEADS, scale=(D // X_HEADS) ** -0.5, rows=layer is not None),
        out_shape=jax.ShapeDtypeStruct((R, D), BF16),
        grid=(R // tq,),
        in_specs=[pl.BlockSpec((tq, D), lambda i: (i, 0)), kv_spec],
        out_specs=pl.BlockSpec((tq, D), lambda i: (i, 0)),
        compiler_params=_cparams(1), name="xattn")(q, kv)


def _compress_kernel(x_ref, w_ref, pe_ref, o_ref):
    x = x_ref[0].astype(BF16)
    ab = jnp.dot(x, w_ref[...], preferred_element_type=F32)
    cab = jnp.dot(pe_ref[...].astype(BF16), w_ref[...], preferred_element_type=F32)
    half = ab.shape[1] // 2
    a, b = ab[:, :half], ab[:, half:]
    const = cab[0:1, :half] + cab[1:2, half:]
    n = a.shape[0]
    o_ref[0] = a + pltpu.roll(b, n - 1, 0) + const


def compress(xc, wbig, pe2):
    nb, n_chunk, kdim = xc.shape
    cout = wbig.shape[1] // 2
    return pl.pallas_call(
        _compress_kernel,
        out_shape=jax.ShapeDtypeStruct((nb, n_chunk, cout), F32),
        grid=(nb,),
        in_specs=[pl.BlockSpec((1, n_chunk, kdim), lambda b: (b, 0, 0)),
                  pl.BlockSpec(wbig.shape, lambda b: (0, 0)),
                  pl.BlockSpec(pe2.shape, lambda b: (0, 0))],
        out_specs=pl.BlockSpec((1, n_chunk, cout), lambda b: (b, 0, 0)),
        compiler_params=_cparams(1), name="compress")(xc, wbig, pe2)


def _cmp_kernel(q_ref, kv_ref, a_ref, o_ref, sel_ref, *, tq, q_off, n_slc, nbp):
    G, hg, d = A_GROUPS, A_HEADS // A_GROUPS, A_HEAD_DIM
    qi = pl.program_id(1)
    ncmp = kv_ref.shape[1]
    pos_q = q_off + qi * tq + lax.broadcasted_iota(jnp.int32, (tq, 1), 0)
    cmp_end = lax.broadcasted_iota(jnp.int32, (1, ncmp), 1) * CMP_STRIDE + (CMP_LEN - 1)
    mask3 = (cmp_end <= pos_q)[None]
    jb = lax.broadcasted_iota(jnp.int32, (1, nbp), 1)
    blk_q = lax.shift_right_logical(pos_q, int(math.log2(SLC_LEN)))
    valid = jb <= blk_q
    forced = valid & ((jb == 0) | (jb >= blk_q - 1))
    k_top = min(SLC_TOPK, n_slc)
    for g in range(G):
        kc = kv_ref[0, :, g * d:(g + 1) * d].astype(BF16)
        vc = kv_ref[0, :, (G + g) * d:(G + g + 1) * d].astype(BF16)
        qs = jnp.concatenate([q_ref[:, (g * hg + h) * d:(g * hg + h + 1) * d] for h in range(hg)], axis=0)
        qs = (qs * d ** -0.5).astype(BF16)
        s = lax.dot_general(qs, kc, NT, preferred_element_type=F32).reshape(hg, tq, ncmp)
        s = jnp.where(mask3, s, NEG)
        e = jnp.where(mask3, jnp.exp(s - jnp.max(s, axis=-1, keepdims=True)), 0.0)
        den = jnp.sum(e, axis=-1, keepdims=True)
        p = e / jnp.where(den > 0.0, den, 1.0)
        o = jnp.dot(p.reshape(hg * tq, ncmp).astype(BF16), vc, preferred_element_type=F32)
        for h in range(hg):
            o_ref[:, (g * hg + h) * d:(g * hg + h + 1) * d] = o[h * tq:(h + 1) * tq]
        p_slc = _dot_x01(jnp.sum(p, axis=0), a_ref[...])
        score = jnp.where(forced, BIG, jnp.where(valid, p_slc, -BIG))
        score = jnp.where(jb < n_slc, score, -jnp.inf)
        sel = jnp.zeros((tq, nbp), F32)
        for _ in range(k_top):
            mx = jnp.max(score, axis=-1, keepdims=True)
            jm = jnp.min(jnp.where(score == mx, jb, nbp), axis=-1, keepdims=True)
            hit = jb == jm
            sel = jnp.where(hit, 1.0, sel)
            score = jnp.where(hit, -jnp.inf, score)
        sel_ref[:, g * nbp:(g + 1) * nbp] = jnp.where(valid, sel, 0.0).astype(BF16)


def cmp_attend(z, q_blk, kvc, a01, nb, Tq, q_off, n_slc):
    aq = A_HEADS * A_HEAD_DIM
    nbp = a01.shape[1]
    tq = _pick(Tq, (128, 64, 32, 16, 8))
    nq = Tq // tq
    ncmp = kvc.shape[1]
    return pl.pallas_call(
        functools.partial(_cmp_kernel, tq=tq, q_off=q_off, n_slc=n_slc, nbp=nbp),
        out_shape=(jax.ShapeDtypeStruct((nb * Tq, aq), F32),
                   jax.ShapeDtypeStruct((nb * Tq, A_GROUPS * nbp), BF16)),
        grid=(nb, nq),
        in_specs=[pl.BlockSpec((tq, aq), lambda b, i: (b * nq + i, q_blk)),
                  pl.BlockSpec((1, ncmp, kvc.shape[2]), lambda b, i: (b, 0, 0)),
                  pl.BlockSpec(a01.shape, lambda b, i: (0, 0))],
        out_specs=(pl.BlockSpec((tq, aq), lambda b, i: (b * nq + i, 0)),
                   pl.BlockSpec((tq, A_GROUPS * nbp), lambda b, i: (b * nq + i, 0))),
        compiler_params=_cparams(2), name="cmp_attend")(z, kvc, a01)


class _FlashCfg:
    def __init__(self, **kw):
        self.__dict__.update(kw)


def _tile_range(c, qi):
    q_lo = c.q_off + qi * c.tq
    last = (q_lo + c.tq - 1 - c.k_off) // c.tk
    if c.mode == "window":
        first = (q_lo - (WINDOW - 1) - c.k_off) // c.tk
    else:
        first = 0 * qi
    return first, last


def _flash_kernel(*refs, c):
    if c.sel:
        q_ref, k_ref, v_ref, sel_ref, o_ref, qs_scr, m_scr, l_scr, acc_scr = refs
    else:
        q_ref, k_ref, v_ref, o_ref, qs_scr, m_scr, l_scr, acc_scr = refs
        sel_ref = None
    qi, j = pl.program_id(1), pl.program_id(2)
    tq, tk, G, hg, dk, dv = c.tq, c.tk, c.G, c.hg, c.dk, c.dv

    @pl.when(j == 0)
    def _():
        m_scr[...] = jnp.full(m_scr.shape, NEG, F32)
        l_scr[...] = jnp.zeros(l_scr.shape, F32)
        acc_scr[...] = jnp.zeros(acc_scr.shape, F32)
        for g in range(G):
            qs = jnp.concatenate([q_ref[:, (g * hg + h) * dk:(g * hg + h + 1) * dk] for h in range(hg)], axis=0)
            qs_scr[g] = (qs * dk ** -0.5).astype(BF16)

    first, last = _tile_range(c, qi)
    jabs = first + j

    @pl.when((jabs >= 0) & (jabs <= jnp.minimum(last, c.nk - 1)))
    def _():
        pos_q = c.q_off + qi * tq + lax.broadcasted_iota(jnp.int32, (tq, 1), 0)
        tok = jabs * tk + lax.broadcasted_iota(jnp.int32, (1, tk), 1)
        dpos = pos_q - (c.k_off + tok)
        mask = dpos >= 0
        if c.mode == "window":
            mask = mask & (dpos < WINDOW)
        if c.sel:
            jb = lax.broadcasted_iota(jnp.int32, (c.nbp, 1), 0)
            e01 = jnp.where(jb == lax.shift_right_logical(tok, int(math.log2(SLC_LEN))), 1.0, 0.0).astype(BF16)
        for g in range(G):
            kg = k_ref[:, g * dk:(g + 1) * dk].astype(BF16)
            vi = g // c.g_per_v
            vg = v_ref[:, vi * dv:(vi + 1) * dv].astype(BF16)
            s = lax.dot_general(qs_scr[g], kg, NT, preferred_element_type=F32).reshape(hg, tq, tk)
            mk = mask
            if c.sel:
                st = jnp.dot(sel_ref[:, g * c.nbp:(g + 1) * c.nbp], e01, preferred_element_type=F32)
                mk = mk & (st > 0.5)
            mk = mk[None]
            s = jnp.where(mk, s, NEG)
            m_old = m_scr[g]
            m_new = jnp.maximum(m_old, jnp.max(s, axis=-1, keepdims=True))
            alpha = jnp.exp(m_old - m_new)
            p = jnp.where(mk, jnp.exp(s - m_new), 0.0)
            l_scr[g] = alpha * l_scr[g] + jnp.sum(p, axis=-1, keepdims=True)
            pv = jnp.dot(p.reshape(hg * tq, tk).astype(BF16), vg, preferred_element_type=F32)
            acc_scr[g] = alpha * acc_scr[g] + pv.reshape(hg, tq, dv)
            m_scr[g] = m_new

    @pl.when(j == pl.num_programs(2) - 1)
    def _():
        for g in range(G):
            l = l_scr[g]
            o = acc_scr[g] / jnp.where(l > 0.0, l, 1.0)
            for h in range(hg):
                o_ref[:, (g * hg + h) * dv:(g * hg + h + 1) * dv] = o[h].astype(o_ref.dtype)


def flash(q2d, q_blk, k2d, k_blk, v2d, v_blk, *, nb, Tq, Tk, tq, tk, G, hg, dk, dv, g_per_v, mode,
          q_off, k_off, sel=None, nbp=0):
    nq, nk = Tq // tq, Tk // tk
    assert Tq % tq == 0 and Tk % tk == 0
    c = _FlashCfg(tq=tq, tk=tk, G=G, hg=hg, dk=dk, dv=dv, g_per_v=g_per_v, mode=mode, q_off=q_off, k_off=k_off,
                  nk=nk, sel=sel is not None, nbp=nbp)
    steps = max(min(_tile_range(c, i)[1], nk - 1) - _tile_range(c, i)[0] + 1 for i in range(nq))
    qw, kw, vw = G * hg * dk, G * dk, (G // g_per_v) * dv

    def kv_map(blk):
        def f(b, i, j):
            first, last = _tile_range(c, i)
            return (b * nk + jnp.clip(first + j, 0, jnp.minimum(last, nk - 1)), blk)
        return f

    in_specs = [pl.BlockSpec((tq, qw), lambda b, i, j: (b * nq + i, q_blk)),
                pl.BlockSpec((tk, kw), kv_map(k_blk)),
                pl.BlockSpec((tk, vw), kv_map(v_blk))]
    args = [q2d, k2d, v2d]
    if sel is not None:
        in_specs.append(pl.BlockSpec((tq, G * nbp), lambda b, i, j: (b * nq + i, 0)))
        args.append(sel)
    return pl.pallas_call(
        functools.partial(_flash_kernel, c=c),
        out_shape=jax.ShapeDtypeStruct((nb * Tq, G * hg * dv), F32),
        grid=(nb, nq, steps),
        in_specs=in_specs,
        out_specs=pl.BlockSpec((tq, G * hg * dv), lambda b, i, j: (b * nq + i, 0)),
        scratch_shapes=[pltpu.VMEM((G, hg * tq, dk), BF16), pltpu.VMEM((G, hg, tq, 1), F32),
                        pltpu.VMEM((G, hg, tq, 1), F32), pltpu.VMEM((G, hg, tq, dv), F32)],
        compiler_params=_cparams(3), name="flash_" + mode + ("_sel" if sel is not None else ""))(*args)


def _seg_mask(C):
    nblk = C // SUB
    nseg = max(SUB * nblk * (nblk - 1) // 2, SUB)
    pm = np.zeros((C, nseg), np.float32)
    for i in range(1, nblk):
        off = SUB * i * (i - 1) // 2
        pm[SUB * i:SUB * (i + 1), off:off + SUB * i] = 1.0
    return pm


def _rec_kernel(*refs, mode, C, H, pos_off, has_s0, layer):
    refs = list(refs)
    a_ref, b_ref, v_ref, gate_ref = refs[:4]
    refs = refs[4:]
    aux_ref = refs.pop(0)
    pm_ref = refs.pop(0)
    s0_ref = refs.pop(0) if has_s0 else None
    o_ref, st_ref, st_scr = refs
    c_id = pl.program_id(1)
    K = 128
    nblk = C // SUB

    @pl.when(c_id == 0)
    def _():
        if has_s0:
            st_scr[...] = s0_ref[0]
        else:
            st_scr[...] = jnp.zeros(st_scr.shape, F32)

    a_all, b_all, v_all, gate_all = a_ref[...], b_ref[...], v_ref[...], gate_ref[...]
    rows = lax.broadcasted_iota(jnp.int32, (C, 1), 0)
    if mode == "ret":
        pos = (pos_off + c_id * C + rows).astype(F32)
        ang = pos * aux_ref[0:1, :]
        cos, sin_s = jnp.cos(ang), jnp.sin(ang) * aux_ref[1:2, :]

        def rope(x):
            return x * cos + pltpu.roll(x, K // 2, 1) * sin_s

        qs = [rope(a_all[:, h * K:(h + 1) * K]) for h in range(H)]
        ks = [rope(b_all[:, h * K:(h + 1) * K]) * B_DK ** -0.5 for h in range(H)]
        gs = [jnp.full((C, K), math.log1p(-2.0 ** (-5.0 - h)), F32) for h in range(H)]
    else:
        x = aux_ref[...]
        ex = jnp.exp(x - jnp.max(x, axis=0, keepdims=True))
        sm = ex / jnp.sum(ex, axis=0, keepdims=True)
        lb_all = jnp.zeros((1, H * K), F32)
        for i in range(1, layer + 1):
            lb_all = lb_all + sm[i:i + 1]
        qs, ks, gs = [], [], []
        for h in range(H):
            ah, zf, lb = a_all[:, h * K:(h + 1) * K], b_all[:, h * K:(h + 1) * K], lb_all[:, h * K:(h + 1) * K]
            f = lb + (1.0 - lb) * _sigmoid(zf)
            qs.append(ah * _sigmoid(ah))
            ks.append(1.0 - f)
            gs.append(jnp.log(jnp.maximum(f, F_FLOOR)))

    ltri = jnp.where(rows >= lax.broadcasted_iota(jnp.int32, (1, C), 1), 1.0, 0.0).astype(BF16)
    bs_all = _dot_01x(ltri, jnp.concatenate(gs, axis=-1))
    srow = lax.broadcasted_iota(jnp.int32, (SUB, 1), 0)

    for h in range(H):
        q, k, bsum = qs[h], ks[h], bs_all[:, h * K:(h + 1) * K]
        v = v_all[:, h * K:(h + 1) * K]
        st = st_scr[h]
        o = lax.dot_general((q * jnp.exp(bsum)).astype(BF16), st.astype(BF16), NT, preferred_element_type=F32)
        if nblk > 1:
            rho = jnp.concatenate(
                [jnp.zeros((SUB, K), F32)] +
                [jnp.broadcast_to(bsum[SUB * i - 1:SUB * i], (SUB, K)) for i in range(1, nblk)], axis=0)
            qt = q * jnp.exp(bsum - rho)
            kt = jnp.concatenate(
                [k[:SUB * i] * jnp.exp(bsum[SUB * i - 1:SUB * i] - bsum[:SUB * i]) for i in range(1, nblk)], axis=0)
            vcat = jnp.concatenate([v[:SUB * i] for i in range(1, nblk)], axis=0)
            p = _dot3_nt(qt, kt) * pm_ref[...]
            o += jnp.dot(p.astype(BF16), vcat.astype(BF16), preferred_element_type=F32)
        diag = []
        for i in range(nblk):
            sl = slice(SUB * i, SUB * (i + 1))
            q8, k8, b8, v8 = q[sl], k[sl], bsum[sl], v[sl]
            od = jnp.zeros((SUB, K), F32)
            for s in range(SUB):
                causal = srow >= s
                dec = jnp.exp(jnp.where(causal, b8 - b8[s:s + 1], 0.0))
                att = jnp.sum(q8 * k8[s:s + 1] * dec, axis=-1, keepdims=True)
                od += jnp.where(causal, att, 0.0) * v8[s:s + 1]
            diag.append(od)
        o += jnp.concatenate(diag, axis=0) if nblk > 1 else diag[0]
        bend = bsum[C - 1:C]
        kst = k * jnp.exp(bend - bsum)
        st_scr[h] = st * jnp.exp(bend) + lax.dot_general(v.astype(BF16), kst.astype(BF16), TN,
                                                         preferred_element_type=F32)
        gate = gate_all[:, h * K:(h + 1) * K]
        o = o * lax.rsqrt(jnp.mean(o * o, axis=-1, keepdims=True) + EPS) * (gate * _sigmoid(gate))
        o_ref[:, h * K:(h + 1) * K] = o.astype(o_ref.dtype)

    @pl.when(c_id == pl.num_programs(1) - 1)
    def _():
        st_ref[0] = st_scr[...]


def recurrence(z, blks, aux, *, mode, nb, T, pos_off, s0t=None, layer=0):
    H, K = 4, 128
    C = CHUNK if T % CHUNK == 0 else T
    nc = T // C
    pm = _seg_mask(C)
    nseg = pm.shape[1]
    row = lambda blk: (lambda b, c: (b * nc + c, blk))
    in_specs = [pl.BlockSpec((C, H * K), row(blks[0])), pl.BlockSpec((C, H * K), row(blks[1])),
                pl.BlockSpec((C, H * K), row(blks[2])), pl.BlockSpec((C, H * K), row(blks[3])),
                pl.BlockSpec(aux.shape, lambda b, c: (0, 0)),
                pl.BlockSpec((C, nseg), lambda b, c: (0, 0))]
    args = [z, z, z, z, aux, jnp.asarray(pm)]
    if s0t is not None:
        in_specs.append(pl.BlockSpec((1, H, K, K), lambda b, c: (b, 0, 0, 0)))
        args.append(s0t)
    return pl.pallas_call(
        functools.partial(_rec_kernel, mode=mode, C=C, H=H, pos_off=pos_off, has_s0=s0t is not None, layer=layer),
        out_shape=(jax.ShapeDtypeStruct((nb * T, H * K), BF16), jax.ShapeDtypeStruct((nb, H, K, K), F32)),
        grid=(nb, nc),
        in_specs=in_specs,
        out_specs=(pl.BlockSpec((C, H * K), lambda b, c: (b * nc + c, 0)),
                   pl.BlockSpec((1, H, K, K), lambda b, c: (b, 0, 0, 0))),
        scratch_shapes=[pltpu.VMEM((H, K, K), F32)],
        compiler_params=_cparams(2), name="recurrence_" + mode)(*args)


def _gather_kernel(pt_ref, cache_ref, *refs, n_pages):
    o_ref = refs[-1]
    p = pl.program_id(1)

    @pl.when(p < n_pages)
    def _():
        o_ref[0, 0] = cache_ref[0, 0]

    if len(refs) == 2:
        @pl.when(p >= n_pages)
        def _():
            o_ref[0, 0] = refs[0][0]


def gather_pages(cache, layer, page_table, tail=None):
    nb, n_pages = page_table.shape
    _, _, ps, C = cache.shape
    n_out = n_pages + (tail is not None)
    in_specs = [pl.BlockSpec((1, 1, ps, C), lambda b, p, pt: (layer, pt[b * n_pages + jnp.minimum(p, n_pages - 1)], 0, 0))]
    args = [page_table.reshape(-1), cache]
    if tail is not None:
        in_specs.append(pl.BlockSpec((1, ps, C), lambda b, p, pt: (b, 0, 0)))
        args.append(tail)
    grid_spec = pltpu.PrefetchScalarGridSpec(
        num_scalar_prefetch=1, grid=(nb, n_out), in_specs=in_specs,
        out_specs=pl.BlockSpec((1, 1, ps, C), lambda b, p, pt: (b, p, 0, 0)))
    return pl.pallas_call(
        functools.partial(_gather_kernel, n_pages=n_pages),
        out_shape=jax.ShapeDtypeStruct((nb, n_out, ps, C), F32),
        grid_spec=grid_spec,
        compiler_params=_cparams(2), name="gather_pages")(*args)


def _page_specs(shape, layer, n_pages, per_step):
    def spec(i):
        return pl.BlockSpec((1, 1) + shape, lambda b, j, pt: (layer, pt[b * n_pages + j * per_step + i], 0, 0))
    return [spec(i) for i in range(per_step)]


def _online_update(s, mk, m_ref, l_ref, acc_ref, pv_fn):
    hg, rows, n = s.shape
    if mk is not None:
        s = jnp.where(mk, s, NEG)
    m_old = m_ref[...]
    m_new = jnp.maximum(m_old, jnp.max(s, axis=-1, keepdims=True))
    alpha = jnp.exp(m_old - m_new)
    p = jnp.exp(s - m_new)
    if mk is not None:
        p = jnp.where(mk, p, 0.0)
    l_ref[...] = alpha * l_ref[...] + jnp.sum(p, axis=-1, keepdims=True)
    pv = pv_fn(p.reshape(hg * rows, n).astype(BF16))
    acc_ref[...] = alpha * acc_ref[...] + pv.reshape(hg, rows, pv.shape[-1])
    m_ref[...] = m_new


def _diff_decode_kernel(pt_ref, q_ref, new_ref, *refs, per_step, H, dh):
    pages = refs[:per_step]
    o_ref, qbd_scr, m_scr, l_scr, acc_scr = refs[per_step:]
    j = pl.program_id(1)
    Ts = q_ref.shape[0]
    dv = 2 * dh
    rows_pp = 2 * H

    @pl.when(j == 0)
    def _():
        m_scr[...] = jnp.full(m_scr.shape, NEG, F32)
        l_scr[...] = jnp.zeros(l_scr.shape, F32)
        acc_scr[...] = jnp.zeros(acc_scr.shape, F32)
        lane = lax.broadcasted_iota(jnp.int32, (Ts, dv), 1)
        for h in range(H):
            q = q_ref[:, h * dv:(h + 1) * dv] * dh ** -0.5
            qbd_scr[h] = jnp.concatenate([jnp.where(lane < dh, q, 0.0), jnp.where(lane >= dh, q, 0.0)],
                                         axis=0).astype(BF16)

    for h in range(H):
        k = jnp.concatenate([pg[0, 0, pl.ds(h, PAGE_SIZE, stride=rows_pp), :] for pg in pages], axis=0).astype(BF16)
        v = jnp.concatenate([pg[0, 0, pl.ds(H + h, PAGE_SIZE, stride=rows_pp), :] for pg in pages], axis=0).astype(BF16)
        s = lax.dot_general(qbd_scr[h], k, NT, preferred_element_type=F32)[None]
        _online_update(s, None, m_scr.at[h], l_scr.at[h], acc_scr.at[h],
                       lambda p: jnp.dot(p, v, preferred_element_type=F32))

    @pl.when(j == pl.num_programs(1) - 1)
    def _():
        r = lax.broadcasted_iota(jnp.int32, (2 * Ts, Ts), 0)
        cidx = lax.broadcasted_iota(jnp.int32, (2 * Ts, Ts), 1)
        causal = (cidx <= jnp.where(r >= Ts, r - Ts, r))[None]
        for h in range(H):
            kn = new_ref[:, h * dv:(h + 1) * dv].astype(BF16)
            vn = new_ref[:, (H + h) * dv:(H + h + 1) * dv].astype(BF16)
            s = lax.dot_general(qbd_scr[h], kn, NT, preferred_element_type=F32)[None]
            _online_update(s, causal, m_scr.at[h], l_scr.at[h], acc_scr.at[h],
                           lambda p: jnp.dot(p, vn, preferred_element_type=F32))
            l = l_scr[h]
            o = acc_scr[h] / jnp.where(l > 0.0, l, 1.0)
            o_ref[:, (2 * h) * dv:(2 * h + 1) * dv] = o[0, :Ts]
            o_ref[:, (2 * h + 1) * dv:(2 * h + 2) * dv] = o[0, Ts:]


def diff_decode(z, cache_rows, layer, page_table, nbs, Ts):
    H, dh = C_HEADS, C_HEAD_DIM
    cw = H * 2 * dh
    n_pages = page_table.shape[1]
    per_step = _pick(n_pages, (8, 4, 2, 1))
    in_specs = [pl.BlockSpec((Ts, cw), lambda b, j, pt: (b, 0)),
                pl.BlockSpec((Ts, 2 * cw), lambda b, j, pt: (b, 0))]
    in_specs += _page_specs(cache_rows.shape[2:], layer, n_pages, per_step)
    grid_spec = pltpu.PrefetchScalarGridSpec(
        num_scalar_prefetch=1, grid=(nbs, n_pages // per_step), in_specs=in_specs,
        out_specs=pl.BlockSpec((Ts, 2 * cw), lambda b, j, pt: (b, 0)),
        scratch_shapes=[pltpu.VMEM((H, 2 * Ts, 2 * dh), BF16), pltpu.VMEM((H, 1, 2 * Ts, 1), F32),
                        pltpu.VMEM((H, 1, 2 * Ts, 1), F32), pltpu.VMEM((H, 1, 2 * Ts, 2 * dh), F32)])
    kv_new = z[:, cw:3 * cw]
    return pl.pallas_call(
        functools.partial(_diff_decode_kernel, per_step=per_step, H=H, dh=dh),
        out_shape=jax.ShapeDtypeStruct((nbs * Ts, 2 * cw), F32),
        grid_spec=grid_spec, compiler_params=_cparams(2),
        name="diff_decode")(page_table.reshape(-1), z, kv_new, *([cache_rows] * per_step))


def _slc_decode_kernel(pt_ref, q_ref, new_ref, sel_ref, *refs, per_step, nbp, past):
    pages = refs[:per_step]
    o_ref, qs_scr, m_scr, l_scr, acc_scr = refs[per_step:]
    G, hg, d = A_GROUPS, A_HEADS // A_GROUPS, A_HEAD_DIM
    j = pl.program_id(1)
    Ts = q_ref.shape[0]
    ntok = per_step * PAGE_SIZE

    @pl.when(j == 0)
    def _():
        m_scr[...] = jnp.full(m_scr.shape, NEG, F32)
        l_scr[...] = jnp.zeros(l_scr.shape, F32)
        acc_scr[...] = jnp.zeros(acc_scr.shape, F32)
        for g in range(G):
            qs = jnp.concatenate([q_ref[:, (g * hg + h) * d:(g * hg + h + 1) * d] for h in range(hg)], axis=0)
            qs_scr[g] = (qs * d ** -0.5).astype(BF16)

    tok = j * ntok + lax.broadcasted_iota(jnp.int32, (1, ntok), 1)
    jb = lax.broadcasted_iota(jnp.int32, (nbp, 1), 0)
    e01 = jnp.where(jb == lax.shift_right_logical(tok, int(math.log2(SLC_LEN))), 1.0, 0.0).astype(BF16)
    for g in range(G):
        kt = jnp.concatenate([pg[0, 0, g * d:(g + 1) * d, :] for pg in pages], axis=1).astype(BF16)
        vt = jnp.concatenate([pg[0, 0, (G + g) * d:(G + g + 1) * d, :] for pg in pages], axis=1).astype(BF16)
        s = jnp.dot(qs_scr[g], kt, preferred_element_type=F32).reshape(hg, Ts, ntok)
        st = jnp.dot(sel_ref[:, g * nbp:(g + 1) * nbp], e01, preferred_element_type=F32)
        _online_update(s, (st > 0.5)[None], m_scr.at[g], l_scr.at[g], acc_scr.at[g],
                       lambda p: lax.dot_general(p, vt, NT, preferred_element_type=F32))

    @pl.when(j == pl.num_programs(1) - 1)
    def _():
        causal = (lax.broadcasted_iota(jnp.int32, (Ts, Ts), 1) <= lax.broadcasted_iota(jnp.int32, (Ts, Ts), 0))[None]
        for g in range(G):
            kn = new_ref[:, g * d:(g + 1) * d].astype(BF16)
            vn = new_ref[:, (G + g) * d:(G + g + 1) * d].astype(BF16)
            s = lax.dot_general(qs_scr[g], kn, NT, preferred_element_type=F32).reshape(hg, Ts, Ts)
            _online_update(s, causal, m_scr.at[g], l_scr.at[g], acc_scr.at[g],
                           lambda p: jnp.dot(p, vn, preferred_element_type=F32))
            l = l_scr[g]
            o = acc_scr[g] / jnp.where(l > 0.0, l, 1.0)
            for h in range(hg):
                o_ref[:, (g * hg + h) * d:(g * hg + h + 1) * d] = o[h]


def slc_decode(z, cache_t, layer, page_table, sel, nbs, Ts, nbp, past):
    G, d = A_GROUPS, A_HEAD_DIM
    aq = A_HEADS * d
    n_pages = page_table.shape[1]
    per_step = _pick(n_pages, (16, 8, 4, 2, 1))
    assert past % SLC_LEN == 0 and Ts <= SLC_LEN
    in_specs = [pl.BlockSpec((Ts, aq), lambda b, j, pt: (b, EV_Q)),
                pl.BlockSpec((Ts, 2 * G * d), lambda b, j, pt: (b, EV_SLC // (2 * G * d))),
                pl.BlockSpec((Ts, G * nbp), lambda b, j, pt: (b, 0))]
    in_specs += _page_specs(cache_t.shape[2:], layer, n_pages, per_step)
    hg = A_HEADS // G
    grid_spec = pltpu.PrefetchScalarGridSpec(
        num_scalar_prefetch=1, grid=(nbs, n_pages // per_step), in_specs=in_specs,
        out_specs=pl.BlockSpec((Ts, aq), lambda b, j, pt: (b, 0)),
        scratch_shapes=[pltpu.VMEM((G, hg * Ts, d), BF16), pltpu.VMEM((G, hg, Ts, 1), F32),
                        pltpu.VMEM((G, hg, Ts, 1), F32), pltpu.VMEM((G, hg, Ts, d), F32)])
    return pl.pallas_call(
        functools.partial(_slc_decode_kernel, per_step=per_step, nbp=nbp, past=past),
        out_shape=jax.ShapeDtypeStruct((nbs * Ts, aq), F32),
        grid_spec=grid_spec, compiler_params=_cparams(2),
        name="slc_decode")(page_table.reshape(-1), z, z, sel, *([cache_t] * per_step))


def _compress_paged_kernel(pt_ref, w_ref, pe_ref, *refs, per_step):
    pages = refs[:per_step]
    o_ref, ab_scr = refs[per_step:]
    j = pl.program_id(1)
    cpp = PAGE_SIZE // CMP_STRIDE
    r = lax.broadcasted_iota(jnp.int32, (PAGE_SIZE, PAGE_SIZE), 0)
    t = lax.broadcasted_iota(jnp.int32, (PAGE_SIZE, PAGE_SIZE), 1)
    assert cpp & (cpp - 1) == 0
    perm = jnp.where(t == (r & (cpp - 1)) * CMP_STRIDE + lax.shift_right_logical(r, int(math.log2(cpp))),
                     1.0, 0.0).astype(BF16)
    xs = [lax.dot_general(perm, pg[0, 0].astype(BF16), NT, preferred_element_type=F32) for pg in pages]
    C = xs[0].shape[1]
    ab = jnp.zeros((per_step * cpp, w_ref.shape[1]), F32)
    for l in range(CMP_STRIDE):
        xl = jnp.concatenate([x[l * cpp:(l + 1) * cpp] for x in xs], axis=0).astype(BF16)
        ab += jnp.dot(xl, w_ref[l * C:(l + 1) * C, :], preferred_element_type=F32)
    n = per_step * cpp
    ab_scr[pl.ds(pl.multiple_of(j * n, n), n), :] = ab

    @pl.when(j == pl.num_programs(1) - 1)
    def _():
        cab = jnp.dot(pe_ref[...].astype(BF16), w_ref[...], preferred_element_type=F32)
        half = ab_scr.shape[1] // 2
        const = cab[0:1, :half] + cab[1:2, half:]
        a, b = ab_scr[:, :half], ab_scr[:, half:]
        o_ref[0] = a + pltpu.roll(b, a.shape[0] - 1, 0) + const


def compress_paged(cache_t, layer, page_table, wbig, pe2):
    nbs, n_pages = page_table.shape
    per_step = _pick(n_pages, (16, 8, 4, 2, 1))
    cpp = PAGE_SIZE // CMP_STRIDE
    n_chunk = n_pages * cpp
    cout = wbig.shape[1] // 2
    in_specs = [pl.BlockSpec(wbig.shape, lambda b, j, pt: (0, 0)), pl.BlockSpec(pe2.shape, lambda b, j, pt: (0, 0))]
    in_specs += _page_specs(cache_t.shape[2:], layer, n_pages, per_step)
    grid_spec = pltpu.PrefetchScalarGridSpec(
        num_scalar_prefetch=1, grid=(nbs, n_pages // per_step), in_specs=in_specs,
        out_specs=pl.BlockSpec((1, n_chunk, cout), lambda b, j, pt: (b, 0, 0)),
        scratch_shapes=[pltpu.VMEM((n_chunk, 2 * cout), F32)])
    return pl.pallas_call(
        functools.partial(_compress_paged_kernel, per_step=per_step),
        out_shape=jax.ShapeDtypeStruct((nbs, n_chunk, cout), F32),
        grid_spec=grid_spec, compiler_params=_cparams(2),
        name="compress_paged")(page_table.reshape(-1), wbig, pe2, *([cache_t] * per_step))


def _even_w_in(w):
    aq, akv = A_HEADS * A_HEAD_DIM, A_GROUPS * A_HEAD_DIM
    splits = np.cumsum([aq] + [akv] * 6 + [3 * A_HEADS] + [512] * 4)[:-1]
    q, kc, vc, ks, vs, kw, vw, gt, rq, rk, rv, rg = jnp.split(w, [int(s) for s in splits], axis=1)
    gt = jnp.pad(gt, ((0, 0), (0, 128 - gt.shape[1])))
    return jnp.concatenate([q, rq, rk, rv, rg, kc, vc, ks, vs, kw, vw, gt], axis=1).astype(BF16)


EV_Q, EV_RQ, EV_RK, EV_RV, EV_RG = 0, 1, 2, 3, 4
EV_KS, EV_VS, EV_KW, EV_VW, EV_GT = 22, 23, 24, 25, 26
EV_CMP, EV_SLC, EV_WIN = 2560, 2816, 3072


def _compress_w(w_cmp, pe):
    G, d = A_GROUPS, A_HEAD_DIM
    r = CMP_LEN // CMP_STRIDE
    assert r == 2
    w = w_cmp.reshape(2, r, CMP_STRIDE, d, d)
    eye_kv = jnp.eye(2, dtype=F32)
    eye_g = jnp.eye(G, dtype=F32)
    big = jnp.einsum("khlde,kq,gp->lqpdhkge", w, eye_kv, eye_g)
    big = big.reshape(CMP_STRIDE * 2 * G * d, r * 2 * G * d).astype(BF16)
    pe_r = pe.reshape(2, r, CMP_STRIDE, d)
    rows = jnp.broadcast_to(pe_r.transpose(1, 2, 0, 3)[:, :, :, None, :], (r, CMP_STRIDE, 2, G, d))
    rows = rows.reshape(r, CMP_STRIDE * 2 * G * d)
    return big, jnp.pad(rows, ((0, 8 - r), (0, 0)))


def _slc_sum_matrix(n_rows, n_cmp, nbp):
    a = np.zeros((n_rows, nbp), np.float32)
    per, left = SLC_LEN // CMP_STRIDE, CMP_LEN // CMP_STRIDE - 1
    for j in range(nbp):
        for n in range(per * j - left, per * j + per):
            if 0 <= n < n_cmp:
                a[n, j] = 1.0
    return jnp.asarray(a, BF16)


def _gate_expand():
    e = np.zeros((128, 3 * A_HEADS * A_HEAD_DIM), np.float32)
    for h in range(A_HEADS):
        for i in range(3):
            e[3 * h + i, i * A_HEADS * A_HEAD_DIM + h * A_HEAD_DIM:i * A_HEADS * A_HEAD_DIM + (h + 1) * A_HEAD_DIM] = 1.0
    return jnp.asarray(e, BF16)


def _rope_aux():
    half = B_DK // 2
    freqs = ROPE_BASE ** (-jnp.arange(half, dtype=F32) / half)
    sign = jnp.concatenate([-jnp.ones((half,), F32), jnp.ones((half,), F32)])
    return jnp.stack([jnp.concatenate([freqs, freqs]), sign])


def _slc_shapes(n_slc, n_cmp_pad):
    nbp = -(-n_slc // 128) * 128
    return nbp, _slc_sum_matrix(n_cmp_pad, n_cmp_pad - 1, nbp)


def _nsa_common(nb, T, q_off):
    return dict(nb=nb, Tq=T, tq=_pick(T, (128, 64, 32, 16, 8)), G=A_GROUPS, hg=A_HEADS // A_GROUPS, dk=A_HEAD_DIM,
                dv=A_HEAD_DIM, g_per_v=1, q_off=q_off)


def _even_layer(xp, xs, p, a, dims, caches):
    B, T, nbs, Ts, past = dims
    g = p["norm_g"]
    w_in = _even_w_in(p["w_in_a"][a])
    wbig, pe2 = _compress_w(p["cmp_w"][a], p["cmp_pos"][a])
    w_out = p["w_out_a"][a].astype(BF16)
    aq = A_HEADS * A_HEAD_DIM
    wa, wb = w_out[:aq], w_out[aq:]
    e01 = _gate_expand()
    aux = _rope_aux()
    akv2 = 2 * A_GROUPS * A_HEAD_DIM

    zp = rms_matmul(xp, g[0:1], w_in, F32)
    kv_cmp_p = zp[:, EV_CMP:EV_CMP + akv2]
    kv_slc_p = zp[:, EV_SLC:EV_SLC + akv2]
    kv_win_p = zp[:, EV_WIN:EV_WIN + akv2]
    kvc_p = compress(kv_cmp_p.reshape(B, T // CMP_STRIDE, CMP_STRIDE * akv2), wbig, pe2)
    tk = _pick(T, (512, 256, 128))
    tw = _pick(T, (128,))
    n_slc = -(-T // SLC_LEN)
    nbp, a01 = _slc_shapes(n_slc, kvc_p.shape[1])
    o_cmp, sel = cmp_attend(zp, EV_Q, kvc_p, a01, B, T, 0, n_slc)
    o_slc = flash(zp, EV_Q, zp, EV_KS, zp, EV_VS, Tk=T, tk=tk, mode="causal", k_off=0, sel=sel, nbp=nbp,
                  **_nsa_common(B, T, 0))
    o_win = flash(zp, EV_Q, zp, EV_KW, zp, EV_VW, Tk=T, tk=tw, mode="window", k_off=0, **_nsa_common(B, T, 0))
    ob_p, st_p = recurrence(zp, (EV_RQ, EV_RK, EV_RV, EV_RG), aux, mode="ret", nb=B, T=T, pos_off=0)
    xp = even_out(o_cmp, o_slc, o_win, zp, EV_GT, ob_p, e01, wa, wb, g[1:2], xp)

    cache_cmp, cache_slc, win_buf, s0, page_table = caches
    zs = rms_matmul(xs, g[0:1], w_in, F32)
    kv_cmp_s = zs[:, EV_CMP:EV_CMP + akv2]
    kv_slc_s = zs[:, EV_SLC:EV_SLC + akv2]
    kv_win_s = zs[:, EV_WIN:EV_WIN + akv2]
    assert Ts < CMP_STRIDE and past % SLC_LEN == 0
    kvc_s = compress_paged(cache_cmp, a, page_table, wbig, pe2)
    n_slc = -(-(past + Ts) // SLC_LEN)
    nbp, a01 = _slc_shapes(n_slc, kvc_s.shape[1])
    o_cmp, sel = cmp_attend(zs, EV_Q, kvc_s, a01, nbs, Ts, past, n_slc)
    o_slc = slc_decode(zs, cache_slc, a, page_table, sel, nbs, Ts, nbp, past)
    band = jnp.concatenate([win_buf, kv_win_s.reshape(nbs, Ts, akv2)], axis=1)
    n_buf = win_buf.shape[1]
    bt = n_buf + Ts
    o_win = flash(zs, EV_Q, band.reshape(nbs * bt, akv2), 0, band.reshape(nbs * bt, akv2), 1, Tk=bt, tk=bt,
                  mode="window", k_off=past - n_buf, **_nsa_common(nbs, Ts, past))
    ob_s, st_s = recurrence(zs, (EV_RQ, EV_RK, EV_RV, EV_RG), aux, mode="ret", nb=nbs, T=Ts, pos_off=past,
                            s0t=jnp.swapaxes(s0, -1, -2))
    xs = even_out(o_cmp, o_slc, o_win, zs, EV_GT, ob_s, e01, wa, wb, g[1:2], xs)

    shp = lambda x, n, t: x.reshape(n, t, 2, A_GROUPS, A_HEAD_DIM)
    wkeep = min(WINDOW, T)
    outs = (shp(kv_cmp_p, B, T), shp(kv_cmp_s, nbs, Ts), shp(kv_slc_p, B, T), shp(kv_slc_s, nbs, Ts),
            shp(kv_win_p, B, T)[:, T - wkeep:], shp(band[:, Ts:], nbs, n_buf),
            jnp.swapaxes(st_p, -1, -2), jnp.swapaxes(st_s, -1, -2).astype(s0.dtype))
    return xp, xs, outs


def _odd_layer(xp, xs, p, o, layer, dims, caches):
    B, T, nbs, Ts, past = dims
    g = p["norm_g"]
    w_in = p["w_in_c"][o].astype(BF16)
    w_out = p["w_out_c"][o].astype(BF16)
    cw = C_HEADS * 2 * C_HEAD_DIM
    wc, wd = w_out[:cw], w_out[cw:]
    lam_init = 0.8 - 0.6 * math.exp(-0.3 * layer)
    lam_p = p["diff_lam"][o]
    lb_logits = p["hgrn_lb"]
    dcommon = dict(G=2 * C_HEADS, hg=1, dk=C_HEAD_DIM, dv=2 * C_HEAD_DIM, g_per_v=2, mode="causal", k_off=0)

    zp = rms_matmul(xp, g[0:1], w_in, F32)
    tq = _pick(T, (256, 128, 64, 32, 16, 8))
    tk = _pick(T, (512, 256, 128))
    o_diff = flash(zp, 0, zp, 1, zp, 2, nb=B, Tq=T, Tk=T, tq=tq, tk=tk, q_off=0, **dcommon)
    od_p, st_p = recurrence(zp, (3, 4, 5, 6), lb_logits, mode="hgrn", nb=B, T=T, pos_off=0, layer=o)
    xp = odd_out(o_diff, od_p, lam_p, lam_init, wc, wd, g[1:2], xp)

    cache_diff, s0, page_table = caches
    zs = rms_matmul(xs, g[0:1], w_in, F32)
    kv_s = zs[:, cw:3 * cw]
    o_diff = diff_decode(zs, cache_diff, o, page_table, nbs, Ts)
    od_s, st_s = recurrence(zs, (3, 4, 5, 6), lb_logits, mode="hgrn", nb=nbs, T=Ts, pos_off=past,
                            s0t=jnp.swapaxes(s0, -1, -2), layer=o)
    xs = odd_out(o_diff, od_s, lam_p, lam_init, wc, wd, g[1:2], xs)

    shp = lambda x, n, t: x.reshape(n, t, 2, C_HEADS, 2 * C_HEAD_DIM)
    outs = (shp(zp[:, cw:3 * cw], B, T), shp(kv_s, nbs, Ts),
            jnp.swapaxes(st_p, -1, -2), jnp.swapaxes(st_s, -1, -2).astype(s0.dtype))
    return xp, xs, outs


def _tail_layers(xp, xs, p, layer, dims, mem_prompt, cache_mem):
    B, T, nbs, Ts, _ = dims
    g = p["norm_g"][layer]
    D = xp.shape[1]
    w_q = p["w_xq"][layer].astype(BF16)
    w_kv = p["w_xkv"][layer].astype(BF16)
    w_o = p["w_xo"][layer].astype(BF16)
    w_up = p["w_up"][layer].astype(BF16)
    w_down = p["w_down"][layer].astype(BF16)
    n_mem = mem_prompt.shape[1]
    kv_mem = rms_matmul(mem_prompt.reshape(B * n_mem, D), g[0:1], w_kv, F32, norm=False).reshape(B, n_mem, 2 * D)
    qp = rms_matmul(xp, g[2:3], w_q, BF16)
    xp = proj_res(xattn(qp, kv_mem, T), w_o, g[3:4], xp)
    qs = rms_matmul(xs, g[2:3], w_q, BF16)
    xs = proj_res(xattn(qs, cache_mem, Ts, layer=layer), w_o, g[3:4], xs)
    xp = mlp(xp, g[4:5], g[5:6], w_up, w_down)
    xs = mlp(xs, g[4:5], g[5:6], w_up, w_down)
    return xp, xs, kv_mem.reshape(B, n_mem, 2, X_HEADS, D // X_HEADS)


def kernel(x_prompt, x_sample, cache_nsa_cmp_kv, cache_nsa_slc_kv, cache_nsa_win_kv, state_ret, cache_diff_kv, state_hgrn, cache_mem_kv, page_table, mem_prompt, norm_g, w_in_a, cmp_pos, cmp_w, w_out_a, w_in_c, diff_lam, hgrn_lb, w_out_c, w_xq, w_xkv, w_xo, w_up, w_down):
    B, T, D = x_prompt.shape
    nbs, Ts, _ = x_sample.shape
    depth = norm_g.shape[0]
    n_pages = page_table.shape[1]
    past = n_pages * PAGE_SIZE
    dims = (B, T, nbs, Ts, past)
    p = dict(w_in_a=w_in_a, cmp_pos=cmp_pos, cmp_w=cmp_w, w_out_a=w_out_a, w_in_c=w_in_c, diff_lam=diff_lam,
             hgrn_lb=hgrn_lb.astype(F32), w_out_c=w_out_c, w_xq=w_xq, w_xkv=w_xkv, w_xo=w_xo, w_up=w_up, w_down=w_down)
    xp = x_prompt.reshape(B * T, D)
    xs = x_sample.reshape(nbs * Ts, D)
    feat_major = lambda c: c.transpose(0, 1, 3, 4, 5, 2).reshape(c.shape[0], c.shape[1], -1, c.shape[2])
    rows_of = lambda c: c.reshape(c.shape[0], c.shape[1], -1, c.shape[-1])
    c_cmp, c_slc, c_diff = feat_major(cache_nsa_cmp_kv), feat_major(cache_nsa_slc_kv), rows_of(cache_diff_kv)
    mshape = cache_mem_kv.shape
    c_mem = cache_mem_kv.reshape(mshape[:5] + (mshape[5] // 128, 128)).transpose(0, 1, 2, 3, 5, 4, 6)
    c_mem = c_mem.reshape(mshape[0], mshape[1], -1, 128)
    win_all = cache_nsa_win_kv.reshape(cache_nsa_win_kv.shape[0], nbs, cache_nsa_win_kv.shape[2], -1)
    ev, od, mem = [], [], []
    for layer in range(depth):
        pl_ = dict(p, norm_g=norm_g[layer])
        if layer % 2 == 0:
            a = layer // 2
            xp, xs, outs = _even_layer(xp, xs, pl_, a, dims, (c_cmp, c_slc, win_all[a], state_ret[a], page_table))
            ev.append(outs)
        else:
            o = layer // 2
            xp, xs, outs = _odd_layer(xp, xs, pl_, o, layer, dims, (c_diff, state_hgrn[o], page_table))
            od.append(outs)
        xp, xs, kvm = _tail_layers(xp, xs, dict(p, norm_g=norm_g), layer, dims, mem_prompt, c_mem)
        mem.append(kvm)
    stack = lambda lst, i: jnp.stack([t[i] for t in lst])
    return (xp.reshape(B, T, D), xs.reshape(nbs, Ts, D),
            stack(ev, 0), stack(ev, 1), stack(ev, 2), stack(ev, 3), stack(ev, 4), stack(ev, 5), stack(ev, 6), stack(ev, 7),
            stack(od, 0), stack(od, 1), stack(od, 2), stack(od, 3), jnp.stack(mem))
```

```python
import functools
import math

import numpy as np
import jax
import jax.numpy as jnp
from jax import lax
from jax.experimental import pallas as pl
from jax.experimental.pallas import tpu as pltpu

F32 = jnp.float32
BF16 = jnp.bfloat16

EPS = 1e-6
NEG = -1e30
BIG = 1e9
A_HEADS, A_GROUPS, A_HEAD_DIM = 8, 2, 64
CMP_LEN, CMP_STRIDE, SLC_LEN, SLC_TOPK, WINDOW = 32, 16, 64, 16, 512
B_HEADS, B_DK = 4, 128
ROPE_BASE = 10000.0
CHUNK = 64
C_HEADS, C_HEAD_DIM = 4, 64
D_HEADS = 4
F_FLOOR = 1e-6
X_HEADS = 4
PAGE_SIZE = 128
SUB = 8
LANES = 128

VMEM_LIMIT = 56 * 1024 * 1024

NT = (((1,), (1,)), ((), ()))
TN = (((0,), (0,)), ((), ()))


def _cparams(n_grid):
    return pltpu.CompilerParams(dimension_semantics=("arbitrary",) * n_grid, vmem_limit_bytes=VMEM_LIMIT)


def _pick(n, cands):
    for c in cands:
        if n % c == 0:
            return c
    return n


def _split3(x):
    hi = x.astype(BF16)
    r = x - hi.astype(F32)
    mid = r.astype(BF16)
    lo = (r - mid.astype(F32)).astype(BF16)
    return hi, mid, lo


def _dot_x01(x, m01):
    return sum(jnp.dot(p, m01, preferred_element_type=F32) for p in _split3(x))


def _dot_01x(m01, x):
    return sum(jnp.dot(m01, p, preferred_element_type=F32) for p in _split3(x))


def _dot3_nt(a, b):
    ah = a.astype(BF16)
    al = (a - ah.astype(F32)).astype(BF16)
    bh = b.astype(BF16)
    bl = (b - bh.astype(F32)).astype(BF16)
    d = lambda x, y: lax.dot_general(x, y, NT, preferred_element_type=F32)
    return d(ah, bh) + d(ah, bl) + d(al, bh)


def _sigmoid(x):
    return 1.0 / (1.0 + jnp.exp(-x))


def _rms(x, g):
    return x * lax.rsqrt(jnp.mean(x * x, axis=-1, keepdims=True) + EPS) * g


def _rms_matmul_kernel(x_ref, g_ref, w_ref, o_ref, *, norm):
    x = x_ref[...]
    if norm:
        x = _rms(x, g_ref[...])
    o_ref[...] = jnp.dot(x.astype(BF16), w_ref[...], preferred_element_type=F32).astype(o_ref.dtype)


def rms_matmul(x, g, w, out_dtype, norm=True):
    R, D = x.shape
    N = w.shape[1]
    tm = _pick(R, (512, 256, 128, 64, 32, 16, 8))
    tn = _pick(N, (1792, 1152, 1024, 896, 768, 640, 512, 384, 256, 128))
    return pl.pallas_call(
        functools.partial(_rms_matmul_kernel, norm=norm),
        out_shape=jax.ShapeDtypeStruct((R, N), out_dtype),
        grid=(R // tm, N // tn),
        in_specs=[pl.BlockSpec((tm, D), lambda i, j: (i, 0)),
                  pl.BlockSpec((1, D), lambda i, j: (0, 0)),
                  pl.BlockSpec((D, tn), lambda i, j: (0, j))],
        out_specs=pl.BlockSpec((tm, tn), lambda i, j: (i, j)),
        compiler_params=_cparams(2), name="rms_matmul")(x, g, w)


def _in_proj_kernel(x_ref, g_ref, w_ref, wt_ref, z_ref, zb_ref, kt_ref, xn_scr):
    j = pl.program_id(1)

    @pl.when(j == 0)
    def _():
        xn = _rms(x_ref[...], g_ref[...]).astype(BF16)
        xn_scr[...] = xn
        kt_ref[...] = lax.dot_general(wt_ref[...], xn, NT, preferred_element_type=F32).astype(kt_ref.dtype)

    z = jnp.dot(xn_scr[...], w_ref[...], preferred_element_type=F32)
    z_ref[...] = z
    zb_ref[...] = z.astype(BF16)


def in_proj(x, g, w, wt):
    R, D = x.shape
    N = w.shape[1]
    Fk = wt.shape[0]
    tm = _pick(R, (512, 256, 128, 64, 32, 16, 8))
    tn = _pick(N, (1792, 1152, 1024, 896, 768, 640, 512, 384, 256, 128))
    return pl.pallas_call(
        _in_proj_kernel,
        out_shape=(jax.ShapeDtypeStruct((R, N), F32), jax.ShapeDtypeStruct((R, N), BF16),
                   jax.ShapeDtypeStruct((Fk, R), BF16)),
        grid=(R // tm, N // tn),
        in_specs=[pl.BlockSpec((tm, D), lambda i, j: (i, 0)),
                  pl.BlockSpec((1, D), lambda i, j: (0, 0)),
                  pl.BlockSpec((D, tn), lambda i, j: (0, j)),
                  pl.BlockSpec((Fk, D), lambda i, j: (0, 0))],
        out_specs=(pl.BlockSpec((tm, tn), lambda i, j: (i, j)), pl.BlockSpec((tm, tn), lambda i, j: (i, j)),
                   pl.BlockSpec((Fk, tm), lambda i, j: (0, i))),
        scratch_shapes=[pltpu.VMEM((tm, D), BF16)],
        compiler_params=_cparams(2), name="in_proj")(x, g, w, wt)


def _mlp_kernel(x_ref, g4_ref, g5_ref, wu_ref, wd_ref, o_ref, xn_scr, acc_scr):
    j = pl.program_id(1)

    @pl.when(j == 0)
    def _():
        xn_scr[...] = _rms(x_ref[...], g4_ref[...]).astype(BF16)
        acc_scr[...] = jnp.zeros_like(acc_scr)

    h = jnp.dot(xn_scr[...], wu_ref[...], preferred_element_type=F32)
    h = jnp.square(jnp.maximum(h, 0.0))
    acc_scr[...] += jnp.dot(h.astype(BF16), wd_ref[...], preferred_element_type=F32)

    @pl.when(j == pl.num_programs(1) - 1)
    def _():
        o_ref[...] = x_ref[...] + _rms(acc_scr[...], g5_ref[...])


def mlp(x, g4, g5, w_up, w_down):
    R, D = x.shape
    F = w_up.shape[1]
    tm = _pick(R, (512, 256, 128, 64, 32, 16, 8))
    tf = _pick(F, (1024, 512, 256, 128))
    return pl.pallas_call(
        _mlp_kernel,
        out_shape=jax.ShapeDtypeStruct((R, D), F32),
        grid=(R // tm, F // tf),
        in_specs=[pl.BlockSpec((tm, D), lambda i, j: (i, 0)),
                  pl.BlockSpec((1, D), lambda i, j: (0, 0)),
                  pl.BlockSpec((1, D), lambda i, j: (0, 0)),
                  pl.BlockSpec((D, tf), lambda i, j: (0, j)),
                  pl.BlockSpec((tf, D), lambda i, j: (j, 0))],
        out_specs=pl.BlockSpec((tm, D), lambda i, j: (i, 0)),
        scratch_shapes=[pltpu.VMEM((tm, D), BF16), pltpu.VMEM((tm, D), F32)],
        compiler_params=_cparams(2), name="mlp")(x, g4, g5, w_up, w_down)


def _proj_res_kernel(a_ref, w_ref, g_ref, x_ref, o_ref):
    y = jnp.dot(a_ref[...], w_ref[...], preferred_element_type=F32)
    o_ref[...] = x_ref[...] + _rms(y, g_ref[...])


def proj_res(a, w, g, x):
    R, D = x.shape
    K = a.shape[1]
    tm = _pick(R, (512, 256, 128, 64, 32, 16, 8))
    return pl.pallas_call(
        _proj_res_kernel,
        out_shape=jax.ShapeDtypeStruct((R, D), F32),
        grid=(R // tm,),
        in_specs=[pl.BlockSpec((tm, K), lambda i: (i, 0)),
                  pl.BlockSpec((K, D), lambda i: (0, 0)),
                  pl.BlockSpec((1, D), lambda i: (0, 0)),
                  pl.BlockSpec((tm, D), lambda i: (i, 0))],
        out_specs=pl.BlockSpec((tm, D), lambda i: (i, 0)),
        compiler_params=_cparams(1), name="proj_res")(a, w, g, x)


def _even_out_kernel(oc_ref, os_ref, ow_ref, gt_ref, ob_ref, e_ref, wa_ref, wb_ref, g_ref, x_ref, o_ref):
    gates = _sigmoid(gt_ref[...])
    ge = _dot_x01(gates, e_ref[...])
    aq = oc_ref.shape[1]
    oa = ge[:, :aq] * oc_ref[...] + ge[:, aq:2 * aq] * os_ref[...] + ge[:, 2 * aq:] * ow_ref[...]
    y = jnp.dot(oa.astype(BF16), wa_ref[...], preferred_element_type=F32)
    y += jnp.dot(ob_ref[...], wb_ref[...], preferred_element_type=F32)
    o_ref[...] = x_ref[...] + _rms(y, g_ref[...])


def even_out(o_cmp, o_slc, o_win, z, gt_blk, o_b, e01, wa, wb, g, x):
    R, D = x.shape
    aq = o_cmp.shape[1]
    tm = _pick(R, (256, 128, 64, 32, 16, 8))
    row = lambda i: (i, 0)
    fix = lambda i: (0, 0)
    return pl.pallas_call(
        _even_out_kernel,
        out_shape=jax.ShapeDtypeStruct((R, D), F32),
        grid=(R // tm,),
        in_specs=[pl.BlockSpec((tm, aq), row), pl.BlockSpec((tm, aq), row), pl.BlockSpec((tm, aq), row),
                  pl.BlockSpec((tm, 128), lambda i: (i, gt_blk)),
                  pl.BlockSpec((tm, o_b.shape[1]), row),
                  pl.BlockSpec(e01.shape, fix), pl.BlockSpec(wa.shape, fix), pl.BlockSpec(wb.shape, fix),
                  pl.BlockSpec((1, D), fix), pl.BlockSpec((tm, D), row)],
        out_specs=pl.BlockSpec((tm, D), row),
        compiler_params=_cparams(1), name="even_out")(o_cmp, o_slc, o_win, z, o_b, e01, wa, wb, g, x)


def _odd_out_kernel(oc_ref, od_ref, lam_ref, wc_ref, wd_ref, g_ref, x_ref, o_ref, *, lam_init, heads, dv):
    lp = lam_ref[...]
    lam = (jnp.exp(jnp.sum(lp[0:1] * lp[1:2], axis=-1, keepdims=True))
           - jnp.exp(jnp.sum(lp[2:3] * lp[3:4], axis=-1, keepdims=True)) + lam_init)
    parts = []
    for h in range(heads):
        o1 = oc_ref[:, (2 * h) * dv:(2 * h + 1) * dv]
        o2 = oc_ref[:, (2 * h + 1) * dv:(2 * h + 2) * dv]
        o = o1 - lam * o2
        o = o * lax.rsqrt(jnp.mean(o * o, axis=-1, keepdims=True) + EPS) * (1.0 - lam_init)
        parts.append(o.astype(BF16))
    oc = jnp.concatenate(parts, axis=-1)
    y = jnp.dot(oc, wc_ref[...], preferred_element_type=F32)
    y += jnp.dot(od_ref[...], wd_ref[...], preferred_element_type=F32)
    o_ref[...] = x_ref[...] + _rms(y, g_ref[...])


def odd_out(o_diff, o_d, lam_p, lam_init, wc, wd, g, x):
    R, D = x.shape
    tm = _pick(R, (256, 128, 64, 32, 16, 8))
    row = lambda i: (i, 0)
    fix = lambda i: (0, 0)
    return pl.pallas_call(
        functools.partial(_odd_out_kernel, lam_init=lam_init, heads=C_HEADS, dv=2 * C_HEAD_DIM),
        out_shape=jax.ShapeDtypeStruct((R, D), F32),
        grid=(R // tm,),
        in_specs=[pl.BlockSpec((tm, o_diff.shape[1]), row), pl.BlockSpec((tm, o_d.shape[1]), row),
                  pl.BlockSpec(lam_p.shape, fix), pl.BlockSpec(wc.shape, fix), pl.BlockSpec(wd.shape, fix),
                  pl.BlockSpec((1, D), fix), pl.BlockSpec((tm, D), row)],
        out_specs=pl.BlockSpec((tm, D), row),
        compiler_params=_cparams(1), name="odd_out")(o_diff, o_d, lam_p, wc, wd, g, x)


def _xattn_kernel(q_ref, kv_ref, o_ref, *, heads, scale, rows):
    dm = q_ref.shape[1]
    hd = dm // heads
    for h in range(heads):
        qh = q_ref[:, h * hd:(h + 1) * hd]
        if rows:
            nh = hd // 128
            per_tok = 2 * nh * heads
            n_mem = kv_ref.shape[2] // per_tok
            row = lambda slot: jnp.concatenate(
                [kv_ref[0, 0, pl.ds((slot * nh + i) * heads + h, n_mem, stride=per_tok), :] for i in range(nh)],
                axis=-1).astype(BF16)
            kh, vh = row(0), row(1)
        else:
            kh = kv_ref[0, :, h * hd:(h + 1) * hd].astype(BF16)
            vh = kv_ref[0, :, dm + h * hd:dm + (h + 1) * hd].astype(BF16)
        s = lax.dot_general(qh, kh, NT, preferred_element_type=F32) * scale
        e = jnp.exp(s - jnp.max(s, axis=-1, keepdims=True))
        p = e / jnp.sum(e, axis=-1, keepdims=True)
        o = jnp.dot(p.astype(BF16), vh, preferred_element_type=F32)
        o_ref[:, h * hd:(h + 1) * hd] = o.astype(o_ref.dtype)


def xattn(q, kv, rows_per_batch, layer=None):
    R, D = q.shape
    tq = _pick(rows_per_batch, (512, 256, 128, 64, 32, 16, 8))
    per = rows_per_batch // tq
    if layer is None:
        kv_spec = pl.BlockSpec((1,) + kv.shape[1:], lambda i: (i // per, 0, 0))
    else:
        kv_spec = pl.BlockSpec((1, 1) + kv.shape[2:], lambda i: (layer, i // per, 0, 0))
    return pl.pallas_call(
        functools.partial(_xattn_kernel, heads=X_HEADS, scale=(D // X_HEADS) ** -0.5, rows=layer is not None),
        out_shape=jax.ShapeDtypeStruct((R, D), BF16),
        grid=(R // tq,),
        in_specs=[pl.BlockSpec((tq, D), lambda i: (i, 0)), kv_spec],
        out_specs=pl.BlockSpec((tq, D), lambda i: (i, 0)),
        compiler_params=_cparams(1), name="xattn")(q, kv)


def _compress_kernel(x_ref, w_ref, pe_ref, o_ref):
    x = x_ref[0].astype(BF16)
    ab = jnp.dot(x, w_ref[...], preferred_element_type=F32)
    cab = jnp.dot(pe_ref[...].astype(BF16), w_ref[...], preferred_element_type=F32)
    half = ab.shape[1] // 2
    a, b = ab[:, :half], ab[:, half:]
    const = cab[0:1, :half] + cab[1:2, half:]
    n = a.shape[0]
    o_ref[0] = a + pltpu.roll(b, n - 1, 0) + const


def compress(xc, wbig, pe2):
    nb, n_chunk, kdim = xc.shape
    cout = wbig.shape[1] // 2
    return pl.pallas_call(
        _compress_kernel,
        out_shape=jax.ShapeDtypeStruct((nb, n_chunk, cout), F32),
        grid=(nb,),
        in_specs=[pl.BlockSpec((1, n_chunk, kdim), lambda b: (b, 0, 0)),
                  pl.BlockSpec(wbig.shape, lambda b: (0, 0)),
                  pl.BlockSpec(pe2.shape, lambda b: (0, 0))],
        out_specs=pl.BlockSpec((1, n_chunk, cout), lambda b: (b, 0, 0)),
        compiler_params=_cparams(1), name="compress")(xc, wbig, pe2)


def _cmp_kernel(q_ref, kv_ref, a_ref, o_ref, sel_ref, *, tq, q_off, n_slc, nbp):
    G, hg, d = A_GROUPS, A_HEADS // A_GROUPS, A_HEAD_DIM
    qi = pl.program_id(1)
    ncmp = kv_ref.shape[1]
    pos_q = q_off + qi * tq + lax.broadcasted_iota(jnp.int32, (tq, 1), 0)
    cmp_end = lax.broadcasted_iota(jnp.int32, (1, ncmp), 1) * CMP_STRIDE + (CMP_LEN - 1)
    mask3 = (cmp_end <= pos_q)[None]
    jb = lax.broadcasted_iota(jnp.int32, (1, nbp), 1)
    blk_q = lax.shift_right_logical(pos_q, int(math.log2(SLC_LEN)))
    valid = jb <= blk_q
    forced = valid & ((jb == 0) | (jb >= blk_q - 1))
    k_top = min(SLC_TOPK, n_slc)
    scores = []
    for g in range(G):
        kc = kv_ref[0, :, g * d:(g + 1) * d].astype(BF16)
        vc = kv_ref[0, :, (G + g) * d:(G + g + 1) * d].astype(BF16)
        qs = jnp.concatenate([q_ref[:, (g * hg + h) * d:(g * hg + h + 1) * d] for h in range(hg)], axis=0)
        qs = (qs * d ** -0.5).astype(BF16)
        s = lax.dot_general(qs, kc, NT, preferred_element_type=F32).reshape(hg, tq, ncmp)
        s = jnp.where(mask3, s, NEG)
        e = jnp.where(mask3, jnp.exp(s - jnp.max(s, axis=-1, keepdims=True)), 0.0)
        den = jnp.sum(e, axis=-1, keepdims=True)
        p = e / jnp.where(den > 0.0, den, 1.0)
        o = jnp.dot(p.reshape(hg * tq, ncmp).astype(BF16), vc, preferred_element_type=F32)
        for h in range(hg):
            o_ref[:, (g * hg + h) * d:(g * hg + h + 1) * d] = o[h * tq:(h + 1) * tq]
        p_slc = _dot_x01(jnp.sum(p, axis=0), a_ref[...])
        score = jnp.where(forced, BIG, jnp.where(valid, p_slc, -BIG))
        scores.append(jnp.where(jb < n_slc, score, -jnp.inf))
    score = jnp.concatenate(scores, axis=0)
    sel = jnp.zeros((G * tq, nbp), F32)
    for _ in range(k_top):
        mx = jnp.max(score, axis=-1, keepdims=True)
        jm = jnp.min(jnp.where(score == mx, jb, nbp), axis=-1, keepdims=True)
        hit = jb == jm
        sel = jnp.where(hit, 1.0, sel)
        score = jnp.where(hit, -jnp.inf, score)
    for g in range(G):
        ok = valid & (sel[g * tq:(g + 1) * tq] > 0.5)
        sel_ref[:, g * nbp:(g + 1) * nbp] = jnp.where(ok, 0.0, SEL_OFF).astype(BF16)


def cmp_attend(z, q_blk, kvc, a01, nb, Tq, q_off, n_slc):
    aq = A_HEADS * A_HEAD_DIM
    nbp = a01.shape[1]
    tq = _pick(Tq, (256, 128, 64, 32, 16, 8))
    nq = Tq // tq
    ncmp = kvc.shape[1]
    return pl.pallas_call(
        functools.partial(_cmp_kernel, tq=tq, q_off=q_off, n_slc=n_slc, nbp=nbp),
        out_shape=(jax.ShapeDtypeStruct((nb * Tq, aq), F32),
                   jax.ShapeDtypeStruct((nb * Tq, A_GROUPS * nbp), BF16)),
        grid=(nb, nq),
        in_specs=[pl.BlockSpec((tq, aq), lambda b, i: (b * nq + i, q_blk)),
                  pl.BlockSpec((1, ncmp, kvc.shape[2]), lambda b, i: (b, 0, 0)),
                  pl.BlockSpec(a01.shape, lambda b, i: (0, 0))],
        out_specs=(pl.BlockSpec((tq, aq), lambda b, i: (b * nq + i, 0)),
                   pl.BlockSpec((tq, A_GROUPS * nbp), lambda b, i: (b * nq + i, 0))),
        compiler_params=_cparams(2), name="cmp_attend")(z, kvc, a01)


class _FlashCfg:
    def __init__(self, **kw):
        self.__dict__.update(kw)


def _tile_range(c, qi):
    q_lo = c.q_off + qi * c.tq
    last = (q_lo + c.tq - 1 - c.k_off) // c.tk
    if c.mode == "window":
        first = (q_lo - (WINDOW - 1) - c.k_off) // c.tk
    else:
        first = 0 * qi
    return first, last


def _flash_kernel(*refs, c):
    if c.sel:
        q_ref, k_ref, v_ref, sel_ref, o_ref, qs_scr, m_scr, l_scr, acc_scr = refs
    else:
        q_ref, k_ref, v_ref, o_ref, qs_scr, m_scr, l_scr, acc_scr = refs
        sel_ref = None
    qi, j = pl.program_id(1), pl.program_id(2)
    tq, tk, G, hg, dk, dv = c.tq, c.tk, c.G, c.hg, c.dk, c.dv

    @pl.when(j == 0)
    def _():
        m_scr[...] = jnp.full(m_scr.shape, NEG, F32)
        l_scr[...] = jnp.zeros(l_scr.shape, F32)
        acc_scr[...] = jnp.zeros(acc_scr.shape, F32)
        for g in range(G):
            qs = jnp.concatenate([q_ref[:, (g * hg + h) * dk:(g * hg + h + 1) * dk] for h in range(hg)], axis=0)
            qs_scr[g] = (qs * dk ** -0.5).astype(BF16)

    first, last = _tile_range(c, qi)
    jabs = first + j

    @pl.when((jabs >= 0) & (jabs <= jnp.minimum(last, c.nk - 1)))
    def _():
        pos_q = c.q_off + qi * tq + lax.broadcasted_iota(jnp.int32, (tq, 1), 0)
        tok = jabs * tk + lax.broadcasted_iota(jnp.int32, (1, tk), 1)
        dpos = pos_q - (c.k_off + tok)
        mask = dpos >= 0
        if c.mode == "window":
            mask = mask & (dpos < WINDOW)
        if c.sel:
            jb = lax.broadcasted_iota(jnp.int32, (c.nbp, 1), 0)
            e01 = jnp.where(jb == lax.shift_right_logical(tok, int(math.log2(SLC_LEN))), 1.0, 0.0).astype(BF16)
        for g in range(G):
            kg = k_ref[:, g * dk:(g + 1) * dk].astype(BF16)
            vi = g // c.g_per_v
            vg = v_ref[:, vi * dv:(vi + 1) * dv].astype(BF16)
            s = lax.dot_general(qs_scr[g], kg, NT, preferred_element_type=F32).reshape(hg, tq, tk)
            mk = mask
            if c.sel:
                st = jnp.dot(sel_ref[:, g * c.nbp:(g + 1) * c.nbp], e01, preferred_element_type=F32)
                mk = mk & (st > 0.5)
            mk = mk[None]
            s = jnp.where(mk, s, NEG)
            m_old = m_scr[g]
            m_new = jnp.maximum(m_old, jnp.max(s, axis=-1, keepdims=True))
            alpha = jnp.exp(m_old - m_new)
            p = jnp.where(mk, jnp.exp(s - m_new), 0.0)
            l_scr[g] = alpha * l_scr[g] + jnp.sum(p, axis=-1, keepdims=True)
            pv = jnp.dot(p.reshape(hg * tq, tk).astype(BF16), vg, preferred_element_type=F32)
            acc_scr[g] = alpha * acc_scr[g] + pv.reshape(hg, tq, dv)
            m_scr[g] = m_new

    @pl.when(j == pl.num_programs(2) - 1)
    def _():
        for g in range(G):
            l = l_scr[g]
            o = acc_scr[g] / jnp.where(l > 0.0, l, 1.0)
            for h in range(hg):
                o_ref[:, (g * hg + h) * dv:(g * hg + h + 1) * dv] = o[h].astype(o_ref.dtype)


def flash(q2d, q_blk, k2d, k_blk, v2d, v_blk, *, nb, Tq, Tk, tq, tk, G, hg, dk, dv, g_per_v, mode,
          q_off, k_off, sel=None, nbp=0):
    nq, nk = Tq // tq, Tk // tk
    assert Tq % tq == 0 and Tk % tk == 0
    c = _FlashCfg(tq=tq, tk=tk, G=G, hg=hg, dk=dk, dv=dv, g_per_v=g_per_v, mode=mode, q_off=q_off, k_off=k_off,
                  nk=nk, sel=sel is not None, nbp=nbp)
    steps = max(min(_tile_range(c, i)[1], nk - 1) - _tile_range(c, i)[0] + 1 for i in range(nq))
    qw, kw, vw = G * hg * dk, G * dk, (G // g_per_v) * dv

    def kv_map(blk):
        def f(b, i, j):
            first, last = _tile_range(c, i)
            return (b * nk + jnp.clip(first + j, 0, jnp.minimum(last, nk - 1)), blk)
        return f

    in_specs = [pl.BlockSpec((tq, qw), lambda b, i, j: (b * nq + i, q_blk)),
                pl.BlockSpec((tk, kw), kv_map(k_blk)),
                pl.BlockSpec((tk, vw), kv_map(v_blk))]
    args = [q2d, k2d, v2d]
    if sel is not None:
        in_specs.append(pl.BlockSpec((tq, G * nbp), lambda b, i, j: (b * nq + i, 0)))
        args.append(sel)
    return pl.pallas_call(
        functools.partial(_flash_kernel, c=c),
        out_shape=jax.ShapeDtypeStruct((nb * Tq, G * hg * dv), F32),
        grid=(nb, nq, steps),
        in_specs=in_specs,
        out_specs=pl.BlockSpec((tq, G * hg * dv), lambda b, i, j: (b * nq + i, 0)),
        scratch_shapes=[pltpu.VMEM((G, hg * tq, dk), BF16), pltpu.VMEM((G, hg, tq, 1), F32),
                        pltpu.VMEM((G, hg, tq, 1), F32), pltpu.VMEM((G, hg, tq, dv), F32)],
        compiler_params=_cparams(3), name="flash_" + mode + ("_sel" if sel is not None else ""))(*args)


LOG2E = 1.4426950408889634
SEL_OFF = -2.0 ** 30


def _attn_kernel(tab_ref, q_ref, kt_ref, v_ref, *refs, c):
    if c.sel:
        sel_ref, e01_ref = refs[:2]
        refs = refs[2:]
    o_ref, qs_scr, s_scr, p_scr, m_scr, l_scr, a_scr, acc_scr = refs
    n = pl.program_id(1)
    qi, kj, flags = tab_ref[0, n], tab_ref[1, n], tab_ref[2, n]
    tq, tk, G, hg, dk, dv = c.tq, c.tk, c.G, c.hg, c.dk, c.dv
    rows = hg * tq
    ncb = tk // LANES

    @pl.when((flags & 1) != 0)
    def _():
        m_scr[...] = jnp.full(m_scr.shape, NEG, F32)
        l_scr[...] = jnp.zeros(l_scr.shape, F32)
        acc_scr[...] = jnp.zeros(acc_scr.shape, F32)
        for g in range(G):
            qs = jnp.concatenate([q_ref[:, (g * hg + h) * dk:(g * hg + h + 1) * dk] for h in range(hg)], axis=0)
            qs_scr[g] = (qs * (dk ** -0.5 * LOG2E)).astype(BF16)

    def step(masked):
        if masked:
            dq = qi * tq - kj * tk + c.q_off - c.k_off
            d = (lax.broadcasted_iota(jnp.int32, (tq, tk), 0) - lax.broadcasted_iota(jnp.int32, (tq, tk), 1)) + dq
            ok = d >= 0
            if c.mode == "window":
                ok = ok & (d < WINDOW)
        for g in range(G):
            s = jnp.dot(qs_scr[g], kt_ref[g * dk:(g + 1) * dk, :], preferred_element_type=F32)
            if c.sel or masked:
                s = s.reshape(hg, tq, tk)
                if c.sel:
                    s = s + jnp.dot(sel_ref[:, g * c.nbp:(g + 1) * c.nbp], e01_ref[...], preferred_element_type=F32)[None]
                if masked:
                    s = jnp.where(ok[None], s, NEG)
                s = s.reshape(rows, tk)
            s_scr[g] = s
        for g in range(G):
            cols = [s_scr[g, :, cb * LANES:(cb + 1) * LANES] for cb in range(ncb)]
            mx = cols[0]
            for x in cols[1:]:
                mx = jnp.maximum(mx, x)
            m_old = m_scr[g]
            m_new = jnp.maximum(m_old, jnp.broadcast_to(jnp.max(mx, axis=-1, keepdims=True), (rows, LANES)))
            alpha = jnp.exp2(m_old - m_new)
            psum = None
            for cb in range(ncb):
                p = jnp.exp2(cols[cb] - m_new)
                if masked:
                    okc = jnp.broadcast_to(ok[None, :, cb * LANES:(cb + 1) * LANES], (hg, tq, LANES)).reshape(rows, LANES)
                    p = jnp.where(okc, p, 0.0)
                psum = p if psum is None else psum + p
                p_scr[g, :, cb * LANES:(cb + 1) * LANES] = p.astype(BF16)
            l_scr[g] = alpha * l_scr[g] + psum
            m_scr[g] = m_new
            a_scr[g] = alpha
        for g in range(G):
            vi = g // c.g_per_v
            pv = jnp.dot(p_scr[g], v_ref[:, vi * dv:(vi + 1) * dv], preferred_element_type=F32)
            acc_scr[g] = a_scr[g, :, :dv] * acc_scr[g] + pv

    @pl.when((flags & 4) != 0)
    def _():
        step(True)

    @pl.when((flags & 4) == 0)
    def _():
        step(False)

    @pl.when((flags & 2) != 0)
    def _():
        for g in range(G):
            l = jnp.sum(l_scr[g], axis=-1, keepdims=True)
            o = acc_scr[g] / jnp.where(l > 0.0, l, 1.0)
            for h in range(hg):
                o_ref[:, (g * hg + h) * dv:(g * hg + h + 1) * dv] = o[h * tq:(h + 1) * tq].astype(o_ref.dtype)


def attend(z, q_blk, kt, kt_blk, zb, v_blk, *, nb, T, tq, tk, G, hg, dk, dv, g_per_v, mode, sel=None, e01=None):
    nq, nk = T // tq, T // tk
    c = _FlashCfg(tq=tq, tk=tk, G=G, hg=hg, dk=dk, dv=dv, g_per_v=g_per_v, mode=mode, q_off=0, k_off=0, nk=nk,
                  sel=sel is not None, nbp=0 if sel is None else sel.shape[1] // G)
    tab = []
    for i in range(nq):
        first, last = _tile_range(c, i)
        first, last = max(first, 0), min(last, nk - 1)
        for j in range(first, last + 1):
            lo, hi = i * tq - (j * tk + tk - 1), i * tq + tq - 1 - j * tk
            masked = lo < 0 or (mode == "window" and hi >= WINDOW)
            tab.append((i, j, (j == first) * 1 + (j == last) * 2 + masked * 4))
    tab = jnp.asarray(np.array(tab, np.int32).T)
    npairs = tab.shape[1]
    rows = hg * tq
    in_specs = [pl.BlockSpec((tq, G * hg * dk), lambda b, n, t: (b * nq + t[0, n], q_blk)),
                pl.BlockSpec((G * dk, tk), lambda b, n, t: (kt_blk, b * nk + t[1, n])),
                pl.BlockSpec((tk, (G // g_per_v) * dv), lambda b, n, t: (b * nk + t[1, n], v_blk))]
    args = [z, kt, zb]
    if sel is not None:
        in_specs += [pl.BlockSpec((tq, sel.shape[1]), lambda b, n, t: (b * nq + t[0, n], 0)),
                     pl.BlockSpec((e01.shape[0], tk), lambda b, n, t: (0, t[1, n]))]
        args += [sel, e01]
    grid_spec = pltpu.PrefetchScalarGridSpec(
        num_scalar_prefetch=1, grid=(nb, npairs), in_specs=in_specs,
        out_specs=pl.BlockSpec((tq, G * hg * dv), lambda b, n, t: (b * nq + t[0, n], 0)),
        scratch_shapes=[pltpu.VMEM((G, rows, dk), BF16), pltpu.VMEM((G, rows, tk), F32),
                        pltpu.VMEM((G, rows, tk), BF16), pltpu.VMEM((G, rows, LANES), F32),
                        pltpu.VMEM((G, rows, LANES), F32), pltpu.VMEM((G, rows, LANES), F32),
                        pltpu.VMEM((G, rows, dv), F32)])
    return pl.pallas_call(
        functools.partial(_attn_kernel, c=c),
        out_shape=jax.ShapeDtypeStruct((nb * T, G * hg * dv), F32),
        grid_spec=grid_spec, compiler_params=_cparams(2),
        name="attend_" + mode + ("_sel" if sel is not None else ""))(tab, *args)


def _seg_mask(C):
    nblk = C // SUB
    nseg = max(SUB * nblk * (nblk - 1) // 2, SUB)
    pm = np.zeros((C, nseg), np.float32)
    for i in range(1, nblk):
        off = SUB * i * (i - 1) // 2
        pm[SUB * i:SUB * (i + 1), off:off + SUB * i] = 1.0
    return pm


def _rec_kernel(*refs, mode, C, H, pos_off, has_s0, layer):
    refs = list(refs)
    a_ref, b_ref, v_ref, gate_ref = refs[:4]
    refs = refs[4:]
    aux_ref = refs.pop(0)
    pm_ref = refs.pop(0)
    s0_ref = refs.pop(0) if has_s0 else None
    o_ref, st_ref, st_scr = refs
    c_id = pl.program_id(1)
    K = 128
    nblk = C // SUB

    @pl.when(c_id == 0)
    def _():
        if has_s0:
            st_scr[...] = s0_ref[0]
        else:
            st_scr[...] = jnp.zeros(st_scr.shape, F32)

    a_all, b_all, v_all, gate_all = a_ref[...], b_ref[...], v_ref[...], gate_ref[...]
    rows = lax.broadcasted_iota(jnp.int32, (C, 1), 0)
    if mode == "ret":
        pos = (pos_off + c_id * C + rows).astype(F32)
        ang = pos * aux_ref[0:1, :]
        cos, sin_s = jnp.cos(ang), jnp.sin(ang) * aux_ref[1:2, :]

        def rope(x):
            return x * cos + pltpu.roll(x, K // 2, 1) * sin_s

        qs = [rope(a_all[:, h * K:(h + 1) * K]) for h in range(H)]
        ks = [rope(b_all[:, h * K:(h + 1) * K]) * B_DK ** -0.5 for h in range(H)]
        gs = [jnp.full((C, K), math.log1p(-2.0 ** (-5.0 - h)), F32) for h in range(H)]
    else:
        x = aux_ref[...]
        ex = jnp.exp(x - jnp.max(x, axis=0, keepdims=True))
        sm = ex / jnp.sum(ex, axis=0, keepdims=True)
        lb_all = jnp.zeros((1, H * K), F32)
        for i in range(1, layer + 1):
            lb_all = lb_all + sm[i:i + 1]
        qs, ks, gs = [], [], []
        for h in range(H):
            ah, zf, lb = a_all[:, h * K:(h + 1) * K], b_all[:, h * K:(h + 1) * K], lb_all[:, h * K:(h + 1) * K]
            f = lb + (1.0 - lb) * _sigmoid(zf)
            qs.append(ah * _sigmoid(ah))
            ks.append(1.0 - f)
            gs.append(jnp.log(jnp.maximum(f, F_FLOOR)))

    ltri = jnp.where(rows >= lax.broadcasted_iota(jnp.int32, (1, C), 1), 1.0, 0.0).astype(BF16)
    bs_all = _dot_01x(ltri, jnp.concatenate(gs, axis=-1))
    srow = lax.broadcasted_iota(jnp.int32, (SUB, 1), 0)

    for h in range(H):
        q, k, bsum = qs[h], ks[h], bs_all[:, h * K:(h + 1) * K]
        v = v_all[:, h * K:(h + 1) * K]
        st = st_scr[h]
        o = lax.dot_general((q * jnp.exp(bsum)).astype(BF16), st.astype(BF16), NT, preferred_element_type=F32)
        if nblk > 1:
            rho = jnp.concatenate(
                [jnp.zeros((SUB, K), F32)] +
                [jnp.broadcast_to(bsum[SUB * i - 1:SUB * i], (SUB, K)) for i in range(1, nblk)], axis=0)
            qt = q * jnp.exp(bsum - rho)
            kt = jnp.concatenate(
                [k[:SUB * i] * jnp.exp(bsum[SUB * i - 1:SUB * i] - bsum[:SUB * i]) for i in range(1, nblk)], axis=0)
            vcat = jnp.concatenate([v[:SUB * i] for i in range(1, nblk)], axis=0)
            p = _dot3_nt(qt, kt) * pm_ref[...]
            o += jnp.dot(p.astype(BF16), vcat.astype(BF16), preferred_element_type=F32)
        diag = []
        for i in range(nblk):
            sl = slice(SUB * i, SUB * (i + 1))
            q8, k8, b8, v8 = q[sl], k[sl], bsum[sl], v[sl]
            od = jnp.zeros((SUB, K), F32)
            for s in range(SUB):
                causal = srow >= s
                dec = jnp.exp(jnp.where(causal, b8 - b8[s:s + 1], 0.0))
                att = jnp.sum(q8 * k8[s:s + 1] * dec, axis=-1, keepdims=True)
                od += jnp.where(causal, att, 0.0) * v8[s:s + 1]
            diag.append(od)
        o += jnp.concatenate(diag, axis=0) if nblk > 1 else diag[0]
        bend = bsum[C - 1:C]
        kst = k * jnp.exp(bend - bsum)
        st_scr[h] = st * jnp.exp(bend) + lax.dot_general(v.astype(BF16), kst.astype(BF16), TN,
                                                         preferred_element_type=F32)
        gate = gate_all[:, h * K:(h + 1) * K]
        o = o * lax.rsqrt(jnp.mean(o * o, axis=-1, keepdims=True) + EPS) * (gate * _sigmoid(gate))
        o_ref[:, h * K:(h + 1) * K] = o.astype(o_ref.dtype)

    @pl.when(c_id == pl.num_programs(1) - 1)
    def _():
        st_ref[0] = st_scr[...]


def recurrence(z, blks, aux, *, mode, nb, T, pos_off, s0t=None, layer=0):
    H, K = 4, 128
    C = CHUNK if T % CHUNK == 0 else T
    nc = T // C
    pm = _seg_mask(C)
    nseg = pm.shape[1]
    row = lambda blk: (lambda b, c: (b * nc + c, blk))
    in_specs = [pl.BlockSpec((C, H * K), row(blks[0])), pl.BlockSpec((C, H * K), row(blks[1])),
                pl.BlockSpec((C, H * K), row(blks[2])), pl.BlockSpec((C, H * K), row(blks[3])),
                pl.BlockSpec(aux.shape, lambda b, c: (0, 0)),
                pl.BlockSpec((C, nseg), lambda b, c: (0, 0))]
    args = [z, z, z, z, aux, jnp.asarray(pm)]
    if s0t is not None:
        in_specs.append(pl.BlockSpec((1, H, K, K), lambda b, c: (b, 0, 0, 0)))
        args.append(s0t)
    return pl.pallas_call(
        functools.partial(_rec_kernel, mode=mode, C=C, H=H, pos_off=pos_off, has_s0=s0t is not None, layer=layer),
        out_shape=(jax.ShapeDtypeStruct((nb * T, H * K), BF16), jax.ShapeDtypeStruct((nb, H, K, K), F32)),
        grid=(nb, nc),
        in_specs=in_specs,
        out_specs=(pl.BlockSpec((C, H * K), lambda b, c: (b * nc + c, 0)),
                   pl.BlockSpec((1, H, K, K), lambda b, c: (b, 0, 0, 0))),
        scratch_shapes=[pltpu.VMEM((H, K, K), F32)],
        compiler_params=_cparams(2), name="recurrence_" + mode)(*args)


def _gather_kernel(pt_ref, cache_ref, *refs, n_pages):
    o_ref = refs[-1]
    p = pl.program_id(1)

    @pl.when(p < n_pages)
    def _():
        o_ref[0, 0] = cache_ref[0, 0]

    if len(refs) == 2:
        @pl.when(p >= n_pages)
        def _():
            o_ref[0, 0] = refs[0][0]


def gather_pages(cache, layer, page_table, tail=None):
    nb, n_pages = page_table.shape
    _, _, ps, C = cache.shape
    n_out = n_pages + (tail is not None)
    in_specs = [pl.BlockSpec((1, 1, ps, C), lambda b, p, pt: (layer, pt[b * n_pages + jnp.minimum(p, n_pages - 1)], 0, 0))]
    args = [page_table.reshape(-1), cache]
    if tail is not None:
        in_specs.append(pl.BlockSpec((1, ps, C), lambda b, p, pt: (b, 0, 0)))
        args.append(tail)
    grid_spec = pltpu.PrefetchScalarGridSpec(
        num_scalar_prefetch=1, grid=(nb, n_out), in_specs=in_specs,
        out_specs=pl.BlockSpec((1, 1, ps, C), lambda b, p, pt: (b, p, 0, 0)))
    return pl.pallas_call(
        functools.partial(_gather_kernel, n_pages=n_pages),
        out_shape=jax.ShapeDtypeStruct((nb, n_out, ps, C), F32),
        grid_spec=grid_spec,
        compiler_params=_cparams(2), name="gather_pages")(*args)


def _page_specs(shape, layer, n_pages, per_step):
    def spec(i):
        return pl.BlockSpec((1, 1) + shape, lambda b, j, pt: (layer, pt[b * n_pages + j * per_step + i], 0, 0))
    return [spec(i) for i in range(per_step)]


def _online_update(s, mk, m_ref, l_ref, acc_ref, pv_fn):
    hg, rows, n = s.shape
    if mk is not None:
        s = jnp.where(mk, s, NEG)
    m_old = m_ref[...]
    m_new = jnp.maximum(m_old, jnp.max(s, axis=-1, keepdims=True))
    alpha = jnp.exp(m_old - m_new)
    p = jnp.exp(s - m_new)
    if mk is not None:
        p = jnp.where(mk, p, 0.0)
    l_ref[...] = alpha * l_ref[...] + jnp.sum(p, axis=-1, keepdims=True)
    pv = pv_fn(p.reshape(hg * rows, n).astype(BF16))
    acc_ref[...] = alpha * acc_ref[...] + pv.reshape(hg, rows, pv.shape[-1])
    m_ref[...] = m_new


def _diff_decode_kernel(pt_ref, q_ref, new_ref, *refs, per_step, H, dh):
    pages = refs[:per_step]
    o_ref, qbd_scr, m_scr, l_scr, acc_scr = refs[per_step:]
    j = pl.program_id(1)
    Ts = q_ref.shape[0]
    dv = 2 * dh
    rows_pp = 2 * H

    @pl.when(j == 0)
    def _():
        m_scr[...] = jnp.full(m_scr.shape, NEG, F32)
        l_scr[...] = jnp.zeros(l_scr.shape, F32)
        acc_scr[...] = jnp.zeros(acc_scr.shape, F32)
        lane = lax.broadcasted_iota(jnp.int32, (Ts, dv), 1)
        for h in range(H):
            q = q_ref[:, h * dv:(h + 1) * dv] * dh ** -0.5
            qbd_scr[h] = jnp.concatenate([jnp.where(lane < dh, q, 0.0), jnp.where(lane >= dh, q, 0.0)],
                                         axis=0).astype(BF16)

    for h in range(H):
        k = jnp.concatenate([pg[0, 0, pl.ds(h, PAGE_SIZE, stride=rows_pp), :] for pg in pages], axis=0).astype(BF16)
        v = jnp.concatenate([pg[0, 0, pl.ds(H + h, PAGE_SIZE, stride=rows_pp), :] for pg in pages], axis=0).astype(BF16)
        s = lax.dot_general(qbd_scr[h], k, NT, preferred_element_type=F32)[None]
        _online_update(s, None, m_scr.at[h], l_scr.at[h], acc_scr.at[h],
                       lambda p: jnp.dot(p, v, preferred_element_type=F32))

    @pl.when(j == pl.num_programs(1) - 1)
    def _():
        r = lax.broadcasted_iota(jnp.int32, (2 * Ts, Ts), 0)
        cidx = lax.broadcasted_iota(jnp.int32, (2 * Ts, Ts), 1)
        causal = (cidx <= jnp.where(r >= Ts, r - Ts, r))[None]
        for h in range(H):
            kn = new_ref[:, h * dv:(h + 1) * dv].astype(BF16)
            vn = new_ref[:, (H + h) * dv:(H + h + 1) * dv].astype(BF16)
            s = lax.dot_general(qbd_scr[h], kn, NT, preferred_element_type=F32)[None]
            _online_update(s, causal, m_scr.at[h], l_scr.at[h], acc_scr.at[h],
                           lambda p: jnp.dot(p, vn, preferred_element_type=F32))
            l = l_scr[h]
            o = acc_scr[h] / jnp.where(l > 0.0, l, 1.0)
            o_ref[:, (2 * h) * dv:(2 * h + 1) * dv] = o[0, :Ts]
            o_ref[:, (2 * h + 1) * dv:(2 * h + 2) * dv] = o[0, Ts:]


def diff_decode(z, cache_rows, layer, page_table, nbs, Ts):
    H, dh = C_HEADS, C_HEAD_DIM
    cw = H * 2 * dh
    n_pages = page_table.shape[1]
    per_step = _pick(n_pages, (8, 4, 2, 1))
    in_specs = [pl.BlockSpec((Ts, cw), lambda b, j, pt: (b, 0)),
                pl.BlockSpec((Ts, 2 * cw), lambda b, j, pt: (b, 0))]
    in_specs += _page_specs(cache_rows.shape[2:], layer, n_pages, per_step)
    grid_spec = pltpu.PrefetchScalarGridSpec(
        num_scalar_prefetch=1, grid=(nbs, n_pages // per_step), in_specs=in_specs,
        out_specs=pl.BlockSpec((Ts, 2 * cw), lambda b, j, pt: (b, 0)),
        scratch_shapes=[pltpu.VMEM((H, 2 * Ts, 2 * dh), BF16), pltpu.VMEM((H, 1, 2 * Ts, 1), F32),
                        pltpu.VMEM((H, 1, 2 * Ts, 1), F32), pltpu.VMEM((H, 1, 2 * Ts, 2 * dh), F32)])
    kv_new = z[:, cw:3 * cw]
    return pl.pallas_call(
        functools.partial(_diff_decode_kernel, per_step=per_step, H=H, dh=dh),
        out_shape=jax.ShapeDtypeStruct((nbs * Ts, 2 * cw), F32),
        grid_spec=grid_spec, compiler_params=_cparams(2),
        name="diff_decode")(page_table.reshape(-1), z, kv_new, *([cache_rows] * per_step))


def _slc_decode_kernel(pt_ref, q_ref, new_ref, sel_ref, *refs, per_step, nbp, past):
    pages = refs[:per_step]
    o_ref, qs_scr, m_scr, l_scr, acc_scr = refs[per_step:]
    G, hg, d = A_GROUPS, A_HEADS // A_GROUPS, A_HEAD_DIM
    j = pl.program_id(1)
    Ts = q_ref.shape[0]
    ntok = per_step * PAGE_SIZE

    @pl.when(j == 0)
    def _():
        m_scr[...] = jnp.full(m_scr.shape, NEG, F32)
        l_scr[...] = jnp.zeros(l_scr.shape, F32)
        acc_scr[...] = jnp.zeros(acc_scr.shape, F32)
        for g in range(G):
            qs = jnp.concatenate([q_ref[:, (g * hg + h) * d:(g * hg + h + 1) * d] for h in range(hg)], axis=0)
            qs_scr[g] = (qs * d ** -0.5).astype(BF16)

    tok = j * ntok + lax.broadcasted_iota(jnp.int32, (1, ntok), 1)
    jb = lax.broadcasted_iota(jnp.int32, (nbp, 1), 0)
    e01 = jnp.where(jb == lax.shift_right_logical(tok, int(math.log2(SLC_LEN))), 1.0, 0.0).astype(BF16)
    for g in range(G):
        kt = jnp.concatenate([pg[0, 0, g * d:(g + 1) * d, :] for pg in pages], axis=1).astype(BF16)
        vt = jnp.concatenate([pg[0, 0, (G + g) * d:(G + g + 1) * d, :] for pg in pages], axis=1).astype(BF16)
        s = jnp.dot(qs_scr[g], kt, preferred_element_type=F32).reshape(hg, Ts, ntok)
        st = jnp.dot(sel_ref[:, g * nbp:(g + 1) * nbp], e01, preferred_element_type=F32)
        _online_update(s, (st > -1.0)[None], m_scr.at[g], l_scr.at[g], acc_scr.at[g],
                       lambda p: lax.dot_general(p, vt, NT, preferred_element_type=F32))

    @pl.when(j == pl.num_programs(1) - 1)
    def _():
        causal = (lax.broadcasted_iota(jnp.int32, (Ts, Ts), 1) <= lax.broadcasted_iota(jnp.int32, (Ts, Ts), 0))[None]
        for g in range(G):
            kn = new_ref[:, g * d:(g + 1) * d].astype(BF16)
            vn = new_ref[:, (G + g) * d:(G + g + 1) * d].astype(BF16)
            s = lax.dot_general(qs_scr[g], kn, NT, preferred_element_type=F32).reshape(hg, Ts, Ts)
            _online_update(s, causal, m_scr.at[g], l_scr.at[g], acc_scr.at[g],
                           lambda p: jnp.dot(p, vn, preferred_element_type=F32))
            l = l_scr[g]
            o = acc_scr[g] / jnp.where(l > 0.0, l, 1.0)
            for h in range(hg):
                o_ref[:, (g * hg + h) * d:(g * hg + h + 1) * d] = o[h]


def slc_decode(z, cache_t, layer, page_table, sel, nbs, Ts, nbp, past):
    G, d = A_GROUPS, A_HEAD_DIM
    aq = A_HEADS * d
    n_pages = page_table.shape[1]
    per_step = _pick(n_pages, (16, 8, 4, 2, 1))
    assert past % SLC_LEN == 0 and Ts <= SLC_LEN
    in_specs = [pl.BlockSpec((Ts, aq), lambda b, j, pt: (b, EV_Q)),
                pl.BlockSpec((Ts, 2 * G * d), lambda b, j, pt: (b, EV_SLC // (2 * G * d))),
                pl.BlockSpec((Ts, G * nbp), lambda b, j, pt: (b, 0))]
    in_specs += _page_specs(cache_t.shape[2:], layer, n_pages, per_step)
    hg = A_HEADS // G
    grid_spec = pltpu.PrefetchScalarGridSpec(
        num_scalar_prefetch=1, grid=(nbs, n_pages // per_step), in_specs=in_specs,
        out_specs=pl.BlockSpec((Ts, aq), lambda b, j, pt: (b, 0)),
        scratch_shapes=[pltpu.VMEM((G, hg * Ts, d), BF16), pltpu.VMEM((G, hg, Ts, 1), F32),
                        pltpu.VMEM((G, hg, Ts, 1), F32), pltpu.VMEM((G, hg, Ts, d), F32)])
    return pl.pallas_call(
        functools.partial(_slc_decode_kernel, per_step=per_step, nbp=nbp, past=past),
        out_shape=jax.ShapeDtypeStruct((nbs * Ts, aq), F32),
        grid_spec=grid_spec, compiler_params=_cparams(2),
        name="slc_decode")(page_table.reshape(-1), z, z, sel, *([cache_t] * per_step))


def _compress_paged_kernel(pt_ref, w_ref, pe_ref, *refs, per_step):
    pages = refs[:per_step]
    o_ref, ab_scr = refs[per_step:]
    j = pl.program_id(1)
    cpp = PAGE_SIZE // CMP_STRIDE
    r = lax.broadcasted_iota(jnp.int32, (PAGE_SIZE, PAGE_SIZE), 0)
    t = lax.broadcasted_iota(jnp.int32, (PAGE_SIZE, PAGE_SIZE), 1)
    assert cpp & (cpp - 1) == 0
    perm = jnp.where(t == (r & (cpp - 1)) * CMP_STRIDE + lax.shift_right_logical(r, int(math.log2(cpp))),
                     1.0, 0.0).astype(BF16)
    xs = [lax.dot_general(perm, pg[0, 0].astype(BF16), NT, preferred_element_type=F32) for pg in pages]
    C = xs[0].shape[1]
    ab = jnp.zeros((per_step * cpp, w_ref.shape[1]), F32)
    for l in range(CMP_STRIDE):
        xl = jnp.concatenate([x[l * cpp:(l + 1) * cpp] for x in xs], axis=0).astype(BF16)
        ab += jnp.dot(xl, w_ref[l * C:(l + 1) * C, :], preferred_element_type=F32)
    n = per_step * cpp
    ab_scr[pl.ds(pl.multiple_of(j * n, n), n), :] = ab

    @pl.when(j == pl.num_programs(1) - 1)
    def _():
        cab = jnp.dot(pe_ref[...].astype(BF16), w_ref[...], preferred_element_type=F32)
        half = ab_scr.shape[1] // 2
        const = cab[0:1, :half] + cab[1:2, half:]
        a, b = ab_scr[:, :half], ab_scr[:, half:]
        o_ref[0] = a + pltpu.roll(b, a.shape[0] - 1, 0) + const


def compress_paged(cache_t, layer, page_table, wbig, pe2):
    nbs, n_pages = page_table.shape
    per_step = _pick(n_pages, (16, 8, 4, 2, 1))
    cpp = PAGE_SIZE // CMP_STRIDE
    n_chunk = n_pages * cpp
    cout = wbig.shape[1] // 2
    in_specs = [pl.BlockSpec(wbig.shape, lambda b, j, pt: (0, 0)), pl.BlockSpec(pe2.shape, lambda b, j, pt: (0, 0))]
    in_specs += _page_specs(cache_t.shape[2:], layer, n_pages, per_step)
    grid_spec = pltpu.PrefetchScalarGridSpec(
        num_scalar_prefetch=1, grid=(nbs, n_pages // per_step), in_specs=in_specs,
        out_specs=pl.BlockSpec((1, n_chunk, cout), lambda b, j, pt: (b, 0, 0)),
        scratch_shapes=[pltpu.VMEM((n_chunk, 2 * cout), F32)])
    return pl.pallas_call(
        functools.partial(_compress_paged_kernel, per_step=per_step),
        out_shape=jax.ShapeDtypeStruct((nbs, n_chunk, cout), F32),
        grid_spec=grid_spec, compiler_params=_cparams(2),
        name="compress_paged")(page_table.reshape(-1), wbig, pe2, *([cache_t] * per_step))


def _even_w_in(w):
    aq, akv = A_HEADS * A_HEAD_DIM, A_GROUPS * A_HEAD_DIM
    splits = np.cumsum([aq] + [akv] * 6 + [3 * A_HEADS] + [512] * 4)[:-1]
    q, kc, vc, ks, vs, kw, vw, gt, rq, rk, rv, rg = jnp.split(w, [int(s) for s in splits], axis=1)
    gt = jnp.pad(gt, ((0, 0), (0, 128 - gt.shape[1])))
    return jnp.concatenate([q, rq, rk, rv, rg, kc, vc, ks, vs, kw, vw, gt], axis=1).astype(BF16)


EV_Q, EV_RQ, EV_RK, EV_RV, EV_RG = 0, 1, 2, 3, 4
EV_KS, EV_VS, EV_KW, EV_VW, EV_GT = 22, 23, 24, 25, 26
EV_CMP, EV_SLC, EV_WIN = 2560, 2816, 3072


def _compress_w(w_cmp, pe):
    G, d = A_GROUPS, A_HEAD_DIM
    r = CMP_LEN // CMP_STRIDE
    assert r == 2
    w = w_cmp.reshape(2, r, CMP_STRIDE, d, d)
    eye_kv = jnp.eye(2, dtype=F32)
    eye_g = jnp.eye(G, dtype=F32)
    big = jnp.einsum("khlde,kq,gp->lqpdhkge", w, eye_kv, eye_g)
    big = big.reshape(CMP_STRIDE * 2 * G * d, r * 2 * G * d).astype(BF16)
    pe_r = pe.reshape(2, r, CMP_STRIDE, d)
    rows = jnp.broadcast_to(pe_r.transpose(1, 2, 0, 3)[:, :, :, None, :], (r, CMP_STRIDE, 2, G, d))
    rows = rows.reshape(r, CMP_STRIDE * 2 * G * d)
    return big, jnp.pad(rows, ((0, 8 - r), (0, 0)))


def _slc_sum_matrix(n_rows, n_cmp, nbp):
    a = np.zeros((n_rows, nbp), np.float32)
    per, left = SLC_LEN // CMP_STRIDE, CMP_LEN // CMP_STRIDE - 1
    for j in range(nbp):
        for n in range(per * j - left, per * j + per):
            if 0 <= n < n_cmp:
                a[n, j] = 1.0
    return jnp.asarray(a, BF16)


def _gate_expand():
    e = np.zeros((128, 3 * A_HEADS * A_HEAD_DIM), np.float32)
    for h in range(A_HEADS):
        for i in range(3):
            e[3 * h + i, i * A_HEADS * A_HEAD_DIM + h * A_HEAD_DIM:i * A_HEADS * A_HEAD_DIM + (h + 1) * A_HEAD_DIM] = 1.0
    return jnp.asarray(e, BF16)


def _rope_aux():
    half = B_DK // 2
    freqs = ROPE_BASE ** (-jnp.arange(half, dtype=F32) / half)
    sign = jnp.concatenate([-jnp.ones((half,), F32), jnp.ones((half,), F32)])
    return jnp.stack([jnp.concatenate([freqs, freqs]), sign])


def _slc_shapes(n_slc, n_cmp_pad):
    nbp = -(-n_slc // 128) * 128
    return nbp, _slc_sum_matrix(n_cmp_pad, n_cmp_pad - 1, nbp)


def _nsa_common(nb, T, q_off):
    return dict(nb=nb, Tq=T, tq=_pick(T, (128, 64, 32, 16, 8)), G=A_GROUPS, hg=A_HEADS // A_GROUPS, dk=A_HEAD_DIM,
                dv=A_HEAD_DIM, g_per_v=1, q_off=q_off)


def _even_layer(xp, xs, p, a, dims, caches):
    B, T, nbs, Ts, past = dims
    g = p["norm_g"]
    w_in = _even_w_in(p["w_in_a"][a])
    wbig, pe2 = _compress_w(p["cmp_w"][a], p["cmp_pos"][a])
    w_out = p["w_out_a"][a].astype(BF16)
    aq = A_HEADS * A_HEAD_DIM
    wa, wb = w_out[:aq], w_out[aq:]
    e01 = _gate_expand()
    aux = _rope_aux()
    akv2 = 2 * A_GROUPS * A_HEAD_DIM

    akv = A_GROUPS * A_HEAD_DIM
    wt = jnp.concatenate([w_in[:, EV_SLC:EV_SLC + akv], w_in[:, EV_WIN:EV_WIN + akv]], axis=1).T
    zp, zpb, ktp = in_proj(xp, g[0:1], w_in, wt)
    kv_cmp_p = zp[:, EV_CMP:EV_CMP + akv2]
    kv_slc_p = zp[:, EV_SLC:EV_SLC + akv2]
    kv_win_p = zp[:, EV_WIN:EV_WIN + akv2]
    kvc_p = compress(kv_cmp_p.reshape(B, T // CMP_STRIDE, CMP_STRIDE * akv2), wbig, pe2)
    tk = _pick(T, (512, 256, 128))
    tw = _pick(T, (128,))
    n_slc = -(-T // SLC_LEN)
    nbp, a01 = _slc_shapes(n_slc, kvc_p.shape[1])
    o_cmp, sel = cmp_attend(zp, EV_Q, kvc_p, a01, B, T, 0, n_slc)
    blk01 = np.zeros((nbp, T), np.float32)
    blk01[np.arange(T) // SLC_LEN, np.arange(T)] = 1.0
    acommon = dict(nb=B, T=T, tq=_pick(T, (128, 64, 32, 16, 8)), G=A_GROUPS, hg=A_HEADS // A_GROUPS, dk=A_HEAD_DIM,
                   dv=A_HEAD_DIM, g_per_v=1)
    o_slc = attend(zp, EV_Q, ktp, 0, zpb, EV_VS, tk=tk, mode="causal", sel=sel, e01=jnp.asarray(blk01, BF16), **acommon)
    o_win = attend(zp, EV_Q, ktp, 1, zpb, EV_VW, tk=tw, mode="window", **acommon)
    ob_p, st_p = recurrence(zp, (EV_RQ, EV_RK, EV_RV, EV_RG), aux, mode="ret", nb=B, T=T, pos_off=0)
    xp = even_out(o_cmp, o_slc, o_win, zp, EV_GT, ob_p, e01, wa, wb, g[1:2], xp)

    cache_cmp, cache_slc, win_buf, s0, page_table = caches
    zs = rms_matmul(xs, g[0:1], w_in, F32)
    kv_cmp_s = zs[:, EV_CMP:EV_CMP + akv2]
    kv_slc_s = zs[:, EV_SLC:EV_SLC + akv2]
    kv_win_s = zs[:, EV_WIN:EV_WIN + akv2]
    assert Ts < CMP_STRIDE and past % SLC_LEN == 0
    kvc_s = compress_paged(cache_cmp, a, page_table, wbig, pe2)
    n_slc = -(-(past + Ts) // SLC_LEN)
    nbp, a01 = _slc_shapes(n_slc, kvc_s.shape[1])
    o_cmp, sel = cmp_attend(zs, EV_Q, kvc_s, a01, nbs, Ts, past, n_slc)
    o_slc = slc_decode(zs, cache_slc, a, page_table, sel, nbs, Ts, nbp, past)
    band = jnp.concatenate([win_buf, kv_win_s.reshape(nbs, Ts, akv2)], axis=1)
    n_buf = win_buf.shape[1]
    bt = n_buf + Ts
    o_win = flash(zs, EV_Q, band.reshape(nbs * bt, akv2), 0, band.reshape(nbs * bt, akv2), 1, Tk=bt, tk=bt,
                  mode="window", k_off=past - n_buf, **_nsa_common(nbs, Ts, past))
    ob_s, st_s = recurrence(zs, (EV_RQ, EV_RK, EV_RV, EV_RG), aux, mode="ret", nb=nbs, T=Ts, pos_off=past,
                            s0t=jnp.swapaxes(s0, -1, -2))
    xs = even_out(o_cmp, o_slc, o_win, zs, EV_GT, ob_s, e01, wa, wb, g[1:2], xs)

    shp = lambda x, n, t: x.reshape(n, t, 2, A_GROUPS, A_HEAD_DIM)
    wkeep = min(WINDOW, T)
    outs = (shp(kv_cmp_p, B, T), shp(kv_cmp_s, nbs, Ts), shp(kv_slc_p, B, T), shp(kv_slc_s, nbs, Ts),
            shp(kv_win_p, B, T)[:, T - wkeep:], shp(band[:, Ts:], nbs, n_buf),
            jnp.swapaxes(st_p, -1, -2), jnp.swapaxes(st_s, -1, -2).astype(s0.dtype))
    return xp, xs, outs


def _odd_layer(xp, xs, p, o, layer, dims, caches):
    B, T, nbs, Ts, past = dims
    g = p["norm_g"]
    w_in = p["w_in_c"][o].astype(BF16)
    w_out = p["w_out_c"][o].astype(BF16)
    cw = C_HEADS * 2 * C_HEAD_DIM
    wc, wd = w_out[:cw], w_out[cw:]
    lam_init = 0.8 - 0.6 * math.exp(-0.3 * layer)
    lam_p = p["diff_lam"][o]
    lb_logits = p["hgrn_lb"]
    dcommon = dict(G=2 * C_HEADS, hg=1, dk=C_HEAD_DIM, dv=2 * C_HEAD_DIM, g_per_v=2, mode="causal", k_off=0)

    zp, zpb, ktp = in_proj(xp, g[0:1], w_in, w_in[:, cw:2 * cw].T)
    tq = _pick(T, (256, 128, 64, 32, 16, 8))
    tk = _pick(T, (512, 256, 128))
    o_diff = attend(zp, 0, ktp, 0, zpb, 2, nb=B, T=T, tq=tq, tk=tk, G=2 * C_HEADS, hg=1, dk=C_HEAD_DIM,
                    dv=2 * C_HEAD_DIM, g_per_v=2, mode="causal")
    od_p, st_p = recurrence(zp, (3, 4, 5, 6), lb_logits, mode="hgrn", nb=B, T=T, pos_off=0, layer=o)
    xp = odd_out(o_diff, od_p, lam_p, lam_init, wc, wd, g[1:2], xp)

    cache_diff, s0, page_table = caches
    zs = rms_matmul(xs, g[0:1], w_in, F32)
    kv_s = zs[:, cw:3 * cw]
    o_diff = diff_decode(zs, cache_diff, o, page_table, nbs, Ts)
    od_s, st_s = recurrence(zs, (3, 4, 5, 6), lb_logits, mode="hgrn", nb=nbs, T=Ts, pos_off=past,
                            s0t=jnp.swapaxes(s0, -1, -2), layer=o)
    xs = odd_out(o_diff, od_s, lam_p, lam_init, wc, wd, g[1:2], xs)

    shp = lambda x, n, t: x.reshape(n, t, 2, C_HEADS, 2 * C_HEAD_DIM)
    outs = (shp(zp[:, cw:3 * cw], B, T), shp(kv_s, nbs, Ts),
            jnp.swapaxes(st_p, -1, -2), jnp.swapaxes(st_s, -1, -2).astype(s0.dtype))
    return xp, xs, outs


def _tail_layers(xp, xs, p, layer, dims, mem_prompt, cache_mem):
    B, T, nbs, Ts, _ = dims
    g = p["norm_g"][layer]
    D = xp.shape[1]
    w_q = p["w_xq"][layer].astype(BF16)
    w_kv = p["w_xkv"][layer].astype(BF16)
    w_o = p["w_xo"][layer].astype(BF16)
    w_up = p["w_up"][layer].astype(BF16)
    w_down = p["w_down"][layer].astype(BF16)
    n_mem = mem_prompt.shape[1]
    kv_mem = rms_matmul(mem_prompt.reshape(B * n_mem, D), g[0:1], w_kv, F32, norm=False).reshape(B, n_mem, 2 * D)
    qp = rms_matmul(xp, g[2:3], w_q, BF16)
    xp = proj_res(xattn(qp, kv_mem, T), w_o, g[3:4], xp)
    qs = rms_matmul(xs, g[2:3], w_q, BF16)
    xs = proj_res(xattn(qs, cache_mem, Ts, layer=layer), w_o, g[3:4], xs)
    xp = mlp(xp, g[4:5], g[5:6], w_up, w_down)
    xs = mlp(xs, g[4:5], g[5:6], w_up, w_down)
    return xp, xs, kv_mem.reshape(B, n_mem, 2, X_HEADS, D // X_HEADS)


def kernel(x_prompt, x_sample, cache_nsa_cmp_kv, cache_nsa_slc_kv, cache_nsa_win_kv, state_ret, cache_diff_kv, state_hgrn, cache_mem_kv, page_table, mem_prompt, norm_g, w_in_a, cmp_pos, cmp_w, w_out_a, w_in_c, diff_lam, hgrn_lb, w_out_c, w_xq, w_xkv, w_xo, w_up, w_down):
    B, T, D = x_prompt.shape
    nbs, Ts, _ = x_sample.shape
    depth = norm_g.shape[0]
    n_pages = page_table.shape[1]
    past = n_pages * PAGE_SIZE
    dims = (B, T, nbs, Ts, past)
    p = dict(w_in_a=w_in_a, cmp_pos=cmp_pos, cmp_w=cmp_w, w_out_a=w_out_a, w_in_c=w_in_c, diff_lam=diff_lam,
             hgrn_lb=hgrn_lb.astype(F32), w_out_c=w_out_c, w_xq=w_xq, w_xkv=w_xkv, w_xo=w_xo, w_up=w_up, w_down=w_down)
    xp = x_prompt.reshape(B * T, D)
    xs = x_sample.reshape(nbs * Ts, D)
    feat_major = lambda c: c.transpose(0, 1, 3, 4, 5, 2).reshape(c.shape[0], c.shape[1], -1, c.shape[2])
    rows_of = lambda c: c.reshape(c.shape[0], c.shape[1], -1, c.shape[-1])
    c_cmp, c_slc, c_diff = feat_major(cache_nsa_cmp_kv), feat_major(cache_nsa_slc_kv), rows_of(cache_diff_kv)
    mshape = cache_mem_kv.shape
    c_mem = cache_mem_kv.reshape(mshape[:5] + (mshape[5] // 128, 128)).transpose(0, 1, 2, 3, 5, 4, 6)
    c_mem = c_mem.reshape(mshape[0], mshape[1], -1, 128)
    win_all = cache_nsa_win_kv.reshape(cache_nsa_win_kv.shape[0], nbs, cache_nsa_win_kv.shape[2], -1)
    ev, od, mem = [], [], []
    for layer in range(depth):
        pl_ = dict(p, norm_g=norm_g[layer])
        if layer % 2 == 0:
            a = layer // 2
            xp, xs, outs = _even_layer(xp, xs, pl_, a, dims, (c_cmp, c_slc, win_all[a], state_ret[a], page_table))
            ev.append(outs)
        else:
            o = layer // 2
            xp, xs, outs = _odd_layer(xp, xs, pl_, o, layer, dims, (c_diff, state_hgrn[o], page_table))
            od.append(outs)
        xp, xs, kvm = _tail_layers(xp, xs, dict(p, norm_g=norm_g), layer, dims, mem_prompt, c_mem)
        mem.append(kvm)
    stack = lambda lst, i: jnp.stack([t[i] for t in lst])
    return (xp.reshape(B, T, D), xs.reshape(nbs, Ts, D),
            stack(ev, 0), stack(ev, 1), stack(ev, 2), stack(ev, 3), stack(ev, 4), stack(ev, 5), stack(ev, 6), stack(ev, 7),
            stack(od, 0), stack(od, 1), stack(od, 2), stack(od, 3), jnp.stack(mem))
```

```python
import functools
import math

import numpy as np
import jax
import jax.numpy as jnp
from jax import lax
from jax.experimental import pallas as pl
from jax.experimental.pallas import tpu as pltpu

F32 = jnp.float32
BF16 = jnp.bfloat16

EPS = 1e-6
NEG = -1e30
BIG = 1e9
A_HEADS, A_GROUPS, A_HEAD_DIM = 8, 2, 64
CMP_LEN, CMP_STRIDE, SLC_LEN, SLC_TOPK, WINDOW = 32, 16, 64, 16, 512
B_HEADS, B_DK = 4, 128
ROPE_BASE = 10000.0
CHUNK = 64
C_HEADS, C_HEAD_DIM = 4, 64
D_HEADS = 4
F_FLOOR = 1e-6
X_HEADS = 4
PAGE_SIZE = 128
SUB = 8
LANES = 128

VMEM_LIMIT = 56 * 1024 * 1024

NT = (((1,), (1,)), ((), ()))
TN = (((0,), (0,)), ((), ()))


def _cparams(n_grid):
    return pltpu.CompilerParams(dimension_semantics=("arbitrary",) * n_grid, vmem_limit_bytes=VMEM_LIMIT)


def _pick(n, cands):
    for c in cands:
        if n % c == 0:
            return c
    return n


def _split3(x):
    hi = x.astype(BF16)
    r = x - hi.astype(F32)
    mid = r.astype(BF16)
    lo = (r - mid.astype(F32)).astype(BF16)
    return hi, mid, lo


def _dot_x01(x, m01):
    return sum(jnp.dot(p, m01, preferred_element_type=F32) for p in _split3(x))


def _dot_01x(m01, x):
    return sum(jnp.dot(m01, p, preferred_element_type=F32) for p in _split3(x))


def _dot3_nt(a, b):
    ah = a.astype(BF16)
    al = (a - ah.astype(F32)).astype(BF16)
    bh = b.astype(BF16)
    bl = (b - bh.astype(F32)).astype(BF16)
    d = lambda x, y: lax.dot_general(x, y, NT, preferred_element_type=F32)
    return d(ah, bh) + d(ah, bl) + d(al, bh)


def _sigmoid(x):
    return 1.0 / (1.0 + jnp.exp(-x))


def _rms(x, g):
    return x * lax.rsqrt(jnp.mean(x * x, axis=-1, keepdims=True) + EPS) * g


def _rms_matmul_kernel(x_ref, g_ref, w_ref, o_ref, *, norm):
    x = x_ref[...]
    if norm:
        x = _rms(x, g_ref[...])
    o_ref[...] = jnp.dot(x.astype(BF16), w_ref[...], preferred_element_type=F32).astype(o_ref.dtype)


def rms_matmul(x, g, w, out_dtype, norm=True):
    R, D = x.shape
    N = w.shape[1]
    tm = _pick(R, (512, 256, 128, 64, 32, 16, 8))
    tn = _pick(N, (1792, 1152, 1024, 896, 768, 640, 512, 384, 256, 128))
    return pl.pallas_call(
        functools.partial(_rms_matmul_kernel, norm=norm),
        out_shape=jax.ShapeDtypeStruct((R, N), out_dtype),
        grid=(R // tm, N // tn),
        in_specs=[pl.BlockSpec((tm, D), lambda i, j: (i, 0)),
                  pl.BlockSpec((1, D), lambda i, j: (0, 0)),
                  pl.BlockSpec((D, tn), lambda i, j: (0, j))],
        out_specs=pl.BlockSpec((tm, tn), lambda i, j: (i, j)),
        compiler_params=_cparams(2), name="rms_matmul")(x, g, w)


def _in_proj_kernel(x_ref, g_ref, w_ref, wt_ref, z_ref, zb_ref, kt_ref, xn_scr):
    j = pl.program_id(1)

    @pl.when(j == 0)
    def _():
        xn = _rms(x_ref[...], g_ref[...]).astype(BF16)
        xn_scr[...] = xn
        kt_ref[...] = lax.dot_general(wt_ref[...], xn, NT, preferred_element_type=F32).astype(kt_ref.dtype)

    z = jnp.dot(xn_scr[...], w_ref[...], preferred_element_type=F32)
    z_ref[...] = z
    zb_ref[...] = z.astype(BF16)


def in_proj(x, g, w, wt):
    R, D = x.shape
    N = w.shape[1]
    Fk = wt.shape[0]
    tm = _pick(R, (512, 256, 128, 64, 32, 16, 8))
    tn = _pick(N, (1792, 1152, 1024, 896, 768, 640, 512, 384, 256, 128))
    return pl.pallas_call(
        _in_proj_kernel,
        out_shape=(jax.ShapeDtypeStruct((R, N), F32), jax.ShapeDtypeStruct((R, N), BF16),
                   jax.ShapeDtypeStruct((Fk, R), BF16)),
        grid=(R // tm, N // tn),
        in_specs=[pl.BlockSpec((tm, D), lambda i, j: (i, 0)),
                  pl.BlockSpec((1, D), lambda i, j: (0, 0)),
                  pl.BlockSpec((D, tn), lambda i, j: (0, j)),
                  pl.BlockSpec((Fk, D), lambda i, j: (0, 0))],
        out_specs=(pl.BlockSpec((tm, tn), lambda i, j: (i, j)), pl.BlockSpec((tm, tn), lambda i, j: (i, j)),
                   pl.BlockSpec((Fk, tm), lambda i, j: (0, i))),
        scratch_shapes=[pltpu.VMEM((tm, D), BF16)],
        compiler_params=_cparams(2), name="in_proj")(x, g, w, wt)


def _mlp_kernel(x_ref, g4_ref, g5_ref, wu_ref, wd_ref, o_ref, xn_scr, acc_scr):
    j = pl.program_id(1)

    @pl.when(j == 0)
    def _():
        xn_scr[...] = _rms(x_ref[...], g4_ref[...]).astype(BF16)
        acc_scr[...] = jnp.zeros_like(acc_scr)

    h = jnp.dot(xn_scr[...], wu_ref[...], preferred_element_type=F32)
    h = jnp.square(jnp.maximum(h, 0.0))
    acc_scr[...] += jnp.dot(h.astype(BF16), wd_ref[...], preferred_element_type=F32)

    @pl.when(j == pl.num_programs(1) - 1)
    def _():
        o_ref[...] = x_ref[...] + _rms(acc_scr[...], g5_ref[...])


def mlp(x, g4, g5, w_up, w_down):
    R, D = x.shape
    F = w_up.shape[1]
    tm = _pick(R, (512, 256, 128, 64, 32, 16, 8))
    tf = _pick(F, (1024, 512, 256, 128))
    return pl.pallas_call(
        _mlp_kernel,
        out_shape=jax.ShapeDtypeStruct((R, D), F32),
        grid=(R // tm, F // tf),
        in_specs=[pl.BlockSpec((tm, D), lambda i, j: (i, 0)),
                  pl.BlockSpec((1, D), lambda i, j: (0, 0)),
                  pl.BlockSpec((1, D), lambda i, j: (0, 0)),
                  pl.BlockSpec((D, tf), lambda i, j: (0, j)),
                  pl.BlockSpec((tf, D), lambda i, j: (j, 0))],
        out_specs=pl.BlockSpec((tm, D), lambda i, j: (i, 0)),
        scratch_shapes=[pltpu.VMEM((tm, D), BF16), pltpu.VMEM((tm, D), F32)],
        compiler_params=_cparams(2), name="mlp")(x, g4, g5, w_up, w_down)


def _proj_res_kernel(a_ref, w_ref, g_ref, x_ref, o_ref):
    y = jnp.dot(a_ref[...], w_ref[...], preferred_element_type=F32)
    o_ref[...] = x_ref[...] + _rms(y, g_ref[...])


def proj_res(a, w, g, x):
    R, D = x.shape
    K = a.shape[1]
    tm = _pick(R, (512, 256, 128, 64, 32, 16, 8))
    return pl.pallas_call(
        _proj_res_kernel,
        out_shape=jax.ShapeDtypeStruct((R, D), F32),
        grid=(R // tm,),
        in_specs=[pl.BlockSpec((tm, K), lambda i: (i, 0)),
                  pl.BlockSpec((K, D), lambda i: (0, 0)),
                  pl.BlockSpec((1, D), lambda i: (0, 0)),
                  pl.BlockSpec((tm, D), lambda i: (i, 0))],
        out_specs=pl.BlockSpec((tm, D), lambda i: (i, 0)),
        compiler_params=_cparams(1), name="proj_res")(a, w, g, x)


def _even_out_kernel(oc_ref, os_ref, ow_ref, gt_ref, ob_ref, e_ref, wa_ref, wb_ref, g_ref, x_ref, o_ref):
    gates = _sigmoid(gt_ref[...])
    ge = _dot_x01(gates, e_ref[...])
    aq = oc_ref.shape[1]
    oa = ge[:, :aq] * oc_ref[...] + ge[:, aq:2 * aq] * os_ref[...] + ge[:, 2 * aq:] * ow_ref[...]
    y = jnp.dot(oa.astype(BF16), wa_ref[...], preferred_element_type=F32)
    y += jnp.dot(ob_ref[...], wb_ref[...], preferred_element_type=F32)
    o_ref[...] = x_ref[...] + _rms(y, g_ref[...])


def even_out(o_cmp, o_slc, o_win, z, gt_blk, o_b, e01, wa, wb, g, x):
    R, D = x.shape
    aq = o_cmp.shape[1]
    tm = _pick(R, (256, 128, 64, 32, 16, 8))
    row = lambda i: (i, 0)
    fix = lambda i: (0, 0)
    return pl.pallas_call(
        _even_out_kernel,
        out_shape=jax.ShapeDtypeStruct((R, D), F32),
        grid=(R // tm,),
        in_specs=[pl.BlockSpec((tm, aq), row), pl.BlockSpec((tm, aq), row), pl.BlockSpec((tm, aq), row),
                  pl.BlockSpec((tm, 128), lambda i: (i, gt_blk)),
                  pl.BlockSpec((tm, o_b.shape[1]), row),
                  pl.BlockSpec(e01.shape, fix), pl.BlockSpec(wa.shape, fix), pl.BlockSpec(wb.shape, fix),
                  pl.BlockSpec((1, D), fix), pl.BlockSpec((tm, D), row)],
        out_specs=pl.BlockSpec((tm, D), row),
        compiler_params=_cparams(1), name="even_out")(o_cmp, o_slc, o_win, z, o_b, e01, wa, wb, g, x)


def _odd_out_kernel(oc_ref, od_ref, lam_ref, wc_ref, wd_ref, g_ref, x_ref, o_ref, *, lam_init, heads, dv):
    lp = lam_ref[...]
    lam = (jnp.exp(jnp.sum(lp[0:1] * lp[1:2], axis=-1, keepdims=True))
           - jnp.exp(jnp.sum(lp[2:3] * lp[3:4], axis=-1, keepdims=True)) + lam_init)
    parts = []
    for h in range(heads):
        o1 = oc_ref[:, (2 * h) * dv:(2 * h + 1) * dv]
        o2 = oc_ref[:, (2 * h + 1) * dv:(2 * h + 2) * dv]
        o = o1 - lam * o2
        o = o * lax.rsqrt(jnp.mean(o * o, axis=-1, keepdims=True) + EPS) * (1.0 - lam_init)
        parts.append(o.astype(BF16))
    oc = jnp.concatenate(parts, axis=-1)
    y = jnp.dot(oc, wc_ref[...], preferred_element_type=F32)
    y += jnp.dot(od_ref[...], wd_ref[...], preferred_element_type=F32)
    o_ref[...] = x_ref[...] + _rms(y, g_ref[...])


def odd_out(o_diff, o_d, lam_p, lam_init, wc, wd, g, x):
    R, D = x.shape
    tm = _pick(R, (256, 128, 64, 32, 16, 8))
    row = lambda i: (i, 0)
    fix = lambda i: (0, 0)
    return pl.pallas_call(
        functools.partial(_odd_out_kernel, lam_init=lam_init, heads=C_HEADS, dv=2 * C_HEAD_DIM),
        out_shape=jax.ShapeDtypeStruct((R, D), F32),
        grid=(R // tm,),
        in_specs=[pl.BlockSpec((tm, o_diff.shape[1]), row), pl.BlockSpec((tm, o_d.shape[1]), row),
                  pl.BlockSpec(lam_p.shape, fix), pl.BlockSpec(wc.shape, fix), pl.BlockSpec(wd.shape, fix),
                  pl.BlockSpec((1, D), fix), pl.BlockSpec((tm, D), row)],
        out_specs=pl.BlockSpec((tm, D), row),
        compiler_params=_cparams(1), name="odd_out")(o_diff, o_d, lam_p, wc, wd, g, x)


def _xattn_kernel(q_ref, kv_ref, o_ref, *, heads, scale, rows):
    dm = q_ref.shape[1]
    hd = dm // heads
    for h in range(heads):
        qh = q_ref[:, h * hd:(h + 1) * hd]
        if rows:
            nh = hd // 128
            per_tok = 2 * nh * heads
            n_mem = kv_ref.shape[2] // per_tok
            row = lambda slot: jnp.concatenate(
                [kv_ref[0, 0, pl.ds((slot * nh + i) * heads + h, n_mem, stride=per_tok), :] for i in range(nh)],
                axis=-1).astype(BF16)
            kh, vh = row(0), row(1)
        else:
            kh = kv_ref[0, :, h * hd:(h + 1) * hd].astype(BF16)
            vh = kv_ref[0, :, dm + h * hd:dm + (h + 1) * hd].astype(BF16)
        s = lax.dot_general(qh, kh, NT, preferred_element_type=F32) * scale
        e = jnp.exp(s - jnp.max(s, axis=-1, keepdims=True))
        p = e / jnp.sum(e, axis=-1, keepdims=True)
        o = jnp.dot(p.astype(BF16), vh, preferred_element_type=F32)
        o_ref[:, h * hd:(h + 1) * hd] = o.astype(o_ref.dtype)


def xattn(q, kv, rows_per_batch, layer=None):
    R, D = q.shape
    tq = _pick(rows_per_batch, (512, 256, 128, 64, 32, 16, 8))
    per = rows_per_batch // tq
    if layer is None:
        kv_spec = pl.BlockSpec((1,) + kv.shape[1:], lambda i: (i // per, 0, 0))
    else:
        kv_spec = pl.BlockSpec((1, 1) + kv.shape[2:], lambda i: (layer, i // per, 0, 0))
    return pl.pallas_call(
        functools.partial(_xattn_kernel, heads=X_HEADS, scale=(D // X_HEADS) ** -0.5, rows=layer is not None),
        out_shape=jax.ShapeDtypeStruct((R, D), BF16),
        grid=(R // tq,),
        in_specs=[pl.BlockSpec((tq, D), lambda i: (i, 0)), kv_spec],
        out_specs=pl.BlockSpec((tq, D), lambda i: (i, 0)),
        compiler_params=_cparams(1), name="xattn")(q, kv)


def _compress_kernel(x_ref, w_ref, pe_ref, o_ref):
    x = x_ref[0].astype(BF16)
    ab = jnp.dot(x, w_ref[...], preferred_element_type=F32)
    cab = jnp.dot(pe_ref[...].astype(BF16), w_ref[...], preferred_element_type=F32)
    half = ab.shape[1] // 2
    a, b = ab[:, :half], ab[:, half:]
    const = cab[0:1, :half] + cab[1:2, half:]
    n = a.shape[0]
    o_ref[0] = a + pltpu.roll(b, n - 1, 0) + const


def compress(xc, wbig, pe2):
    nb, n_chunk, kdim = xc.shape
    cout = wbig.shape[1] // 2
    return pl.pallas_call(
        _compress_kernel,
        out_shape=jax.ShapeDtypeStruct((nb, n_chunk, cout), F32),
        grid=(nb,),
        in_specs=[pl.BlockSpec((1, n_chunk, kdim), lambda b: (b, 0, 0)),
                  pl.BlockSpec(wbig.shape, lambda b: (0, 0)),
                  pl.BlockSpec(pe2.shape, lambda b: (0, 0))],
        out_specs=pl.BlockSpec((1, n_chunk, cout), lambda b: (b, 0, 0)),
        compiler_params=_cparams(1), name="compress")(xc, wbig, pe2)


def _cmp_kernel(q_ref, kv_ref, a_ref, o_ref, sel_ref, *, tq, q_off, n_slc, nbp):
    G, hg, d = A_GROUPS, A_HEADS // A_GROUPS, A_HEAD_DIM
    qi = pl.program_id(1)
    ncmp = kv_ref.shape[1]
    pos_q = q_off + qi * tq + lax.broadcasted_iota(jnp.int32, (tq, 1), 0)
    cmp_end = lax.broadcasted_iota(jnp.int32, (1, ncmp), 1) * CMP_STRIDE + (CMP_LEN - 1)
    mask3 = (cmp_end <= pos_q)[None]
    jb = lax.broadcasted_iota(jnp.int32, (1, nbp), 1)
    blk_q = lax.shift_right_logical(pos_q, int(math.log2(SLC_LEN)))
    valid = jb <= blk_q
    forced = valid & ((jb == 0) | (jb >= blk_q - 1))
    k_top = min(SLC_TOPK, n_slc)
    scores = []
    for g in range(G):
        kc = kv_ref[0, :, g * d:(g + 1) * d].astype(BF16)
        vc = kv_ref[0, :, (G + g) * d:(G + g + 1) * d].astype(BF16)
        qs = jnp.concatenate([q_ref[:, (g * hg + h) * d:(g * hg + h + 1) * d] for h in range(hg)], axis=0)
        qs = (qs * d ** -0.5).astype(BF16)
        s = lax.dot_general(qs, kc, NT, preferred_element_type=F32).reshape(hg, tq, ncmp)
        s = jnp.where(mask3, s, NEG)
        e = jnp.where(mask3, jnp.exp(s - jnp.max(s, axis=-1, keepdims=True)), 0.0)
        den = jnp.sum(e, axis=-1, keepdims=True)
        p = e / jnp.where(den > 0.0, den, 1.0)
        o = jnp.dot(p.reshape(hg * tq, ncmp).astype(BF16), vc, preferred_element_type=F32)
        for h in range(hg):
            o_ref[:, (g * hg + h) * d:(g * hg + h + 1) * d] = o[h * tq:(h + 1) * tq]
        p_slc = _dot_x01(jnp.sum(p, axis=0), a_ref[...])
        score = jnp.where(forced, BIG, jnp.where(valid, p_slc, -BIG))
        scores.append(jnp.where(jb < n_slc, score, -jnp.inf))
    score = jnp.concatenate(scores, axis=0)
    sel = jnp.zeros((G * tq, nbp), F32)
    jbf = jb.astype(F32)
    for _ in range(k_top):
        mx = jnp.max(score, axis=-1, keepdims=True)
        jm = jnp.min(jnp.where(score == mx, jbf, float(nbp)), axis=-1, keepdims=True)
        hit = jbf == jm
        sel = jnp.where(hit, 1.0, sel)
        score = jnp.where(hit, -jnp.inf, score)
    for g in range(G):
        ok = valid & (sel[g * tq:(g + 1) * tq] > 0.5)
        sel_ref[:, g * nbp:(g + 1) * nbp] = jnp.where(ok, 0.0, SEL_OFF).astype(BF16)


def cmp_attend(z, q_blk, kvc, a01, nb, Tq, q_off, n_slc):
    aq = A_HEADS * A_HEAD_DIM
    nbp = a01.shape[1]
    tq = _pick(Tq, (256, 128, 64, 32, 16, 8))
    nq = Tq // tq
    ncmp = kvc.shape[1]
    return pl.pallas_call(
        functools.partial(_cmp_kernel, tq=tq, q_off=q_off, n_slc=n_slc, nbp=nbp),
        out_shape=(jax.ShapeDtypeStruct((nb * Tq, aq), F32),
                   jax.ShapeDtypeStruct((nb * Tq, A_GROUPS * nbp), BF16)),
        grid=(nb, nq),
        in_specs=[pl.BlockSpec((tq, aq), lambda b, i: (b * nq + i, q_blk)),
                  pl.BlockSpec((1, ncmp, kvc.shape[2]), lambda b, i: (b, 0, 0)),
                  pl.BlockSpec(a01.shape, lambda b, i: (0, 0))],
        out_specs=(pl.BlockSpec((tq, aq), lambda b, i: (b * nq + i, 0)),
                   pl.BlockSpec((tq, A_GROUPS * nbp), lambda b, i: (b * nq + i, 0))),
        compiler_params=_cparams(2), name="cmp_attend")(z, kvc, a01)


class _FlashCfg:
    def __init__(self, **kw):
        self.__dict__.update(kw)


def _tile_range(c, qi):
    q_lo = c.q_off + qi * c.tq
    last = (q_lo + c.tq - 1 - c.k_off) // c.tk
    if c.mode == "window":
        first = (q_lo - (WINDOW - 1) - c.k_off) // c.tk
    else:
        first = 0 * qi
    return first, last


def _flash_kernel(*refs, c):
    if c.sel:
        q_ref, k_ref, v_ref, sel_ref, o_ref, qs_scr, m_scr, l_scr, acc_scr = refs
    else:
        q_ref, k_ref, v_ref, o_ref, qs_scr, m_scr, l_scr, acc_scr = refs
        sel_ref = None
    qi, j = pl.program_id(1), pl.program_id(2)
    tq, tk, G, hg, dk, dv = c.tq, c.tk, c.G, c.hg, c.dk, c.dv

    @pl.when(j == 0)
    def _():
        m_scr[...] = jnp.full(m_scr.shape, NEG, F32)
        l_scr[...] = jnp.zeros(l_scr.shape, F32)
        acc_scr[...] = jnp.zeros(acc_scr.shape, F32)
        for g in range(G):
            qs = jnp.concatenate([q_ref[:, (g * hg + h) * dk:(g * hg + h + 1) * dk] for h in range(hg)], axis=0)
            qs_scr[g] = (qs * dk ** -0.5).astype(BF16)

    first, last = _tile_range(c, qi)
    jabs = first + j

    @pl.when((jabs >= 0) & (jabs <= jnp.minimum(last, c.nk - 1)))
    def _():
        pos_q = c.q_off + qi * tq + lax.broadcasted_iota(jnp.int32, (tq, 1), 0)
        tok = jabs * tk + lax.broadcasted_iota(jnp.int32, (1, tk), 1)
        dpos = pos_q - (c.k_off + tok)
        mask = dpos >= 0
        if c.mode == "window":
            mask = mask & (dpos < WINDOW)
        if c.sel:
            jb = lax.broadcasted_iota(jnp.int32, (c.nbp, 1), 0)
            e01 = jnp.where(jb == lax.shift_right_logical(tok, int(math.log2(SLC_LEN))), 1.0, 0.0).astype(BF16)
        for g in range(G):
            kg = k_ref[:, g * dk:(g + 1) * dk].astype(BF16)
            vi = g // c.g_per_v
            vg = v_ref[:, vi * dv:(vi + 1) * dv].astype(BF16)
            s = lax.dot_general(qs_scr[g], kg, NT, preferred_element_type=F32).reshape(hg, tq, tk)
            mk = mask
            if c.sel:
                st = jnp.dot(sel_ref[:, g * c.nbp:(g + 1) * c.nbp], e01, preferred_element_type=F32)
                mk = mk & (st > 0.5)
            mk = mk[None]
            s = jnp.where(mk, s, NEG)
            m_old = m_scr[g]
            m_new = jnp.maximum(m_old, jnp.max(s, axis=-1, keepdims=True))
            alpha = jnp.exp(m_old - m_new)
            p = jnp.where(mk, jnp.exp(s - m_new), 0.0)
            l_scr[g] = alpha * l_scr[g] + jnp.sum(p, axis=-1, keepdims=True)
            pv = jnp.dot(p.reshape(hg * tq, tk).astype(BF16), vg, preferred_element_type=F32)
            acc_scr[g] = alpha * acc_scr[g] + pv.reshape(hg, tq, dv)
            m_scr[g] = m_new

    @pl.when(j == pl.num_programs(2) - 1)
    def _():
        for g in range(G):
            l = l_scr[g]
            o = acc_scr[g] / jnp.where(l > 0.0, l, 1.0)
            for h in range(hg):
                o_ref[:, (g * hg + h) * dv:(g * hg + h + 1) * dv] = o[h].astype(o_ref.dtype)


def flash(q2d, q_blk, k2d, k_blk, v2d, v_blk, *, nb, Tq, Tk, tq, tk, G, hg, dk, dv, g_per_v, mode,
          q_off, k_off, sel=None, nbp=0):
    nq, nk = Tq // tq, Tk // tk
    assert Tq % tq == 0 and Tk % tk == 0
    c = _FlashCfg(tq=tq, tk=tk, G=G, hg=hg, dk=dk, dv=dv, g_per_v=g_per_v, mode=mode, q_off=q_off, k_off=k_off,
                  nk=nk, sel=sel is not None, nbp=nbp)
    steps = max(min(_tile_range(c, i)[1], nk - 1) - _tile_range(c, i)[0] + 1 for i in range(nq))
    qw, kw, vw = G * hg * dk, G * dk, (G // g_per_v) * dv

    def kv_map(blk):
        def f(b, i, j):
            first, last = _tile_range(c, i)
            return (b * nk + jnp.clip(first + j, 0, jnp.minimum(last, nk - 1)), blk)
        return f

    in_specs = [pl.BlockSpec((tq, qw), lambda b, i, j: (b * nq + i, q_blk)),
                pl.BlockSpec((tk, kw), kv_map(k_blk)),
                pl.BlockSpec((tk, vw), kv_map(v_blk))]
    args = [q2d, k2d, v2d]
    if sel is not None:
        in_specs.append(pl.BlockSpec((tq, G * nbp), lambda b, i, j: (b * nq + i, 0)))
        args.append(sel)
    return pl.pallas_call(
        functools.partial(_flash_kernel, c=c),
        out_shape=jax.ShapeDtypeStruct((nb * Tq, G * hg * dv), F32),
        grid=(nb, nq, steps),
        in_specs=in_specs,
        out_specs=pl.BlockSpec((tq, G * hg * dv), lambda b, i, j: (b * nq + i, 0)),
        scratch_shapes=[pltpu.VMEM((G, hg * tq, dk), BF16), pltpu.VMEM((G, hg, tq, 1), F32),
                        pltpu.VMEM((G, hg, tq, 1), F32), pltpu.VMEM((G, hg, tq, dv), F32)],
        compiler_params=_cparams(3), name="flash_" + mode + ("_sel" if sel is not None else ""))(*args)


LOG2E = 1.4426950408889634
SEL_OFF = -2.0 ** 30


def _attn_kernel(tab_ref, q_ref, kt_ref, v_ref, *refs, c):
    if c.sel:
        sel_ref, e01_ref = refs[:2]
        refs = refs[2:]
    o_ref, qs_scr, s_scr, p_scr, m_scr, l_scr, a_scr, acc_scr = refs
    n = pl.program_id(1)
    qi, kj, flags = tab_ref[0, n], tab_ref[1, n], tab_ref[2, n]
    tq, tk, G, hg, dk, dv = c.tq, c.tk, c.G, c.hg, c.dk, c.dv
    rows = hg * tq
    ncb = tk // LANES

    @pl.when((flags & 1) != 0)
    def _():
        m_scr[...] = jnp.full(m_scr.shape, NEG, F32)
        l_scr[...] = jnp.zeros(l_scr.shape, F32)
        acc_scr[...] = jnp.zeros(acc_scr.shape, F32)
        for g in range(G):
            qs = jnp.concatenate([q_ref[:, (g * hg + h) * dk:(g * hg + h + 1) * dk] for h in range(hg)], axis=0)
            qs_scr[g] = (qs * (dk ** -0.5 * LOG2E)).astype(BF16)

    def step(masked):
        if masked:
            dq = qi * tq - kj * tk + c.q_off - c.k_off
            d = (lax.broadcasted_iota(jnp.int32, (tq, tk), 0) - lax.broadcasted_iota(jnp.int32, (tq, tk), 1)) + dq
            ok = d >= 0
            if c.mode == "window":
                ok = ok & (d < WINDOW)
        for g in range(G):
            s = jnp.dot(qs_scr[g], kt_ref[g * dk:(g + 1) * dk, :], preferred_element_type=F32)
            if c.sel or masked:
                s = s.reshape(hg, tq, tk)
                if c.sel:
                    s = s + jnp.dot(sel_ref[:, g * c.nbp:(g + 1) * c.nbp], e01_ref[...], preferred_element_type=F32)[None]
                if masked:
                    s = jnp.where(ok[None], s, NEG)
                s = s.reshape(rows, tk)
            s_scr[g] = s
        for g in range(G):
            cols = [s_scr[g, :, cb * LANES:(cb + 1) * LANES] for cb in range(ncb)]
            mx = cols[0]
            for x in cols[1:]:
                mx = jnp.maximum(mx, x)
            m_old = m_scr[g]
            m_new = jnp.maximum(m_old, jnp.broadcast_to(jnp.max(mx, axis=-1, keepdims=True), (rows, LANES)))
            alpha = jnp.exp2(m_old - m_new)
            psum = None
            for cb in range(ncb):
                p = jnp.exp2(cols[cb] - m_new)
                if masked:
                    okc = jnp.broadcast_to(ok[None, :, cb * LANES:(cb + 1) * LANES], (hg, tq, LANES)).reshape(rows, LANES)
                    p = jnp.where(okc, p, 0.0)
                psum = p if psum is None else psum + p
                p_scr[g, :, cb * LANES:(cb + 1) * LANES] = p.astype(BF16)
            l_scr[g] = alpha * l_scr[g] + psum
            m_scr[g] = m_new
            a_scr[g] = alpha
        for g in range(G):
            vi = g // c.g_per_v
            pv = jnp.dot(p_scr[g], v_ref[:, vi * dv:(vi + 1) * dv], preferred_element_type=F32)
            acc_scr[g] = a_scr[g, :, :dv] * acc_scr[g] + pv

    @pl.when((flags & 4) != 0)
    def _():
        step(True)

    @pl.when((flags & 4) == 0)
    def _():
        step(False)

    @pl.when((flags & 2) != 0)
    def _():
        for g in range(G):
            l = jnp.sum(l_scr[g], axis=-1, keepdims=True)
            o = acc_scr[g] / jnp.where(l > 0.0, l, 1.0)
            for h in range(hg):
                o_ref[:, (g * hg + h) * dv:(g * hg + h + 1) * dv] = o[h * tq:(h + 1) * tq].astype(o_ref.dtype)


def attend(z, q_blk, kt, kt_blk, zb, v_blk, *, nb, T, tq, tk, G, hg, dk, dv, g_per_v, mode, sel=None, e01=None):
    nq, nk = T // tq, T // tk
    c = _FlashCfg(tq=tq, tk=tk, G=G, hg=hg, dk=dk, dv=dv, g_per_v=g_per_v, mode=mode, q_off=0, k_off=0, nk=nk,
                  sel=sel is not None, nbp=0 if sel is None else sel.shape[1] // G)
    tab = []
    for i in range(nq):
        first, last = _tile_range(c, i)
        first, last = max(first, 0), min(last, nk - 1)
        for j in range(first, last + 1):
            lo, hi = i * tq - (j * tk + tk - 1), i * tq + tq - 1 - j * tk
            masked = lo < 0 or (mode == "window" and hi >= WINDOW)
            tab.append((i, j, (j == first) * 1 + (j == last) * 2 + masked * 4))
    tab = jnp.asarray(np.array(tab, np.int32).T)
    npairs = tab.shape[1]
    rows = hg * tq
    in_specs = [pl.BlockSpec((tq, G * hg * dk), lambda b, n, t: (b * nq + t[0, n], q_blk)),
                pl.BlockSpec((G * dk, tk), lambda b, n, t: (kt_blk, b * nk + t[1, n])),
                pl.BlockSpec((tk, (G // g_per_v) * dv), lambda b, n, t: (b * nk + t[1, n], v_blk))]
    args = [z, kt, zb]
    if sel is not None:
        in_specs += [pl.BlockSpec((tq, sel.shape[1]), lambda b, n, t: (b * nq + t[0, n], 0)),
                     pl.BlockSpec((e01.shape[0], tk), lambda b, n, t: (0, t[1, n]))]
        args += [sel, e01]
    grid_spec = pltpu.PrefetchScalarGridSpec(
        num_scalar_prefetch=1, grid=(nb, npairs), in_specs=in_specs,
        out_specs=pl.BlockSpec((tq, G * hg * dv), lambda b, n, t: (b * nq + t[0, n], 0)),
        scratch_shapes=[pltpu.VMEM((G, rows, dk), BF16), pltpu.VMEM((G, rows, tk), F32),
                        pltpu.VMEM((G, rows, tk), BF16), pltpu.VMEM((G, rows, LANES), F32),
                        pltpu.VMEM((G, rows, LANES), F32), pltpu.VMEM((G, rows, LANES), F32),
                        pltpu.VMEM((G, rows, dv), F32)])
    return pl.pallas_call(
        functools.partial(_attn_kernel, c=c),
        out_shape=jax.ShapeDtypeStruct((nb * T, G * hg * dv), F32),
        grid_spec=grid_spec, compiler_params=_cparams(2),
        name="attend_" + mode + ("_sel" if sel is not None else ""))(tab, *args)


def _window_kernel(q_ref, *refs, tq, nband):
    kts, vs = refs[:nband], refs[nband:2 * nband]
    o_ref = refs[2 * nband]
    G, hg, d = A_GROUPS, A_HEADS // A_GROUPS, A_HEAD_DIM
    qi = pl.program_id(1)
    ncol = nband * tq
    row = lax.broadcasted_iota(jnp.int32, (tq, ncol), 0)
    col = lax.broadcasted_iota(jnp.int32, (tq, ncol), 1)
    pos_k = (qi - (nband - 1)) * tq + col
    dpos = qi * tq + row - pos_k
    ok = ((dpos >= 0) & (dpos < WINDOW) & (pos_k >= 0))[None]
    for g in range(G):
        qs = jnp.concatenate([q_ref[:, (g * hg + h) * d:(g * hg + h + 1) * d] for h in range(hg)], axis=0)
        qs = (qs * (d ** -0.5 * LOG2E)).astype(BF16)
        kt = jnp.concatenate([k[g * d:(g + 1) * d, :] for k in kts], axis=1)
        v = jnp.concatenate([x[:, g * d:(g + 1) * d] for x in vs], axis=0)
        s = jnp.dot(qs, kt, preferred_element_type=F32).reshape(hg, tq, ncol)
        s = jnp.where(ok, s, NEG)
        p = jnp.where(ok, jnp.exp2(s - jnp.max(s, axis=-1, keepdims=True)), 0.0)
        den = jnp.sum(p, axis=-1, keepdims=True)
        o = jnp.dot(p.reshape(hg * tq, ncol).astype(BF16), v, preferred_element_type=F32).reshape(hg, tq, d)
        o = o / jnp.where(den > 0.0, den, 1.0)
        for h in range(hg):
            o_ref[:, (g * hg + h) * d:(g * hg + h + 1) * d] = o[h]


def window_attend(z, kt, kt_blk, zb, v_blk, nb, T):
    G, d = A_GROUPS, A_HEAD_DIM
    aq = A_HEADS * d
    tq = 128
    assert T % tq == 0 and WINDOW % tq == 0
    nq = T // tq
    nband = WINDOW // tq + 1
    band = lambda i, blk, tr: (lambda b, q: (blk, b * nq + jnp.maximum(q - (nband - 1) + i, 0)) if tr
                               else (b * nq + jnp.maximum(q - (nband - 1) + i, 0), blk))
    in_specs = [pl.BlockSpec((tq, aq), lambda b, q: (b * nq + q, EV_Q))]
    in_specs += [pl.BlockSpec((G * d, tq), band(i, kt_blk, True)) for i in range(nband)]
    in_specs += [pl.BlockSpec((tq, G * d), band(i, v_blk, False)) for i in range(nband)]
    return pl.pallas_call(
        functools.partial(_window_kernel, tq=tq, nband=nband),
        out_shape=jax.ShapeDtypeStruct((nb * T, aq), F32),
        grid=(nb, nq), in_specs=in_specs,
        out_specs=pl.BlockSpec((tq, aq), lambda b, q: (b * nq + q, 0)),
        compiler_params=_cparams(2), name="window_attend")(z, *([kt] * nband), *([zb] * nband))


def _seg_mask(C):
    nblk = C // SUB
    nseg = max(SUB * nblk * (nblk - 1) // 2, SUB)
    pm = np.zeros((C, nseg), np.float32)
    for i in range(1, nblk):
        off = SUB * i * (i - 1) // 2
        pm[SUB * i:SUB * (i + 1), off:off + SUB * i] = 1.0
    return pm


def _rec_kernel(*refs, mode, C, nsub, H, pos_off, has_s0, layer):
    refs = list(refs)
    a_ref, b_ref, v_ref, gate_ref = refs[:4]
    refs = refs[4:]
    aux_ref = refs.pop(0)
    pm_ref = refs.pop(0)
    s0_ref = refs.pop(0) if has_s0 else None
    o_ref, st_ref, st_scr = refs
    c_id = pl.program_id(1)
    K = 128

    @pl.when(c_id == 0)
    def _():
        if has_s0:
            st_scr[...] = s0_ref[0]
        else:
            st_scr[...] = jnp.zeros(st_scr.shape, F32)

    rows = lax.broadcasted_iota(jnp.int32, (C, 1), 0)
    ltri = jnp.where(rows >= lax.broadcasted_iota(jnp.int32, (1, C), 1), 1.0, 0.0).astype(BF16)
    srow = lax.broadcasted_iota(jnp.int32, (SUB, 1), 0)
    if mode == "hgrn":
        x = aux_ref[...]
        ex = jnp.exp(x - jnp.max(x, axis=0, keepdims=True))
        sm = ex / jnp.sum(ex, axis=0, keepdims=True)
        lb_all = jnp.zeros((1, H * K), F32)
        for i in range(1, layer + 1):
            lb_all = lb_all + sm[i:i + 1]
    sts = [st_scr[h] for h in range(H)]
    for sc in range(nsub):
        rsl = slice(sc * C, (sc + 1) * C)
        sts = _rec_chunk(sts, a_ref[rsl, :], b_ref[rsl, :], v_ref[rsl, :], gate_ref[rsl, :], o_ref, rsl,
                         mode=mode, C=C, H=H, pos0=pos_off + (c_id * nsub + sc) * C, aux_ref=aux_ref, pm_ref=pm_ref,
                         lb_all=lb_all if mode == "hgrn" else None, rows=rows, ltri=ltri, srow=srow)
    for h in range(H):
        st_scr[h] = sts[h]

    @pl.when(c_id == pl.num_programs(1) - 1)
    def _():
        st_ref[0] = st_scr[...]


def _rec_chunk(sts, a_all, b_all, v_all, gate_all, o_ref, rsl, *, mode, C, H, pos0, aux_ref, pm_ref, lb_all, rows,
               ltri, srow):
    K = 128
    nblk = C // SUB
    if mode == "ret":
        pos = (pos0 + rows).astype(F32)
        ang = pos * aux_ref[0:1, :]
        cos, sin_s = jnp.cos(ang), jnp.sin(ang) * aux_ref[1:2, :]

        def rope(x):
            return x * cos + pltpu.roll(x, K // 2, 1) * sin_s

        qs = [rope(a_all[:, h * K:(h + 1) * K]) for h in range(H)]
        ks = [rope(b_all[:, h * K:(h + 1) * K]) * B_DK ** -0.5 for h in range(H)]
        gs = [jnp.full((C, K), math.log1p(-2.0 ** (-5.0 - h)), F32) for h in range(H)]
    else:
        qs, ks, gs = [], [], []
        for h in range(H):
            ah, zf, lb = a_all[:, h * K:(h + 1) * K], b_all[:, h * K:(h + 1) * K], lb_all[:, h * K:(h + 1) * K]
            f = lb + (1.0 - lb) * _sigmoid(zf)
            qs.append(ah * _sigmoid(ah))
            ks.append(1.0 - f)
            gs.append(jnp.log(jnp.maximum(f, F_FLOOR)))

    bs_all = _dot_01x(ltri, jnp.concatenate(gs, axis=-1))
    new_sts = []
    for h in range(H):
        q, k, bsum = qs[h], ks[h], bs_all[:, h * K:(h + 1) * K]
        v = v_all[:, h * K:(h + 1) * K]
        st = sts[h]
        o = lax.dot_general((q * jnp.exp(bsum)).astype(BF16), st.astype(BF16), NT, preferred_element_type=F32)
        if nblk > 1:
            rho = jnp.concatenate(
                [jnp.zeros((SUB, K), F32)] +
                [jnp.broadcast_to(bsum[SUB * i - 1:SUB * i], (SUB, K)) for i in range(1, nblk)], axis=0)
            qt = q * jnp.exp(bsum - rho)
            kt = jnp.concatenate(
                [k[:SUB * i] * jnp.exp(bsum[SUB * i - 1:SUB * i] - bsum[:SUB * i]) for i in range(1, nblk)], axis=0)
            vcat = jnp.concatenate([v[:SUB * i] for i in range(1, nblk)], axis=0)
            p = _dot3_nt(qt, kt) * pm_ref[...]
            o += jnp.dot(p.astype(BF16), vcat.astype(BF16), preferred_element_type=F32)
        diag = []
        for i in range(nblk):
            sl = slice(SUB * i, SUB * (i + 1))
            q8, k8, b8, v8 = q[sl], k[sl], bsum[sl], v[sl]
            od = jnp.zeros((SUB, K), F32)
            for s in range(SUB):
                causal = srow >= s
                dec = jnp.exp(jnp.where(causal, b8 - b8[s:s + 1], 0.0))
                att = jnp.sum(q8 * k8[s:s + 1] * dec, axis=-1, keepdims=True)
                od += jnp.where(causal, att, 0.0) * v8[s:s + 1]
            diag.append(od)
        o += jnp.concatenate(diag, axis=0) if nblk > 1 else diag[0]
        bend = bsum[C - 1:C]
        kst = k * jnp.exp(bend - bsum)
        new_sts.append(st * jnp.exp(bend) + lax.dot_general(v.astype(BF16), kst.astype(BF16), TN,
                                                            preferred_element_type=F32))
        gate = gate_all[:, h * K:(h + 1) * K]
        o = o * lax.rsqrt(jnp.mean(o * o, axis=-1, keepdims=True) + EPS) * (gate * _sigmoid(gate))
        o_ref[rsl, h * K:(h + 1) * K] = o.astype(o_ref.dtype)
    return new_sts


def recurrence(z, blks, aux, *, mode, nb, T, pos_off, s0t=None, layer=0):
    H, K = 4, 128
    C = CHUNK if T % CHUNK == 0 else T
    nsub = _pick(T // C, (4, 2, 1))
    CB = nsub * C
    nc = T // CB
    pm = _seg_mask(C)
    nseg = pm.shape[1]
    row = lambda blk: (lambda b, c: (b * nc + c, blk))
    in_specs = [pl.BlockSpec((CB, H * K), row(blks[0])), pl.BlockSpec((CB, H * K), row(blks[1])),
                pl.BlockSpec((CB, H * K), row(blks[2])), pl.BlockSpec((CB, H * K), row(blks[3])),
                pl.BlockSpec(aux.shape, lambda b, c: (0, 0)),
                pl.BlockSpec((C, nseg), lambda b, c: (0, 0))]
    args = [z, z, z, z, aux, jnp.asarray(pm)]
    if s0t is not None:
        in_specs.append(pl.BlockSpec((1, H, K, K), lambda b, c: (b, 0, 0, 0)))
        args.append(s0t)
    return pl.pallas_call(
        functools.partial(_rec_kernel, mode=mode, C=C, nsub=nsub, H=H, pos_off=pos_off, has_s0=s0t is not None,
                          layer=layer),
        out_shape=(jax.ShapeDtypeStruct((nb * T, H * K), BF16), jax.ShapeDtypeStruct((nb, H, K, K), F32)),
        grid=(nb, nc),
        in_specs=in_specs,
        out_specs=(pl.BlockSpec((CB, H * K), lambda b, c: (b * nc + c, 0)),
                   pl.BlockSpec((1, H, K, K), lambda b, c: (b, 0, 0, 0))),
        scratch_shapes=[pltpu.VMEM((H, K, K), F32)],
        compiler_params=_cparams(2), name="recurrence_" + mode)(*args)


def _gather_kernel(pt_ref, cache_ref, *refs, n_pages):
    o_ref = refs[-1]
    p = pl.program_id(1)

    @pl.when(p < n_pages)
    def _():
        o_ref[0, 0] = cache_ref[0, 0]

    if len(refs) == 2:
        @pl.when(p >= n_pages)
        def _():
            o_ref[0, 0] = refs[0][0]


def gather_pages(cache, layer, page_table, tail=None):
    nb, n_pages = page_table.shape
    _, _, ps, C = cache.shape
    n_out = n_pages + (tail is not None)
    in_specs = [pl.BlockSpec((1, 1, ps, C), lambda b, p, pt: (layer, pt[b * n_pages + jnp.minimum(p, n_pages - 1)], 0, 0))]
    args = [page_table.reshape(-1), cache]
    if tail is not None:
        in_specs.append(pl.BlockSpec((1, ps, C), lambda b, p, pt: (b, 0, 0)))
        args.append(tail)
    grid_spec = pltpu.PrefetchScalarGridSpec(
        num_scalar_prefetch=1, grid=(nb, n_out), in_specs=in_specs,
        out_specs=pl.BlockSpec((1, 1, ps, C), lambda b, p, pt: (b, p, 0, 0)))
    return pl.pallas_call(
        functools.partial(_gather_kernel, n_pages=n_pages),
        out_shape=jax.ShapeDtypeStruct((nb, n_out, ps, C), F32),
        grid_spec=grid_spec,
        compiler_params=_cparams(2), name="gather_pages")(*args)


def _page_specs(shape, layer, n_pages, per_step):
    def spec(i):
        return pl.BlockSpec((1, 1) + shape, lambda b, j, pt: (layer, pt[b * n_pages + j * per_step + i], 0, 0))
    return [spec(i) for i in range(per_step)]


def _online_update(s, mk, m_ref, l_ref, acc_ref, pv_fn):
    hg, rows, n = s.shape
    if mk is not None:
        s = jnp.where(mk, s, NEG)
    m_old = m_ref[...]
    m_new = jnp.maximum(m_old, jnp.max(s, axis=-1, keepdims=True))
    alpha = jnp.exp(m_old - m_new)
    p = jnp.exp(s - m_new)
    if mk is not None:
        p = jnp.where(mk, p, 0.0)
    l_ref[...] = alpha * l_ref[...] + jnp.sum(p, axis=-1, keepdims=True)
    pv = pv_fn(p.reshape(hg * rows, n).astype(BF16))
    acc_ref[...] = alpha * acc_ref[...] + pv.reshape(hg, rows, pv.shape[-1])
    m_ref[...] = m_new


def _diff_decode_kernel(pt_ref, q_ref, new_ref, *refs, per_step, H, dh):
    pages = refs[:per_step]
    o_ref, qbd_scr, m_scr, l_scr, acc_scr = refs[per_step:]
    j = pl.program_id(1)
    Ts = q_ref.shape[0]
    dv = 2 * dh
    rows_pp = 2 * H

    @pl.when(j == 0)
    def _():
        m_scr[...] = jnp.full(m_scr.shape, NEG, F32)
        l_scr[...] = jnp.zeros(l_scr.shape, F32)
        acc_scr[...] = jnp.zeros(acc_scr.shape, F32)
        lane = lax.broadcasted_iota(jnp.int32, (Ts, dv), 1)
        for h in range(H):
            q = q_ref[:, h * dv:(h + 1) * dv] * dh ** -0.5
            qbd_scr[h] = jnp.concatenate([jnp.where(lane < dh, q, 0.0), jnp.where(lane >= dh, q, 0.0)],
                                         axis=0).astype(BF16)

    for h in range(H):
        k = jnp.concatenate([pg[0, 0, pl.ds(h, PAGE_SIZE, stride=rows_pp), :] for pg in pages], axis=0).astype(BF16)
        v = jnp.concatenate([pg[0, 0, pl.ds(H + h, PAGE_SIZE, stride=rows_pp), :] for pg in pages], axis=0).astype(BF16)
        s = lax.dot_general(qbd_scr[h], k, NT, preferred_element_type=F32)[None]
        _online_update(s, None, m_scr.at[h], l_scr.at[h], acc_scr.at[h],
                       lambda p: jnp.dot(p, v, preferred_element_type=F32))

    @pl.when(j == pl.num_programs(1) - 1)
    def _():
        r = lax.broadcasted_iota(jnp.int32, (2 * Ts, Ts), 0)
        cidx = lax.broadcasted_iota(jnp.int32, (2 * Ts, Ts), 1)
        causal = (cidx <= jnp.where(r >= Ts, r - Ts, r))[None]
        for h in range(H):
            kn = new_ref[:, h * dv:(h + 1) * dv].astype(BF16)
            vn = new_ref[:, (H + h) * dv:(H + h + 1) * dv].astype(BF16)
            s = lax.dot_general(qbd_scr[h], kn, NT, preferred_element_type=F32)[None]
            _online_update(s, causal, m_scr.at[h], l_scr.at[h], acc_scr.at[h],
                           lambda p: jnp.dot(p, vn, preferred_element_type=F32))
            l = l_scr[h]
            o = acc_scr[h] / jnp.where(l > 0.0, l, 1.0)
            o_ref[:, (2 * h) * dv:(2 * h + 1) * dv] = o[0, :Ts]
            o_ref[:, (2 * h + 1) * dv:(2 * h + 2) * dv] = o[0, Ts:]


def diff_decode(z, cache_rows, layer, page_table, nbs, Ts):
    H, dh = C_HEADS, C_HEAD_DIM
    cw = H * 2 * dh
    n_pages = page_table.shape[1]
    per_step = _pick(n_pages, (16, 8, 4, 2, 1))
    in_specs = [pl.BlockSpec((Ts, cw), lambda b, j, pt: (b, 0)),
                pl.BlockSpec((Ts, 2 * cw), lambda b, j, pt: (b, 0))]
    in_specs += _page_specs(cache_rows.shape[2:], layer, n_pages, per_step)
    grid_spec = pltpu.PrefetchScalarGridSpec(
        num_scalar_prefetch=1, grid=(nbs, n_pages // per_step), in_specs=in_specs,
        out_specs=pl.BlockSpec((Ts, 2 * cw), lambda b, j, pt: (b, 0)),
        scratch_shapes=[pltpu.VMEM((H, 2 * Ts, 2 * dh), BF16), pltpu.VMEM((H, 1, 2 * Ts, 1), F32),
                        pltpu.VMEM((H, 1, 2 * Ts, 1), F32), pltpu.VMEM((H, 1, 2 * Ts, 2 * dh), F32)])
    kv_new = z[:, cw:3 * cw]
    return pl.pallas_call(
        functools.partial(_diff_decode_kernel, per_step=per_step, H=H, dh=dh),
        out_shape=jax.ShapeDtypeStruct((nbs * Ts, 2 * cw), F32),
        grid_spec=grid_spec, compiler_params=_cparams(2),
        name="diff_decode")(page_table.reshape(-1), z, kv_new, *([cache_rows] * per_step))


def _slc_decode_kernel(pt_ref, q_ref, new_ref, sel_ref, *refs, per_step, nbp, past):
    pages = refs[:per_step]
    o_ref, qs_scr, m_scr, l_scr, acc_scr = refs[per_step:]
    G, hg, d = A_GROUPS, A_HEADS // A_GROUPS, A_HEAD_DIM
    j = pl.program_id(1)
    Ts = q_ref.shape[0]
    ntok = per_step * PAGE_SIZE

    @pl.when(j == 0)
    def _():
        m_scr[...] = jnp.full(m_scr.shape, NEG, F32)
        l_scr[...] = jnp.zeros(l_scr.shape, F32)
        acc_scr[...] = jnp.zeros(acc_scr.shape, F32)
        for g in range(G):
            qs = jnp.concatenate([q_ref[:, (g * hg + h) * d:(g * hg + h + 1) * d] for h in range(hg)], axis=0)
            qs_scr[g] = (qs * d ** -0.5).astype(BF16)

    tok = j * ntok + lax.broadcasted_iota(jnp.int32, (1, ntok), 1)
    jb = lax.broadcasted_iota(jnp.int32, (nbp, 1), 0)
    e01 = jnp.where(jb == lax.shift_right_logical(tok, int(math.log2(SLC_LEN))), 1.0, 0.0).astype(BF16)
    for g in range(G):
        kt = jnp.concatenate([pg[0, 0, g * d:(g + 1) * d, :] for pg in pages], axis=1).astype(BF16)
        vt = jnp.concatenate([pg[0, 0, (G + g) * d:(G + g + 1) * d, :] for pg in pages], axis=1).astype(BF16)
        s = jnp.dot(qs_scr[g], kt, preferred_element_type=F32).reshape(hg, Ts, ntok)
        st = jnp.dot(sel_ref[:, g * nbp:(g + 1) * nbp], e01, preferred_element_type=F32)
        _online_update(s, (st > -1.0)[None], m_scr.at[g], l_scr.at[g], acc_scr.at[g],
                       lambda p: lax.dot_general(p, vt, NT, preferred_element_type=F32))

    @pl.when(j == pl.num_programs(1) - 1)
    def _():
        causal = (lax.broadcasted_iota(jnp.int32, (Ts, Ts), 1) <= lax.broadcasted_iota(jnp.int32, (Ts, Ts), 0))[None]
        for g in range(G):
            kn = new_ref[:, g * d:(g + 1) * d].astype(BF16)
            vn = new_ref[:, (G + g) * d:(G + g + 1) * d].astype(BF16)
            s = lax.dot_general(qs_scr[g], kn, NT, preferred_element_type=F32).reshape(hg, Ts, Ts)
            _online_update(s, causal, m_scr.at[g], l_scr.at[g], acc_scr.at[g],
                           lambda p: jnp.dot(p, vn, preferred_element_type=F32))
            l = l_scr[g]
            o = acc_scr[g] / jnp.where(l > 0.0, l, 1.0)
            for h in range(hg):
                o_ref[:, (g * hg + h) * d:(g * hg + h + 1) * d] = o[h]


def slc_decode(z, cache_t, layer, page_table, sel, nbs, Ts, nbp, past):
    G, d = A_GROUPS, A_HEAD_DIM
    aq = A_HEADS * d
    n_pages = page_table.shape[1]
    per_step = _pick(n_pages, (16, 8, 4, 2, 1))
    assert past % SLC_LEN == 0 and Ts <= SLC_LEN
    in_specs = [pl.BlockSpec((Ts, aq), lambda b, j, pt: (b, EV_Q)),
                pl.BlockSpec((Ts, 2 * G * d), lambda b, j, pt: (b, EV_SLC // (2 * G * d))),
                pl.BlockSpec((Ts, G * nbp), lambda b, j, pt: (b, 0))]
    in_specs += _page_specs(cache_t.shape[2:], layer, n_pages, per_step)
    hg = A_HEADS // G
    grid_spec = pltpu.PrefetchScalarGridSpec(
        num_scalar_prefetch=1, grid=(nbs, n_pages // per_step), in_specs=in_specs,
        out_specs=pl.BlockSpec((Ts, aq), lambda b, j, pt: (b, 0)),
        scratch_shapes=[pltpu.VMEM((G, hg * Ts, d), BF16), pltpu.VMEM((G, hg, Ts, 1), F32),
                        pltpu.VMEM((G, hg, Ts, 1), F32), pltpu.VMEM((G, hg, Ts, d), F32)])
    return pl.pallas_call(
        functools.partial(_slc_decode_kernel, per_step=per_step, nbp=nbp, past=past),
        out_shape=jax.ShapeDtypeStruct((nbs * Ts, aq), F32),
        grid_spec=grid_spec, compiler_params=_cparams(2),
        name="slc_decode")(page_table.reshape(-1), z, z, sel, *([cache_t] * per_step))


def _compress_paged_kernel(pt_ref, w_ref, pe_ref, *refs, per_step):
    pages = refs[:per_step]
    o_ref, ab_scr = refs[per_step:]
    j = pl.program_id(1)
    cpp = PAGE_SIZE // CMP_STRIDE
    r = lax.broadcasted_iota(jnp.int32, (PAGE_SIZE, PAGE_SIZE), 0)
    t = lax.broadcasted_iota(jnp.int32, (PAGE_SIZE, PAGE_SIZE), 1)
    assert cpp & (cpp - 1) == 0
    perm = jnp.where(t == (r & (cpp - 1)) * CMP_STRIDE + lax.shift_right_logical(r, int(math.log2(cpp))),
                     1.0, 0.0).astype(BF16)
    xs = [lax.dot_general(perm, pg[0, 0].astype(BF16), NT, preferred_element_type=F32) for pg in pages]
    C = xs[0].shape[1]
    ab = jnp.zeros((per_step * cpp, w_ref.shape[1]), F32)
    for l in range(CMP_STRIDE):
        xl = jnp.concatenate([x[l * cpp:(l + 1) * cpp] for x in xs], axis=0).astype(BF16)
        ab += jnp.dot(xl, w_ref[l * C:(l + 1) * C, :], preferred_element_type=F32)
    n = per_step * cpp
    ab_scr[pl.ds(pl.multiple_of(j * n, n), n), :] = ab

    @pl.when(j == pl.num_programs(1) - 1)
    def _():
        cab = jnp.dot(pe_ref[...].astype(BF16), w_ref[...], preferred_element_type=F32)
        half = ab_scr.shape[1] // 2
        const = cab[0:1, :half] + cab[1:2, half:]
        a, b = ab_scr[:, :half], ab_scr[:, half:]
        o_ref[0] = a + pltpu.roll(b, a.shape[0] - 1, 0) + const


def compress_paged(cache_t, layer, page_table, wbig, pe2):
    nbs, n_pages = page_table.shape
    per_step = _pick(n_pages, (16, 8, 4, 2, 1))
    cpp = PAGE_SIZE // CMP_STRIDE
    n_chunk = n_pages * cpp
    cout = wbig.shape[1] // 2
    in_specs = [pl.BlockSpec(wbig.shape, lambda b, j, pt: (0, 0)), pl.BlockSpec(pe2.shape, lambda b, j, pt: (0, 0))]
    in_specs += _page_specs(cache_t.shape[2:], layer, n_pages, per_step)
    grid_spec = pltpu.PrefetchScalarGridSpec(
        num_scalar_prefetch=1, grid=(nbs, n_pages // per_step), in_specs=in_specs,
        out_specs=pl.BlockSpec((1, n_chunk, cout), lambda b, j, pt: (b, 0, 0)),
        scratch_shapes=[pltpu.VMEM((n_chunk, 2 * cout), F32)])
    return pl.pallas_call(
        functools.partial(_compress_paged_kernel, per_step=per_step),
        out_shape=jax.ShapeDtypeStruct((nbs, n_chunk, cout), F32),
        grid_spec=grid_spec, compiler_params=_cparams(2),
        name="compress_paged")(page_table.reshape(-1), wbig, pe2, *([cache_t] * per_step))


def _even_w_in(w):
    aq, akv = A_HEADS * A_HEAD_DIM, A_GROUPS * A_HEAD_DIM
    splits = np.cumsum([aq] + [akv] * 6 + [3 * A_HEADS] + [512] * 4)[:-1]
    q, kc, vc, ks, vs, kw, vw, gt, rq, rk, rv, rg = jnp.split(w, [int(s) for s in splits], axis=1)
    gt = jnp.pad(gt, ((0, 0), (0, 128 - gt.shape[1])))
    return jnp.concatenate([q, rq, rk, rv, rg, kc, vc, ks, vs, kw, vw, gt], axis=1).astype(BF16)


EV_Q, EV_RQ, EV_RK, EV_RV, EV_RG = 0, 1, 2, 3, 4
EV_KS, EV_VS, EV_KW, EV_VW, EV_GT = 22, 23, 24, 25, 26
EV_CMP, EV_SLC, EV_WIN = 2560, 2816, 3072


def _compress_w(w_cmp, pe):
    G, d = A_GROUPS, A_HEAD_DIM
    r = CMP_LEN // CMP_STRIDE
    assert r == 2
    w = w_cmp.reshape(2, r, CMP_STRIDE, d, d)
    eye_kv = jnp.eye(2, dtype=F32)
    eye_g = jnp.eye(G, dtype=F32)
    big = jnp.einsum("khlde,kq,gp->lqpdhkge", w, eye_kv, eye_g)
    big = big.reshape(CMP_STRIDE * 2 * G * d, r * 2 * G * d).astype(BF16)
    pe_r = pe.reshape(2, r, CMP_STRIDE, d)
    rows = jnp.broadcast_to(pe_r.transpose(1, 2, 0, 3)[:, :, :, None, :], (r, CMP_STRIDE, 2, G, d))
    rows = rows.reshape(r, CMP_STRIDE * 2 * G * d)
    return big, jnp.pad(rows, ((0, 8 - r), (0, 0)))


def _slc_sum_matrix(n_rows, n_cmp, nbp):
    a = np.zeros((n_rows, nbp), np.float32)
    per, left = SLC_LEN // CMP_STRIDE, CMP_LEN // CMP_STRIDE - 1
    for j in range(nbp):
        for n in range(per * j - left, per * j + per):
            if 0 <= n < n_cmp:
                a[n, j] = 1.0
    return jnp.asarray(a, BF16)


def _gate_expand():
    e = np.zeros((128, 3 * A_HEADS * A_HEAD_DIM), np.float32)
    for h in range(A_HEADS):
        for i in range(3):
            e[3 * h + i, i * A_HEADS * A_HEAD_DIM + h * A_HEAD_DIM:i * A_HEADS * A_HEAD_DIM + (h + 1) * A_HEAD_DIM] = 1.0
    return jnp.asarray(e, BF16)


def _rope_aux():
    half = B_DK // 2
    freqs = ROPE_BASE ** (-jnp.arange(half, dtype=F32) / half)
    sign = jnp.concatenate([-jnp.ones((half,), F32), jnp.ones((half,), F32)])
    return jnp.stack([jnp.concatenate([freqs, freqs]), sign])


def _slc_shapes(n_slc, n_cmp_pad):
    nbp = -(-n_slc // 128) * 128
    return nbp, _slc_sum_matrix(n_cmp_pad, n_cmp_pad - 1, nbp)


def _nsa_common(nb, T, q_off):
    return dict(nb=nb, Tq=T, tq=_pick(T, (128, 64, 32, 16, 8)), G=A_GROUPS, hg=A_HEADS // A_GROUPS, dk=A_HEAD_DIM,
                dv=A_HEAD_DIM, g_per_v=1, q_off=q_off)


def _even_layer(xp, xs, p, a, dims, caches):
    B, T, nbs, Ts, past = dims
    g = p["norm_g"]
    w_in = _even_w_in(p["w_in_a"][a])
    wbig, pe2 = _compress_w(p["cmp_w"][a], p["cmp_pos"][a])
    w_out = p["w_out_a"][a].astype(BF16)
    aq = A_HEADS * A_HEAD_DIM
    wa, wb = w_out[:aq], w_out[aq:]
    e01 = _gate_expand()
    aux = _rope_aux()
    akv2 = 2 * A_GROUPS * A_HEAD_DIM

    akv = A_GROUPS * A_HEAD_DIM
    wt = jnp.concatenate([w_in[:, EV_SLC:EV_SLC + akv], w_in[:, EV_WIN:EV_WIN + akv]], axis=1).T
    zp, zpb, ktp = in_proj(xp, g[0:1], w_in, wt)
    kv_cmp_p = zp[:, EV_CMP:EV_CMP + akv2]
    kv_slc_p = zp[:, EV_SLC:EV_SLC + akv2]
    kv_win_p = zp[:, EV_WIN:EV_WIN + akv2]
    kvc_p = compress(kv_cmp_p.reshape(B, T // CMP_STRIDE, CMP_STRIDE * akv2), wbig, pe2)
    tk = _pick(T, (512, 256, 128))
    tw = _pick(T, (128,))
    n_slc = -(-T // SLC_LEN)
    nbp, a01 = _slc_shapes(n_slc, kvc_p.shape[1])
    o_cmp, sel = cmp_attend(zp, EV_Q, kvc_p, a01, B, T, 0, n_slc)
    blk01 = np.zeros((nbp, T), np.float32)
    blk01[np.arange(T) // SLC_LEN, np.arange(T)] = 1.0
    o_slc = attend(zp, EV_Q, ktp, 0, zpb, EV_VS, nb=B, T=T, tq=_pick(T, (256, 128, 64, 32, 16, 8)),
                   tk=_pick(T, (1024, 512, 256, 128)), G=A_GROUPS, hg=A_HEADS // A_GROUPS, dk=A_HEAD_DIM,
                   dv=A_HEAD_DIM, g_per_v=1, mode="causal", sel=sel, e01=jnp.asarray(blk01, BF16))
    o_win = window_attend(zp, ktp, 1, zpb, EV_VW, B, T)
    ob_p, st_p = recurrence(zp, (EV_RQ, EV_RK, EV_RV, EV_RG), aux, mode="ret", nb=B, T=T, pos_off=0)
    xp = even_out(o_cmp, o_slc, o_win, zp, EV_GT, ob_p, e01, wa, wb, g[1:2], xp)

    cache_cmp, cache_slc, win_buf, s0, page_table = caches
    zs = rms_matmul(xs, g[0:1], w_in, F32)
    kv_cmp_s = zs[:, EV_CMP:EV_CMP + akv2]
    kv_slc_s = zs[:, EV_SLC:EV_SLC + akv2]
    kv_win_s = zs[:, EV_WIN:EV_WIN + akv2]
    assert Ts < CMP_STRIDE and past % SLC_LEN == 0
    kvc_s = compress_paged(cache_cmp, a, page_table, wbig, pe2)
    n_slc = -(-(past + Ts) // SLC_LEN)
    nbp, a01 = _slc_shapes(n_slc, kvc_s.shape[1])
    o_cmp, sel = cmp_attend(zs, EV_Q, kvc_s, a01, nbs, Ts, past, n_slc)
    o_slc = slc_decode(zs, cache_slc, a, page_table, sel, nbs, Ts, nbp, past)
    band = jnp.concatenate([win_buf, kv_win_s.reshape(nbs, Ts, akv2)], axis=1)
    n_buf = win_buf.shape[1]
    bt = n_buf + Ts
    o_win = flash(zs, EV_Q, band.reshape(nbs * bt, akv2), 0, band.reshape(nbs * bt, akv2), 1, Tk=bt, tk=bt,
                  mode="window", k_off=past - n_buf, **_nsa_common(nbs, Ts, past))
    ob_s, st_s = recurrence(zs, (EV_RQ, EV_RK, EV_RV, EV_RG), aux, mode="ret", nb=nbs, T=Ts, pos_off=past,
                            s0t=jnp.swapaxes(s0, -1, -2))
    xs = even_out(o_cmp, o_slc, o_win, zs, EV_GT, ob_s, e01, wa, wb, g[1:2], xs)

    shp = lambda x, n, t: x.reshape(n, t, 2, A_GROUPS, A_HEAD_DIM)
    wkeep = min(WINDOW, T)
    outs = (shp(kv_cmp_p, B, T), shp(kv_cmp_s, nbs, Ts), shp(kv_slc_p, B, T), shp(kv_slc_s, nbs, Ts),
            shp(kv_win_p, B, T)[:, T - wkeep:], shp(band[:, Ts:], nbs, n_buf),
            jnp.swapaxes(st_p, -1, -2), jnp.swapaxes(st_s, -1, -2).astype(s0.dtype))
    return xp, xs, outs


def _odd_layer(xp, xs, p, o, layer, dims, caches):
    B, T, nbs, Ts, past = dims
    g = p["norm_g"]
    w_in = p["w_in_c"][o].astype(BF16)
    w_out = p["w_out_c"][o].astype(BF16)
    cw = C_HEADS * 2 * C_HEAD_DIM
    wc, wd = w_out[:cw], w_out[cw:]
    lam_init = 0.8 - 0.6 * math.exp(-0.3 * layer)
    lam_p = p["diff_lam"][o]
    lb_logits = p["hgrn_lb"]
    dcommon = dict(G=2 * C_HEADS, hg=1, dk=C_HEAD_DIM, dv=2 * C_HEAD_DIM, g_per_v=2, mode="causal", k_off=0)

    zp, zpb, ktp = in_proj(xp, g[0:1], w_in, w_in[:, cw:2 * cw].T)
    tq = _pick(T, (512, 256, 128, 64, 32, 16, 8))
    tk = _pick(T, (512, 256, 128))
    o_diff = attend(zp, 0, ktp, 0, zpb, 2, nb=B, T=T, tq=tq, tk=tk, G=2 * C_HEADS, hg=1, dk=C_HEAD_DIM,
                    dv=2 * C_HEAD_DIM, g_per_v=2, mode="causal")
    od_p, st_p = recurrence(zp, (3, 4, 5, 6), lb_logits, mode="hgrn", nb=B, T=T, pos_off=0, layer=o)
    xp = odd_out(o_diff, od_p, lam_p, lam_init, wc, wd, g[1:2], xp)

    cache_diff, s0, page_table = caches
    zs = rms_matmul(xs, g[0:1], w_in, F32)
    kv_s = zs[:, cw:3 * cw]
    o_diff = diff_decode(zs, cache_diff, o, page_table, nbs, Ts)
    od_s, st_s = recurrence(zs, (3, 4, 5, 6), lb_logits, mode="hgrn", nb=nbs, T=Ts, pos_off=past,
                            s0t=jnp.swapaxes(s0, -1, -2), layer=o)
    xs = odd_out(o_diff, od_s, lam_p, lam_init, wc, wd, g[1:2], xs)

    shp = lambda x, n, t: x.reshape(n, t, 2, C_HEADS, 2 * C_HEAD_DIM)
    outs = (shp(zp[:, cw:3 * cw], B, T), shp(kv_s, nbs, Ts),
            jnp.swapaxes(st_p, -1, -2), jnp.swapaxes(st_s, -1, -2).astype(s0.dtype))
    return xp, xs, outs


def _tail_layers(xp, xs, p, layer, dims, mem_prompt, cache_mem):
    B, T, nbs, Ts, _ = dims
    g = p["norm_g"][layer]
    D = xp.shape[1]
    w_q = p["w_xq"][layer].astype(BF16)
    w_kv = p["w_xkv"][layer].astype(BF16)
    w_o = p["w_xo"][layer].astype(BF16)
    w_up = p["w_up"][layer].astype(BF16)
    w_down = p["w_down"][layer].astype(BF16)
    n_mem = mem_prompt.shape[1]
    kv_mem = rms_matmul(mem_prompt.reshape(B * n_mem, D), g[0:1], w_kv, F32, norm=False).reshape(B, n_mem, 2 * D)
    qp = rms_matmul(xp, g[2:3], w_q, BF16)
    xp = proj_res(xattn(qp, kv_mem, T), w_o, g[3:4], xp)
    qs = rms_matmul(xs, g[2:3], w_q, BF16)
    xs = proj_res(xattn(qs, cache_mem, Ts, layer=layer), w_o, g[3:4], xs)
    xp = mlp(xp, g[4:5], g[5:6], w_up, w_down)
    xs = mlp(xs, g[4:5], g[5:6], w_up, w_down)
    return xp, xs, kv_mem.reshape(B, n_mem, 2, X_HEADS, D // X_HEADS)


def kernel(x_prompt, x_sample, cache_nsa_cmp_kv, cache_nsa_slc_kv, cache_nsa_win_kv, state_ret, cache_diff_kv, state_hgrn, cache_mem_kv, page_table, mem_prompt, norm_g, w_in_a, cmp_pos, cmp_w, w_out_a, w_in_c, diff_lam, hgrn_lb, w_out_c, w_xq, w_xkv, w_xo, w_up, w_down):
    B, T, D = x_prompt.shape
    nbs, Ts, _ = x_sample.shape
    depth = norm_g.shape[0]
    n_pages = page_table.shape[1]
    past = n_pages * PAGE_SIZE
    dims = (B, T, nbs, Ts, past)
    p = dict(w_in_a=w_in_a, cmp_pos=cmp_pos, cmp_w=cmp_w, w_out_a=w_out_a, w_in_c=w_in_c, diff_lam=diff_lam,
             hgrn_lb=hgrn_lb.astype(F32), w_out_c=w_out_c, w_xq=w_xq, w_xkv=w_xkv, w_xo=w_xo, w_up=w_up, w_down=w_down)
    xp = x_prompt.reshape(B * T, D)
    xs = x_sample.reshape(nbs * Ts, D)
    feat_major = lambda c: c.transpose(0, 1, 3, 4, 5, 2).reshape(c.shape[0], c.shape[1], -1, c.shape[2])
    rows_of = lambda c: c.reshape(c.shape[0], c.shape[1], -1, c.shape[-1])
    c_cmp, c_slc, c_diff = feat_major(cache_nsa_cmp_kv), feat_major(cache_nsa_slc_kv), rows_of(cache_diff_kv)
    mshape = cache_mem_kv.shape
    c_mem = cache_mem_kv.reshape(mshape[:5] + (mshape[5] // 128, 128)).transpose(0, 1, 2, 3, 5, 4, 6)
    c_mem = c_mem.reshape(mshape[0], mshape[1], -1, 128)
    win_all = cache_nsa_win_kv.reshape(cache_nsa_win_kv.shape[0], nbs, cache_nsa_win_kv.shape[2], -1)
    ev, od, mem = [], [], []
    for layer in range(depth):
        pl_ = dict(p, norm_g=norm_g[layer])
        if layer % 2 == 0:
            a = layer // 2
            xp, xs, outs = _even_layer(xp, xs, pl_, a, dims, (c_cmp, c_slc, win_all[a], state_ret[a], page_table))
            ev.append(outs)
        else:
            o = layer // 2
            xp, xs, outs = _odd_layer(xp, xs, pl_, o, layer, dims, (c_diff, state_hgrn[o], page_table))
            od.append(outs)
        xp, xs, kvm = _tail_layers(xp, xs, dict(p, norm_g=norm_g), layer, dims, mem_prompt, c_mem)
        mem.append(kvm)
    stack = lambda lst, i: jnp.stack([t[i] for t in lst])
    return (xp.reshape(B, T, D), xs.reshape(nbs, Ts, D),
            stack(ev, 0), stack(ev, 1), stack(ev, 2), stack(ev, 3), stack(ev, 4), stack(ev, 5), stack(ev, 6), stack(ev, 7),
            stack(od, 0), stack(od, 1), stack(od, 2), stack(od, 3), jnp.stack(mem))
```

```python
import functools
import math

import numpy as np
import jax
import jax.numpy as jnp
from jax import lax
from jax.experimental import pallas as pl
from jax.experimental.pallas import tpu as pltpu

F32 = jnp.float32
BF16 = jnp.bfloat16

EPS = 1e-6
NEG = -1e30
BIG = 1e9
A_HEADS, A_GROUPS, A_HEAD_DIM = 8, 2, 64
CMP_LEN, CMP_STRIDE, SLC_LEN, SLC_TOPK, WINDOW = 32, 16, 64, 16, 512
B_HEADS, B_DK = 4, 128
ROPE_BASE = 10000.0
CHUNK = 64
C_HEADS, C_HEAD_DIM = 4, 64
D_HEADS = 4
F_FLOOR = 1e-6
X_HEADS = 4
PAGE_SIZE = 128
SUB = 8
LANES = 128

VMEM_LIMIT = 56 * 1024 * 1024

NT = (((1,), (1,)), ((), ()))
TN = (((0,), (0,)), ((), ()))


def _cparams(n_grid):
    return pltpu.CompilerParams(dimension_semantics=("arbitrary",) * n_grid, vmem_limit_bytes=VMEM_LIMIT)


def _pick(n, cands):
    for c in cands:
        if n % c == 0:
            return c
    return n


def _split3(x):
    hi = x.astype(BF16)
    r = x - hi.astype(F32)
    mid = r.astype(BF16)
    lo = (r - mid.astype(F32)).astype(BF16)
    return hi, mid, lo


def _dot_x01(x, m01):
    return sum(jnp.dot(p, m01, preferred_element_type=F32) for p in _split3(x))


def _dot_01x(m01, x):
    return sum(jnp.dot(m01, p, preferred_element_type=F32) for p in _split3(x))


def _dot3_nt(a, b):
    ah = a.astype(BF16)
    al = (a - ah.astype(F32)).astype(BF16)
    bh = b.astype(BF16)
    bl = (b - bh.astype(F32)).astype(BF16)
    d = lambda x, y: lax.dot_general(x, y, NT, preferred_element_type=F32)
    return d(ah, bh) + d(ah, bl) + d(al, bh)


def _sigmoid(x):
    return 1.0 / (1.0 + jnp.exp(-x))


def _rms(x, g):
    return x * lax.rsqrt(jnp.mean(x * x, axis=-1, keepdims=True) + EPS) * g


def _rms_matmul_kernel(x_ref, g_ref, w_ref, o_ref, *, norm):
    x = x_ref[...]
    if norm:
        x = _rms(x, g_ref[...])
    o_ref[...] = jnp.dot(x.astype(BF16), w_ref[...], preferred_element_type=F32).astype(o_ref.dtype)


def rms_matmul(x, g, w, out_dtype, norm=True):
    R, D = x.shape
    N = w.shape[1]
    tm = _pick(R, (512, 256, 128, 64, 32, 16, 8))
    tn = _pick(N, (1792, 1152, 1024, 896, 768, 640, 512, 384, 256, 128))
    return pl.pallas_call(
        functools.partial(_rms_matmul_kernel, norm=norm),
        out_shape=jax.ShapeDtypeStruct((R, N), out_dtype),
        grid=(R // tm, N // tn),
        in_specs=[pl.BlockSpec((tm, D), lambda i, j: (i, 0)),
                  pl.BlockSpec((1, D), lambda i, j: (0, 0)),
                  pl.BlockSpec((D, tn), lambda i, j: (0, j))],
        out_specs=pl.BlockSpec((tm, tn), lambda i, j: (i, j)),
        compiler_params=_cparams(2), name="rms_matmul")(x, g, w)


def _in_proj_kernel(x_ref, g_ref, w_ref, wt_ref, ws_ref, z_ref, zb_ref, kt_ref, xn_scr):
    j = pl.program_id(1)

    @pl.when(j == 0)
    def _():
        xn = _rms(x_ref[...], g_ref[...]).astype(BF16)
        xn_scr[...] = xn
        t = lax.dot_general(wt_ref[...], xn, NT, preferred_element_type=F32)
        kt_ref[...] = (t * ws_ref[...]).astype(kt_ref.dtype)

    z = jnp.dot(xn_scr[...], w_ref[...], preferred_element_type=F32)
    z_ref[...] = z
    zb_ref[...] = z.astype(BF16)


def in_proj(x, g, w, wt, wscale):
    R, D = x.shape
    N = w.shape[1]
    Fk = wt.shape[0]
    tm = _pick(R, (1024, 512, 256, 128, 64, 32, 16, 8))
    tn = _pick(N, (1792, 1152, 1024, 896, 768, 640, 512, 384, 256, 128))
    return pl.pallas_call(
        _in_proj_kernel,
        out_shape=(jax.ShapeDtypeStruct((R, N), F32), jax.ShapeDtypeStruct((R, N), BF16),
                   jax.ShapeDtypeStruct((Fk, R), BF16)),
        grid=(R // tm, N // tn),
        in_specs=[pl.BlockSpec((tm, D), lambda i, j: (i, 0)),
                  pl.BlockSpec((1, D), lambda i, j: (0, 0)),
                  pl.BlockSpec((D, tn), lambda i, j: (0, j)),
                  pl.BlockSpec((Fk, D), lambda i, j: (0, 0)),
                  pl.BlockSpec((Fk, 1), lambda i, j: (0, 0))],
        out_specs=(pl.BlockSpec((tm, tn), lambda i, j: (i, j)), pl.BlockSpec((tm, tn), lambda i, j: (i, j)),
                   pl.BlockSpec((Fk, tm), lambda i, j: (0, i))),
        scratch_shapes=[pltpu.VMEM((tm, D), BF16)],
        compiler_params=_cparams(2), name="in_proj")(x, g, w, wt, wscale)


def _mlp_kernel(x_ref, g4_ref, g5_ref, wu_ref, wd_ref, o_ref, xn_scr, acc_scr):
    j = pl.program_id(1)

    @pl.when(j == 0)
    def _():
        xn_scr[...] = _rms(x_ref[...], g4_ref[...]).astype(BF16)
        acc_scr[...] = jnp.zeros_like(acc_scr)

    h = jnp.dot(xn_scr[...], wu_ref[...], preferred_element_type=F32)
    h = jnp.square(jnp.maximum(h, 0.0))
    acc_scr[...] += jnp.dot(h.astype(BF16), wd_ref[...], preferred_element_type=F32)

    @pl.when(j == pl.num_programs(1) - 1)
    def _():
        o_ref[...] = x_ref[...] + _rms(acc_scr[...], g5_ref[...])


def mlp(x, g4, g5, w_up, w_down):
    R, D = x.shape
    F = w_up.shape[1]
    tm = _pick(R, (1024, 512, 256, 128, 64, 32, 16, 8))
    tf = _pick(F, (1024, 512, 256, 128))
    return pl.pallas_call(
        _mlp_kernel,
        out_shape=jax.ShapeDtypeStruct((R, D), F32),
        grid=(R // tm, F // tf),
        in_specs=[pl.BlockSpec((tm, D), lambda i, j: (i, 0)),
                  pl.BlockSpec((1, D), lambda i, j: (0, 0)),
                  pl.BlockSpec((1, D), lambda i, j: (0, 0)),
                  pl.BlockSpec((D, tf), lambda i, j: (0, j)),
                  pl.BlockSpec((tf, D), lambda i, j: (j, 0))],
        out_specs=pl.BlockSpec((tm, D), lambda i, j: (i, 0)),
        scratch_shapes=[pltpu.VMEM((tm, D), BF16), pltpu.VMEM((tm, D), F32)],
        compiler_params=_cparams(2), name="mlp")(x, g4, g5, w_up, w_down)


def _proj_res_kernel(a_ref, w_ref, g_ref, x_ref, o_ref):
    y = jnp.dot(a_ref[...], w_ref[...], preferred_element_type=F32)
    o_ref[...] = x_ref[...] + _rms(y, g_ref[...])


def proj_res(a, w, g, x):
    R, D = x.shape
    K = a.shape[1]
    tm = _pick(R, (512, 256, 128, 64, 32, 16, 8))
    return pl.pallas_call(
        _proj_res_kernel,
        out_shape=jax.ShapeDtypeStruct((R, D), F32),
        grid=(R // tm,),
        in_specs=[pl.BlockSpec((tm, K), lambda i: (i, 0)),
                  pl.BlockSpec((K, D), lambda i: (0, 0)),
                  pl.BlockSpec((1, D), lambda i: (0, 0)),
                  pl.BlockSpec((tm, D), lambda i: (i, 0))],
        out_specs=pl.BlockSpec((tm, D), lambda i: (i, 0)),
        compiler_params=_cparams(1), name="proj_res")(a, w, g, x)


def _even_out_kernel(oc_ref, os_ref, ow_ref, gt_ref, ob_ref, e_ref, wa_ref, wb_ref, g_ref, x_ref, o_ref):
    gates = _sigmoid(gt_ref[...])
    ge = _dot_x01(gates, e_ref[...])
    aq = oc_ref.shape[1]
    oa = ge[:, :aq] * oc_ref[...] + ge[:, aq:2 * aq] * os_ref[...] + ge[:, 2 * aq:] * ow_ref[...]
    y = jnp.dot(oa.astype(BF16), wa_ref[...], preferred_element_type=F32)
    y += jnp.dot(ob_ref[...], wb_ref[...], preferred_element_type=F32)
    o_ref[...] = x_ref[...] + _rms(y, g_ref[...])


def even_out(o_cmp, o_slc, o_win, z, gt_blk, o_b, e01, wa, wb, g, x):
    R, D = x.shape
    aq = o_cmp.shape[1]
    tm = _pick(R, (256, 128, 64, 32, 16, 8))
    row = lambda i: (i, 0)
    fix = lambda i: (0, 0)
    return pl.pallas_call(
        _even_out_kernel,
        out_shape=jax.ShapeDtypeStruct((R, D), F32),
        grid=(R // tm,),
        in_specs=[pl.BlockSpec((tm, aq), row), pl.BlockSpec((tm, aq), row), pl.BlockSpec((tm, aq), row),
                  pl.BlockSpec((tm, 128), lambda i: (i, gt_blk)),
                  pl.BlockSpec((tm, o_b.shape[1]), row),
                  pl.BlockSpec(e01.shape, fix), pl.BlockSpec(wa.shape, fix), pl.BlockSpec(wb.shape, fix),
                  pl.BlockSpec((1, D), fix), pl.BlockSpec((tm, D), row)],
        out_specs=pl.BlockSpec((tm, D), row),
        compiler_params=_cparams(1), name="even_out")(o_cmp, o_slc, o_win, z, o_b, e01, wa, wb, g, x)


def _odd_out_kernel(oc_ref, od_ref, lam_ref, wc_ref, wd_ref, g_ref, x_ref, o_ref, *, lam_init, heads, dv):
    lp = lam_ref[...]
    lam = (jnp.exp(jnp.sum(lp[0:1] * lp[1:2], axis=-1, keepdims=True))
           - jnp.exp(jnp.sum(lp[2:3] * lp[3:4], axis=-1, keepdims=True)) + lam_init)
    parts = []
    for h in range(heads):
        o1 = oc_ref[:, (2 * h) * dv:(2 * h + 1) * dv]
        o2 = oc_ref[:, (2 * h + 1) * dv:(2 * h + 2) * dv]
        o = o1 - lam * o2
        o = o * lax.rsqrt(jnp.mean(o * o, axis=-1, keepdims=True) + EPS) * (1.0 - lam_init)
        parts.append(o.astype(BF16))
    oc = jnp.concatenate(parts, axis=-1)
    y = jnp.dot(oc, wc_ref[...], preferred_element_type=F32)
    y += jnp.dot(od_ref[...], wd_ref[...], preferred_element_type=F32)
    o_ref[...] = x_ref[...] + _rms(y, g_ref[...])


def odd_out(o_diff, o_d, lam_p, lam_init, wc, wd, g, x):
    R, D = x.shape
    tm = _pick(R, (256, 128, 64, 32, 16, 8))
    row = lambda i: (i, 0)
    fix = lambda i: (0, 0)
    return pl.pallas_call(
        functools.partial(_odd_out_kernel, lam_init=lam_init, heads=C_HEADS, dv=2 * C_HEAD_DIM),
        out_shape=jax.ShapeDtypeStruct((R, D), F32),
        grid=(R // tm,),
        in_specs=[pl.BlockSpec((tm, o_diff.shape[1]), row), pl.BlockSpec((tm, o_d.shape[1]), row),
                  pl.BlockSpec(lam_p.shape, fix), pl.BlockSpec(wc.shape, fix), pl.BlockSpec(wd.shape, fix),
                  pl.BlockSpec((1, D), fix), pl.BlockSpec((tm, D), row)],
        out_specs=pl.BlockSpec((tm, D), row),
        compiler_params=_cparams(1), name="odd_out")(o_diff, o_d, lam_p, wc, wd, g, x)


def _xattn_kernel(q_ref, kv_ref, o_ref, *, heads, scale, rows):
    dm = q_ref.shape[1]
    hd = dm // heads
    for h in range(heads):
        qh = q_ref[:, h * hd:(h + 1) * hd]
        if rows:
            nh = hd // 128
            per_tok = 2 * nh * heads
            n_mem = kv_ref.shape[2] // per_tok
            row = lambda slot: jnp.concatenate(
                [kv_ref[0, 0, pl.ds((slot * nh + i) * heads + h, n_mem, stride=per_tok), :] for i in range(nh)],
                axis=-1).astype(BF16)
            kh, vh = row(0), row(1)
        else:
            kh = kv_ref[0, :, h * hd:(h + 1) * hd].astype(BF16)
            vh = kv_ref[0, :, dm + h * hd:dm + (h + 1) * hd].astype(BF16)
        s = lax.dot_general(qh, kh, NT, preferred_element_type=F32) * scale
        e = jnp.exp(s - jnp.max(s, axis=-1, keepdims=True))
        p = e / jnp.sum(e, axis=-1, keepdims=True)
        o = jnp.dot(p.astype(BF16), vh, preferred_element_type=F32)
        o_ref[:, h * hd:(h + 1) * hd] = o.astype(o_ref.dtype)


def xattn(q, kv, rows_per_batch, layer=None):
    R, D = q.shape
    tq = _pick(rows_per_batch, (512, 256, 128, 64, 32, 16, 8))
    per = rows_per_batch // tq
    if layer is None:
        kv_spec = pl.BlockSpec((1,) + kv.shape[1:], lambda i: (i // per, 0, 0))
    else:
        kv_spec = pl.BlockSpec((1, 1) + kv.shape[2:], lambda i: (layer, i // per, 0, 0))
    return pl.pallas_call(
        functools.partial(_xattn_kernel, heads=X_HEADS, scale=(D // X_HEADS) ** -0.5, rows=layer is not None),
        out_shape=jax.ShapeDtypeStruct((R, D), BF16),
        grid=(R // tq,),
        in_specs=[pl.BlockSpec((tq, D), lambda i: (i, 0)), kv_spec],
        out_specs=pl.BlockSpec((tq, D), lambda i: (i, 0)),
        compiler_params=_cparams(1), name="xattn")(q, kv)


def _compress_kernel(x_ref, w_ref, pe_ref, o_ref, vt_ref):
    x = x_ref[0].astype(BF16)
    ab = jnp.dot(x, w_ref[...], preferred_element_type=F32)
    cab = jnp.dot(pe_ref[...].astype(BF16), w_ref[...], preferred_element_type=F32)
    half = ab.shape[1] // 2
    a, b = ab[:, :half], ab[:, half:]
    const = cab[0:1, :half] + cab[1:2, half:]
    n = a.shape[0]
    out = a + pltpu.roll(b, n - 1, 0) + const
    o_ref[0] = out
    vt_ref[0] = out[:, half // 2:].T.astype(vt_ref.dtype)


def compress(xc, wbig, pe2):
    nb, n_chunk, kdim = xc.shape
    cout = wbig.shape[1] // 2
    return pl.pallas_call(
        _compress_kernel,
        out_shape=(jax.ShapeDtypeStruct((nb, n_chunk, cout), F32), jax.ShapeDtypeStruct((nb, cout // 2, n_chunk), BF16)),
        grid=(nb,),
        in_specs=[pl.BlockSpec((1, n_chunk, kdim), lambda b: (b, 0, 0)),
                  pl.BlockSpec(wbig.shape, lambda b: (0, 0)),
                  pl.BlockSpec(pe2.shape, lambda b: (0, 0))],
        out_specs=(pl.BlockSpec((1, n_chunk, cout), lambda b: (b, 0, 0)),
                   pl.BlockSpec((1, cout // 2, n_chunk), lambda b: (b, 0, 0))),
        compiler_params=_cparams(1), name="compress")(xc, wbig, pe2)


def _cmp_kernel(q_ref, kv_ref, a_ref, o_ref, sel_ref, *, tq, q_off, n_slc, nbp):
    G, hg, d = A_GROUPS, A_HEADS // A_GROUPS, A_HEAD_DIM
    qi = pl.program_id(1)
    ncmp = kv_ref.shape[1]
    pos_q = q_off + qi * tq + lax.broadcasted_iota(jnp.int32, (tq, 1), 0)
    cmp_end = lax.broadcasted_iota(jnp.int32, (1, ncmp), 1) * CMP_STRIDE + (CMP_LEN - 1)
    mask3 = (cmp_end <= pos_q)[None]
    jb = lax.broadcasted_iota(jnp.int32, (1, nbp), 1)
    blk_q = lax.shift_right_logical(pos_q, int(math.log2(SLC_LEN)))
    valid = jb <= blk_q
    forced = valid & ((jb == 0) | (jb >= blk_q - 1))
    k_top = min(SLC_TOPK, n_slc)
    scores = []
    for g in range(G):
        kc = kv_ref[0, :, g * d:(g + 1) * d].astype(BF16)
        vc = kv_ref[0, :, (G + g) * d:(G + g + 1) * d].astype(BF16)
        qs = jnp.concatenate([q_ref[:, (g * hg + h) * d:(g * hg + h + 1) * d] for h in range(hg)], axis=0)
        qs = (qs * d ** -0.5).astype(BF16)
        s = lax.dot_general(qs, kc, NT, preferred_element_type=F32).reshape(hg, tq, ncmp)
        s = jnp.where(mask3, s, NEG)
        e = jnp.where(mask3, jnp.exp(s - jnp.max(s, axis=-1, keepdims=True)), 0.0)
        den = jnp.sum(e, axis=-1, keepdims=True)
        p = e / jnp.where(den > 0.0, den, 1.0)
        o = jnp.dot(p.reshape(hg * tq, ncmp).astype(BF16), vc, preferred_element_type=F32)
        for h in range(hg):
            o_ref[:, (g * hg + h) * d:(g * hg + h + 1) * d] = o[h * tq:(h + 1) * tq]
        p_slc = _dot_x01(jnp.sum(p, axis=0), a_ref[...])
        score = jnp.where(forced, BIG, jnp.where(valid, p_slc, -BIG))
        scores.append(jnp.where(jb < n_slc, score, -jnp.inf))
    score = jnp.concatenate(scores, axis=0)
    sel = jnp.zeros((G * tq, nbp), F32)
    jbf = jb.astype(F32)
    for _ in range(k_top):
        mx = jnp.max(score, axis=-1, keepdims=True)
        jm = jnp.min(jnp.where(score == mx, jbf, float(nbp)), axis=-1, keepdims=True)
        hit = jbf == jm
        sel = jnp.where(hit, 1.0, sel)
        score = jnp.where(hit, -jnp.inf, score)
    for g in range(G):
        ok = valid & (sel[g * tq:(g + 1) * tq] > 0.5)
        sel_ref[:, g * nbp:(g + 1) * nbp] = jnp.where(ok, 0.0, SEL_OFF).astype(BF16)


def cmp_attend(z, q_blk, kvc, a01, nb, Tq, q_off, n_slc):
    aq = A_HEADS * A_HEAD_DIM
    nbp = a01.shape[1]
    tq = _pick(Tq, (256, 128, 64, 32, 16, 8))
    nq = Tq // tq
    ncmp = kvc.shape[1]
    return pl.pallas_call(
        functools.partial(_cmp_kernel, tq=tq, q_off=q_off, n_slc=n_slc, nbp=nbp),
        out_shape=(jax.ShapeDtypeStruct((nb * Tq, aq), F32),
                   jax.ShapeDtypeStruct((nb * Tq, A_GROUPS * nbp), BF16)),
        grid=(nb, nq),
        in_specs=[pl.BlockSpec((tq, aq), lambda b, i: (b * nq + i, q_blk)),
                  pl.BlockSpec((1, ncmp, kvc.shape[2]), lambda b, i: (b, 0, 0)),
                  pl.BlockSpec(a01.shape, lambda b, i: (0, 0))],
        out_specs=(pl.BlockSpec((tq, aq), lambda b, i: (b * nq + i, 0)),
                   pl.BlockSpec((tq, A_GROUPS * nbp), lambda b, i: (b * nq + i, 0))),
        compiler_params=_cparams(2), name="cmp_attend")(z, kvc, a01)


class _FlashCfg:
    def __init__(self, **kw):
        self.__dict__.update(kw)


def _tile_range(c, qi):
    q_lo = c.q_off + qi * c.tq
    last = (q_lo + c.tq - 1 - c.k_off) // c.tk
    if c.mode == "window":
        first = (q_lo - (WINDOW - 1) - c.k_off) // c.tk
    else:
        first = 0 * qi
    return first, last


def _flash_kernel(*refs, c):
    if c.sel:
        q_ref, k_ref, v_ref, sel_ref, o_ref, qs_scr, m_scr, l_scr, acc_scr = refs
    else:
        q_ref, k_ref, v_ref, o_ref, qs_scr, m_scr, l_scr, acc_scr = refs
        sel_ref = None
    qi, j = pl.program_id(1), pl.program_id(2)
    tq, tk, G, hg, dk, dv = c.tq, c.tk, c.G, c.hg, c.dk, c.dv

    @pl.when(j == 0)
    def _():
        m_scr[...] = jnp.full(m_scr.shape, NEG, F32)
        l_scr[...] = jnp.zeros(l_scr.shape, F32)
        acc_scr[...] = jnp.zeros(acc_scr.shape, F32)
        for g in range(G):
            qs = jnp.concatenate([q_ref[:, (g * hg + h) * dk:(g * hg + h + 1) * dk] for h in range(hg)], axis=0)
            qs_scr[g] = (qs * dk ** -0.5).astype(BF16)

    first, last = _tile_range(c, qi)
    jabs = first + j

    @pl.when((jabs >= 0) & (jabs <= jnp.minimum(last, c.nk - 1)))
    def _():
        pos_q = c.q_off + qi * tq + lax.broadcasted_iota(jnp.int32, (tq, 1), 0)
        tok = jabs * tk + lax.broadcasted_iota(jnp.int32, (1, tk), 1)
        dpos = pos_q - (c.k_off + tok)
        mask = dpos >= 0
        if c.mode == "window":
            mask = mask & (dpos < WINDOW)
        if c.sel:
            jb = lax.broadcasted_iota(jnp.int32, (c.nbp, 1), 0)
            e01 = jnp.where(jb == lax.shift_right_logical(tok, int(math.log2(SLC_LEN))), 1.0, 0.0).astype(BF16)
        for g in range(G):
            kg = k_ref[:, g * dk:(g + 1) * dk].astype(BF16)
            vi = g // c.g_per_v
            vg = v_ref[:, vi * dv:(vi + 1) * dv].astype(BF16)
            s = lax.dot_general(qs_scr[g], kg, NT, preferred_element_type=F32).reshape(hg, tq, tk)
            mk = mask
            if c.sel:
                st = jnp.dot(sel_ref[:, g * c.nbp:(g + 1) * c.nbp], e01, preferred_element_type=F32)
                mk = mk & (st > 0.5)
            mk = mk[None]
            s = jnp.where(mk, s, NEG)
            m_old = m_scr[g]
            m_new = jnp.maximum(m_old, jnp.max(s, axis=-1, keepdims=True))
            alpha = jnp.exp(m_old - m_new)
            p = jnp.where(mk, jnp.exp(s - m_new), 0.0)
            l_scr[g] = alpha * l_scr[g] + jnp.sum(p, axis=-1, keepdims=True)
            pv = jnp.dot(p.reshape(hg * tq, tk).astype(BF16), vg, preferred_element_type=F32)
            acc_scr[g] = alpha * acc_scr[g] + pv.reshape(hg, tq, dv)
            m_scr[g] = m_new

    @pl.when(j == pl.num_programs(2) - 1)
    def _():
        for g in range(G):
            l = l_scr[g]
            o = acc_scr[g] / jnp.where(l > 0.0, l, 1.0)
            for h in range(hg):
                o_ref[:, (g * hg + h) * dv:(g * hg + h + 1) * dv] = o[h].astype(o_ref.dtype)


def flash(q2d, q_blk, k2d, k_blk, v2d, v_blk, *, nb, Tq, Tk, tq, tk, G, hg, dk, dv, g_per_v, mode,
          q_off, k_off, sel=None, nbp=0):
    nq, nk = Tq // tq, Tk // tk
    assert Tq % tq == 0 and Tk % tk == 0
    c = _FlashCfg(tq=tq, tk=tk, G=G, hg=hg, dk=dk, dv=dv, g_per_v=g_per_v, mode=mode, q_off=q_off, k_off=k_off,
                  nk=nk, sel=sel is not None, nbp=nbp)
    steps = max(min(_tile_range(c, i)[1], nk - 1) - _tile_range(c, i)[0] + 1 for i in range(nq))
    qw, kw, vw = G * hg * dk, G * dk, (G // g_per_v) * dv

    def kv_map(blk):
        def f(b, i, j):
            first, last = _tile_range(c, i)
            return (b * nk + jnp.clip(first + j, 0, jnp.minimum(last, nk - 1)), blk)
        return f

    in_specs = [pl.BlockSpec((tq, qw), lambda b, i, j: (b * nq + i, q_blk)),
                pl.BlockSpec((tk, kw), kv_map(k_blk)),
                pl.BlockSpec((tk, vw), kv_map(v_blk))]
    args = [q2d, k2d, v2d]
    if sel is not None:
        in_specs.append(pl.BlockSpec((tq, G * nbp), lambda b, i, j: (b * nq + i, 0)))
        args.append(sel)
    return pl.pallas_call(
        functools.partial(_flash_kernel, c=c),
        out_shape=jax.ShapeDtypeStruct((nb * Tq, G * hg * dv), F32),
        grid=(nb, nq, steps),
        in_specs=in_specs,
        out_specs=pl.BlockSpec((tq, G * hg * dv), lambda b, i, j: (b * nq + i, 0)),
        scratch_shapes=[pltpu.VMEM((G, hg * tq, dk), BF16), pltpu.VMEM((G, hg, tq, 1), F32),
                        pltpu.VMEM((G, hg, tq, 1), F32), pltpu.VMEM((G, hg, tq, dv), F32)],
        compiler_params=_cparams(3), name="flash_" + mode + ("_sel" if sel is not None else ""))(*args)


LOG2E = 1.4426950408889634
SEL_OFF = -2.0 ** 30


def _attn_kernel(tab_ref, q_ref, kt_ref, v_ref, *refs, c):
    if c.sel:
        sel_ref, e01_ref = refs[:2]
        refs = refs[2:]
    o_ref, qs_scr, s_scr, p_scr, m_scr, l_scr, a_scr, acc_scr = refs
    n = pl.program_id(1)
    qi, kj, flags = tab_ref[0, n], tab_ref[1, n], tab_ref[2, n]
    tq, tk, G, hg, dk, dv = c.tq, c.tk, c.G, c.hg, c.dk, c.dv
    rows = hg * tq
    ncb = tk // LANES

    @pl.when((flags & 1) != 0)
    def _():
        m_scr[...] = jnp.full(m_scr.shape, NEG, F32)
        l_scr[...] = jnp.zeros(l_scr.shape, F32)
        acc_scr[...] = jnp.zeros(acc_scr.shape, F32)
        for g in range(G):
            qs = jnp.concatenate([q_ref[:, (g * hg + h) * dk:(g * hg + h + 1) * dk] for h in range(hg)], axis=0)
            qs_scr[g] = (qs * (dk ** -0.5 * LOG2E)).astype(BF16)

    def step(masked):
        if masked:
            dq = qi * tq - kj * tk + c.q_off - c.k_off
            d = (lax.broadcasted_iota(jnp.int32, (tq, tk), 0) - lax.broadcasted_iota(jnp.int32, (tq, tk), 1)) + dq
            ok = d >= 0
            if c.mode == "window":
                ok = ok & (d < WINDOW)
        for g in range(G):
            s = jnp.dot(qs_scr[g], kt_ref[g * dk:(g + 1) * dk, :], preferred_element_type=F32)
            if c.sel or masked:
                s = s.reshape(hg, tq, tk)
                if c.sel:
                    s = s + jnp.dot(sel_ref[:, g * c.nbp:(g + 1) * c.nbp], e01_ref[...], preferred_element_type=F32)[None]
                if masked:
                    s = jnp.where(ok[None], s, NEG)
                s = s.reshape(rows, tk)
            s_scr[g] = s
        for g in range(G):
            cols = [s_scr[g, :, cb * LANES:(cb + 1) * LANES] for cb in range(ncb)]
            mx = cols[0]
            for x in cols[1:]:
                mx = jnp.maximum(mx, x)
            m_old = m_scr[g]
            m_new = jnp.maximum(m_old, jnp.broadcast_to(jnp.max(mx, axis=-1, keepdims=True), (rows, LANES)))
            alpha = jnp.exp2(m_old - m_new)
            psum = None
            for cb in range(ncb):
                p = jnp.exp2(cols[cb] - m_new)
                if masked:
                    okc = jnp.broadcast_to(ok[None, :, cb * LANES:(cb + 1) * LANES], (hg, tq, LANES)).reshape(rows, LANES)
                    p = jnp.where(okc, p, 0.0)
                psum = p if psum is None else psum + p
                p_scr[g, :, cb * LANES:(cb + 1) * LANES] = p.astype(BF16)
            l_scr[g] = alpha * l_scr[g] + psum
            m_scr[g] = m_new
            a_scr[g] = alpha
        for g in range(G):
            vi = g // c.g_per_v
            pv = jnp.dot(p_scr[g], v_ref[:, vi * dv:(vi + 1) * dv], preferred_element_type=F32)
            acc_scr[g] = a_scr[g, :, :dv] * acc_scr[g] + pv

    @pl.when((flags & 4) != 0)
    def _():
        step(True)

    @pl.when((flags & 4) == 0)
    def _():
        step(False)

    @pl.when((flags & 2) != 0)
    def _():
        for g in range(G):
            l = jnp.sum(l_scr[g], axis=-1, keepdims=True)
            o = acc_scr[g] / jnp.where(l > 0.0, l, 1.0)
            for h in range(hg):
                o_ref[:, (g * hg + h) * dv:(g * hg + h + 1) * dv] = o[h * tq:(h + 1) * tq].astype(o_ref.dtype)


def attend(z, q_blk, kt, kt_blk, zb, v_blk, *, nb, T, tq, tk, G, hg, dk, dv, g_per_v, mode, sel=None, e01=None):
    nq, nk = T // tq, T // tk
    c = _FlashCfg(tq=tq, tk=tk, G=G, hg=hg, dk=dk, dv=dv, g_per_v=g_per_v, mode=mode, q_off=0, k_off=0, nk=nk,
                  sel=sel is not None, nbp=0 if sel is None else sel.shape[1] // G)
    tab = []
    for i in range(nq):
        first, last = _tile_range(c, i)
        first, last = max(first, 0), min(last, nk - 1)
        for j in range(first, last + 1):
            lo, hi = i * tq - (j * tk + tk - 1), i * tq + tq - 1 - j * tk
            masked = lo < 0 or (mode == "window" and hi >= WINDOW)
            tab.append((i, j, (j == first) * 1 + (j == last) * 2 + masked * 4))
    tab = jnp.asarray(np.array(tab, np.int32).T)
    npairs = tab.shape[1]
    rows = hg * tq
    in_specs = [pl.BlockSpec((tq, G * hg * dk), lambda b, n, t: (b * nq + t[0, n], q_blk)),
                pl.BlockSpec((G * dk, tk), lambda b, n, t: (kt_blk, b * nk + t[1, n])),
                pl.BlockSpec((tk, (G // g_per_v) * dv), lambda b, n, t: (b * nk + t[1, n], v_blk))]
    args = [z, kt, zb]
    if sel is not None:
        in_specs += [pl.BlockSpec((tq, sel.shape[1]), lambda b, n, t: (b * nq + t[0, n], 0)),
                     pl.BlockSpec((e01.shape[0], tk), lambda b, n, t: (0, t[1, n]))]
        args += [sel, e01]
    grid_spec = pltpu.PrefetchScalarGridSpec(
        num_scalar_prefetch=1, grid=(nb, npairs), in_specs=in_specs,
        out_specs=pl.BlockSpec((tq, G * hg * dv), lambda b, n, t: (b * nq + t[0, n], 0)),
        scratch_shapes=[pltpu.VMEM((G, rows, dk), BF16), pltpu.VMEM((G, rows, tk), F32),
                        pltpu.VMEM((G, rows, tk), BF16), pltpu.VMEM((G, rows, LANES), F32),
                        pltpu.VMEM((G, rows, LANES), F32), pltpu.VMEM((G, rows, LANES), F32),
                        pltpu.VMEM((G, rows, dv), F32)])
    return pl.pallas_call(
        functools.partial(_attn_kernel, c=c),
        out_shape=jax.ShapeDtypeStruct((nb * T, G * hg * dv), F32),
        grid_spec=grid_spec, compiler_params=_cparams(2),
        name="attend_" + mode + ("_sel" if sel is not None else ""))(tab, *args)


def _block_diag_queries(qt_ref, heads, tq, dk):
    zero = jnp.zeros((dk, tq), BF16)
    half = heads // 2
    top = jnp.concatenate([qt_ref[i * dk:(i + 1) * dk, :] if i < half else zero for i in range(heads)], axis=1)
    bot = jnp.concatenate([zero if i < half else qt_ref[i * dk:(i + 1) * dk, :] for i in range(heads)], axis=1)
    return jnp.concatenate([top, bot], axis=0)


def _cmp_t_kernel(qt_ref, kvc_ref, vct_ref, at_ref, o_ref, sel_ref, *, tq, n_slc, nbp):
    G, hg, d = A_GROUPS, A_HEADS // A_GROUPS, A_HEAD_DIM
    assert G == 2
    qi = pl.program_id(1)
    ncmp = kvc_ref.shape[1]
    pos_q = qi * tq + lax.broadcasted_iota(jnp.int32, (1, tq), 1)
    cmp_end = lax.broadcasted_iota(jnp.int32, (ncmp, 1), 0) * CMP_STRIDE + (CMP_LEN - 1)
    ok = cmp_end <= pos_q
    jb = lax.broadcasted_iota(jnp.int32, (nbp, 1), 0)
    blk_q = lax.shift_right_logical(pos_q, int(math.log2(SLC_LEN)))
    valid = jb <= blk_q
    forced = valid & ((jb == 0) | (jb >= blk_q - 1))
    st = jnp.dot(kvc_ref[0, :, :G * d].astype(BF16), _block_diag_queries(qt_ref, A_HEADS, tq, d),
                 preferred_element_type=F32)
    scores, outs = [], []
    for g in range(G):
        ps = []
        for h in range(hg):
            e = g * hg + h
            x = jnp.where(ok, st[:, e * tq:(e + 1) * tq], NEG)
            p = jnp.where(ok, jnp.exp2(x - jnp.max(x, axis=0, keepdims=True)), 0.0)
            den = jnp.sum(p, axis=0, keepdims=True)
            ps.append(p / jnp.where(den > 0.0, den, 1.0))
        pg = jnp.concatenate(ps, axis=1).astype(BF16)
        outs.append(jnp.dot(vct_ref[0, g * d:(g + 1) * d, :], pg, preferred_element_type=F32))
        p_slc = _dot_01x(at_ref[...], ps[0] + ps[1] + ps[2] + ps[3])
        score = jnp.where(forced, BIG, jnp.where(valid, p_slc, -BIG))
        scores.append(jnp.where(jb < n_slc, score, -jnp.inf))
    score = jnp.concatenate(scores, axis=1)
    jbf = jb.astype(F32)
    sel = jnp.zeros(score.shape, F32)
    for _ in range(min(SLC_TOPK, n_slc)):
        mx = jnp.max(score, axis=0, keepdims=True)
        jm = jnp.min(jnp.where(score == mx, jbf, float(nbp)), axis=0, keepdims=True)
        hit = jbf == jm
        sel = jnp.where(hit, 1.0, sel)
        score = jnp.where(hit, -jnp.inf, score)
    for g in range(G):
        okg = valid & (sel[:, g * tq:(g + 1) * tq] > 0.5)
        sel_ref[:, g * nbp:(g + 1) * nbp] = jnp.where(okg, 0.0, SEL_OFF).T.astype(BF16)
    ot = jnp.concatenate([outs[g][:, h * tq:(h + 1) * tq] for g in range(G) for h in range(hg)], axis=0)
    o_ref[...] = ot.T


def cmp_attend_t(tt, kvc, vct, at01, nb, T, n_slc):
    aq = A_HEADS * A_HEAD_DIM
    nbp = at01.shape[0]
    tq = _pick(T, (256, 128))
    nq = T // tq
    return pl.pallas_call(
        functools.partial(_cmp_t_kernel, tq=tq, n_slc=n_slc, nbp=nbp),
        out_shape=(jax.ShapeDtypeStruct((nb * T, aq), F32), jax.ShapeDtypeStruct((nb * T, A_GROUPS * nbp), BF16)),
        grid=(nb, nq),
        in_specs=[pl.BlockSpec((aq, tq), lambda b, i: (0, b * nq + i)),
                  pl.BlockSpec((1,) + kvc.shape[1:], lambda b, i: (b, 0, 0)),
                  pl.BlockSpec((1,) + vct.shape[1:], lambda b, i: (b, 0, 0)),
                  pl.BlockSpec(at01.shape, lambda b, i: (0, 0))],
        out_specs=(pl.BlockSpec((tq, aq), lambda b, i: (b * nq + i, 0)),
                   pl.BlockSpec((tq, A_GROUPS * nbp), lambda b, i: (b * nq + i, 0))),
        compiler_params=_cparams(2), name="cmp_attend_t")(tt, kvc, vct, at01)


def _attn_t_kernel(tab_ref, qt_ref, k_ref, vt_ref, *refs, c):
    if c.sel:
        selt_ref, e01t_ref = refs[:2]
        refs = refs[2:]
        o_ref, qbd_scr, s_scr, p_scr, m_scr, l_scr, a_scr, acc_scr, off_scr = refs
    else:
        o_ref, qbd_scr, s_scr, p_scr, m_scr, l_scr, a_scr, acc_scr = refs
    n = pl.program_id(1)
    qi, kj, flags = tab_ref[0, n], tab_ref[1, n], tab_ref[2, n]
    tq, tk, U, hpu, NV, hpv, dk, dvv = c.tq, c.tk, c.U, c.hpu, c.NV, c.hpv, c.dk, c.dvv
    ncu = hpu * tq
    NC = U * ncu

    @pl.when((flags & 1) != 0)
    def _():
        m_scr[...] = jnp.full(m_scr.shape, NEG, F32)
        l_scr[...] = jnp.zeros(l_scr.shape, F32)
        acc_scr[...] = jnp.zeros(acc_scr.shape, F32)
        for u in range(U):
            qbd_scr[u] = _block_diag_queries(qt_ref.at[u * hpu * dk:(u + 1) * hpu * dk, :], hpu, tq, dk)

    def step(masked):
        if masked:
            dq = qi * tq - kj * tk
            ok = (lax.broadcasted_iota(jnp.int32, (tk, tq), 1) - lax.broadcasted_iota(jnp.int32, (tk, tq), 0)) + dq >= 0
        ncv = hpv * tq
        for v in range(NV):
            u, uc = (v * hpv) // hpu, ((v * hpv) % hpu) * tq
            s_scr[v] = jnp.dot(k_ref[:, u * 2 * dk:(u + 1) * 2 * dk], qbd_scr[u, :, uc:uc + ncv],
                               preferred_element_type=F32)
        if c.sel:
            for g in range(c.G):
                off_scr[g] = jnp.dot(e01t_ref[...], selt_ref[g * c.nbp:(g + 1) * c.nbp, :], preferred_element_type=F32)

        rch = 64

        def scores(v, cb, r):
            cs = slice(cb * LANES, (cb + 1) * LANES)
            rs = slice(r * rch, (r + 1) * rch)
            c0 = (cb * LANES) % tq
            x = s_scr[v, rs, cs]
            if c.sel:
                x = x + off_scr[(v * hpv + (cb * LANES) // tq) // (NV * hpv // c.G), rs, c0:c0 + LANES]
            okc = ok[rs, c0:c0 + LANES] if masked else None
            if masked:
                x = jnp.where(okc, x, NEG)
            return x, okc

        fold = lambda x, op: functools.reduce(op, [x[i * SUB:(i + 1) * SUB] for i in range(rch // SUB)])
        for v in range(NV):
            for cb in range(ncv // LANES):
                cs = slice(cb * LANES, (cb + 1) * LANES)
                mx = None
                for r in range(tk // rch):
                    m8 = fold(scores(v, cb, r)[0], jnp.maximum)
                    mx = m8 if mx is None else jnp.maximum(mx, m8)
                m_old = m_scr[v, :, cs]
                m_new = jnp.maximum(m_old, jnp.max(mx, axis=0, keepdims=True))
                a_scr[v, :, cs] = jnp.exp2(m_old - m_new)
                m_scr[v, :, cs] = m_new
            for cb in range(ncv // LANES):
                cs = slice(cb * LANES, (cb + 1) * LANES)
                m_row = m_scr[v, 0:1, cs]
                psum = None
                for r in range(tk // rch):
                    x, okc = scores(v, cb, r)
                    p = jnp.exp2(x - m_row)
                    if masked:
                        p = jnp.where(okc, p, 0.0)
                    p8 = fold(p, jnp.add)
                    psum = p8 if psum is None else psum + p8
                    p_scr[v, r * rch:(r + 1) * rch, cs] = p.astype(BF16)
                l_scr[v, :, cs] = a_scr[v, :, cs] * l_scr[v, :, cs] + psum
        for v in range(NV):
            pv = jnp.dot(vt_ref[v * dvv:(v + 1) * dvv, :], p_scr[v], preferred_element_type=F32)
            acc_scr[v] = a_scr[v, 0:1, :] * acc_scr[v] + pv

    @pl.when((flags & 4) != 0)
    def _():
        step(True)

    @pl.when((flags & 4) == 0)
    def _():
        step(False)

    @pl.when((flags & 2) != 0)
    def _():
        parts = []
        for v in range(NV):
            l = jnp.sum(l_scr[v], axis=0, keepdims=True)
            o = acc_scr[v] / jnp.where(l > 0.0, l, 1.0)
            parts += [o[:, i * tq:(i + 1) * tq] for i in range(hpv)]
        o_ref[...] = jnp.concatenate(parts, axis=0).T


def attend_t(tt, v_rblk, zb, k_blk, *, nb, T, tq, tk, U, hpu, NV, hpv, dvv, selt=None, e01t=None, G=1):
    dk = A_HEAD_DIM
    nq, nk = T // tq, T // tk
    NH = U * hpu
    assert NH == NV * hpv and tq % LANES == 0
    c = _FlashCfg(tq=tq, tk=tk, U=U, hpu=hpu, NV=NV, hpv=hpv, dk=dk, dvv=dvv, mode="causal", q_off=0, k_off=0, nk=nk,
                  sel=selt is not None, nbp=0 if selt is None else selt.shape[0] // G, G=G)
    tab = []
    for i in range(nq):
        last = min((i * tq + tq - 1) // tk, nk - 1)
        for j in range(last + 1):
            masked = i * tq - (j * tk + tk - 1) < 0
            tab.append((i, j, (j == 0) * 1 + (j == last) * 2 + masked * 4))
    tab = jnp.asarray(np.array(tab, np.int32).T)
    NC = NH * tq
    in_specs = [pl.BlockSpec((NH * dk, tq), lambda b, n, t: (0, b * nq + t[0, n])),
                pl.BlockSpec((tk, U * 2 * dk), lambda b, n, t: (b * nk + t[1, n], k_blk)),
                pl.BlockSpec((NV * dvv, tk), lambda b, n, t: (v_rblk, b * nk + t[1, n]))]
    args = [tt, zb, tt]
    ncv = hpv * tq
    scratch = [pltpu.VMEM((U, 2 * dk, hpu * tq), BF16), pltpu.VMEM((NV, tk, ncv), F32), pltpu.VMEM((NV, tk, ncv), BF16),
               pltpu.VMEM((NV, SUB, ncv), F32), pltpu.VMEM((NV, SUB, ncv), F32), pltpu.VMEM((NV, SUB, ncv), F32),
               pltpu.VMEM((NV, dvv, ncv), F32)]
    if selt is not None:
        in_specs += [pl.BlockSpec((selt.shape[0], tq), lambda b, n, t: (0, b * nq + t[0, n])),
                     pl.BlockSpec((tk, e01t.shape[1]), lambda b, n, t: (t[1, n], 0))]
        args += [selt, e01t]
        scratch.append(pltpu.VMEM((G, tk, tq), F32))
    grid_spec = pltpu.PrefetchScalarGridSpec(
        num_scalar_prefetch=1, grid=(nb, tab.shape[1]), in_specs=in_specs,
        out_specs=pl.BlockSpec((tq, NH * dvv), lambda b, n, t: (b * nq + t[0, n], 0)),
        scratch_shapes=scratch)
    return pl.pallas_call(
        functools.partial(_attn_t_kernel, c=c),
        out_shape=jax.ShapeDtypeStruct((nb * T, NH * dvv), F32),
        grid_spec=grid_spec, compiler_params=_cparams(2),
        name="attend_t" + ("_sel" if selt is not None else ""))(tab, *args)


def _window_kernel(q_ref, *refs, tq, nband):
    kts, vs = refs[:nband], refs[nband:2 * nband]
    o_ref = refs[2 * nband]
    G, hg, d = A_GROUPS, A_HEADS // A_GROUPS, A_HEAD_DIM
    qi = pl.program_id(1)
    ncol = nband * tq
    row = lax.broadcasted_iota(jnp.int32, (tq, ncol), 0)
    col = lax.broadcasted_iota(jnp.int32, (tq, ncol), 1)
    pos_k = (qi - (nband - 1)) * tq + col
    dpos = qi * tq + row - pos_k
    ok = ((dpos >= 0) & (dpos < WINDOW) & (pos_k >= 0))[None]
    for g in range(G):
        qs = jnp.concatenate([q_ref[:, (g * hg + h) * d:(g * hg + h + 1) * d] for h in range(hg)], axis=0)
        qs = (qs * (d ** -0.5 * LOG2E)).astype(BF16)
        kt = jnp.concatenate([k[g * d:(g + 1) * d, :] for k in kts], axis=1)
        v = jnp.concatenate([x[:, g * d:(g + 1) * d] for x in vs], axis=0)
        s = jnp.dot(qs, kt, preferred_element_type=F32).reshape(hg, tq, ncol)
        s = jnp.where(ok, s, NEG)
        p = jnp.where(ok, jnp.exp2(s - jnp.max(s, axis=-1, keepdims=True)), 0.0)
        den = jnp.sum(p, axis=-1, keepdims=True)
        o = jnp.dot(p.reshape(hg * tq, ncol).astype(BF16), v, preferred_element_type=F32).reshape(hg, tq, d)
        o = o / jnp.where(den > 0.0, den, 1.0)
        for h in range(hg):
            o_ref[:, (g * hg + h) * d:(g * hg + h + 1) * d] = o[h]


def window_attend(z, kt, kt_blk, zb, v_blk, nb, T):
    G, d = A_GROUPS, A_HEAD_DIM
    aq = A_HEADS * d
    tq = 128
    assert T % tq == 0 and WINDOW % tq == 0
    nq = T // tq
    nband = WINDOW // tq + 1
    band = lambda i, blk, tr: (lambda b, q: (blk, b * nq + jnp.maximum(q - (nband - 1) + i, 0)) if tr
                               else (b * nq + jnp.maximum(q - (nband - 1) + i, 0), blk))
    in_specs = [pl.BlockSpec((tq, aq), lambda b, q: (b * nq + q, EV_Q))]
    in_specs += [pl.BlockSpec((G * d, tq), band(i, kt_blk, True)) for i in range(nband)]
    in_specs += [pl.BlockSpec((tq, G * d), band(i, v_blk, False)) for i in range(nband)]
    return pl.pallas_call(
        functools.partial(_window_kernel, tq=tq, nband=nband),
        out_shape=jax.ShapeDtypeStruct((nb * T, aq), F32),
        grid=(nb, nq), in_specs=in_specs,
        out_specs=pl.BlockSpec((tq, aq), lambda b, q: (b * nq + q, 0)),
        compiler_params=_cparams(2), name="window_attend")(z, *([kt] * nband), *([zb] * nband))


def _seg_mask(C):
    nblk = C // SUB
    nseg = max(SUB * nblk * (nblk - 1) // 2, SUB)
    pm = np.zeros((C, nseg), np.float32)
    for i in range(1, nblk):
        off = SUB * i * (i - 1) // 2
        pm[SUB * i:SUB * (i + 1), off:off + SUB * i] = 1.0
    return pm


def _rec_kernel(*refs, mode, C, nsub, H, pos_off, has_s0, layer):
    refs = list(refs)
    a_ref, b_ref, v_ref, gate_ref = refs[:4]
    refs = refs[4:]
    aux_ref = refs.pop(0)
    pm_ref = refs.pop(0)
    s0_ref = refs.pop(0) if has_s0 else None
    o_ref, st_ref, st_scr = refs
    c_id = pl.program_id(1)
    K = 128

    @pl.when(c_id == 0)
    def _():
        if has_s0:
            st_scr[...] = s0_ref[0]
        else:
            st_scr[...] = jnp.zeros(st_scr.shape, F32)

    rows = lax.broadcasted_iota(jnp.int32, (C, 1), 0)
    ltri = jnp.where(rows >= lax.broadcasted_iota(jnp.int32, (1, C), 1), 1.0, 0.0).astype(BF16)
    srow = lax.broadcasted_iota(jnp.int32, (SUB, 1), 0)
    if mode == "hgrn":
        x = aux_ref[...]
        ex = jnp.exp(x - jnp.max(x, axis=0, keepdims=True))
        sm = ex / jnp.sum(ex, axis=0, keepdims=True)
        lb_all = jnp.zeros((1, H * K), F32)
        for i in range(1, layer + 1):
            lb_all = lb_all + sm[i:i + 1]
    sts = [st_scr[h] for h in range(H)]
    for sc in range(nsub):
        rsl = slice(sc * C, (sc + 1) * C)
        sts = _rec_chunk(sts, a_ref[rsl, :], b_ref[rsl, :], v_ref[rsl, :], gate_ref[rsl, :], o_ref, rsl,
                         mode=mode, C=C, H=H, pos0=pos_off + (c_id * nsub + sc) * C, aux_ref=aux_ref, pm_ref=pm_ref,
                         lb_all=lb_all if mode == "hgrn" else None, rows=rows, ltri=ltri, srow=srow)
    for h in range(H):
        st_scr[h] = sts[h]

    @pl.when(c_id == pl.num_programs(1) - 1)
    def _():
        st_ref[0] = st_scr[...]


def _rec_chunk(sts, a_all, b_all, v_all, gate_all, o_ref, rsl, *, mode, C, H, pos0, aux_ref, pm_ref, lb_all, rows,
               ltri, srow):
    K = 128
    nblk = C // SUB
    if mode == "ret":
        pos = (pos0 + rows).astype(F32)
        ang = pos * aux_ref[0:1, :]
        cos, sin_s = jnp.cos(ang), jnp.sin(ang) * aux_ref[1:2, :]

        def rope(x):
            return x * cos + pltpu.roll(x, K // 2, 1) * sin_s

        qs = [rope(a_all[:, h * K:(h + 1) * K]) for h in range(H)]
        ks = [rope(b_all[:, h * K:(h + 1) * K]) * B_DK ** -0.5 for h in range(H)]
        gs = [jnp.full((C, K), math.log1p(-2.0 ** (-5.0 - h)), F32) for h in range(H)]
    else:
        qs, ks, gs = [], [], []
        for h in range(H):
            ah, zf, lb = a_all[:, h * K:(h + 1) * K], b_all[:, h * K:(h + 1) * K], lb_all[:, h * K:(h + 1) * K]
            f = lb + (1.0 - lb) * _sigmoid(zf)
            qs.append(ah * _sigmoid(ah))
            ks.append(1.0 - f)
            gs.append(jnp.log(jnp.maximum(f, F_FLOOR)))

    bs_all = _dot_01x(ltri, jnp.concatenate(gs, axis=-1))
    new_sts = []
    for h in range(H):
        q, k, bsum = qs[h], ks[h], bs_all[:, h * K:(h + 1) * K]
        v = v_all[:, h * K:(h + 1) * K]
        st = sts[h]
        o = lax.dot_general((q * jnp.exp(bsum)).astype(BF16), st.astype(BF16), NT, preferred_element_type=F32)
        if nblk > 1:
            rho = jnp.concatenate(
                [jnp.zeros((SUB, K), F32)] +
                [jnp.broadcast_to(bsum[SUB * i - 1:SUB * i], (SUB, K)) for i in range(1, nblk)], axis=0)
            qt = q * jnp.exp(bsum - rho)
            kt = jnp.concatenate(
                [k[:SUB * i] * jnp.exp(bsum[SUB * i - 1:SUB * i] - bsum[:SUB * i]) for i in range(1, nblk)], axis=0)
            vcat = jnp.concatenate([v[:SUB * i] for i in range(1, nblk)], axis=0)
            p = _dot3_nt(qt, kt) * pm_ref[...]
            o += jnp.dot(p.astype(BF16), vcat.astype(BF16), preferred_element_type=F32)
        diag = []
        for i in range(nblk):
            sl = slice(SUB * i, SUB * (i + 1))
            q8, k8, b8, v8 = q[sl], k[sl], bsum[sl], v[sl]
            od = jnp.zeros((SUB, K), F32)
            for s in range(SUB):
                causal = srow >= s
                dec = jnp.exp(jnp.where(causal, b8 - b8[s:s + 1], 0.0))
                att = jnp.sum(q8 * k8[s:s + 1] * dec, axis=-1, keepdims=True)
                od += jnp.where(causal, att, 0.0) * v8[s:s + 1]
            diag.append(od)
        o += jnp.concatenate(diag, axis=0) if nblk > 1 else diag[0]
        bend = bsum[C - 1:C]
        kst = k * jnp.exp(bend - bsum)
        new_sts.append(st * jnp.exp(bend) + lax.dot_general(v.astype(BF16), kst.astype(BF16), TN,
                                                            preferred_element_type=F32))
        gate = gate_all[:, h * K:(h + 1) * K]
        o = o * lax.rsqrt(jnp.mean(o * o, axis=-1, keepdims=True) + EPS) * (gate * _sigmoid(gate))
        o_ref[rsl, h * K:(h + 1) * K] = o.astype(o_ref.dtype)
    return new_sts


def recurrence(z, blks, aux, *, mode, nb, T, pos_off, s0t=None, layer=0):
    H, K = 4, 128
    C = CHUNK if T % CHUNK == 0 else T
    nsub = _pick(T // C, (4, 2, 1))
    CB = nsub * C
    nc = T // CB
    pm = _seg_mask(C)
    nseg = pm.shape[1]
    row = lambda blk: (lambda b, c: (b * nc + c, blk))
    in_specs = [pl.BlockSpec((CB, H * K), row(blks[0])), pl.BlockSpec((CB, H * K), row(blks[1])),
                pl.BlockSpec((CB, H * K), row(blks[2])), pl.BlockSpec((CB, H * K), row(blks[3])),
                pl.BlockSpec(aux.shape, lambda b, c: (0, 0)),
                pl.BlockSpec((C, nseg), lambda b, c: (0, 0))]
    args = [z, z, z, z, aux, jnp.asarray(pm)]
    if s0t is not None:
        in_specs.append(pl.BlockSpec((1, H, K, K), lambda b, c: (b, 0, 0, 0)))
        args.append(s0t)
    return pl.pallas_call(
        functools.partial(_rec_kernel, mode=mode, C=C, nsub=nsub, H=H, pos_off=pos_off, has_s0=s0t is not None,
                          layer=layer),
        out_shape=(jax.ShapeDtypeStruct((nb * T, H * K), BF16), jax.ShapeDtypeStruct((nb, H, K, K), F32)),
        grid=(nb, nc),
        in_specs=in_specs,
        out_specs=(pl.BlockSpec((CB, H * K), lambda b, c: (b * nc + c, 0)),
                   pl.BlockSpec((1, H, K, K), lambda b, c: (b, 0, 0, 0))),
        scratch_shapes=[pltpu.VMEM((H, K, K), F32)],
        compiler_params=_cparams(2), name="recurrence_" + mode)(*args)


def _gather_kernel(pt_ref, cache_ref, *refs, n_pages):
    o_ref = refs[-1]
    p = pl.program_id(1)

    @pl.when(p < n_pages)
    def _():
        o_ref[0, 0] = cache_ref[0, 0]

    if len(refs) == 2:
        @pl.when(p >= n_pages)
        def _():
            o_ref[0, 0] = refs[0][0]


def gather_pages(cache, layer, page_table, tail=None):
    nb, n_pages = page_table.shape
    _, _, ps, C = cache.shape
    n_out = n_pages + (tail is not None)
    in_specs = [pl.BlockSpec((1, 1, ps, C), lambda b, p, pt: (layer, pt[b * n_pages + jnp.minimum(p, n_pages - 1)], 0, 0))]
    args = [page_table.reshape(-1), cache]
    if tail is not None:
        in_specs.append(pl.BlockSpec((1, ps, C), lambda b, p, pt: (b, 0, 0)))
        args.append(tail)
    grid_spec = pltpu.PrefetchScalarGridSpec(
        num_scalar_prefetch=1, grid=(nb, n_out), in_specs=in_specs,
        out_specs=pl.BlockSpec((1, 1, ps, C), lambda b, p, pt: (b, p, 0, 0)))
    return pl.pallas_call(
        functools.partial(_gather_kernel, n_pages=n_pages),
        out_shape=jax.ShapeDtypeStruct((nb, n_out, ps, C), F32),
        grid_spec=grid_spec,
        compiler_params=_cparams(2), name="gather_pages")(*args)


def _page_specs(shape, layer, n_pages, per_step):
    def spec(i):
        return pl.BlockSpec((1, 1) + shape, lambda b, j, pt: (layer, pt[b * n_pages + j * per_step + i], 0, 0))
    return [spec(i) for i in range(per_step)]


def _online_update(s, mk, m_ref, l_ref, acc_ref, pv_fn):
    hg, rows, n = s.shape
    if mk is not None:
        s = jnp.where(mk, s, NEG)
    m_old = m_ref[...]
    m_new = jnp.maximum(m_old, jnp.max(s, axis=-1, keepdims=True))
    alpha = jnp.exp(m_old - m_new)
    p = jnp.exp(s - m_new)
    if mk is not None:
        p = jnp.where(mk, p, 0.0)
    l_ref[...] = alpha * l_ref[...] + jnp.sum(p, axis=-1, keepdims=True)
    pv = pv_fn(p.reshape(hg * rows, n).astype(BF16))
    acc_ref[...] = alpha * acc_ref[...] + pv.reshape(hg, rows, pv.shape[-1])
    m_ref[...] = m_new


def _diff_decode_kernel(pt_ref, q_ref, new_ref, *refs, per_step, H, dh):
    pages = refs[:per_step]
    o_ref, qbd_scr, m_scr, l_scr, acc_scr = refs[per_step:]
    j = pl.program_id(1)
    Ts = q_ref.shape[0]
    dv = 2 * dh
    rows_pp = 2 * H

    @pl.when(j == 0)
    def _():
        m_scr[...] = jnp.full(m_scr.shape, NEG, F32)
        l_scr[...] = jnp.zeros(l_scr.shape, F32)
        acc_scr[...] = jnp.zeros(acc_scr.shape, F32)
        lane = lax.broadcasted_iota(jnp.int32, (Ts, dv), 1)
        for h in range(H):
            q = q_ref[:, h * dv:(h + 1) * dv] * dh ** -0.5
            qbd_scr[h] = jnp.concatenate([jnp.where(lane < dh, q, 0.0), jnp.where(lane >= dh, q, 0.0)],
                                         axis=0).astype(BF16)

    for h in range(H):
        k = jnp.concatenate([pg[0, 0, pl.ds(h, PAGE_SIZE, stride=rows_pp), :] for pg in pages], axis=0).astype(BF16)
        v = jnp.concatenate([pg[0, 0, pl.ds(H + h, PAGE_SIZE, stride=rows_pp), :] for pg in pages], axis=0).astype(BF16)
        s = lax.dot_general(qbd_scr[h], k, NT, preferred_element_type=F32)[None]
        _online_update(s, None, m_scr.at[h], l_scr.at[h], acc_scr.at[h],
                       lambda p: jnp.dot(p, v, preferred_element_type=F32))

    @pl.when(j == pl.num_programs(1) - 1)
    def _():
        r = lax.broadcasted_iota(jnp.int32, (2 * Ts, Ts), 0)
        cidx = lax.broadcasted_iota(jnp.int32, (2 * Ts, Ts), 1)
        causal = (cidx <= jnp.where(r >= Ts, r - Ts, r))[None]
        for h in range(H):
            kn = new_ref[:, h * dv:(h + 1) * dv].astype(BF16)
            vn = new_ref[:, (H + h) * dv:(H + h + 1) * dv].astype(BF16)
            s = lax.dot_general(qbd_scr[h], kn, NT, preferred_element_type=F32)[None]
            _online_update(s, causal, m_scr.at[h], l_scr.at[h], acc_scr.at[h],
                           lambda p: jnp.dot(p, vn, preferred_element_type=F32))
            l = l_scr[h]
            o = acc_scr[h] / jnp.where(l > 0.0, l, 1.0)
            o_ref[:, (2 * h) * dv:(2 * h + 1) * dv] = o[0, :Ts]
            o_ref[:, (2 * h + 1) * dv:(2 * h + 2) * dv] = o[0, Ts:]


def diff_decode(z, cache_rows, layer, page_table, nbs, Ts):
    H, dh = C_HEADS, C_HEAD_DIM
    cw = H * 2 * dh
    n_pages = page_table.shape[1]
    per_step = _pick(n_pages, (16, 8, 4, 2, 1))
    in_specs = [pl.BlockSpec((Ts, cw), lambda b, j, pt: (b, 0)),
                pl.BlockSpec((Ts, 2 * cw), lambda b, j, pt: (b, 0))]
    in_specs += _page_specs(cache_rows.shape[2:], layer, n_pages, per_step)
    grid_spec = pltpu.PrefetchScalarGridSpec(
        num_scalar_prefetch=1, grid=(nbs, n_pages // per_step), in_specs=in_specs,
        out_specs=pl.BlockSpec((Ts, 2 * cw), lambda b, j, pt: (b, 0)),
        scratch_shapes=[pltpu.VMEM((H, 2 * Ts, 2 * dh), BF16), pltpu.VMEM((H, 1, 2 * Ts, 1), F32),
                        pltpu.VMEM((H, 1, 2 * Ts, 1), F32), pltpu.VMEM((H, 1, 2 * Ts, 2 * dh), F32)])
    kv_new = z[:, cw:3 * cw]
    return pl.pallas_call(
        functools.partial(_diff_decode_kernel, per_step=per_step, H=H, dh=dh),
        out_shape=jax.ShapeDtypeStruct((nbs * Ts, 2 * cw), F32),
        grid_spec=grid_spec, compiler_params=_cparams(2),
        name="diff_decode")(page_table.reshape(-1), z, kv_new, *([cache_rows] * per_step))


def _slc_decode_kernel(pt_ref, q_ref, new_ref, sel_ref, *refs, per_step, nbp, past):
    pages = refs[:per_step]
    o_ref, qs_scr, m_scr, l_scr, acc_scr = refs[per_step:]
    G, hg, d = A_GROUPS, A_HEADS // A_GROUPS, A_HEAD_DIM
    j = pl.program_id(1)
    Ts = q_ref.shape[0]
    ntok = per_step * PAGE_SIZE

    @pl.when(j == 0)
    def _():
        m_scr[...] = jnp.full(m_scr.shape, NEG, F32)
        l_scr[...] = jnp.zeros(l_scr.shape, F32)
        acc_scr[...] = jnp.zeros(acc_scr.shape, F32)
        for g in range(G):
            qs = jnp.concatenate([q_ref[:, (g * hg + h) * d:(g * hg + h + 1) * d] for h in range(hg)], axis=0)
            qs_scr[g] = (qs * d ** -0.5).astype(BF16)

    tok = j * ntok + lax.broadcasted_iota(jnp.int32, (1, ntok), 1)
    jb = lax.broadcasted_iota(jnp.int32, (nbp, 1), 0)
    e01 = jnp.where(jb == lax.shift_right_logical(tok, int(math.log2(SLC_LEN))), 1.0, 0.0).astype(BF16)
    for g in range(G):
        kt = jnp.concatenate([pg[0, 0, g * d:(g + 1) * d, :] for pg in pages], axis=1).astype(BF16)
        vt = jnp.concatenate([pg[0, 0, (G + g) * d:(G + g + 1) * d, :] for pg in pages], axis=1).astype(BF16)
        s = jnp.dot(qs_scr[g], kt, preferred_element_type=F32).reshape(hg, Ts, ntok)
        st = jnp.dot(sel_ref[:, g * nbp:(g + 1) * nbp], e01, preferred_element_type=F32)
        _online_update(s, (st > -1.0)[None], m_scr.at[g], l_scr.at[g], acc_scr.at[g],
                       lambda p: lax.dot_general(p, vt, NT, preferred_element_type=F32))

    @pl.when(j == pl.num_programs(1) - 1)
    def _():
        causal = (lax.broadcasted_iota(jnp.int32, (Ts, Ts), 1) <= lax.broadcasted_iota(jnp.int32, (Ts, Ts), 0))[None]
        for g in range(G):
            kn = new_ref[:, g * d:(g + 1) * d].astype(BF16)
            vn = new_ref[:, (G + g) * d:(G + g + 1) * d].astype(BF16)
            s = lax.dot_general(qs_scr[g], kn, NT, preferred_element_type=F32).reshape(hg, Ts, Ts)
            _online_update(s, causal, m_scr.at[g], l_scr.at[g], acc_scr.at[g],
                           lambda p: jnp.dot(p, vn, preferred_element_type=F32))
            l = l_scr[g]
            o = acc_scr[g] / jnp.where(l > 0.0, l, 1.0)
            for h in range(hg):
                o_ref[:, (g * hg + h) * d:(g * hg + h + 1) * d] = o[h]


def slc_decode(z, cache_t, layer, page_table, sel, nbs, Ts, nbp, past):
    G, d = A_GROUPS, A_HEAD_DIM
    aq = A_HEADS * d
    n_pages = page_table.shape[1]
    per_step = _pick(n_pages, (16, 8, 4, 2, 1))
    assert past % SLC_LEN == 0 and Ts <= SLC_LEN
    in_specs = [pl.BlockSpec((Ts, aq), lambda b, j, pt: (b, EV_Q)),
                pl.BlockSpec((Ts, 2 * G * d), lambda b, j, pt: (b, EV_SLC // (2 * G * d))),
                pl.BlockSpec((Ts, G * nbp), lambda b, j, pt: (b, 0))]
    in_specs += _page_specs(cache_t.shape[2:], layer, n_pages, per_step)
    hg = A_HEADS // G
    grid_spec = pltpu.PrefetchScalarGridSpec(
        num_scalar_prefetch=1, grid=(nbs, n_pages // per_step), in_specs=in_specs,
        out_specs=pl.BlockSpec((Ts, aq), lambda b, j, pt: (b, 0)),
        scratch_shapes=[pltpu.VMEM((G, hg * Ts, d), BF16), pltpu.VMEM((G, hg, Ts, 1), F32),
                        pltpu.VMEM((G, hg, Ts, 1), F32), pltpu.VMEM((G, hg, Ts, d), F32)])
    return pl.pallas_call(
        functools.partial(_slc_decode_kernel, per_step=per_step, nbp=nbp, past=past),
        out_shape=jax.ShapeDtypeStruct((nbs * Ts, aq), F32),
        grid_spec=grid_spec, compiler_params=_cparams(2),
        name="slc_decode")(page_table.reshape(-1), z, z, sel, *([cache_t] * per_step))


def _compress_paged_kernel(pt_ref, w_ref, pe_ref, *refs, per_step):
    pages = refs[:per_step]
    o_ref, ab_scr = refs[per_step:]
    j = pl.program_id(1)
    cpp = PAGE_SIZE // CMP_STRIDE
    r = lax.broadcasted_iota(jnp.int32, (PAGE_SIZE, PAGE_SIZE), 0)
    t = lax.broadcasted_iota(jnp.int32, (PAGE_SIZE, PAGE_SIZE), 1)
    assert cpp & (cpp - 1) == 0
    perm = jnp.where(t == (r & (cpp - 1)) * CMP_STRIDE + lax.shift_right_logical(r, int(math.log2(cpp))),
                     1.0, 0.0).astype(BF16)
    xs = [lax.dot_general(perm, pg[0, 0].astype(BF16), NT, preferred_element_type=F32) for pg in pages]
    C = xs[0].shape[1]
    ab = jnp.zeros((per_step * cpp, w_ref.shape[1]), F32)
    for l in range(CMP_STRIDE):
        xl = jnp.concatenate([x[l * cpp:(l + 1) * cpp] for x in xs], axis=0).astype(BF16)
        ab += jnp.dot(xl, w_ref[l * C:(l + 1) * C, :], preferred_element_type=F32)
    n = per_step * cpp
    ab_scr[pl.ds(pl.multiple_of(j * n, n), n), :] = ab

    @pl.when(j == pl.num_programs(1) - 1)
    def _():
        cab = jnp.dot(pe_ref[...].astype(BF16), w_ref[...], preferred_element_type=F32)
        half = ab_scr.shape[1] // 2
        const = cab[0:1, :half] + cab[1:2, half:]
        a, b = ab_scr[:, :half], ab_scr[:, half:]
        o_ref[0] = a + pltpu.roll(b, a.shape[0] - 1, 0) + const


def compress_paged(cache_t, layer, page_table, wbig, pe2):
    nbs, n_pages = page_table.shape
    per_step = _pick(n_pages, (16, 8, 4, 2, 1))
    cpp = PAGE_SIZE // CMP_STRIDE
    n_chunk = n_pages * cpp
    cout = wbig.shape[1] // 2
    in_specs = [pl.BlockSpec(wbig.shape, lambda b, j, pt: (0, 0)), pl.BlockSpec(pe2.shape, lambda b, j, pt: (0, 0))]
    in_specs += _page_specs(cache_t.shape[2:], layer, n_pages, per_step)
    grid_spec = pltpu.PrefetchScalarGridSpec(
        num_scalar_prefetch=1, grid=(nbs, n_pages // per_step), in_specs=in_specs,
        out_specs=pl.BlockSpec((1, n_chunk, cout), lambda b, j, pt: (b, 0, 0)),
        scratch_shapes=[pltpu.VMEM((n_chunk, 2 * cout), F32)])
    return pl.pallas_call(
        functools.partial(_compress_paged_kernel, per_step=per_step),
        out_shape=jax.ShapeDtypeStruct((nbs, n_chunk, cout), F32),
        grid_spec=grid_spec, compiler_params=_cparams(2),
        name="compress_paged")(page_table.reshape(-1), wbig, pe2, *([cache_t] * per_step))


def _even_w_in(w):
    aq, akv = A_HEADS * A_HEAD_DIM, A_GROUPS * A_HEAD_DIM
    splits = np.cumsum([aq] + [akv] * 6 + [3 * A_HEADS] + [512] * 4)[:-1]
    q, kc, vc, ks, vs, kw, vw, gt, rq, rk, rv, rg = jnp.split(w, [int(s) for s in splits], axis=1)
    gt = jnp.pad(gt, ((0, 0), (0, 128 - gt.shape[1])))
    return jnp.concatenate([q, rq, rk, rv, rg, kc, vc, ks, vs, kw, vw, gt], axis=1).astype(BF16)


EV_Q, EV_RQ, EV_RK, EV_RV, EV_RG = 0, 1, 2, 3, 4
EV_KS, EV_VS, EV_KW, EV_VW, EV_GT = 22, 23, 24, 25, 26
EV_CMP, EV_SLC, EV_WIN = 2560, 2816, 3072


def _compress_w(w_cmp, pe):
    G, d = A_GROUPS, A_HEAD_DIM
    r = CMP_LEN // CMP_STRIDE
    assert r == 2
    w = w_cmp.reshape(2, r, CMP_STRIDE, d, d)
    eye_kv = jnp.eye(2, dtype=F32)
    eye_g = jnp.eye(G, dtype=F32)
    big = jnp.einsum("khlde,kq,gp->lqpdhkge", w, eye_kv, eye_g)
    big = big.reshape(CMP_STRIDE * 2 * G * d, r * 2 * G * d).astype(BF16)
    pe_r = pe.reshape(2, r, CMP_STRIDE, d)
    rows = jnp.broadcast_to(pe_r.transpose(1, 2, 0, 3)[:, :, :, None, :], (r, CMP_STRIDE, 2, G, d))
    rows = rows.reshape(r, CMP_STRIDE * 2 * G * d)
    return big, jnp.pad(rows, ((0, 8 - r), (0, 0)))


def _slc_sum_matrix(n_rows, n_cmp, nbp):
    a = np.zeros((n_rows, nbp), np.float32)
    per, left = SLC_LEN // CMP_STRIDE, CMP_LEN // CMP_STRIDE - 1
    for j in range(nbp):
        for n in range(per * j - left, per * j + per):
            if 0 <= n < n_cmp:
                a[n, j] = 1.0
    return jnp.asarray(a, BF16)


def _gate_expand():
    e = np.zeros((128, 3 * A_HEADS * A_HEAD_DIM), np.float32)
    for h in range(A_HEADS):
        for i in range(3):
            e[3 * h + i, i * A_HEADS * A_HEAD_DIM + h * A_HEAD_DIM:i * A_HEADS * A_HEAD_DIM + (h + 1) * A_HEAD_DIM] = 1.0
    return jnp.asarray(e, BF16)


def _rope_aux():
    half = B_DK // 2
    freqs = ROPE_BASE ** (-jnp.arange(half, dtype=F32) / half)
    sign = jnp.concatenate([-jnp.ones((half,), F32), jnp.ones((half,), F32)])
    return jnp.stack([jnp.concatenate([freqs, freqs]), sign])


def _slc_shapes(n_slc, n_cmp_pad):
    nbp = -(-n_slc // 128) * 128
    return nbp, _slc_sum_matrix(n_cmp_pad, n_cmp_pad - 1, nbp)


def _nsa_common(nb, T, q_off):
    return dict(nb=nb, Tq=T, tq=_pick(T, (128, 64, 32, 16, 8)), G=A_GROUPS, hg=A_HEADS // A_GROUPS, dk=A_HEAD_DIM,
                dv=A_HEAD_DIM, g_per_v=1, q_off=q_off)


def _even_layer(xp, xs, p, a, dims, caches):
    B, T, nbs, Ts, past = dims
    g = p["norm_g"]
    w_in = _even_w_in(p["w_in_a"][a])
    wbig, pe2 = _compress_w(p["cmp_w"][a], p["cmp_pos"][a])
    w_out = p["w_out_a"][a].astype(BF16)
    aq = A_HEADS * A_HEAD_DIM
    wa, wb = w_out[:aq], w_out[aq:]
    e01 = _gate_expand()
    aux = _rope_aux()
    akv2 = 2 * A_GROUPS * A_HEAD_DIM

    akv = A_GROUPS * A_HEAD_DIM
    wt = jnp.concatenate([w_in[:, :aq], w_in[:, EV_SLC:EV_SLC + akv], w_in[:, EV_WIN:EV_WIN + akv]], axis=1).T
    wscale = jnp.concatenate([jnp.full((aq, 1), A_HEAD_DIM ** -0.5 * LOG2E, F32), jnp.ones((2 * akv, 1), F32)])
    zp, zpb, ttp = in_proj(xp, g[0:1], w_in, wt, wscale)
    kv_cmp_p = zp[:, EV_CMP:EV_CMP + akv2]
    kv_slc_p = zp[:, EV_SLC:EV_SLC + akv2]
    kv_win_p = zp[:, EV_WIN:EV_WIN + akv2]
    kvc_p, vct_p = compress(kv_cmp_p.reshape(B, T // CMP_STRIDE, CMP_STRIDE * akv2), wbig, pe2)
    n_slc = -(-T // SLC_LEN)
    nbp, a01 = _slc_shapes(n_slc, kvc_p.shape[1])
    o_cmp, sel = cmp_attend_t(ttp, kvc_p, vct_p, a01.T, B, T, n_slc)
    blk01 = np.zeros((nbp, T), np.float32)
    blk01[np.arange(T) // SLC_LEN, np.arange(T)] = 1.0
    o_slc = attend(zp, EV_Q, ttp, aq // akv, zpb, EV_VS, nb=B, T=T, tq=_pick(T, (256, 128, 64, 32, 16, 8)),
                   tk=_pick(T, (1024, 512, 256, 128)), G=A_GROUPS, hg=A_HEADS // A_GROUPS, dk=A_HEAD_DIM,
                   dv=A_HEAD_DIM, g_per_v=1, mode="causal", sel=sel, e01=jnp.asarray(blk01, BF16))
    o_win = window_attend(zp, ttp, (aq + akv) // akv, zpb, EV_VW, B, T)
    ob_p, st_p = recurrence(zp, (EV_RQ, EV_RK, EV_RV, EV_RG), aux, mode="ret", nb=B, T=T, pos_off=0)
    xp = even_out(o_cmp, o_slc, o_win, zp, EV_GT, ob_p, e01, wa, wb, g[1:2], xp)

    cache_cmp, cache_slc, win_buf, s0, page_table = caches
    zs = rms_matmul(xs, g[0:1], w_in, F32)
    kv_cmp_s = zs[:, EV_CMP:EV_CMP + akv2]
    kv_slc_s = zs[:, EV_SLC:EV_SLC + akv2]
    kv_win_s = zs[:, EV_WIN:EV_WIN + akv2]
    assert Ts < CMP_STRIDE and past % SLC_LEN == 0
    kvc_s = compress_paged(cache_cmp, a, page_table, wbig, pe2)
    n_slc = -(-(past + Ts) // SLC_LEN)
    nbp, a01 = _slc_shapes(n_slc, kvc_s.shape[1])
    o_cmp, sel = cmp_attend(zs, EV_Q, kvc_s, a01, nbs, Ts, past, n_slc)
    o_slc = slc_decode(zs, cache_slc, a, page_table, sel, nbs, Ts, nbp, past)
    band = jnp.concatenate([win_buf, kv_win_s.reshape(nbs, Ts, akv2)], axis=1)
    n_buf = win_buf.shape[1]
    bt = n_buf + Ts
    o_win = flash(zs, EV_Q, band.reshape(nbs * bt, akv2), 0, band.reshape(nbs * bt, akv2), 1, Tk=bt, tk=bt,
                  mode="window", k_off=past - n_buf, **_nsa_common(nbs, Ts, past))
    ob_s, st_s = recurrence(zs, (EV_RQ, EV_RK, EV_RV, EV_RG), aux, mode="ret", nb=nbs, T=Ts, pos_off=past,
                            s0t=jnp.swapaxes(s0, -1, -2))
    xs = even_out(o_cmp, o_slc, o_win, zs, EV_GT, ob_s, e01, wa, wb, g[1:2], xs)

    shp = lambda x, n, t: x.reshape(n, t, 2, A_GROUPS, A_HEAD_DIM)
    wkeep = min(WINDOW, T)
    outs = (shp(kv_cmp_p, B, T), shp(kv_cmp_s, nbs, Ts), shp(kv_slc_p, B, T), shp(kv_slc_s, nbs, Ts),
            shp(kv_win_p, B, T)[:, T - wkeep:], shp(band[:, Ts:], nbs, n_buf),
            jnp.swapaxes(st_p, -1, -2), jnp.swapaxes(st_s, -1, -2).astype(s0.dtype))
    return xp, xs, outs


def _odd_layer(xp, xs, p, o, layer, dims, caches):
    B, T, nbs, Ts, past = dims
    g = p["norm_g"]
    w_in = p["w_in_c"][o].astype(BF16)
    w_out = p["w_out_c"][o].astype(BF16)
    cw = C_HEADS * 2 * C_HEAD_DIM
    wc, wd = w_out[:cw], w_out[cw:]
    lam_init = 0.8 - 0.6 * math.exp(-0.3 * layer)
    lam_p = p["diff_lam"][o]
    lb_logits = p["hgrn_lb"]
    dcommon = dict(G=2 * C_HEADS, hg=1, dk=C_HEAD_DIM, dv=2 * C_HEAD_DIM, g_per_v=2, mode="causal", k_off=0)

    zp, zpb, ktp = in_proj(xp, g[0:1], w_in, w_in[:, cw:2 * cw].T, jnp.ones((cw, 1), F32))
    o_diff = attend(zp, 0, ktp, 0, zpb, 2, nb=B, T=T, tq=_pick(T, (512, 256, 128, 64, 32, 16, 8)),
                    tk=_pick(T, (512, 256, 128)), G=2 * C_HEADS, hg=1, dk=C_HEAD_DIM, dv=2 * C_HEAD_DIM, g_per_v=2,
                    mode="causal")
    od_p, st_p = recurrence(zp, (3, 4, 5, 6), lb_logits, mode="hgrn", nb=B, T=T, pos_off=0, layer=o)
    xp = odd_out(o_diff, od_p, lam_p, lam_init, wc, wd, g[1:2], xp)

    cache_diff, s0, page_table = caches
    zs = rms_matmul(xs, g[0:1], w_in, F32)
    kv_s = zs[:, cw:3 * cw]
    o_diff = diff_decode(zs, cache_diff, o, page_table, nbs, Ts)
    od_s, st_s = recurrence(zs, (3, 4, 5, 6), lb_logits, mode="hgrn", nb=nbs, T=Ts, pos_off=past,
                            s0t=jnp.swapaxes(s0, -1, -2), layer=o)
    xs = odd_out(o_diff, od_s, lam_p, lam_init, wc, wd, g[1:2], xs)

    shp = lambda x, n, t: x.reshape(n, t, 2, C_HEADS, 2 * C_HEAD_DIM)
    outs = (shp(zp[:, cw:3 * cw], B, T), shp(kv_s, nbs, Ts),
            jnp.swapaxes(st_p, -1, -2), jnp.swapaxes(st_s, -1, -2).astype(s0.dtype))
    return xp, xs, outs


def _tail_layers(xp, xs, p, layer, dims, mem_prompt, cache_mem):
    B, T, nbs, Ts, _ = dims
    g = p["norm_g"][layer]
    D = xp.shape[1]
    w_q = p["w_xq"][layer].astype(BF16)
    w_kv = p["w_xkv"][layer].astype(BF16)
    w_o = p["w_xo"][layer].astype(BF16)
    w_up = p["w_up"][layer].astype(BF16)
    w_down = p["w_down"][layer].astype(BF16)
    n_mem = mem_prompt.shape[1]
    kv_mem = rms_matmul(mem_prompt.reshape(B * n_mem, D), g[0:1], w_kv, F32, norm=False).reshape(B, n_mem, 2 * D)
    qp = rms_matmul(xp, g[2:3], w_q, BF16)
    xp = proj_res(xattn(qp, kv_mem, T), w_o, g[3:4], xp)
    qs = rms_matmul(xs, g[2:3], w_q, BF16)
    xs = proj_res(xattn(qs, cache_mem, Ts, layer=layer), w_o, g[3:4], xs)
    xp = mlp(xp, g[4:5], g[5:6], w_up, w_down)
    xs = mlp(xs, g[4:5], g[5:6], w_up, w_down)
    return xp, xs, kv_mem.reshape(B, n_mem, 2, X_HEADS, D // X_HEADS)


def kernel(x_prompt, x_sample, cache_nsa_cmp_kv, cache_nsa_slc_kv, cache_nsa_win_kv, state_ret, cache_diff_kv, state_hgrn, cache_mem_kv, page_table, mem_prompt, norm_g, w_in_a, cmp_pos, cmp_w, w_out_a, w_in_c, diff_lam, hgrn_lb, w_out_c, w_xq, w_xkv, w_xo, w_up, w_down):
    B, T, D = x_prompt.shape
    nbs, Ts, _ = x_sample.shape
    depth = norm_g.shape[0]
    n_pages = page_table.shape[1]
    past = n_pages * PAGE_SIZE
    dims = (B, T, nbs, Ts, past)
    p = dict(w_in_a=w_in_a, cmp_pos=cmp_pos, cmp_w=cmp_w, w_out_a=w_out_a, w_in_c=w_in_c, diff_lam=diff_lam,
             hgrn_lb=hgrn_lb.astype(F32), w_out_c=w_out_c, w_xq=w_xq, w_xkv=w_xkv, w_xo=w_xo, w_up=w_up, w_down=w_down)
    xp = x_prompt.reshape(B * T, D)
    xs = x_sample.reshape(nbs * Ts, D)
    feat_major = lambda c: c.transpose(0, 1, 3, 4, 5, 2).reshape(c.shape[0], c.shape[1], -1, c.shape[2])
    rows_of = lambda c: c.reshape(c.shape[0], c.shape[1], -1, c.shape[-1])
    c_cmp, c_slc, c_diff = feat_major(cache_nsa_cmp_kv), feat_major(cache_nsa_slc_kv), rows_of(cache_diff_kv)
    mshape = cache_mem_kv.shape
    c_mem = cache_mem_kv.reshape(mshape[:5] + (mshape[5] // 128, 128)).transpose(0, 1, 2, 3, 5, 4, 6)
    c_mem = c_mem.reshape(mshape[0], mshape[1], -1, 128)
    win_all = cache_nsa_win_kv.reshape(cache_nsa_win_kv.shape[0], nbs, cache_nsa_win_kv.shape[2], -1)
    ev, od, mem = [], [], []
    for layer in range(depth):
        pl_ = dict(p, norm_g=norm_g[layer])
        if layer % 2 == 0:
            a = layer // 2
            xp, xs, outs = _even_layer(xp, xs, pl_, a, dims, (c_cmp, c_slc, win_all[a], state_ret[a], page_table))
            ev.append(outs)
        else:
            o = layer // 2
            xp, xs, outs = _odd_layer(xp, xs, pl_, o, layer, dims, (c_diff, state_hgrn[o], page_table))
            od.append(outs)
        xp, xs, kvm = _tail_layers(xp, xs, dict(p, norm_g=norm_g), layer, dims, mem_prompt, c_mem)
        mem.append(kvm)
    stack = lambda lst, i: jnp.stack([t[i] for t in lst])
    return (xp.reshape(B, T, D), xs.reshape(nbs, Ts, D),
            stack(ev, 0), stack(ev, 1), stack(ev, 2), stack(ev, 3), stack(ev, 4), stack(ev, 5), stack(ev, 6), stack(ev, 7),
            stack(od, 0), stack(od, 1), stack(od, 2), stack(od, 3), jnp.stack(mem))
```

```python
import functools
import math

import numpy as np
import jax
import jax.numpy as jnp
from jax import lax
from jax.experimental import pallas as pl
from jax.experimental.pallas import tpu as pltpu

F32 = jnp.float32
BF16 = jnp.bfloat16

EPS = 1e-6
NEG = -1e30
BIG = 1e9
A_HEADS, A_GROUPS, A_HEAD_DIM = 8, 2, 64
CMP_LEN, CMP_STRIDE, SLC_LEN, SLC_TOPK, WINDOW = 32, 16, 64, 16, 512
B_HEADS, B_DK = 4, 128
ROPE_BASE = 10000.0
CHUNK = 64
C_HEADS, C_HEAD_DIM = 4, 64
D_HEADS = 4
F_FLOOR = 1e-6
X_HEADS = 4
PAGE_SIZE = 128
SUB = 8
LANES = 128

VMEM_LIMIT = 56 * 1024 * 1024

NT = (((1,), (1,)), ((), ()))
TN = (((0,), (0,)), ((), ()))


def _cparams(n_grid):
    return pltpu.CompilerParams(dimension_semantics=("arbitrary",) * n_grid, vmem_limit_bytes=VMEM_LIMIT)


def _pick(n, cands):
    for c in cands:
        if n % c == 0:
            return c
    return n


def _split3(x):
    hi = x.astype(BF16)
    r = x - hi.astype(F32)
    mid = r.astype(BF16)
    lo = (r - mid.astype(F32)).astype(BF16)
    return hi, mid, lo


def _dot_x01(x, m01):
    return sum(jnp.dot(p, m01, preferred_element_type=F32) for p in _split3(x))


def _dot_01x(m01, x):
    return sum(jnp.dot(m01, p, preferred_element_type=F32) for p in _split3(x))


def _dot3_nt(a, b):
    ah = a.astype(BF16)
    al = (a - ah.astype(F32)).astype(BF16)
    bh = b.astype(BF16)
    bl = (b - bh.astype(F32)).astype(BF16)
    d = lambda x, y: lax.dot_general(x, y, NT, preferred_element_type=F32)
    return d(ah, bh) + d(ah, bl) + d(al, bh)


def _sigmoid(x):
    return 1.0 / (1.0 + jnp.exp(-x))


def _rms(x, g):
    return x * lax.rsqrt(jnp.mean(x * x, axis=-1, keepdims=True) + EPS) * g


def _rms_matmul_kernel(x_ref, g_ref, w_ref, o_ref, *, norm):
    x = x_ref[...]
    if norm:
        x = _rms(x, g_ref[...])
    o_ref[...] = jnp.dot(x.astype(BF16), w_ref[...], preferred_element_type=F32).astype(o_ref.dtype)


def rms_matmul(x, g, w, out_dtype, norm=True):
    R, D = x.shape
    N = w.shape[1]
    tm = _pick(R, (512, 256, 128, 64, 32, 16, 8))
    tn = _pick(N, (1792, 1152, 1024, 896, 768, 640, 512, 384, 256, 128))
    return pl.pallas_call(
        functools.partial(_rms_matmul_kernel, norm=norm),
        out_shape=jax.ShapeDtypeStruct((R, N), out_dtype),
        grid=(R // tm, N // tn),
        in_specs=[pl.BlockSpec((tm, D), lambda i, j: (i, 0)),
                  pl.BlockSpec((1, D), lambda i, j: (0, 0)),
                  pl.BlockSpec((D, tn), lambda i, j: (0, j))],
        out_specs=pl.BlockSpec((tm, tn), lambda i, j: (i, j)),
        compiler_params=_cparams(2), name="rms_matmul")(x, g, w)


def _in_proj_kernel(x_ref, g_ref, w_ref, wt_ref, ws_ref, z_ref, zb_ref, kt_ref, xn_scr):
    j = pl.program_id(1)

    @pl.when(j == 0)
    def _():
        xn = _rms(x_ref[...], g_ref[...]).astype(BF16)
        xn_scr[...] = xn
        t = lax.dot_general(wt_ref[...], xn, NT, preferred_element_type=F32)
        kt_ref[...] = (t * ws_ref[...]).astype(kt_ref.dtype)

    z = jnp.dot(xn_scr[...], w_ref[...], preferred_element_type=F32)
    z_ref[...] = z
    zb_ref[...] = z.astype(BF16)


def in_proj(x, g, w, wt, wscale):
    R, D = x.shape
    N = w.shape[1]
    Fk = wt.shape[0]
    tm = _pick(R, (1024, 512, 256, 128, 64, 32, 16, 8))
    tn = _pick(N, (1792, 1152, 1024, 896, 768, 640, 512, 384, 256, 128))
    return pl.pallas_call(
        _in_proj_kernel,
        out_shape=(jax.ShapeDtypeStruct((R, N), F32), jax.ShapeDtypeStruct((R, N), BF16),
                   jax.ShapeDtypeStruct((Fk, R), BF16)),
        grid=(R // tm, N // tn),
        in_specs=[pl.BlockSpec((tm, D), lambda i, j: (i, 0)),
                  pl.BlockSpec((1, D), lambda i, j: (0, 0)),
                  pl.BlockSpec((D, tn), lambda i, j: (0, j)),
                  pl.BlockSpec((Fk, D), lambda i, j: (0, 0)),
                  pl.BlockSpec((Fk, 1), lambda i, j: (0, 0))],
        out_specs=(pl.BlockSpec((tm, tn), lambda i, j: (i, j)), pl.BlockSpec((tm, tn), lambda i, j: (i, j)),
                   pl.BlockSpec((Fk, tm), lambda i, j: (0, i))),
        scratch_shapes=[pltpu.VMEM((tm, D), BF16)],
        compiler_params=_cparams(2), name="in_proj")(x, g, w, wt, wscale)


def _mlp_kernel(x_ref, g4_ref, g5_ref, wu_ref, wd_ref, o_ref, xn_scr, acc_scr):
    j = pl.program_id(1)

    @pl.when(j == 0)
    def _():
        xn_scr[...] = _rms(x_ref[...], g4_ref[...]).astype(BF16)
        acc_scr[...] = jnp.zeros_like(acc_scr)

    h = jnp.dot(xn_scr[...], wu_ref[...], preferred_element_type=F32)
    h = jnp.square(jnp.maximum(h, 0.0))
    acc_scr[...] += jnp.dot(h.astype(BF16), wd_ref[...], preferred_element_type=F32)

    @pl.when(j == pl.num_programs(1) - 1)
    def _():
        o_ref[...] = x_ref[...] + _rms(acc_scr[...], g5_ref[...])


def mlp(x, g4, g5, w_up, w_down):
    R, D = x.shape
    F = w_up.shape[1]
    tm = _pick(R, (1024, 512, 256, 128, 64, 32, 16, 8))
    tf = _pick(F, (1024, 512, 256, 128))
    return pl.pallas_call(
        _mlp_kernel,
        out_shape=jax.ShapeDtypeStruct((R, D), F32),
        grid=(R // tm, F // tf),
        in_specs=[pl.BlockSpec((tm, D), lambda i, j: (i, 0)),
                  pl.BlockSpec((1, D), lambda i, j: (0, 0)),
                  pl.BlockSpec((1, D), lambda i, j: (0, 0)),
                  pl.BlockSpec((D, tf), lambda i, j: (0, j)),
                  pl.BlockSpec((tf, D), lambda i, j: (j, 0))],
        out_specs=pl.BlockSpec((tm, D), lambda i, j: (i, 0)),
        scratch_shapes=[pltpu.VMEM((tm, D), BF16), pltpu.VMEM((tm, D), F32)],
        compiler_params=_cparams(2), name="mlp")(x, g4, g5, w_up, w_down)


def _proj_res_kernel(a_ref, w_ref, g_ref, x_ref, o_ref):
    y = jnp.dot(a_ref[...], w_ref[...], preferred_element_type=F32)
    o_ref[...] = x_ref[...] + _rms(y, g_ref[...])


def proj_res(a, w, g, x):
    R, D = x.shape
    K = a.shape[1]
    tm = _pick(R, (512, 256, 128, 64, 32, 16, 8))
    return pl.pallas_call(
        _proj_res_kernel,
        out_shape=jax.ShapeDtypeStruct((R, D), F32),
        grid=(R // tm,),
        in_specs=[pl.BlockSpec((tm, K), lambda i: (i, 0)),
                  pl.BlockSpec((K, D), lambda i: (0, 0)),
                  pl.BlockSpec((1, D), lambda i: (0, 0)),
                  pl.BlockSpec((tm, D), lambda i: (i, 0))],
        out_specs=pl.BlockSpec((tm, D), lambda i: (i, 0)),
        compiler_params=_cparams(1), name="proj_res")(a, w, g, x)


def _even_out_kernel(oc_ref, os_ref, ow_ref, gt_ref, ob_ref, e_ref, wa_ref, wb_ref, g_ref, x_ref, o_ref):
    gates = _sigmoid(gt_ref[...])
    ge = _dot_x01(gates, e_ref[...])
    aq = oc_ref.shape[1]
    oa = ge[:, :aq] * oc_ref[...] + ge[:, aq:2 * aq] * os_ref[...] + ge[:, 2 * aq:] * ow_ref[...]
    y = jnp.dot(oa.astype(BF16), wa_ref[...], preferred_element_type=F32)
    y += jnp.dot(ob_ref[...], wb_ref[...], preferred_element_type=F32)
    o_ref[...] = x_ref[...] + _rms(y, g_ref[...])


def even_out(o_cmp, o_slc, o_win, z, gt_blk, o_b, e01, wa, wb, g, x):
    R, D = x.shape
    aq = o_cmp.shape[1]
    tm = _pick(R, (256, 128, 64, 32, 16, 8))
    row = lambda i: (i, 0)
    fix = lambda i: (0, 0)
    return pl.pallas_call(
        _even_out_kernel,
        out_shape=jax.ShapeDtypeStruct((R, D), F32),
        grid=(R // tm,),
        in_specs=[pl.BlockSpec((tm, aq), row), pl.BlockSpec((tm, aq), row), pl.BlockSpec((tm, aq), row),
                  pl.BlockSpec((tm, 128), lambda i: (i, gt_blk)),
                  pl.BlockSpec((tm, o_b.shape[1]), row),
                  pl.BlockSpec(e01.shape, fix), pl.BlockSpec(wa.shape, fix), pl.BlockSpec(wb.shape, fix),
                  pl.BlockSpec((1, D), fix), pl.BlockSpec((tm, D), row)],
        out_specs=pl.BlockSpec((tm, D), row),
        compiler_params=_cparams(1), name="even_out")(o_cmp, o_slc, o_win, z, o_b, e01, wa, wb, g, x)


def _odd_out_kernel(oc_ref, od_ref, lam_ref, wc_ref, wd_ref, g_ref, x_ref, o_ref, *, lam_init, heads, dv):
    lp = lam_ref[...]
    lam = (jnp.exp(jnp.sum(lp[0:1] * lp[1:2], axis=-1, keepdims=True))
           - jnp.exp(jnp.sum(lp[2:3] * lp[3:4], axis=-1, keepdims=True)) + lam_init)
    parts = []
    for h in range(heads):
        o1 = oc_ref[:, (2 * h) * dv:(2 * h + 1) * dv]
        o2 = oc_ref[:, (2 * h + 1) * dv:(2 * h + 2) * dv]
        o = o1 - lam * o2
        o = o * lax.rsqrt(jnp.mean(o * o, axis=-1, keepdims=True) + EPS) * (1.0 - lam_init)
        parts.append(o.astype(BF16))
    oc = jnp.concatenate(parts, axis=-1)
    y = jnp.dot(oc, wc_ref[...], preferred_element_type=F32)
    y += jnp.dot(od_ref[...], wd_ref[...], preferred_element_type=F32)
    o_ref[...] = x_ref[...] + _rms(y, g_ref[...])


def odd_out(o_diff, o_d, lam_p, lam_init, wc, wd, g, x):
    R, D = x.shape
    tm = _pick(R, (256, 128, 64, 32, 16, 8))
    row = lambda i: (i, 0)
    fix = lambda i: (0, 0)
    return pl.pallas_call(
        functools.partial(_odd_out_kernel, lam_init=lam_init, heads=C_HEADS, dv=2 * C_HEAD_DIM),
        out_shape=jax.ShapeDtypeStruct((R, D), F32),
        grid=(R // tm,),
        in_specs=[pl.BlockSpec((tm, o_diff.shape[1]), row), pl.BlockSpec((tm, o_d.shape[1]), row),
                  pl.BlockSpec(lam_p.shape, fix), pl.BlockSpec(wc.shape, fix), pl.BlockSpec(wd.shape, fix),
                  pl.BlockSpec((1, D), fix), pl.BlockSpec((tm, D), row)],
        out_specs=pl.BlockSpec((tm, D), row),
        compiler_params=_cparams(1), name="odd_out")(o_diff, o_d, lam_p, wc, wd, g, x)


def _xattn_kernel(q_ref, kv_ref, o_ref, *, heads, scale, rows):
    dm = q_ref.shape[1]
    hd = dm // heads
    ss, vhs = [], []
    for h in range(heads):
        qh = q_ref[:, h * hd:(h + 1) * hd]
        if rows:
            nh = hd // 128
            per_tok = 2 * nh * heads
            n_mem = kv_ref.shape[2] // per_tok
            row = lambda slot: jnp.concatenate(
                [kv_ref[0, 0, pl.ds((slot * nh + i) * heads + h, n_mem, stride=per_tok), :] for i in range(nh)],
                axis=-1).astype(BF16)
            kh, vh = row(0), row(1)
        else:
            kh = kv_ref[0, :, h * hd:(h + 1) * hd].astype(BF16)
            vh = kv_ref[0, :, dm + h * hd:dm + (h + 1) * hd].astype(BF16)
        ss.append(lax.dot_general(qh, kh, NT, preferred_element_type=F32) * scale)
        vhs.append(vh)
    ps = []
    for s in ss:
        e = jnp.exp(s - jnp.max(s, axis=-1, keepdims=True))
        ps.append((e / jnp.sum(e, axis=-1, keepdims=True)).astype(BF16))
    for h in range(heads):
        o = jnp.dot(ps[h], vhs[h], preferred_element_type=F32)
        o_ref[:, h * hd:(h + 1) * hd] = o.astype(o_ref.dtype)


def xattn(q, kv, rows_per_batch, layer=None):
    R, D = q.shape
    tq = _pick(rows_per_batch, (512, 256, 128, 64, 32, 16, 8))
    per = rows_per_batch // tq
    if layer is None:
        kv_spec = pl.BlockSpec((1,) + kv.shape[1:], lambda i: (i // per, 0, 0))
    else:
        kv_spec = pl.BlockSpec((1, 1) + kv.shape[2:], lambda i: (layer, i // per, 0, 0))
    return pl.pallas_call(
        functools.partial(_xattn_kernel, heads=X_HEADS, scale=(D // X_HEADS) ** -0.5, rows=layer is not None),
        out_shape=jax.ShapeDtypeStruct((R, D), BF16),
        grid=(R // tq,),
        in_specs=[pl.BlockSpec((tq, D), lambda i: (i, 0)), kv_spec],
        out_specs=pl.BlockSpec((tq, D), lambda i: (i, 0)),
        compiler_params=_cparams(1), name="xattn")(q, kv)


def _compress_kernel(x_ref, w_ref, pe_ref, o_ref, vt_ref):
    x = x_ref[0].astype(BF16)
    ab = jnp.dot(x, w_ref[...], preferred_element_type=F32)
    cab = jnp.dot(pe_ref[...].astype(BF16), w_ref[...], preferred_element_type=F32)
    half = ab.shape[1] // 2
    a, b = ab[:, :half], ab[:, half:]
    const = cab[0:1, :half] + cab[1:2, half:]
    n = a.shape[0]
    out = a + pltpu.roll(b, n - 1, 0) + const
    o_ref[0] = out
    vt_ref[0] = out[:, half // 2:].T.astype(vt_ref.dtype)


def compress(xc, wbig, pe2):
    nb, n_chunk, kdim = xc.shape
    cout = wbig.shape[1] // 2
    return pl.pallas_call(
        _compress_kernel,
        out_shape=(jax.ShapeDtypeStruct((nb, n_chunk, cout), F32), jax.ShapeDtypeStruct((nb, cout // 2, n_chunk), BF16)),
        grid=(nb,),
        in_specs=[pl.BlockSpec((1, n_chunk, kdim), lambda b: (b, 0, 0)),
                  pl.BlockSpec(wbig.shape, lambda b: (0, 0)),
                  pl.BlockSpec(pe2.shape, lambda b: (0, 0))],
        out_specs=(pl.BlockSpec((1, n_chunk, cout), lambda b: (b, 0, 0)),
                   pl.BlockSpec((1, cout // 2, n_chunk), lambda b: (b, 0, 0))),
        compiler_params=_cparams(1), name="compress")(xc, wbig, pe2)


def _cmp_kernel(q_ref, kv_ref, a_ref, o_ref, sel_ref, *, tq, q_off, n_slc, nbp, bpb):
    G, hg, d = A_GROUPS, A_HEADS // A_GROUPS, A_HEAD_DIM
    qi = pl.program_id(1)
    ncmp = kv_ref.shape[1]
    pos_q = q_off + qi * tq + lax.broadcasted_iota(jnp.int32, (tq, 1), 0)
    cmp_end = lax.broadcasted_iota(jnp.int32, (1, ncmp), 1) * CMP_STRIDE + (CMP_LEN - 1)
    mask3 = (cmp_end <= pos_q)[None]
    jb = lax.broadcasted_iota(jnp.int32, (1, nbp), 1)
    blk_q = lax.shift_right_logical(pos_q, int(math.log2(SLC_LEN)))
    valid = jb <= blk_q
    forced = valid & ((jb == 0) | (jb >= blk_q - 1))
    k_top = min(SLC_TOPK, n_slc)
    scores = []
    for bi in range(bpb):
        rs = slice(bi * tq, (bi + 1) * tq)
        for g in range(G):
            kc = kv_ref[bi, :, g * d:(g + 1) * d].astype(BF16)
            vc = kv_ref[bi, :, (G + g) * d:(G + g + 1) * d].astype(BF16)
            qs = jnp.concatenate([q_ref[rs, (g * hg + h) * d:(g * hg + h + 1) * d] for h in range(hg)], axis=0)
            qs = (qs * d ** -0.5).astype(BF16)
            s = lax.dot_general(qs, kc, NT, preferred_element_type=F32).reshape(hg, tq, ncmp)
            s = jnp.where(mask3, s, NEG)
            e = jnp.where(mask3, jnp.exp(s - jnp.max(s, axis=-1, keepdims=True)), 0.0)
            den = jnp.sum(e, axis=-1, keepdims=True)
            p = e / jnp.where(den > 0.0, den, 1.0)
            o = jnp.dot(p.reshape(hg * tq, ncmp).astype(BF16), vc, preferred_element_type=F32)
            for h in range(hg):
                o_ref[rs, (g * hg + h) * d:(g * hg + h + 1) * d] = o[h * tq:(h + 1) * tq]
            p_slc = _dot_x01(jnp.sum(p, axis=0), a_ref[...])
            score = jnp.where(forced, BIG, jnp.where(valid, p_slc, -BIG))
            scores.append(jnp.where(jb < n_slc, score, -jnp.inf))
    score = jnp.concatenate(scores, axis=0)
    sel = jnp.zeros(score.shape, F32)
    jbf = jb.astype(F32)
    for _ in range(k_top):
        mx = jnp.max(score, axis=-1, keepdims=True)
        jm = jnp.min(jnp.where(score == mx, jbf, float(nbp)), axis=-1, keepdims=True)
        hit = jbf == jm
        sel = jnp.where(hit, 1.0, sel)
        score = jnp.where(hit, -jnp.inf, score)
    for bi in range(bpb):
        for g in range(G):
            r0 = (bi * G + g) * tq
            ok = valid & (sel[r0:r0 + tq] > 0.5)
            sel_ref[bi * tq:(bi + 1) * tq, g * nbp:(g + 1) * nbp] = jnp.where(ok, 0.0, SEL_OFF).astype(BF16)


def cmp_attend(z, q_blk, kvc, a01, nb, Tq, q_off, n_slc):
    aq = A_HEADS * A_HEAD_DIM
    nbp = a01.shape[1]
    tq = _pick(Tq, (256, 128, 64, 32, 16, 8))
    nq = Tq // tq
    bpb = _pick(nb, (8, 4, 2, 1)) if nq == 1 and tq <= 16 else 1
    ncmp = kvc.shape[1]
    return pl.pallas_call(
        functools.partial(_cmp_kernel, tq=tq, q_off=q_off, n_slc=n_slc, nbp=nbp, bpb=bpb),
        out_shape=(jax.ShapeDtypeStruct((nb * Tq, aq), F32),
                   jax.ShapeDtypeStruct((nb * Tq, A_GROUPS * nbp), BF16)),
        grid=(nb // bpb, nq),
        in_specs=[pl.BlockSpec((bpb * tq, aq), lambda b, i: (b * nq + i, q_blk)),
                  pl.BlockSpec((bpb, ncmp, kvc.shape[2]), lambda b, i: (b, 0, 0)),
                  pl.BlockSpec(a01.shape, lambda b, i: (0, 0))],
        out_specs=(pl.BlockSpec((bpb * tq, aq), lambda b, i: (b * nq + i, 0)),
                   pl.BlockSpec((bpb * tq, A_GROUPS * nbp), lambda b, i: (b * nq + i, 0))),
        compiler_params=_cparams(2), name="cmp_attend")(z, kvc, a01)


class _FlashCfg:
    def __init__(self, **kw):
        self.__dict__.update(kw)


def _tile_range(c, qi):
    q_lo = c.q_off + qi * c.tq
    last = (q_lo + c.tq - 1 - c.k_off) // c.tk
    if c.mode == "window":
        first = (q_lo - (WINDOW - 1) - c.k_off) // c.tk
    else:
        first = 0 * qi
    return first, last


def _flash_kernel(*refs, c):
    if c.sel:
        q_ref, k_ref, v_ref, sel_ref, o_ref, qs_scr, m_scr, l_scr, acc_scr = refs
    else:
        q_ref, k_ref, v_ref, o_ref, qs_scr, m_scr, l_scr, acc_scr = refs
        sel_ref = None
    qi, j = pl.program_id(1), pl.program_id(2)
    tq, tk, G, hg, dk, dv = c.tq, c.tk, c.G, c.hg, c.dk, c.dv

    @pl.when(j == 0)
    def _():
        m_scr[...] = jnp.full(m_scr.shape, NEG, F32)
        l_scr[...] = jnp.zeros(l_scr.shape, F32)
        acc_scr[...] = jnp.zeros(acc_scr.shape, F32)
        for g in range(G):
            qs = jnp.concatenate([q_ref[:, (g * hg + h) * dk:(g * hg + h + 1) * dk] for h in range(hg)], axis=0)
            qs_scr[g] = (qs * dk ** -0.5).astype(BF16)

    first, last = _tile_range(c, qi)
    jabs = first + j

    @pl.when((jabs >= 0) & (jabs <= jnp.minimum(last, c.nk - 1)))
    def _():
        pos_q = c.q_off + qi * tq + lax.broadcasted_iota(jnp.int32, (tq, 1), 0)
        tok = jabs * tk + lax.broadcasted_iota(jnp.int32, (1, tk), 1)
        dpos = pos_q - (c.k_off + tok)
        mask = dpos >= 0
        if c.mode == "window":
            mask = mask & (dpos < WINDOW)
        if c.sel:
            jb = lax.broadcasted_iota(jnp.int32, (c.nbp, 1), 0)
            e01 = jnp.where(jb == lax.shift_right_logical(tok, int(math.log2(SLC_LEN))), 1.0, 0.0).astype(BF16)
        for g in range(G):
            kg = k_ref[:, g * dk:(g + 1) * dk].astype(BF16)
            vi = g // c.g_per_v
            vg = v_ref[:, vi * dv:(vi + 1) * dv].astype(BF16)
            s = lax.dot_general(qs_scr[g], kg, NT, preferred_element_type=F32).reshape(hg, tq, tk)
            mk = mask
            if c.sel:
                st = jnp.dot(sel_ref[:, g * c.nbp:(g + 1) * c.nbp], e01, preferred_element_type=F32)
                mk = mk & (st > 0.5)
            mk = mk[None]
            s = jnp.where(mk, s, NEG)
            m_old = m_scr[g]
            m_new = jnp.maximum(m_old, jnp.max(s, axis=-1, keepdims=True))
            alpha = jnp.exp(m_old - m_new)
            p = jnp.where(mk, jnp.exp(s - m_new), 0.0)
            l_scr[g] = alpha * l_scr[g] + jnp.sum(p, axis=-1, keepdims=True)
            pv = jnp.dot(p.reshape(hg * tq, tk).astype(BF16), vg, preferred_element_type=F32)
            acc_scr[g] = alpha * acc_scr[g] + pv.reshape(hg, tq, dv)
            m_scr[g] = m_new

    @pl.when(j == pl.num_programs(2) - 1)
    def _():
        for g in range(G):
            l = l_scr[g]
            o = acc_scr[g] / jnp.where(l > 0.0, l, 1.0)
            for h in range(hg):
                o_ref[:, (g * hg + h) * dv:(g * hg + h + 1) * dv] = o[h].astype(o_ref.dtype)


def flash(q2d, q_blk, k2d, k_blk, v2d, v_blk, *, nb, Tq, Tk, tq, tk, G, hg, dk, dv, g_per_v, mode,
          q_off, k_off, sel=None, nbp=0):
    nq, nk = Tq // tq, Tk // tk
    assert Tq % tq == 0 and Tk % tk == 0
    c = _FlashCfg(tq=tq, tk=tk, G=G, hg=hg, dk=dk, dv=dv, g_per_v=g_per_v, mode=mode, q_off=q_off, k_off=k_off,
                  nk=nk, sel=sel is not None, nbp=nbp)
    steps = max(min(_tile_range(c, i)[1], nk - 1) - _tile_range(c, i)[0] + 1 for i in range(nq))
    qw, kw, vw = G * hg * dk, G * dk, (G // g_per_v) * dv

    def kv_map(blk):
        def f(b, i, j):
            first, last = _tile_range(c, i)
            return (b * nk + jnp.clip(first + j, 0, jnp.minimum(last, nk - 1)), blk)
        return f

    in_specs = [pl.BlockSpec((tq, qw), lambda b, i, j: (b * nq + i, q_blk)),
                pl.BlockSpec((tk, kw), kv_map(k_blk)),
                pl.BlockSpec((tk, vw), kv_map(v_blk))]
    args = [q2d, k2d, v2d]
    if sel is not None:
        in_specs.append(pl.BlockSpec((tq, G * nbp), lambda b, i, j: (b * nq + i, 0)))
        args.append(sel)
    return pl.pallas_call(
        functools.partial(_flash_kernel, c=c),
        out_shape=jax.ShapeDtypeStruct((nb * Tq, G * hg * dv), F32),
        grid=(nb, nq, steps),
        in_specs=in_specs,
        out_specs=pl.BlockSpec((tq, G * hg * dv), lambda b, i, j: (b * nq + i, 0)),
        scratch_shapes=[pltpu.VMEM((G, hg * tq, dk), BF16), pltpu.VMEM((G, hg, tq, 1), F32),
                        pltpu.VMEM((G, hg, tq, 1), F32), pltpu.VMEM((G, hg, tq, dv), F32)],
        compiler_params=_cparams(3), name="flash_" + mode + ("_sel" if sel is not None else ""))(*args)


LOG2E = 1.4426950408889634
SEL_OFF = -2.0 ** 30


def _attn_kernel(tab_ref, q_ref, kt_ref, v_ref, *refs, c):
    if c.sel:
        sel_ref, e01_ref = refs[:2]
        refs = refs[2:]
    o_ref, qs_scr, s_scr, p_scr, m_scr, l_scr, a_scr, acc_scr = refs
    n = pl.program_id(1)
    qi, kj, flags = tab_ref[0, n], tab_ref[1, n], tab_ref[2, n]
    tq, tk, G, hg, dk, dv = c.tq, c.tk, c.G, c.hg, c.dk, c.dv
    rows = hg * tq
    ncb = tk // LANES

    @pl.when((flags & 1) != 0)
    def _():
        m_scr[...] = jnp.full(m_scr.shape, NEG, F32)
        l_scr[...] = jnp.zeros(l_scr.shape, F32)
        acc_scr[...] = jnp.zeros(acc_scr.shape, F32)
        for g in range(G):
            qs = jnp.concatenate([q_ref[:, (g * hg + h) * dk:(g * hg + h + 1) * dk] for h in range(hg)], axis=0)
            qs_scr[g] = (qs * (dk ** -0.5 * LOG2E)).astype(BF16)

    def step(masked):
        if masked:
            dq = qi * tq - kj * tk + c.q_off - c.k_off
            d = (lax.broadcasted_iota(jnp.int32, (tq, tk), 0) - lax.broadcasted_iota(jnp.int32, (tq, tk), 1)) + dq
            ok = d >= 0
            if c.mode == "window":
                ok = ok & (d < WINDOW)
        for g in range(G):
            s = jnp.dot(qs_scr[g], kt_ref[g * dk:(g + 1) * dk, :], preferred_element_type=F32)
            if c.sel or masked:
                s = s.reshape(hg, tq, tk)
                if c.sel:
                    s = s + jnp.dot(sel_ref[:, g * c.nbp:(g + 1) * c.nbp], e01_ref[...], preferred_element_type=F32)[None]
                if masked:
                    s = jnp.where(ok[None], s, NEG)
                s = s.reshape(rows, tk)
            s_scr[g] = s
        for g in range(G):
            cols = [s_scr[g, :, cb * LANES:(cb + 1) * LANES] for cb in range(ncb)]
            mx = cols[0]
            for x in cols[1:]:
                mx = jnp.maximum(mx, x)
            m_old = m_scr[g]
            m_new = jnp.maximum(m_old, jnp.broadcast_to(jnp.max(mx, axis=-1, keepdims=True), (rows, LANES)))
            alpha = jnp.exp2(m_old - m_new)
            psum = None
            for cb in range(ncb):
                p = jnp.exp2(cols[cb] - m_new)
                if masked:
                    okc = jnp.broadcast_to(ok[None, :, cb * LANES:(cb + 1) * LANES], (hg, tq, LANES)).reshape(rows, LANES)
                    p = jnp.where(okc, p, 0.0)
                psum = p if psum is None else psum + p
                p_scr[g, :, cb * LANES:(cb + 1) * LANES] = p.astype(BF16)
            l_scr[g] = alpha * l_scr[g] + psum
            m_scr[g] = m_new
            a_scr[g] = alpha
        for g in range(G):
            vi = g // c.g_per_v
            pv = jnp.dot(p_scr[g], v_ref[:, vi * dv:(vi + 1) * dv], preferred_element_type=F32)
            acc_scr[g] = a_scr[g, :, :dv] * acc_scr[g] + pv

    @pl.when((flags & 4) != 0)
    def _():
        step(True)

    @pl.when((flags & 4) == 0)
    def _():
        step(False)

    @pl.when((flags & 2) != 0)
    def _():
        for g in range(G):
            l = jnp.sum(l_scr[g], axis=-1, keepdims=True)
            o = acc_scr[g] / jnp.where(l > 0.0, l, 1.0)
            for h in range(hg):
                o_ref[:, (g * hg + h) * dv:(g * hg + h + 1) * dv] = o[h * tq:(h + 1) * tq].astype(o_ref.dtype)


def attend(z, q_blk, kt, kt_blk, zb, v_blk, *, nb, T, tq, tk, G, hg, dk, dv, g_per_v, mode, sel=None, e01=None):
    nq, nk = T // tq, T // tk
    c = _FlashCfg(tq=tq, tk=tk, G=G, hg=hg, dk=dk, dv=dv, g_per_v=g_per_v, mode=mode, q_off=0, k_off=0, nk=nk,
                  sel=sel is not None, nbp=0 if sel is None else sel.shape[1] // G)
    tab = []
    for i in range(nq):
        first, last = _tile_range(c, i)
        first, last = max(first, 0), min(last, nk - 1)
        for j in range(first, last + 1):
            lo, hi = i * tq - (j * tk + tk - 1), i * tq + tq - 1 - j * tk
            masked = lo < 0 or (mode == "window" and hi >= WINDOW)
            tab.append((i, j, (j == first) * 1 + (j == last) * 2 + masked * 4))
    tab = jnp.asarray(np.array(tab, np.int32).T)
    npairs = tab.shape[1]
    rows = hg * tq
    in_specs = [pl.BlockSpec((tq, G * hg * dk), lambda b, n, t: (b * nq + t[0, n], q_blk)),
                pl.BlockSpec((G * dk, tk), lambda b, n, t: (kt_blk, b * nk + t[1, n])),
                pl.BlockSpec((tk, (G // g_per_v) * dv), lambda b, n, t: (b * nk + t[1, n], v_blk))]
    args = [z, kt, zb]
    if sel is not None:
        in_specs += [pl.BlockSpec((tq, sel.shape[1]), lambda b, n, t: (b * nq + t[0, n], 0)),
                     pl.BlockSpec((e01.shape[0], tk), lambda b, n, t: (0, t[1, n]))]
        args += [sel, e01]
    grid_spec = pltpu.PrefetchScalarGridSpec(
        num_scalar_prefetch=1, grid=(nb, npairs), in_specs=in_specs,
        out_specs=pl.BlockSpec((tq, G * hg * dv), lambda b, n, t: (b * nq + t[0, n], 0)),
        scratch_shapes=[pltpu.VMEM((G, rows, dk), BF16), pltpu.VMEM((G, rows, tk), F32),
                        pltpu.VMEM((G, rows, tk), BF16), pltpu.VMEM((G, rows, LANES), F32),
                        pltpu.VMEM((G, rows, LANES), F32), pltpu.VMEM((G, rows, LANES), F32),
                        pltpu.VMEM((G, rows, dv), F32)])
    return pl.pallas_call(
        functools.partial(_attn_kernel, c=c),
        out_shape=jax.ShapeDtypeStruct((nb * T, G * hg * dv), F32),
        grid_spec=grid_spec, compiler_params=_cparams(2),
        name="attend_" + mode + ("_sel" if sel is not None else ""))(tab, *args)


def _block_diag_queries(qt_ref, heads, tq, dk):
    zero = jnp.zeros((dk, tq), BF16)
    half = heads // 2
    top = jnp.concatenate([qt_ref[i * dk:(i + 1) * dk, :] if i < half else zero for i in range(heads)], axis=1)
    bot = jnp.concatenate([zero if i < half else qt_ref[i * dk:(i + 1) * dk, :] for i in range(heads)], axis=1)
    return jnp.concatenate([top, bot], axis=0)


def _cmp_t_kernel(qt_ref, kvc_ref, vct_ref, at_ref, o_ref, sel_ref, *, tq, n_slc, nbp):
    G, hg, d = A_GROUPS, A_HEADS // A_GROUPS, A_HEAD_DIM
    assert G == 2
    qi = pl.program_id(1)
    ncmp = kvc_ref.shape[1]
    pos_q = qi * tq + lax.broadcasted_iota(jnp.int32, (1, tq), 1)
    cmp_end = lax.broadcasted_iota(jnp.int32, (ncmp, 1), 0) * CMP_STRIDE + (CMP_LEN - 1)
    ok = cmp_end <= pos_q
    jb = lax.broadcasted_iota(jnp.int32, (nbp, 1), 0)
    blk_q = lax.shift_right_logical(pos_q, int(math.log2(SLC_LEN)))
    valid = jb <= blk_q
    forced = valid & ((jb == 0) | (jb >= blk_q - 1))
    st = jnp.dot(kvc_ref[0, :, :G * d].astype(BF16), _block_diag_queries(qt_ref, A_HEADS, tq, d),
                 preferred_element_type=F32)
    scores, outs, pgs, pgrp = [], [], [], []
    for g in range(G):
        ps = []
        for h in range(hg):
            e = g * hg + h
            x = jnp.where(ok, st[:, e * tq:(e + 1) * tq], NEG)
            p = jnp.where(ok, jnp.exp2(x - jnp.max(x, axis=0, keepdims=True)), 0.0)
            den = jnp.sum(p, axis=0, keepdims=True)
            ps.append(p / jnp.where(den > 0.0, den, 1.0))
        pgs.append(jnp.concatenate(ps, axis=1).astype(BF16))
        pgrp.append(functools.reduce(jnp.add, ps))
    for g in range(G):
        outs.append(jnp.dot(vct_ref[0, g * d:(g + 1) * d, :], pgs[g], preferred_element_type=F32))
    for g in range(G):
        p_slc = _dot_01x(at_ref[...], pgrp[g])
        score = jnp.where(forced, BIG, jnp.where(valid, p_slc, -BIG))
        scores.append(jnp.where(jb < n_slc, score, -jnp.inf))
    score = jnp.concatenate(scores, axis=1)
    jbf = jb.astype(F32)
    sel = jnp.zeros(score.shape, F32)
    for _ in range(min(SLC_TOPK, n_slc)):
        mx = jnp.max(score, axis=0, keepdims=True)
        jm = jnp.min(jnp.where(score == mx, jbf, float(nbp)), axis=0, keepdims=True)
        hit = jbf == jm
        sel = jnp.where(hit, 1.0, sel)
        score = jnp.where(hit, -jnp.inf, score)
    for g in range(G):
        okg = valid & (sel[:, g * tq:(g + 1) * tq] > 0.5)
        sel_ref[:, g * nbp:(g + 1) * nbp] = jnp.where(okg, 0.0, SEL_OFF).T.astype(BF16)
    ot = jnp.concatenate([outs[g][:, h * tq:(h + 1) * tq] for g in range(G) for h in range(hg)], axis=0)
    o_ref[...] = ot.T


def cmp_attend_t(tt, kvc, vct, at01, nb, T, n_slc):
    aq = A_HEADS * A_HEAD_DIM
    nbp = at01.shape[0]
    tq = _pick(T, (256, 128))
    nq = T // tq
    return pl.pallas_call(
        functools.partial(_cmp_t_kernel, tq=tq, n_slc=n_slc, nbp=nbp),
        out_shape=(jax.ShapeDtypeStruct((nb * T, aq), F32), jax.ShapeDtypeStruct((nb * T, A_GROUPS * nbp), BF16)),
        grid=(nb, nq),
        in_specs=[pl.BlockSpec((aq, tq), lambda b, i: (0, b * nq + i)),
                  pl.BlockSpec((1,) + kvc.shape[1:], lambda b, i: (b, 0, 0)),
                  pl.BlockSpec((1,) + vct.shape[1:], lambda b, i: (b, 0, 0)),
                  pl.BlockSpec(at01.shape, lambda b, i: (0, 0))],
        out_specs=(pl.BlockSpec((tq, aq), lambda b, i: (b * nq + i, 0)),
                   pl.BlockSpec((tq, A_GROUPS * nbp), lambda b, i: (b * nq + i, 0))),
        compiler_params=_cparams(2), name="cmp_attend_t")(tt, kvc, vct, at01)


def _attn_t_kernel(tab_ref, qt_ref, k_ref, vt_ref, *refs, c):
    if c.sel:
        selt_ref, e01t_ref = refs[:2]
        refs = refs[2:]
        o_ref, qbd_scr, s_scr, p_scr, m_scr, l_scr, a_scr, acc_scr, off_scr = refs
    else:
        o_ref, qbd_scr, s_scr, p_scr, m_scr, l_scr, a_scr, acc_scr = refs
    n = pl.program_id(1)
    qi, kj, flags = tab_ref[0, n], tab_ref[1, n], tab_ref[2, n]
    tq, tk, U, hpu, NV, hpv, dk, dvv = c.tq, c.tk, c.U, c.hpu, c.NV, c.hpv, c.dk, c.dvv
    ncu = hpu * tq
    NC = U * ncu

    @pl.when((flags & 1) != 0)
    def _():
        m_scr[...] = jnp.full(m_scr.shape, NEG, F32)
        l_scr[...] = jnp.zeros(l_scr.shape, F32)
        acc_scr[...] = jnp.zeros(acc_scr.shape, F32)
        for u in range(U):
            qbd_scr[u] = _block_diag_queries(qt_ref.at[u * hpu * dk:(u + 1) * hpu * dk, :], hpu, tq, dk)

    def step(masked):
        if masked:
            dq = qi * tq - kj * tk
            ok = (lax.broadcasted_iota(jnp.int32, (tk, tq), 1) - lax.broadcasted_iota(jnp.int32, (tk, tq), 0)) + dq >= 0
        ncv = hpv * tq
        for v in range(NV):
            u, uc = (v * hpv) // hpu, ((v * hpv) % hpu) * tq
            s_scr[v] = jnp.dot(k_ref[:, u * 2 * dk:(u + 1) * 2 * dk], qbd_scr[u, :, uc:uc + ncv],
                               preferred_element_type=F32)
        if c.sel:
            for g in range(c.G):
                off_scr[g] = jnp.dot(e01t_ref[...], selt_ref[g * c.nbp:(g + 1) * c.nbp, :], preferred_element_type=F32)

        rch = 64

        def scores(v, cb, r):
            cs = slice(cb * LANES, (cb + 1) * LANES)
            rs = slice(r * rch, (r + 1) * rch)
            c0 = (cb * LANES) % tq
            x = s_scr[v, rs, cs]
            if c.sel:
                x = x + off_scr[(v * hpv + (cb * LANES) // tq) // (NV * hpv // c.G), rs, c0:c0 + LANES]
            okc = ok[rs, c0:c0 + LANES] if masked else None
            if masked:
                x = jnp.where(okc, x, NEG)
            return x, okc

        fold = lambda x, op: functools.reduce(op, [x[i * SUB:(i + 1) * SUB] for i in range(rch // SUB)])
        for v in range(NV):
            for cb in range(ncv // LANES):
                cs = slice(cb * LANES, (cb + 1) * LANES)
                mx = None
                for r in range(tk // rch):
                    m8 = fold(scores(v, cb, r)[0], jnp.maximum)
                    mx = m8 if mx is None else jnp.maximum(mx, m8)
                m_old = m_scr[v, :, cs]
                m_new = jnp.maximum(m_old, jnp.max(mx, axis=0, keepdims=True))
                a_scr[v, :, cs] = jnp.exp2(m_old - m_new)
                m_scr[v, :, cs] = m_new
            for cb in range(ncv // LANES):
                cs = slice(cb * LANES, (cb + 1) * LANES)
                m_row = m_scr[v, 0:1, cs]
                psum = None
                for r in range(tk // rch):
                    x, okc = scores(v, cb, r)
                    p = jnp.exp2(x - m_row)
                    if masked:
                        p = jnp.where(okc, p, 0.0)
                    p8 = fold(p, jnp.add)
                    psum = p8 if psum is None else psum + p8
                    p_scr[v, r * rch:(r + 1) * rch, cs] = p.astype(BF16)
                l_scr[v, :, cs] = a_scr[v, :, cs] * l_scr[v, :, cs] + psum
        for v in range(NV):
            pv = jnp.dot(vt_ref[v * dvv:(v + 1) * dvv, :], p_scr[v], preferred_element_type=F32)
            acc_scr[v] = a_scr[v, 0:1, :] * acc_scr[v] + pv

    @pl.when((flags & 4) != 0)
    def _():
        step(True)

    @pl.when((flags & 4) == 0)
    def _():
        step(False)

    @pl.when((flags & 2) != 0)
    def _():
        parts = []
        for v in range(NV):
            l = jnp.sum(l_scr[v], axis=0, keepdims=True)
            o = acc_scr[v] / jnp.where(l > 0.0, l, 1.0)
            parts += [o[:, i * tq:(i + 1) * tq] for i in range(hpv)]
        o_ref[...] = jnp.concatenate(parts, axis=0).T


def attend_t(tt, v_rblk, zb, k_blk, *, nb, T, tq, tk, U, hpu, NV, hpv, dvv, selt=None, e01t=None, G=1):
    dk = A_HEAD_DIM
    nq, nk = T // tq, T // tk
    NH = U * hpu
    assert NH == NV * hpv and tq % LANES == 0
    c = _FlashCfg(tq=tq, tk=tk, U=U, hpu=hpu, NV=NV, hpv=hpv, dk=dk, dvv=dvv, mode="causal", q_off=0, k_off=0, nk=nk,
                  sel=selt is not None, nbp=0 if selt is None else selt.shape[0] // G, G=G)
    tab = []
    for i in range(nq):
        last = min((i * tq + tq - 1) // tk, nk - 1)
        for j in range(last + 1):
            masked = i * tq - (j * tk + tk - 1) < 0
            tab.append((i, j, (j == 0) * 1 + (j == last) * 2 + masked * 4))
    tab = jnp.asarray(np.array(tab, np.int32).T)
    NC = NH * tq
    in_specs = [pl.BlockSpec((NH * dk, tq), lambda b, n, t: (0, b * nq + t[0, n])),
                pl.BlockSpec((tk, U * 2 * dk), lambda b, n, t: (b * nk + t[1, n], k_blk)),
                pl.BlockSpec((NV * dvv, tk), lambda b, n, t: (v_rblk, b * nk + t[1, n]))]
    args = [tt, zb, tt]
    ncv = hpv * tq
    scratch = [pltpu.VMEM((U, 2 * dk, hpu * tq), BF16), pltpu.VMEM((NV, tk, ncv), F32), pltpu.VMEM((NV, tk, ncv), BF16),
               pltpu.VMEM((NV, SUB, ncv), F32), pltpu.VMEM((NV, SUB, ncv), F32), pltpu.VMEM((NV, SUB, ncv), F32),
               pltpu.VMEM((NV, dvv, ncv), F32)]
    if selt is not None:
        in_specs += [pl.BlockSpec((selt.shape[0], tq), lambda b, n, t: (0, b * nq + t[0, n])),
                     pl.BlockSpec((tk, e01t.shape[1]), lambda b, n, t: (t[1, n], 0))]
        args += [selt, e01t]
        scratch.append(pltpu.VMEM((G, tk, tq), F32))
    grid_spec = pltpu.PrefetchScalarGridSpec(
        num_scalar_prefetch=1, grid=(nb, tab.shape[1]), in_specs=in_specs,
        out_specs=pl.BlockSpec((tq, NH * dvv), lambda b, n, t: (b * nq + t[0, n], 0)),
        scratch_shapes=scratch)
    return pl.pallas_call(
        functools.partial(_attn_t_kernel, c=c),
        out_shape=jax.ShapeDtypeStruct((nb * T, NH * dvv), F32),
        grid_spec=grid_spec, compiler_params=_cparams(2),
        name="attend_t" + ("_sel" if selt is not None else ""))(tab, *args)


def _window_kernel(q_ref, *refs, tq, nband):
    kts, vs = refs[:nband], refs[nband:2 * nband]
    o_ref = refs[2 * nband]
    G, hg, d = A_GROUPS, A_HEADS // A_GROUPS, A_HEAD_DIM
    qi = pl.program_id(1)
    ncol = nband * tq
    row = lax.broadcasted_iota(jnp.int32, (tq, ncol), 0)
    col = lax.broadcasted_iota(jnp.int32, (tq, ncol), 1)
    pos_k = (qi - (nband - 1)) * tq + col
    dpos = qi * tq + row - pos_k
    ok = ((dpos >= 0) & (dpos < WINDOW) & (pos_k >= 0))[None]
    ss = []
    for g in range(G):
        qs = jnp.concatenate([q_ref[:, (g * hg + h) * d:(g * hg + h + 1) * d] for h in range(hg)], axis=0)
        qs = (qs * (d ** -0.5 * LOG2E)).astype(BF16)
        kt = jnp.concatenate([k[g * d:(g + 1) * d, :] for k in kts], axis=1)
        ss.append(jnp.dot(qs, kt, preferred_element_type=F32).reshape(hg, tq, ncol))
    ps, dens = [], []
    for s in ss:
        s = jnp.where(ok, s, NEG)
        p = jnp.where(ok, jnp.exp2(s - jnp.max(s, axis=-1, keepdims=True)), 0.0)
        dens.append(jnp.sum(p, axis=-1, keepdims=True))
        ps.append(p.reshape(hg * tq, ncol).astype(BF16))
    for g in range(G):
        v = jnp.concatenate([x[:, g * d:(g + 1) * d] for x in vs], axis=0)
        o = jnp.dot(ps[g], v, preferred_element_type=F32).reshape(hg, tq, d)
        o = o / jnp.where(dens[g] > 0.0, dens[g], 1.0)
        for h in range(hg):
            o_ref[:, (g * hg + h) * d:(g * hg + h + 1) * d] = o[h]


def window_attend(z, kt, kt_blk, zb, v_blk, nb, T):
    G, d = A_GROUPS, A_HEAD_DIM
    aq = A_HEADS * d
    tq = 128
    assert T % tq == 0 and WINDOW % tq == 0
    nq = T // tq
    nband = WINDOW // tq + 1
    band = lambda i, blk, tr: (lambda b, q: (blk, b * nq + jnp.maximum(q - (nband - 1) + i, 0)) if tr
                               else (b * nq + jnp.maximum(q - (nband - 1) + i, 0), blk))
    in_specs = [pl.BlockSpec((tq, aq), lambda b, q: (b * nq + q, EV_Q))]
    in_specs += [pl.BlockSpec((G * d, tq), band(i, kt_blk, True)) for i in range(nband)]
    in_specs += [pl.BlockSpec((tq, G * d), band(i, v_blk, False)) for i in range(nband)]
    return pl.pallas_call(
        functools.partial(_window_kernel, tq=tq, nband=nband),
        out_shape=jax.ShapeDtypeStruct((nb * T, aq), F32),
        grid=(nb, nq), in_specs=in_specs,
        out_specs=pl.BlockSpec((tq, aq), lambda b, q: (b * nq + q, 0)),
        compiler_params=_cparams(2), name="window_attend")(z, *([kt] * nband), *([zb] * nband))


def _seg_mask(C):
    nblk = C // SUB
    nseg = max(SUB * nblk * (nblk - 1) // 2, SUB)
    pm = np.zeros((C, nseg), np.float32)
    for i in range(1, nblk):
        off = SUB * i * (i - 1) // 2
        pm[SUB * i:SUB * (i + 1), off:off + SUB * i] = 1.0
    return pm


def _rec_kernel(*refs, mode, C, nsub, H, pos_off, has_s0, layer):
    refs = list(refs)
    a_ref, b_ref, v_ref, gate_ref = refs[:4]
    refs = refs[4:]
    aux_ref = refs.pop(0)
    pm_ref = refs.pop(0)
    s0_ref = refs.pop(0) if has_s0 else None
    o_ref, st_ref, st_scr = refs
    c_id = pl.program_id(1)
    K = 128

    @pl.when(c_id == 0)
    def _():
        if has_s0:
            st_scr[...] = s0_ref[0]
        else:
            st_scr[...] = jnp.zeros(st_scr.shape, F32)

    rows = lax.broadcasted_iota(jnp.int32, (C, 1), 0)
    ltri = jnp.where(rows >= lax.broadcasted_iota(jnp.int32, (1, C), 1), 1.0, 0.0).astype(BF16)
    srow = lax.broadcasted_iota(jnp.int32, (SUB, 1), 0)
    if mode == "hgrn":
        x = aux_ref[...]
        ex = jnp.exp(x - jnp.max(x, axis=0, keepdims=True))
        sm = ex / jnp.sum(ex, axis=0, keepdims=True)
        lb_all = jnp.zeros((1, H * K), F32)
        for i in range(1, layer + 1):
            lb_all = lb_all + sm[i:i + 1]
    nblk = C // SUB
    hs = lambda x, h: x[:, h * K:(h + 1) * K]
    pre = []
    for sc in range(nsub):
        rsl = slice(sc * C, (sc + 1) * C)
        a_all, b_all = a_ref[rsl, :], b_ref[rsl, :]
        if mode == "ret":
            cos, sin_s = aux_ref[rsl, :K], aux_ref[rsl, K:]
            rope = lambda x, cos=cos, sin_s=sin_s: x * cos + pltpu.roll(x, K // 2, 1) * sin_s
            qs = [rope(hs(a_all, h)) for h in range(H)]
            ks = [rope(hs(b_all, h)) * B_DK ** -0.5 for h in range(H)]
            gs = [jnp.full((C, K), math.log1p(-2.0 ** (-5.0 - h)), F32) for h in range(H)]
        else:
            qs, ks, gs = [], [], []
            for h in range(H):
                ah, zf, lb = hs(a_all, h), hs(b_all, h), hs(lb_all, h)
                f = lb + (1.0 - lb) * _sigmoid(zf)
                qs.append(ah * _sigmoid(ah))
                ks.append(1.0 - f)
                gs.append(jnp.log(jnp.maximum(f, F_FLOOR)))
        pre.append((qs, ks, gs))
    bss = [_dot_01x(ltri, jnp.concatenate(pre[sc][2], axis=-1)) for sc in range(nsub)]
    mid = []
    for sc in range(nsub):
        rsl = slice(sc * C, (sc + 1) * C)
        v_all = v_ref[rsl, :]
        per_head = []
        for h in range(H):
            q, k, bsum, v = pre[sc][0][h], pre[sc][1][h], hs(bss[sc], h), hs(v_all, h)
            qt = kt = vcat = None
            if nblk > 1:
                rho = jnp.concatenate(
                    [jnp.zeros((SUB, K), F32)] +
                    [jnp.broadcast_to(bsum[SUB * i - 1:SUB * i], (SUB, K)) for i in range(1, nblk)], axis=0)
                qt = q * jnp.exp(bsum - rho)
                kt = jnp.concatenate(
                    [k[:SUB * i] * jnp.exp(bsum[SUB * i - 1:SUB * i] - bsum[:SUB * i]) for i in range(1, nblk)], axis=0)
                vcat = jnp.concatenate([v[:SUB * i] for i in range(1, nblk)], axis=0).astype(BF16)
            diag = []
            for i in range(nblk):
                sl = slice(SUB * i, SUB * (i + 1))
                q8, k8, b8, v8 = q[sl], k[sl], bsum[sl], v[sl]
                od = jnp.zeros((SUB, K), F32)
                for s in range(SUB):
                    causal = srow >= s
                    dec = jnp.exp(jnp.where(causal, b8 - b8[s:s + 1], 0.0))
                    att = jnp.sum(q8 * k8[s:s + 1] * dec, axis=-1, keepdims=True)
                    od += jnp.where(causal, att, 0.0) * v8[s:s + 1]
                diag.append(od)
            od = jnp.concatenate(diag, axis=0) if nblk > 1 else diag[0]
            bend = bsum[C - 1:C]
            per_head.append(dict(qt=qt, kt=kt, vcat=vcat, od=od, qe=(q * jnp.exp(bsum)).astype(BF16),
                                 kst=(k * jnp.exp(bend - bsum)).astype(BF16), vb=v.astype(BF16), dec=jnp.exp(bend)))
        mid.append(per_head)
    if nblk > 1:
        ps = [[(_dot3_nt(m["qt"], m["kt"]) * pm_ref[...]).astype(BF16) for m in per_head] for per_head in mid]
        for sc in range(nsub):
            for h in range(H):
                mid[sc][h]["od"] = mid[sc][h]["od"] + jnp.dot(ps[sc][h], mid[sc][h]["vcat"], preferred_element_type=F32)
    sts = [st_scr[h] for h in range(H)]
    for sc in range(nsub):
        rsl = slice(sc * C, (sc + 1) * C)
        gate_all = gate_ref[rsl, :]
        for h in range(H):
            m = mid[sc][h]
            o = m["od"] + lax.dot_general(m["qe"], sts[h].astype(BF16), NT, preferred_element_type=F32)
            sts[h] = sts[h] * m["dec"] + lax.dot_general(m["vb"], m["kst"], TN, preferred_element_type=F32)
            gate = hs(gate_all, h)
            o = o * lax.rsqrt(jnp.mean(o * o, axis=-1, keepdims=True) + EPS) * (gate * _sigmoid(gate))
            o_ref[rsl, h * K:(h + 1) * K] = o.astype(o_ref.dtype)
    for h in range(H):
        st_scr[h] = sts[h]

    @pl.when(c_id == pl.num_programs(1) - 1)
    def _():
        st_ref[0] = st_scr[...]


def _rope_table_kernel(fs_ref, o_ref, *, pos_off):
    t = o_ref.shape[0]
    pos = (pos_off + pl.program_id(0) * t + lax.broadcasted_iota(jnp.int32, (t, 1), 0)).astype(F32)
    ang = pos * fs_ref[0:1, :]
    o_ref[...] = jnp.concatenate([jnp.cos(ang), jnp.sin(ang) * fs_ref[1:2, :]], axis=-1)


def rope_table(fs, T, pos_off):
    t = _pick(T, (512, 256, 128, 64, 32, 16, 8))
    return pl.pallas_call(
        functools.partial(_rope_table_kernel, pos_off=pos_off),
        out_shape=jax.ShapeDtypeStruct((T, 2 * fs.shape[1]), F32),
        grid=(T // t,),
        in_specs=[pl.BlockSpec(fs.shape, lambda i: (0, 0))],
        out_specs=pl.BlockSpec((t, 2 * fs.shape[1]), lambda i: (i, 0)),
        compiler_params=_cparams(1), name="rope_table")(fs)


def recurrence(z, blks, aux, *, mode, nb, T, pos_off, s0t=None, layer=0):
    H, K = 4, 128
    C = CHUNK if T % CHUNK == 0 else T
    nsub = _pick(T // C, (4, 2, 1))
    CB = nsub * C
    nc = T // CB
    pm = _seg_mask(C)
    nseg = pm.shape[1]
    row = lambda blk: (lambda b, c: (b * nc + c, blk))
    in_specs = [pl.BlockSpec((CB, H * K), row(blks[0])), pl.BlockSpec((CB, H * K), row(blks[1])),
                pl.BlockSpec((CB, H * K), row(blks[2])), pl.BlockSpec((CB, H * K), row(blks[3])),
                pl.BlockSpec((CB, aux.shape[1]), lambda b, c: (c, 0)) if mode == "ret"
                else pl.BlockSpec(aux.shape, lambda b, c: (0, 0)),
                pl.BlockSpec((C, nseg), lambda b, c: (0, 0))]
    args = [z, z, z, z, aux, jnp.asarray(pm)]
    if s0t is not None:
        in_specs.append(pl.BlockSpec((1, H, K, K), lambda b, c: (b, 0, 0, 0)))
        args.append(s0t)
    return pl.pallas_call(
        functools.partial(_rec_kernel, mode=mode, C=C, nsub=nsub, H=H, pos_off=pos_off, has_s0=s0t is not None,
                          layer=layer),
        out_shape=(jax.ShapeDtypeStruct((nb * T, H * K), BF16), jax.ShapeDtypeStruct((nb, H, K, K), F32)),
        grid=(nb, nc),
        in_specs=in_specs,
        out_specs=(pl.BlockSpec((CB, H * K), lambda b, c: (b * nc + c, 0)),
                   pl.BlockSpec((1, H, K, K), lambda b, c: (b, 0, 0, 0))),
        scratch_shapes=[pltpu.VMEM((H, K, K), F32)],
        compiler_params=_cparams(2), name="recurrence_" + mode)(*args)


def _gather_kernel(pt_ref, cache_ref, *refs, n_pages):
    o_ref = refs[-1]
    p = pl.program_id(1)

    @pl.when(p < n_pages)
    def _():
        o_ref[0, 0] = cache_ref[0, 0]

    if len(refs) == 2:
        @pl.when(p >= n_pages)
        def _():
            o_ref[0, 0] = refs[0][0]


def gather_pages(cache, layer, page_table, tail=None):
    nb, n_pages = page_table.shape
    _, _, ps, C = cache.shape
    n_out = n_pages + (tail is not None)
    in_specs = [pl.BlockSpec((1, 1, ps, C), lambda b, p, pt: (layer, pt[b * n_pages + jnp.minimum(p, n_pages - 1)], 0, 0))]
    args = [page_table.reshape(-1), cache]
    if tail is not None:
        in_specs.append(pl.BlockSpec((1, ps, C), lambda b, p, pt: (b, 0, 0)))
        args.append(tail)
    grid_spec = pltpu.PrefetchScalarGridSpec(
        num_scalar_prefetch=1, grid=(nb, n_out), in_specs=in_specs,
        out_specs=pl.BlockSpec((1, 1, ps, C), lambda b, p, pt: (b, p, 0, 0)))
    return pl.pallas_call(
        functools.partial(_gather_kernel, n_pages=n_pages),
        out_shape=jax.ShapeDtypeStruct((nb, n_out, ps, C), F32),
        grid_spec=grid_spec,
        compiler_params=_cparams(2), name="gather_pages")(*args)


def _page_specs(shape, layer, n_pages, per_step):
    def spec(i):
        return pl.BlockSpec((1, 1) + shape, lambda b, j, pt: (layer, pt[b * n_pages + j * per_step + i], 0, 0))
    return [spec(i) for i in range(per_step)]


def _online_update_all(items):
    staged = []
    for s, mk, m_ref, l_ref, acc_ref, pv_fn in items:
        hg, rows, n = s.shape
        if mk is not None:
            s = jnp.where(mk, s, NEG)
        m_old = m_ref[...]
        m_new = jnp.maximum(m_old, jnp.max(s, axis=-1, keepdims=True))
        alpha = jnp.exp(m_old - m_new)
        p = jnp.exp(s - m_new)
        if mk is not None:
            p = jnp.where(mk, p, 0.0)
        l_ref[...] = alpha * l_ref[...] + jnp.sum(p, axis=-1, keepdims=True)
        m_ref[...] = m_new
        staged.append((p.reshape(hg * rows, n).astype(BF16), alpha, (hg, rows)))
    for (pb, alpha, (hg, rows)), (_, _, _, _, acc_ref, pv_fn) in zip(staged, items):
        pv = pv_fn(pb)
        acc_ref[...] = alpha * acc_ref[...] + pv.reshape(hg, rows, pv.shape[-1])


def _diff_decode_kernel(pt_ref, q_ref, new_ref, *refs, per_step, H, dh):
    pages = refs[:per_step]
    o_ref, qbd_scr, m_scr, l_scr, acc_scr = refs[per_step:]
    j = pl.program_id(1)
    Ts = q_ref.shape[0]
    dv = 2 * dh
    rows_pp = 2 * H

    @pl.when(j == 0)
    def _():
        m_scr[...] = jnp.full(m_scr.shape, NEG, F32)
        l_scr[...] = jnp.zeros(l_scr.shape, F32)
        acc_scr[...] = jnp.zeros(acc_scr.shape, F32)
        lane = lax.broadcasted_iota(jnp.int32, (Ts, dv), 1)
        for h in range(H):
            q = q_ref[:, h * dv:(h + 1) * dv] * dh ** -0.5
            qbd_scr[h] = jnp.concatenate([jnp.where(lane < dh, q, 0.0), jnp.where(lane >= dh, q, 0.0)],
                                         axis=0).astype(BF16)

    items = []
    for h in range(H):
        k = jnp.concatenate([pg[0, 0, pl.ds(h, PAGE_SIZE, stride=rows_pp), :] for pg in pages], axis=0).astype(BF16)
        v = jnp.concatenate([pg[0, 0, pl.ds(H + h, PAGE_SIZE, stride=rows_pp), :] for pg in pages], axis=0).astype(BF16)
        s = lax.dot_general(qbd_scr[h], k, NT, preferred_element_type=F32)[None]
        items.append((s, None, m_scr.at[h], l_scr.at[h], acc_scr.at[h],
                      lambda p, v=v: jnp.dot(p, v, preferred_element_type=F32)))
    _online_update_all(items)

    @pl.when(j == pl.num_programs(1) - 1)
    def _():
        r = lax.broadcasted_iota(jnp.int32, (2 * Ts, Ts), 0)
        cidx = lax.broadcasted_iota(jnp.int32, (2 * Ts, Ts), 1)
        causal = (cidx <= jnp.where(r >= Ts, r - Ts, r))[None]
        last = []
        for h in range(H):
            kn = new_ref[:, h * dv:(h + 1) * dv].astype(BF16)
            vn = new_ref[:, (H + h) * dv:(H + h + 1) * dv].astype(BF16)
            s = lax.dot_general(qbd_scr[h], kn, NT, preferred_element_type=F32)[None]
            last.append((s, causal, m_scr.at[h], l_scr.at[h], acc_scr.at[h],
                         lambda p, vn=vn: jnp.dot(p, vn, preferred_element_type=F32)))
        _online_update_all(last)
        for h in range(H):
            l = l_scr[h]
            o = acc_scr[h] / jnp.where(l > 0.0, l, 1.0)
            o_ref[:, (2 * h) * dv:(2 * h + 1) * dv] = o[0, :Ts]
            o_ref[:, (2 * h + 1) * dv:(2 * h + 2) * dv] = o[0, Ts:]


def diff_decode(z, cache_rows, layer, page_table, nbs, Ts):
    H, dh = C_HEADS, C_HEAD_DIM
    cw = H * 2 * dh
    n_pages = page_table.shape[1]
    per_step = _pick(n_pages, (16, 8, 4, 2, 1))
    in_specs = [pl.BlockSpec((Ts, cw), lambda b, j, pt: (b, 0)),
                pl.BlockSpec((Ts, 2 * cw), lambda b, j, pt: (b, 0))]
    in_specs += _page_specs(cache_rows.shape[2:], layer, n_pages, per_step)
    grid_spec = pltpu.PrefetchScalarGridSpec(
        num_scalar_prefetch=1, grid=(nbs, n_pages // per_step), in_specs=in_specs,
        out_specs=pl.BlockSpec((Ts, 2 * cw), lambda b, j, pt: (b, 0)),
        scratch_shapes=[pltpu.VMEM((H, 2 * Ts, 2 * dh), BF16), pltpu.VMEM((H, 1, 2 * Ts, 1), F32),
                        pltpu.VMEM((H, 1, 2 * Ts, 1), F32), pltpu.VMEM((H, 1, 2 * Ts, 2 * dh), F32)])
    kv_new = z[:, cw:3 * cw]
    return pl.pallas_call(
        functools.partial(_diff_decode_kernel, per_step=per_step, H=H, dh=dh),
        out_shape=jax.ShapeDtypeStruct((nbs * Ts, 2 * cw), F32),
        grid_spec=grid_spec, compiler_params=_cparams(2),
        name="diff_decode")(page_table.reshape(-1), z, kv_new, *([cache_rows] * per_step))


def _slc_decode_kernel(pt_ref, q_ref, new_ref, sel_ref, *refs, per_step, nbp, past):
    pages = refs[:per_step]
    o_ref, qs_scr, m_scr, l_scr, acc_scr = refs[per_step:]
    G, hg, d = A_GROUPS, A_HEADS // A_GROUPS, A_HEAD_DIM
    j = pl.program_id(1)
    Ts = q_ref.shape[0]
    ntok = per_step * PAGE_SIZE

    @pl.when(j == 0)
    def _():
        m_scr[...] = jnp.full(m_scr.shape, NEG, F32)
        l_scr[...] = jnp.zeros(l_scr.shape, F32)
        acc_scr[...] = jnp.zeros(acc_scr.shape, F32)
        for g in range(G):
            qs = jnp.concatenate([q_ref[:, (g * hg + h) * d:(g * hg + h + 1) * d] for h in range(hg)], axis=0)
            qs_scr[g] = (qs * d ** -0.5).astype(BF16)

    tok = j * ntok + lax.broadcasted_iota(jnp.int32, (1, ntok), 1)
    jb = lax.broadcasted_iota(jnp.int32, (nbp, 1), 0)
    e01 = jnp.where(jb == lax.shift_right_logical(tok, int(math.log2(SLC_LEN))), 1.0, 0.0).astype(BF16)
    items = []
    for g in range(G):
        kt = jnp.concatenate([pg[0, 0, g * d:(g + 1) * d, :] for pg in pages], axis=1).astype(BF16)
        vt = jnp.concatenate([pg[0, 0, (G + g) * d:(G + g + 1) * d, :] for pg in pages], axis=1).astype(BF16)
        s = jnp.dot(qs_scr[g], kt, preferred_element_type=F32).reshape(hg, Ts, ntok)
        st = jnp.dot(sel_ref[:, g * nbp:(g + 1) * nbp], e01, preferred_element_type=F32)
        items.append((s, (st > -1.0)[None], m_scr.at[g], l_scr.at[g], acc_scr.at[g],
                      lambda p, vt=vt: lax.dot_general(p, vt, NT, preferred_element_type=F32)))
    _online_update_all(items)

    @pl.when(j == pl.num_programs(1) - 1)
    def _():
        causal = (lax.broadcasted_iota(jnp.int32, (Ts, Ts), 1) <= lax.broadcasted_iota(jnp.int32, (Ts, Ts), 0))[None]
        last = []
        for g in range(G):
            kn = new_ref[:, g * d:(g + 1) * d].astype(BF16)
            vn = new_ref[:, (G + g) * d:(G + g + 1) * d].astype(BF16)
            s = lax.dot_general(qs_scr[g], kn, NT, preferred_element_type=F32).reshape(hg, Ts, Ts)
            last.append((s, causal, m_scr.at[g], l_scr.at[g], acc_scr.at[g],
                         lambda p, vn=vn: jnp.dot(p, vn, preferred_element_type=F32)))
        _online_update_all(last)
        for g in range(G):
            l = l_scr[g]
            o = acc_scr[g] / jnp.where(l > 0.0, l, 1.0)
            for h in range(hg):
                o_ref[:, (g * hg + h) * d:(g * hg + h + 1) * d] = o[h]


def slc_decode(z, cache_t, layer, page_table, sel, nbs, Ts, nbp, past):
    G, d = A_GROUPS, A_HEAD_DIM
    aq = A_HEADS * d
    n_pages = page_table.shape[1]
    per_step = _pick(n_pages, (16, 8, 4, 2, 1))
    assert past % SLC_LEN == 0 and Ts <= SLC_LEN
    in_specs = [pl.BlockSpec((Ts, aq), lambda b, j, pt: (b, EV_Q)),
                pl.BlockSpec((Ts, 2 * G * d), lambda b, j, pt: (b, EV_SLC // (2 * G * d))),
                pl.BlockSpec((Ts, G * nbp), lambda b, j, pt: (b, 0))]
    in_specs += _page_specs(cache_t.shape[2:], layer, n_pages, per_step)
    hg = A_HEADS // G
    grid_spec = pltpu.PrefetchScalarGridSpec(
        num_scalar_prefetch=1, grid=(nbs, n_pages // per_step), in_specs=in_specs,
        out_specs=pl.BlockSpec((Ts, aq), lambda b, j, pt: (b, 0)),
        scratch_shapes=[pltpu.VMEM((G, hg * Ts, d), BF16), pltpu.VMEM((G, hg, Ts, 1), F32),
                        pltpu.VMEM((G, hg, Ts, 1), F32), pltpu.VMEM((G, hg, Ts, d), F32)])
    return pl.pallas_call(
        functools.partial(_slc_decode_kernel, per_step=per_step, nbp=nbp, past=past),
        out_shape=jax.ShapeDtypeStruct((nbs * Ts, aq), F32),
        grid_spec=grid_spec, compiler_params=_cparams(2),
        name="slc_decode")(page_table.reshape(-1), z, z, sel, *([cache_t] * per_step))


def _compress_paged_kernel(pt_ref, w_ref, pe_ref, *refs, per_step):
    pages = refs[:per_step]
    o_ref, ab_scr = refs[per_step:]
    j = pl.program_id(1)
    cpp = PAGE_SIZE // CMP_STRIDE
    r = lax.broadcasted_iota(jnp.int32, (PAGE_SIZE, PAGE_SIZE), 0)
    t = lax.broadcasted_iota(jnp.int32, (PAGE_SIZE, PAGE_SIZE), 1)
    assert cpp & (cpp - 1) == 0
    perm = jnp.where(t == (r & (cpp - 1)) * CMP_STRIDE + lax.shift_right_logical(r, int(math.log2(cpp))),
                     1.0, 0.0).astype(BF16)
    xs = [lax.dot_general(perm, pg[0, 0].astype(BF16), NT, preferred_element_type=F32) for pg in pages]
    C = xs[0].shape[1]
    ab = jnp.zeros((per_step * cpp, w_ref.shape[1]), F32)
    for l in range(CMP_STRIDE):
        xl = jnp.concatenate([x[l * cpp:(l + 1) * cpp] for x in xs], axis=0).astype(BF16)
        ab += jnp.dot(xl, w_ref[l * C:(l + 1) * C, :], preferred_element_type=F32)
    n = per_step * cpp
    ab_scr[pl.ds(pl.multiple_of(j * n, n), n), :] = ab

    @pl.when(j == pl.num_programs(1) - 1)
    def _():
        cab = jnp.dot(pe_ref[...].astype(BF16), w_ref[...], preferred_element_type=F32)
        half = ab_scr.shape[1] // 2
        const = cab[0:1, :half] + cab[1:2, half:]
        a, b = ab_scr[:, :half], ab_scr[:, half:]
        o_ref[0] = a + pltpu.roll(b, a.shape[0] - 1, 0) + const


def compress_paged(cache_t, layer, page_table, wbig, pe2):
    nbs, n_pages = page_table.shape
    per_step = _pick(n_pages, (16, 8, 4, 2, 1))
    cpp = PAGE_SIZE // CMP_STRIDE
    n_chunk = n_pages * cpp
    cout = wbig.shape[1] // 2
    in_specs = [pl.BlockSpec(wbig.shape, lambda b, j, pt: (0, 0)), pl.BlockSpec(pe2.shape, lambda b, j, pt: (0, 0))]
    in_specs += _page_specs(cache_t.shape[2:], layer, n_pages, per_step)
    grid_spec = pltpu.PrefetchScalarGridSpec(
        num_scalar_prefetch=1, grid=(nbs, n_pages // per_step), in_specs=in_specs,
        out_specs=pl.BlockSpec((1, n_chunk, cout), lambda b, j, pt: (b, 0, 0)),
        scratch_shapes=[pltpu.VMEM((n_chunk, 2 * cout), F32)])
    return pl.pallas_call(
        functools.partial(_compress_paged_kernel, per_step=per_step),
        out_shape=jax.ShapeDtypeStruct((nbs, n_chunk, cout), F32),
        grid_spec=grid_spec, compiler_params=_cparams(2),
        name="compress_paged")(page_table.reshape(-1), wbig, pe2, *([cache_t] * per_step))


def _even_w_in(w):
    aq, akv = A_HEADS * A_HEAD_DIM, A_GROUPS * A_HEAD_DIM
    splits = np.cumsum([aq] + [akv] * 6 + [3 * A_HEADS] + [512] * 4)[:-1]
    q, kc, vc, ks, vs, kw, vw, gt, rq, rk, rv, rg = jnp.split(w, [int(s) for s in splits], axis=1)
    gt = jnp.pad(gt, ((0, 0), (0, 128 - gt.shape[1])))
    return jnp.concatenate([q, rq, rk, rv, rg, kc, vc, ks, vs, kw, vw, gt], axis=1).astype(BF16)


EV_Q, EV_RQ, EV_RK, EV_RV, EV_RG = 0, 1, 2, 3, 4
EV_KS, EV_VS, EV_KW, EV_VW, EV_GT = 22, 23, 24, 25, 26
EV_CMP, EV_SLC, EV_WIN = 2560, 2816, 3072


def _compress_w(w_cmp, pe):
    G, d = A_GROUPS, A_HEAD_DIM
    r = CMP_LEN // CMP_STRIDE
    assert r == 2
    w = w_cmp.reshape(2, r, CMP_STRIDE, d, d)
    eye_kv = jnp.eye(2, dtype=F32)
    eye_g = jnp.eye(G, dtype=F32)
    big = jnp.einsum("khlde,kq,gp->lqpdhkge", w, eye_kv, eye_g)
    big = big.reshape(CMP_STRIDE * 2 * G * d, r * 2 * G * d).astype(BF16)
    pe_r = pe.reshape(2, r, CMP_STRIDE, d)
    rows = jnp.broadcast_to(pe_r.transpose(1, 2, 0, 3)[:, :, :, None, :], (r, CMP_STRIDE, 2, G, d))
    rows = rows.reshape(r, CMP_STRIDE * 2 * G * d)
    return big, jnp.pad(rows, ((0, 8 - r), (0, 0)))


def _slc_sum_matrix(n_rows, n_cmp, nbp):
    a = np.zeros((n_rows, nbp), np.float32)
    per, left = SLC_LEN // CMP_STRIDE, CMP_LEN // CMP_STRIDE - 1
    for j in range(nbp):
        for n in range(per * j - left, per * j + per):
            if 0 <= n < n_cmp:
                a[n, j] = 1.0
    return jnp.asarray(a, BF16)


def _gate_expand():
    e = np.zeros((128, 3 * A_HEADS * A_HEAD_DIM), np.float32)
    for h in range(A_HEADS):
        for i in range(3):
            e[3 * h + i, i * A_HEADS * A_HEAD_DIM + h * A_HEAD_DIM:i * A_HEADS * A_HEAD_DIM + (h + 1) * A_HEAD_DIM] = 1.0
    return jnp.asarray(e, BF16)


def _rope_aux():
    half = B_DK // 2
    freqs = ROPE_BASE ** (-jnp.arange(half, dtype=F32) / half)
    sign = jnp.concatenate([-jnp.ones((half,), F32), jnp.ones((half,), F32)])
    return jnp.stack([jnp.concatenate([freqs, freqs]), sign])


def _slc_shapes(n_slc, n_cmp_pad):
    nbp = -(-n_slc // 128) * 128
    return nbp, _slc_sum_matrix(n_cmp_pad, n_cmp_pad - 1, nbp)


def _nsa_common(nb, T, q_off):
    return dict(nb=nb, Tq=T, tq=_pick(T, (128, 64, 32, 16, 8)), G=A_GROUPS, hg=A_HEADS // A_GROUPS, dk=A_HEAD_DIM,
                dv=A_HEAD_DIM, g_per_v=1, q_off=q_off)


def _even_layer(xp, xs, p, a, dims, caches):
    B, T, nbs, Ts, past = dims
    g = p["norm_g"]
    w_in = _even_w_in(p["w_in_a"][a])
    wbig, pe2 = _compress_w(p["cmp_w"][a], p["cmp_pos"][a])
    w_out = p["w_out_a"][a].astype(BF16)
    aq = A_HEADS * A_HEAD_DIM
    wa, wb = w_out[:aq], w_out[aq:]
    e01 = _gate_expand()
    akv2 = 2 * A_GROUPS * A_HEAD_DIM

    akv = A_GROUPS * A_HEAD_DIM
    wt = jnp.concatenate([w_in[:, :aq], w_in[:, EV_SLC:EV_SLC + akv], w_in[:, EV_WIN:EV_WIN + akv]], axis=1).T
    wscale = jnp.concatenate([jnp.full((aq, 1), A_HEAD_DIM ** -0.5 * LOG2E, F32), jnp.ones((2 * akv, 1), F32)])
    zp, zpb, ttp = in_proj(xp, g[0:1], w_in, wt, wscale)
    kv_cmp_p = zp[:, EV_CMP:EV_CMP + akv2]
    kv_slc_p = zp[:, EV_SLC:EV_SLC + akv2]
    kv_win_p = zp[:, EV_WIN:EV_WIN + akv2]
    kvc_p, vct_p = compress(kv_cmp_p.reshape(B, T // CMP_STRIDE, CMP_STRIDE * akv2), wbig, pe2)
    n_slc = -(-T // SLC_LEN)
    nbp, a01 = _slc_shapes(n_slc, kvc_p.shape[1])
    o_cmp, sel = cmp_attend_t(ttp, kvc_p, vct_p, a01.T, B, T, n_slc)
    blk01 = np.zeros((nbp, T), np.float32)
    blk01[np.arange(T) // SLC_LEN, np.arange(T)] = 1.0
    o_slc = attend(zp, EV_Q, ttp, aq // akv, zpb, EV_VS, nb=B, T=T, tq=_pick(T, (512, 256, 128, 64, 32, 16, 8)),
                   tk=_pick(T, (1024, 512, 256, 128)), G=A_GROUPS, hg=A_HEADS // A_GROUPS, dk=A_HEAD_DIM,
                   dv=A_HEAD_DIM, g_per_v=1, mode="causal", sel=sel, e01=jnp.asarray(blk01, BF16))
    o_win = window_attend(zp, ttp, (aq + akv) // akv, zpb, EV_VW, B, T)
    ob_p, st_p = recurrence(zp, (EV_RQ, EV_RK, EV_RV, EV_RG), p["rope_p"], mode="ret", nb=B, T=T, pos_off=0)
    xp = even_out(o_cmp, o_slc, o_win, zp, EV_GT, ob_p, e01, wa, wb, g[1:2], xp)

    cache_cmp, cache_slc, win_buf, s0, page_table = caches
    zs = rms_matmul(xs, g[0:1], w_in, F32)
    kv_cmp_s = zs[:, EV_CMP:EV_CMP + akv2]
    kv_slc_s = zs[:, EV_SLC:EV_SLC + akv2]
    kv_win_s = zs[:, EV_WIN:EV_WIN + akv2]
    assert Ts < CMP_STRIDE and past % SLC_LEN == 0
    kvc_s = compress_paged(cache_cmp, a, page_table, wbig, pe2)
    n_slc = -(-(past + Ts) // SLC_LEN)
    nbp, a01 = _slc_shapes(n_slc, kvc_s.shape[1])
    o_cmp, sel = cmp_attend(zs, EV_Q, kvc_s, a01, nbs, Ts, past, n_slc)
    o_slc = slc_decode(zs, cache_slc, a, page_table, sel, nbs, Ts, nbp, past)
    band = jnp.concatenate([win_buf, kv_win_s.reshape(nbs, Ts, akv2)], axis=1)
    n_buf = win_buf.shape[1]
    bt = n_buf + Ts
    o_win = flash(zs, EV_Q, band.reshape(nbs * bt, akv2), 0, band.reshape(nbs * bt, akv2), 1, Tk=bt, tk=bt,
                  mode="window", k_off=past - n_buf, **_nsa_common(nbs, Ts, past))
    ob_s, st_s = recurrence(zs, (EV_RQ, EV_RK, EV_RV, EV_RG), p["rope_s"], mode="ret", nb=nbs, T=Ts, pos_off=past,
                            s0t=jnp.swapaxes(s0, -1, -2))
    xs = even_out(o_cmp, o_slc, o_win, zs, EV_GT, ob_s, e01, wa, wb, g[1:2], xs)

    shp = lambda x, n, t: x.reshape(n, t, 2, A_GROUPS, A_HEAD_DIM)
    wkeep = min(WINDOW, T)
    outs = (shp(kv_cmp_p, B, T), shp(kv_cmp_s, nbs, Ts), shp(kv_slc_p, B, T), shp(kv_slc_s, nbs, Ts),
            shp(kv_win_p, B, T)[:, T - wkeep:], shp(band[:, Ts:], nbs, n_buf),
            jnp.swapaxes(st_p, -1, -2), jnp.swapaxes(st_s, -1, -2).astype(s0.dtype))
    return xp, xs, outs


def _odd_layer(xp, xs, p, o, layer, dims, caches):
    B, T, nbs, Ts, past = dims
    g = p["norm_g"]
    w_in = p["w_in_c"][o].astype(BF16)
    w_out = p["w_out_c"][o].astype(BF16)
    cw = C_HEADS * 2 * C_HEAD_DIM
    wc, wd = w_out[:cw], w_out[cw:]
    lam_init = 0.8 - 0.6 * math.exp(-0.3 * layer)
    lam_p = p["diff_lam"][o]
    lb_logits = p["hgrn_lb"]
    dcommon = dict(G=2 * C_HEADS, hg=1, dk=C_HEAD_DIM, dv=2 * C_HEAD_DIM, g_per_v=2, mode="causal", k_off=0)

    zp, zpb, ktp = in_proj(xp, g[0:1], w_in, w_in[:, cw:2 * cw].T, jnp.ones((cw, 1), F32))
    o_diff = attend(zp, 0, ktp, 0, zpb, 2, nb=B, T=T, tq=_pick(T, (512, 256, 128, 64, 32, 16, 8)),
                    tk=_pick(T, (512, 256, 128)), G=2 * C_HEADS, hg=1, dk=C_HEAD_DIM, dv=2 * C_HEAD_DIM, g_per_v=2,
                    mode="causal")
    od_p, st_p = recurrence(zp, (3, 4, 5, 6), lb_logits, mode="hgrn", nb=B, T=T, pos_off=0, layer=o)
    xp = odd_out(o_diff, od_p, lam_p, lam_init, wc, wd, g[1:2], xp)

    cache_diff, s0, page_table = caches
    zs = rms_matmul(xs, g[0:1], w_in, F32)
    kv_s = zs[:, cw:3 * cw]
    o_diff = diff_decode(zs, cache_diff, o, page_table, nbs, Ts)
    od_s, st_s = recurrence(zs, (3, 4, 5, 6), lb_logits, mode="hgrn", nb=nbs, T=Ts, pos_off=past,
                            s0t=jnp.swapaxes(s0, -1, -2), layer=o)
    xs = odd_out(o_diff, od_s, lam_p, lam_init, wc, wd, g[1:2], xs)

    shp = lambda x, n, t: x.reshape(n, t, 2, C_HEADS, 2 * C_HEAD_DIM)
    outs = (shp(zp[:, cw:3 * cw], B, T), shp(kv_s, nbs, Ts),
            jnp.swapaxes(st_p, -1, -2), jnp.swapaxes(st_s, -1, -2).astype(s0.dtype))
    return xp, xs, outs


def _tail_layers(xp, xs, p, layer, dims, mem_prompt, cache_mem):
    B, T, nbs, Ts, _ = dims
    g = p["norm_g"][layer]
    D = xp.shape[1]
    w_q = p["w_xq"][layer].astype(BF16)
    w_kv = p["w_xkv"][layer].astype(BF16)
    w_o = p["w_xo"][layer].astype(BF16)
    w_up = p["w_up"][layer].astype(BF16)
    w_down = p["w_down"][layer].astype(BF16)
    n_mem = mem_prompt.shape[1]
    kv_mem = rms_matmul(mem_prompt.reshape(B * n_mem, D), g[0:1], w_kv, F32, norm=False).reshape(B, n_mem, 2 * D)
    qp = rms_matmul(xp, g[2:3], w_q, BF16)
    xp = proj_res(xattn(qp, kv_mem, T), w_o, g[3:4], xp)
    qs = rms_matmul(xs, g[2:3], w_q, BF16)
    xs = proj_res(xattn(qs, cache_mem, Ts, layer=layer), w_o, g[3:4], xs)
    xp = mlp(xp, g[4:5], g[5:6], w_up, w_down)
    xs = mlp(xs, g[4:5], g[5:6], w_up, w_down)
    return xp, xs, kv_mem.reshape(B, n_mem, 2, X_HEADS, D // X_HEADS)


def kernel(x_prompt, x_sample, cache_nsa_cmp_kv, cache_nsa_slc_kv, cache_nsa_win_kv, state_ret, cache_diff_kv, state_hgrn, cache_mem_kv, page_table, mem_prompt, norm_g, w_in_a, cmp_pos, cmp_w, w_out_a, w_in_c, diff_lam, hgrn_lb, w_out_c, w_xq, w_xkv, w_xo, w_up, w_down):
    B, T, D = x_prompt.shape
    nbs, Ts, _ = x_sample.shape
    depth = norm_g.shape[0]
    n_pages = page_table.shape[1]
    past = n_pages * PAGE_SIZE
    dims = (B, T, nbs, Ts, past)
    p = dict(w_in_a=w_in_a, cmp_pos=cmp_pos, cmp_w=cmp_w, w_out_a=w_out_a, w_in_c=w_in_c, diff_lam=diff_lam,
             hgrn_lb=hgrn_lb.astype(F32), w_out_c=w_out_c, w_xq=w_xq, w_xkv=w_xkv, w_xo=w_xo, w_up=w_up, w_down=w_down)
    p["rope_p"] = rope_table(_rope_aux(), T, 0)
    p["rope_s"] = rope_table(_rope_aux(), Ts, past)
    xp = x_prompt.reshape(B * T, D)
    xs = x_sample.reshape(nbs * Ts, D)
    feat_major = lambda c: c.transpose(0, 1, 3, 4, 5, 2).reshape(c.shape[0], c.shape[1], -1, c.shape[2])
    rows_of = lambda c: c.reshape(c.shape[0], c.shape[1], -1, c.shape[-1])
    c_cmp, c_slc, c_diff = feat_major(cache_nsa_cmp_kv), feat_major(cache_nsa_slc_kv), rows_of(cache_diff_kv)
    mshape = cache_mem_kv.shape
    c_mem = cache_mem_kv.reshape(mshape[:5] + (mshape[5] // 128, 128)).transpose(0, 1, 2, 3, 5, 4, 6)
    c_mem = c_mem.reshape(mshape[0], mshape[1], -1, 128)
    win_all = cache_nsa_win_kv.reshape(cache_nsa_win_kv.shape[0], nbs, cache_nsa_win_kv.shape[2], -1)
    ev, od, mem = [], [], []
    for layer in range(depth):
        pl_ = dict(p, norm_g=norm_g[layer])
        if layer % 2 == 0:
            a = layer // 2
            xp, xs, outs = _even_layer(xp, xs, pl_, a, dims, (c_cmp, c_slc, win_all[a], state_ret[a], page_table))
            ev.append(outs)
        else:
            o = layer // 2
            xp, xs, outs = _odd_layer(xp, xs, pl_, o, layer, dims, (c_diff, state_hgrn[o], page_table))
            od.append(outs)
        xp, xs, kvm = _tail_layers(xp, xs, dict(p, norm_g=norm_g), layer, dims, mem_prompt, c_mem)
        mem.append(kvm)
    stack = lambda lst, i: jnp.stack([t[i] for t in lst])
    return (xp.reshape(B, T, D), xs.reshape(nbs, Ts, D),
            stack(ev, 0), stack(ev, 1), stack(ev, 2), stack(ev, 3), stack(ev, 4), stack(ev, 5), stack(ev, 6), stack(ev, 7),
            stack(od, 0), stack(od, 1), stack(od, 2), stack(od, 3), jnp.stack(mem))
```

```python
import functools
import math

import numpy as np
import jax
import jax.numpy as jnp
from jax import lax
from jax.experimental import pallas as pl
from jax.experimental.pallas import tpu as pltpu

F32 = jnp.float32
BF16 = jnp.bfloat16

EPS = 1e-6
NEG = -1e30
BIG = 1e9
A_HEADS, A_GROUPS, A_HEAD_DIM = 8, 2, 64
CMP_LEN, CMP_STRIDE, SLC_LEN, SLC_TOPK, WINDOW = 32, 16, 64, 16, 512
B_HEADS, B_DK = 4, 128
ROPE_BASE = 10000.0
CHUNK = 64
C_HEADS, C_HEAD_DIM = 4, 64
D_HEADS = 4
F_FLOOR = 1e-6
X_HEADS = 4
PAGE_SIZE = 128
SUB = 8
LANES = 128

VMEM_LIMIT = 56 * 1024 * 1024

NT = (((1,), (1,)), ((), ()))
TN = (((0,), (0,)), ((), ()))


def _cparams(n_grid):
    return pltpu.CompilerParams(dimension_semantics=("arbitrary",) * n_grid, vmem_limit_bytes=VMEM_LIMIT)


def _pick(n, cands):
    for c in cands:
        if n % c == 0:
            return c
    return n


def _split3(x):
    hi = x.astype(BF16)
    r = x - hi.astype(F32)
    mid = r.astype(BF16)
    lo = (r - mid.astype(F32)).astype(BF16)
    return hi, mid, lo


def _dot_x01(x, m01):
    return sum(jnp.dot(p, m01, preferred_element_type=F32) for p in _split3(x))


def _dot_01x(m01, x):
    return sum(jnp.dot(m01, p, preferred_element_type=F32) for p in _split3(x))


def _dot3_nt(a, b):
    ah = a.astype(BF16)
    al = (a - ah.astype(F32)).astype(BF16)
    bh = b.astype(BF16)
    bl = (b - bh.astype(F32)).astype(BF16)
    d = lambda x, y: lax.dot_general(x, y, NT, preferred_element_type=F32)
    return d(ah, bh) + d(ah, bl) + d(al, bh)


def _sigmoid(x):
    return 1.0 / (1.0 + jnp.exp(-x))


def _rms(x, g):
    return x * lax.rsqrt(jnp.mean(x * x, axis=-1, keepdims=True) + EPS) * g


def _rms_matmul_kernel(x_ref, g_ref, w_ref, o_ref, *, norm):
    x = x_ref[...]
    if norm:
        x = _rms(x, g_ref[...])
    o_ref[...] = jnp.dot(x.astype(BF16), w_ref[...], preferred_element_type=F32).astype(o_ref.dtype)


def rms_matmul(x, g, w, out_dtype, norm=True):
    R, D = x.shape
    N = w.shape[1]
    tm = _pick(R, (512, 256, 128, 64, 32, 16, 8))
    tn = _pick(N, (1792, 1152, 1024, 896, 768, 640, 512, 384, 256, 128))
    return pl.pallas_call(
        functools.partial(_rms_matmul_kernel, norm=norm),
        out_shape=jax.ShapeDtypeStruct((R, N), out_dtype),
        grid=(R // tm, N // tn),
        in_specs=[pl.BlockSpec((tm, D), lambda i, j: (i, 0)),
                  pl.BlockSpec((1, D), lambda i, j: (0, 0)),
                  pl.BlockSpec((D, tn), lambda i, j: (0, j))],
        out_specs=pl.BlockSpec((tm, tn), lambda i, j: (i, j)),
        compiler_params=_cparams(2), name="rms_matmul")(x, g, w)


def _in_proj_kernel(x_ref, g_ref, w_ref, wt_ref, ws_ref, z_ref, zb_ref, kt_ref, xn_scr):
    j = pl.program_id(1)

    @pl.when(j == 0)
    def _():
        xn = _rms(x_ref[...], g_ref[...]).astype(BF16)
        xn_scr[...] = xn
        t = lax.dot_general(wt_ref[...], xn, NT, preferred_element_type=F32)
        kt_ref[...] = (t * ws_ref[...]).astype(kt_ref.dtype)

    z = jnp.dot(xn_scr[...], w_ref[...], preferred_element_type=F32)
    z_ref[...] = z
    zb_ref[...] = z.astype(BF16)


def in_proj(x, g, w, wt, wscale):
    R, D = x.shape
    N = w.shape[1]
    Fk = wt.shape[0]
    tm = _pick(R, (1024, 512, 256, 128, 64, 32, 16, 8))
    tn = _pick(N, (1792, 1152, 1024, 896, 768, 640, 512, 384, 256, 128))
    return pl.pallas_call(
        _in_proj_kernel,
        out_shape=(jax.ShapeDtypeStruct((R, N), F32), jax.ShapeDtypeStruct((R, N), BF16),
                   jax.ShapeDtypeStruct((Fk, R), BF16)),
        grid=(R // tm, N // tn),
        in_specs=[pl.BlockSpec((tm, D), lambda i, j: (i, 0)),
                  pl.BlockSpec((1, D), lambda i, j: (0, 0)),
                  pl.BlockSpec((D, tn), lambda i, j: (0, j)),
                  pl.BlockSpec((Fk, D), lambda i, j: (0, 0)),
                  pl.BlockSpec((Fk, 1), lambda i, j: (0, 0))],
        out_specs=(pl.BlockSpec((tm, tn), lambda i, j: (i, j)), pl.BlockSpec((tm, tn), lambda i, j: (i, j)),
                   pl.BlockSpec((Fk, tm), lambda i, j: (0, i))),
        scratch_shapes=[pltpu.VMEM((tm, D), BF16)],
        compiler_params=_cparams(2), name="in_proj")(x, g, w, wt, wscale)


def _mlp_kernel(x_ref, g4_ref, g5_ref, wu_ref, wd_ref, o_ref, xn_scr, acc_scr):
    j = pl.program_id(1)

    @pl.when(j == 0)
    def _():
        xn_scr[...] = _rms(x_ref[...], g4_ref[...]).astype(BF16)
        acc_scr[...] = jnp.zeros_like(acc_scr)

    h = jnp.dot(xn_scr[...], wu_ref[...], preferred_element_type=F32)
    h = jnp.square(jnp.maximum(h, 0.0))
    acc_scr[...] += jnp.dot(h.astype(BF16), wd_ref[...], preferred_element_type=F32)

    @pl.when(j == pl.num_programs(1) - 1)
    def _():
        o_ref[...] = x_ref[...] + _rms(acc_scr[...], g5_ref[...])


def mlp(x, g4, g5, w_up, w_down):
    R, D = x.shape
    F = w_up.shape[1]
    tm = _pick(R, (1024, 512, 256, 128, 64, 32, 16, 8))
    tf = _pick(F, (1024, 512, 256, 128))
    return pl.pallas_call(
        _mlp_kernel,
        out_shape=jax.ShapeDtypeStruct((R, D), F32),
        grid=(R // tm, F // tf),
        in_specs=[pl.BlockSpec((tm, D), lambda i, j: (i, 0)),
                  pl.BlockSpec((1, D), lambda i, j: (0, 0)),
                  pl.BlockSpec((1, D), lambda i, j: (0, 0)),
                  pl.BlockSpec((D, tf), lambda i, j: (0, j)),
                  pl.BlockSpec((tf, D), lambda i, j: (j, 0))],
        out_specs=pl.BlockSpec((tm, D), lambda i, j: (i, 0)),
        scratch_shapes=[pltpu.VMEM((tm, D), BF16), pltpu.VMEM((tm, D), F32)],
        compiler_params=_cparams(2), name="mlp")(x, g4, g5, w_up, w_down)


def _proj_res_kernel(a_ref, w_ref, g_ref, x_ref, o_ref):
    y = jnp.dot(a_ref[...], w_ref[...], preferred_element_type=F32)
    o_ref[...] = x_ref[...] + _rms(y, g_ref[...])


def proj_res(a, w, g, x):
    R, D = x.shape
    K = a.shape[1]
    tm = _pick(R, (512, 256, 128, 64, 32, 16, 8))
    return pl.pallas_call(
        _proj_res_kernel,
        out_shape=jax.ShapeDtypeStruct((R, D), F32),
        grid=(R // tm,),
        in_specs=[pl.BlockSpec((tm, K), lambda i: (i, 0)),
                  pl.BlockSpec((K, D), lambda i: (0, 0)),
                  pl.BlockSpec((1, D), lambda i: (0, 0)),
                  pl.BlockSpec((tm, D), lambda i: (i, 0))],
        out_specs=pl.BlockSpec((tm, D), lambda i: (i, 0)),
        compiler_params=_cparams(1), name="proj_res")(a, w, g, x)


def _even_out_kernel(oc_ref, os_ref, ow_ref, gt_ref, ob_ref, e_ref, wa_ref, wb_ref, g_ref, x_ref, o_ref):
    gates = _sigmoid(gt_ref[...])
    ge = _dot_x01(gates, e_ref[...])
    aq = oc_ref.shape[1]
    oa = ge[:, :aq] * oc_ref[...] + ge[:, aq:2 * aq] * os_ref[...] + ge[:, 2 * aq:] * ow_ref[...]
    y = jnp.dot(oa.astype(BF16), wa_ref[...], preferred_element_type=F32)
    y += jnp.dot(ob_ref[...], wb_ref[...], preferred_element_type=F32)
    o_ref[...] = x_ref[...] + _rms(y, g_ref[...])


def even_out(o_cmp, o_slc, o_win, z, gt_blk, o_b, e01, wa, wb, g, x):
    R, D = x.shape
    aq = o_cmp.shape[1]
    tm = _pick(R, (256, 128, 64, 32, 16, 8))
    row = lambda i: (i, 0)
    fix = lambda i: (0, 0)
    return pl.pallas_call(
        _even_out_kernel,
        out_shape=jax.ShapeDtypeStruct((R, D), F32),
        grid=(R // tm,),
        in_specs=[pl.BlockSpec((tm, aq), row), pl.BlockSpec((tm, aq), row), pl.BlockSpec((tm, aq), row),
                  pl.BlockSpec((tm, 128), lambda i: (i, gt_blk)),
                  pl.BlockSpec((tm, o_b.shape[1]), row),
                  pl.BlockSpec(e01.shape, fix), pl.BlockSpec(wa.shape, fix), pl.BlockSpec(wb.shape, fix),
                  pl.BlockSpec((1, D), fix), pl.BlockSpec((tm, D), row)],
        out_specs=pl.BlockSpec((tm, D), row),
        compiler_params=_cparams(1), name="even_out")(o_cmp, o_slc, o_win, z, o_b, e01, wa, wb, g, x)


def _odd_out_kernel(oc_ref, od_ref, lam_ref, wc_ref, wd_ref, g_ref, x_ref, o_ref, *, lam_init, heads, dv):
    lp = lam_ref[...]
    lam = (jnp.exp(jnp.sum(lp[0:1] * lp[1:2], axis=-1, keepdims=True))
           - jnp.exp(jnp.sum(lp[2:3] * lp[3:4], axis=-1, keepdims=True)) + lam_init)
    parts = []
    for h in range(heads):
        o1 = oc_ref[:, (2 * h) * dv:(2 * h + 1) * dv]
        o2 = oc_ref[:, (2 * h + 1) * dv:(2 * h + 2) * dv]
        o = o1 - lam * o2
        o = o * lax.rsqrt(jnp.mean(o * o, axis=-1, keepdims=True) + EPS) * (1.0 - lam_init)
        parts.append(o.astype(BF16))
    oc = jnp.concatenate(parts, axis=-1)
    y = jnp.dot(oc, wc_ref[...], preferred_element_type=F32)
    y += jnp.dot(od_ref[...], wd_ref[...], preferred_element_type=F32)
    o_ref[...] = x_ref[...] + _rms(y, g_ref[...])


def odd_out(o_diff, o_d, lam_p, lam_init, wc, wd, g, x):
    R, D = x.shape
    tm = _pick(R, (256, 128, 64, 32, 16, 8))
    row = lambda i: (i, 0)
    fix = lambda i: (0, 0)
    return pl.pallas_call(
        functools.partial(_odd_out_kernel, lam_init=lam_init, heads=C_HEADS, dv=2 * C_HEAD_DIM),
        out_shape=jax.ShapeDtypeStruct((R, D), F32),
        grid=(R // tm,),
        in_specs=[pl.BlockSpec((tm, o_diff.shape[1]), row), pl.BlockSpec((tm, o_d.shape[1]), row),
                  pl.BlockSpec(lam_p.shape, fix), pl.BlockSpec(wc.shape, fix), pl.BlockSpec(wd.shape, fix),
                  pl.BlockSpec((1, D), fix), pl.BlockSpec((tm, D), row)],
        out_specs=pl.BlockSpec((tm, D), row),
        compiler_params=_cparams(1), name="odd_out")(o_diff, o_d, lam_p, wc, wd, g, x)


def _xattn_kernel(q_ref, kv_ref, o_ref, *, heads, scale, rows):
    dm = q_ref.shape[1]
    hd = dm // heads
    ss, vhs = [], []
    for h in range(heads):
        qh = q_ref[:, h * hd:(h + 1) * hd]
        if rows:
            nh = hd // 128
            per_tok = 2 * nh * heads
            n_mem = kv_ref.shape[2] // per_tok
            row = lambda slot: jnp.concatenate(
                [kv_ref[0, 0, pl.ds((slot * nh + i) * heads + h, n_mem, stride=per_tok), :] for i in range(nh)],
                axis=-1).astype(BF16)
            kh, vh = row(0), row(1)
        else:
            kh = kv_ref[0, :, h * hd:(h + 1) * hd].astype(BF16)
            vh = kv_ref[0, :, dm + h * hd:dm + (h + 1) * hd].astype(BF16)
        ss.append(lax.dot_general(qh, kh, NT, preferred_element_type=F32) * scale)
        vhs.append(vh)
    ps = []
    for s in ss:
        e = jnp.exp(s - jnp.max(s, axis=-1, keepdims=True))
        ps.append((e / jnp.sum(e, axis=-1, keepdims=True)).astype(BF16))
    for h in range(heads):
        o = jnp.dot(ps[h], vhs[h], preferred_element_type=F32)
        o_ref[:, h * hd:(h + 1) * hd] = o.astype(o_ref.dtype)


def xattn(q, kv, rows_per_batch, layer=None):
    R, D = q.shape
    tq = _pick(rows_per_batch, (512, 256, 128, 64, 32, 16, 8))
    per = rows_per_batch // tq
    if layer is None:
        kv_spec = pl.BlockSpec((1,) + kv.shape[1:], lambda i: (i // per, 0, 0))
    else:
        kv_spec = pl.BlockSpec((1, 1) + kv.shape[2:], lambda i: (layer, i // per, 0, 0))
    return pl.pallas_call(
        functools.partial(_xattn_kernel, heads=X_HEADS, scale=(D // X_HEADS) ** -0.5, rows=layer is not None),
        out_shape=jax.ShapeDtypeStruct((R, D), BF16),
        grid=(R // tq,),
        in_specs=[pl.BlockSpec((tq, D), lambda i: (i, 0)), kv_spec],
        out_specs=pl.BlockSpec((tq, D), lambda i: (i, 0)),
        compiler_params=_cparams(1), name="xattn")(q, kv)


def _compress_kernel(x_ref, w_ref, pe_ref, o_ref, vt_ref):
    x = x_ref[0].astype(BF16)
    ab = jnp.dot(x, w_ref[...], preferred_element_type=F32)
    cab = jnp.dot(pe_ref[...].astype(BF16), w_ref[...], preferred_element_type=F32)
    half = ab.shape[1] // 2
    a, b = ab[:, :half], ab[:, half:]
    const = cab[0:1, :half] + cab[1:2, half:]
    n = a.shape[0]
    out = a + pltpu.roll(b, n - 1, 0) + const
    o_ref[0] = out
    vt_ref[0] = out[:, half // 2:].T.astype(vt_ref.dtype)


def compress(xc, wbig, pe2):
    nb, n_chunk, kdim = xc.shape
    cout = wbig.shape[1] // 2
    return pl.pallas_call(
        _compress_kernel,
        out_shape=(jax.ShapeDtypeStruct((nb, n_chunk, cout), F32), jax.ShapeDtypeStruct((nb, cout // 2, n_chunk), BF16)),
        grid=(nb,),
        in_specs=[pl.BlockSpec((1, n_chunk, kdim), lambda b: (b, 0, 0)),
                  pl.BlockSpec(wbig.shape, lambda b: (0, 0)),
                  pl.BlockSpec(pe2.shape, lambda b: (0, 0))],
        out_specs=(pl.BlockSpec((1, n_chunk, cout), lambda b: (b, 0, 0)),
                   pl.BlockSpec((1, cout // 2, n_chunk), lambda b: (b, 0, 0))),
        compiler_params=_cparams(1), name="compress")(xc, wbig, pe2)


def _cmp_kernel(q_ref, kv_ref, a_ref, o_ref, sel_ref, *, tq, q_off, n_slc, nbp, bpb):
    G, hg, d = A_GROUPS, A_HEADS // A_GROUPS, A_HEAD_DIM
    qi = pl.program_id(1)
    ncmp = kv_ref.shape[1]
    pos_q = q_off + qi * tq + lax.broadcasted_iota(jnp.int32, (tq, 1), 0)
    cmp_end = lax.broadcasted_iota(jnp.int32, (1, ncmp), 1) * CMP_STRIDE + (CMP_LEN - 1)
    mask3 = (cmp_end <= pos_q)[None]
    jb = lax.broadcasted_iota(jnp.int32, (1, nbp), 1)
    blk_q = lax.shift_right_logical(pos_q, int(math.log2(SLC_LEN)))
    valid = jb <= blk_q
    forced = valid & ((jb == 0) | (jb >= blk_q - 1))
    k_top = min(SLC_TOPK, n_slc)
    scores = []
    for bi in range(bpb):
        rs = slice(bi * tq, (bi + 1) * tq)
        for g in range(G):
            kc = kv_ref[bi, :, g * d:(g + 1) * d].astype(BF16)
            vc = kv_ref[bi, :, (G + g) * d:(G + g + 1) * d].astype(BF16)
            qs = jnp.concatenate([q_ref[rs, (g * hg + h) * d:(g * hg + h + 1) * d] for h in range(hg)], axis=0)
            qs = (qs * d ** -0.5).astype(BF16)
            s = lax.dot_general(qs, kc, NT, preferred_element_type=F32).reshape(hg, tq, ncmp)
            s = jnp.where(mask3, s, NEG)
            e = jnp.where(mask3, jnp.exp(s - jnp.max(s, axis=-1, keepdims=True)), 0.0)
            den = jnp.sum(e, axis=-1, keepdims=True)
            p = e / jnp.where(den > 0.0, den, 1.0)
            o = jnp.dot(p.reshape(hg * tq, ncmp).astype(BF16), vc, preferred_element_type=F32)
            for h in range(hg):
                o_ref[rs, (g * hg + h) * d:(g * hg + h + 1) * d] = o[h * tq:(h + 1) * tq]
            p_slc = _dot_x01(jnp.sum(p, axis=0), a_ref[...])
            score = jnp.where(forced, BIG, jnp.where(valid, p_slc, -BIG))
            scores.append(jnp.where(jb < n_slc, score, -jnp.inf))
    score = jnp.concatenate(scores, axis=0)
    sel = jnp.zeros(score.shape, F32)
    jbf = jb.astype(F32)
    for _ in range(k_top):
        mx = jnp.max(score, axis=-1, keepdims=True)
        jm = jnp.min(jnp.where(score == mx, jbf, float(nbp)), axis=-1, keepdims=True)
        hit = jbf == jm
        sel = jnp.where(hit, 1.0, sel)
        score = jnp.where(hit, -jnp.inf, score)
    for bi in range(bpb):
        for g in range(G):
            r0 = (bi * G + g) * tq
            ok = valid & (sel[r0:r0 + tq] > 0.5)
            sel_ref[bi * tq:(bi + 1) * tq, g * nbp:(g + 1) * nbp] = jnp.where(ok, 0.0, SEL_OFF).astype(BF16)


def cmp_attend(z, q_blk, kvc, a01, nb, Tq, q_off, n_slc):
    aq = A_HEADS * A_HEAD_DIM
    nbp = a01.shape[1]
    tq = _pick(Tq, (256, 128, 64, 32, 16, 8))
    nq = Tq // tq
    bpb = _pick(nb, (8, 4, 2, 1)) if nq == 1 and tq <= 16 else 1
    ncmp = kvc.shape[1]
    return pl.pallas_call(
        functools.partial(_cmp_kernel, tq=tq, q_off=q_off, n_slc=n_slc, nbp=nbp, bpb=bpb),
        out_shape=(jax.ShapeDtypeStruct((nb * Tq, aq), F32),
                   jax.ShapeDtypeStruct((nb * Tq, A_GROUPS * nbp), BF16)),
        grid=(nb // bpb, nq),
        in_specs=[pl.BlockSpec((bpb * tq, aq), lambda b, i: (b * nq + i, q_blk)),
                  pl.BlockSpec((bpb, ncmp, kvc.shape[2]), lambda b, i: (b, 0, 0)),
                  pl.BlockSpec(a01.shape, lambda b, i: (0, 0))],
        out_specs=(pl.BlockSpec((bpb * tq, aq), lambda b, i: (b * nq + i, 0)),
                   pl.BlockSpec((bpb * tq, A_GROUPS * nbp), lambda b, i: (b * nq + i, 0))),
        compiler_params=_cparams(2), name="cmp_attend")(z, kvc, a01)


class _FlashCfg:
    def __init__(self, **kw):
        self.__dict__.update(kw)


def _tile_range(c, qi):
    q_lo = c.q_off + qi * c.tq
    last = (q_lo + c.tq - 1 - c.k_off) // c.tk
    if c.mode == "window":
        first = (q_lo - (WINDOW - 1) - c.k_off) // c.tk
    else:
        first = 0 * qi
    return first, last


def _flash_kernel(*refs, c):
    if c.sel:
        q_ref, k_ref, v_ref, sel_ref, o_ref, qs_scr, m_scr, l_scr, acc_scr = refs
    else:
        q_ref, k_ref, v_ref, o_ref, qs_scr, m_scr, l_scr, acc_scr = refs
        sel_ref = None
    qi, j = pl.program_id(1), pl.program_id(2)
    tq, tk, G, hg, dk, dv = c.tq, c.tk, c.G, c.hg, c.dk, c.dv

    @pl.when(j == 0)
    def _():
        m_scr[...] = jnp.full(m_scr.shape, NEG, F32)
        l_scr[...] = jnp.zeros(l_scr.shape, F32)
        acc_scr[...] = jnp.zeros(acc_scr.shape, F32)
        for g in range(G):
            qs = jnp.concatenate([q_ref[:, (g * hg + h) * dk:(g * hg + h + 1) * dk] for h in range(hg)], axis=0)
            qs_scr[g] = (qs * dk ** -0.5).astype(BF16)

    first, last = _tile_range(c, qi)
    jabs = first + j

    @pl.when((jabs >= 0) & (jabs <= jnp.minimum(last, c.nk - 1)))
    def _():
        pos_q = c.q_off + qi * tq + lax.broadcasted_iota(jnp.int32, (tq, 1), 0)
        tok = jabs * tk + lax.broadcasted_iota(jnp.int32, (1, tk), 1)
        dpos = pos_q - (c.k_off + tok)
        mask = dpos >= 0
        if c.mode == "window":
            mask = mask & (dpos < WINDOW)
        if c.sel:
            jb = lax.broadcasted_iota(jnp.int32, (c.nbp, 1), 0)
            e01 = jnp.where(jb == lax.shift_right_logical(tok, int(math.log2(SLC_LEN))), 1.0, 0.0).astype(BF16)
        for g in range(G):
            kg = k_ref[:, g * dk:(g + 1) * dk].astype(BF16)
            vi = g // c.g_per_v
            vg = v_ref[:, vi * dv:(vi + 1) * dv].astype(BF16)
            s = lax.dot_general(qs_scr[g], kg, NT, preferred_element_type=F32).reshape(hg, tq, tk)
            mk = mask
            if c.sel:
                st = jnp.dot(sel_ref[:, g * c.nbp:(g + 1) * c.nbp], e01, preferred_element_type=F32)
                mk = mk & (st > 0.5)
            mk = mk[None]
            s = jnp.where(mk, s, NEG)
            m_old = m_scr[g]
            m_new = jnp.maximum(m_old, jnp.max(s, axis=-1, keepdims=True))
            alpha = jnp.exp(m_old - m_new)
            p = jnp.where(mk, jnp.exp(s - m_new), 0.0)
            l_scr[g] = alpha * l_scr[g] + jnp.sum(p, axis=-1, keepdims=True)
            pv = jnp.dot(p.reshape(hg * tq, tk).astype(BF16), vg, preferred_element_type=F32)
            acc_scr[g] = alpha * acc_scr[g] + pv.reshape(hg, tq, dv)
            m_scr[g] = m_new

    @pl.when(j == pl.num_programs(2) - 1)
    def _():
        for g in range(G):
            l = l_scr[g]
            o = acc_scr[g] / jnp.where(l > 0.0, l, 1.0)
            for h in range(hg):
                o_ref[:, (g * hg + h) * dv:(g * hg + h + 1) * dv] = o[h].astype(o_ref.dtype)


def flash(q2d, q_blk, k2d, k_blk, v2d, v_blk, *, nb, Tq, Tk, tq, tk, G, hg, dk, dv, g_per_v, mode,
          q_off, k_off, sel=None, nbp=0):
    nq, nk = Tq // tq, Tk // tk
    assert Tq % tq == 0 and Tk % tk == 0
    c = _FlashCfg(tq=tq, tk=tk, G=G, hg=hg, dk=dk, dv=dv, g_per_v=g_per_v, mode=mode, q_off=q_off, k_off=k_off,
                  nk=nk, sel=sel is not None, nbp=nbp)
    steps = max(min(_tile_range(c, i)[1], nk - 1) - _tile_range(c, i)[0] + 1 for i in range(nq))
    qw, kw, vw = G * hg * dk, G * dk, (G // g_per_v) * dv

    def kv_map(blk):
        def f(b, i, j):
            first, last = _tile_range(c, i)
            return (b * nk + jnp.clip(first + j, 0, jnp.minimum(last, nk - 1)), blk)
        return f

    in_specs = [pl.BlockSpec((tq, qw), lambda b, i, j: (b * nq + i, q_blk)),
                pl.BlockSpec((tk, kw), kv_map(k_blk)),
                pl.BlockSpec((tk, vw), kv_map(v_blk))]
    args = [q2d, k2d, v2d]
    if sel is not None:
        in_specs.append(pl.BlockSpec((tq, G * nbp), lambda b, i, j: (b * nq + i, 0)))
        args.append(sel)
    return pl.pallas_call(
        functools.partial(_flash_kernel, c=c),
        out_shape=jax.ShapeDtypeStruct((nb * Tq, G * hg * dv), F32),
        grid=(nb, nq, steps),
        in_specs=in_specs,
        out_specs=pl.BlockSpec((tq, G * hg * dv), lambda b, i, j: (b * nq + i, 0)),
        scratch_shapes=[pltpu.VMEM((G, hg * tq, dk), BF16), pltpu.VMEM((G, hg, tq, 1), F32),
                        pltpu.VMEM((G, hg, tq, 1), F32), pltpu.VMEM((G, hg, tq, dv), F32)],
        compiler_params=_cparams(3), name="flash_" + mode + ("_sel" if sel is not None else ""))(*args)


LOG2E = 1.4426950408889634
SEL_OFF = -2.0 ** 30


def _attn_kernel(tab_ref, q_ref, kt_ref, v_ref, *refs, c):
    if c.sel:
        sel_ref, e01_ref = refs[:2]
        refs = refs[2:]
    o_ref, qs_scr, s_scr, p_scr, m_scr, l_scr, a_scr, acc_scr = refs
    n = pl.program_id(1)
    qi, kj, flags = tab_ref[0, n], tab_ref[1, n], tab_ref[2, n]
    tq, tk, G, hg, dk, dv = c.tq, c.tk, c.G, c.hg, c.dk, c.dv
    rows = hg * tq
    ncb = tk // LANES

    @pl.when((flags & 1) != 0)
    def _():
        m_scr[...] = jnp.full(m_scr.shape, NEG, F32)
        l_scr[...] = jnp.zeros(l_scr.shape, F32)
        acc_scr[...] = jnp.zeros(acc_scr.shape, F32)
        for g in range(G):
            qs = jnp.concatenate([q_ref[:, (g * hg + h) * dk:(g * hg + h + 1) * dk] for h in range(hg)], axis=0)
            qs_scr[g] = (qs * (dk ** -0.5 * LOG2E)).astype(BF16)

    def scores(masked):
        if masked:
            dq = qi * tq - kj * tk
            ok = ((lax.broadcasted_iota(jnp.int32, (tq, tk), 0) - lax.broadcasted_iota(jnp.int32, (tq, tk), 1)) + dq >= 0)[None]
        for g in range(G):
            s = jnp.dot(qs_scr[g], kt_ref[g * dk:(g + 1) * dk, :], preferred_element_type=F32)
            if c.sel or masked:
                s = s.reshape(hg, tq, tk)
                if c.sel:
                    s = s + jnp.dot(sel_ref[:, g * c.nbp:(g + 1) * c.nbp], e01_ref[...], preferred_element_type=F32)[None]
                if masked:
                    s = jnp.where(ok, s, NEG)
                s = s.reshape(rows, tk)
            s_scr[g] = s

    @pl.when((flags & 4) != 0)
    def _():
        scores(True)

    @pl.when((flags & 4) == 0)
    def _():
        scores(False)

    for g in range(G):
        cols = [s_scr[g, :, cb * LANES:(cb + 1) * LANES] for cb in range(ncb)]
        mx = cols[0]
        for x in cols[1:]:
            mx = jnp.maximum(mx, x)
        m_old = m_scr[g]
        m_new = jnp.maximum(m_old, jnp.broadcast_to(jnp.max(mx, axis=-1, keepdims=True), (rows, LANES)))
        alpha = jnp.exp2(m_old - m_new)
        psum = None
        for cb in range(ncb):
            p = jnp.exp2(cols[cb] - m_new)
            psum = p if psum is None else psum + p
            p_scr[g, :, cb * LANES:(cb + 1) * LANES] = p.astype(BF16)
        l_scr[g] = alpha * l_scr[g] + psum
        m_scr[g] = m_new
        a_scr[g] = alpha
    for g in range(G):
        vi = g // c.g_per_v
        pv = jnp.dot(p_scr[g], v_ref[:, vi * dv:(vi + 1) * dv], preferred_element_type=F32)
        acc_scr[g] = a_scr[g, :, :dv] * acc_scr[g] + pv

    @pl.when((flags & 2) != 0)
    def _():
        for g in range(G):
            l = jnp.sum(l_scr[g], axis=-1, keepdims=True)
            o = acc_scr[g] / jnp.where(l > 0.0, l, 1.0)
            for h in range(hg):
                o_ref[:, (g * hg + h) * dv:(g * hg + h + 1) * dv] = o[h * tq:(h + 1) * tq].astype(o_ref.dtype)


def attend(z, q_blk, kt, kt_blk, zb, v_blk, *, nb, T, tq, tk, G, hg, dk, dv, g_per_v, mode, sel=None, e01=None):
    nq, nk = T // tq, T // tk
    assert mode == "causal"
    c = _FlashCfg(tq=tq, tk=tk, G=G, hg=hg, dk=dk, dv=dv, g_per_v=g_per_v, mode=mode, q_off=0, k_off=0, nk=nk,
                  sel=sel is not None, nbp=0 if sel is None else sel.shape[1] // G)
    tab = []
    for i in range(nq):
        first, last = _tile_range(c, i)
        first, last = max(first, 0), min(last, nk - 1)
        for j in range(first, last + 1):
            lo, hi = i * tq - (j * tk + tk - 1), i * tq + tq - 1 - j * tk
            masked = lo < 0 or (mode == "window" and hi >= WINDOW)
            tab.append((i, j, (j == first) * 1 + (j == last) * 2 + masked * 4))
    tab = jnp.asarray(np.array(tab, np.int32).T)
    npairs = tab.shape[1]
    rows = hg * tq
    in_specs = [pl.BlockSpec((tq, G * hg * dk), lambda b, n, t: (b * nq + t[0, n], q_blk)),
                pl.BlockSpec((G * dk, tk), lambda b, n, t: (kt_blk, b * nk + t[1, n])),
                pl.BlockSpec((tk, (G // g_per_v) * dv), lambda b, n, t: (b * nk + t[1, n], v_blk))]
    args = [z, kt, zb]
    if sel is not None:
        in_specs += [pl.BlockSpec((tq, sel.shape[1]), lambda b, n, t: (b * nq + t[0, n], 0)),
                     pl.BlockSpec((e01.shape[0], tk), lambda b, n, t: (0, t[1, n]))]
        args += [sel, e01]
    grid_spec = pltpu.PrefetchScalarGridSpec(
        num_scalar_prefetch=1, grid=(nb, npairs), in_specs=in_specs,
        out_specs=pl.BlockSpec((tq, G * hg * dv), lambda b, n, t: (b * nq + t[0, n], 0)),
        scratch_shapes=[pltpu.VMEM((G, rows, dk), BF16), pltpu.VMEM((G, rows, tk), F32),
                        pltpu.VMEM((G, rows, tk), BF16), pltpu.VMEM((G, rows, LANES), F32),
                        pltpu.VMEM((G, rows, LANES), F32), pltpu.VMEM((G, rows, LANES), F32),
                        pltpu.VMEM((G, rows, dv), F32)])
    return pl.pallas_call(
        functools.partial(_attn_kernel, c=c),
        out_shape=jax.ShapeDtypeStruct((nb * T, G * hg * dv), F32),
        grid_spec=grid_spec, compiler_params=_cparams(2),
        name="attend_" + mode + ("_sel" if sel is not None else ""))(tab, *args)


def _block_diag_queries(qt_ref, heads, tq, dk):
    zero = jnp.zeros((dk, tq), BF16)
    half = heads // 2
    top = jnp.concatenate([qt_ref[i * dk:(i + 1) * dk, :] if i < half else zero for i in range(heads)], axis=1)
    bot = jnp.concatenate([zero if i < half else qt_ref[i * dk:(i + 1) * dk, :] for i in range(heads)], axis=1)
    return jnp.concatenate([top, bot], axis=0)


def _cmp_t_kernel(qt_ref, kvc_ref, vct_ref, at_ref, o_ref, sel_ref, *, tq, n_slc, nbp):
    G, hg, d = A_GROUPS, A_HEADS // A_GROUPS, A_HEAD_DIM
    assert G == 2
    qi = pl.program_id(1)
    ncmp = kvc_ref.shape[1]
    pos_q = qi * tq + lax.broadcasted_iota(jnp.int32, (1, tq), 1)
    cmp_end = lax.broadcasted_iota(jnp.int32, (ncmp, 1), 0) * CMP_STRIDE + (CMP_LEN - 1)
    ok = cmp_end <= pos_q
    jb = lax.broadcasted_iota(jnp.int32, (nbp, 1), 0)
    blk_q = lax.shift_right_logical(pos_q, int(math.log2(SLC_LEN)))
    valid = jb <= blk_q
    forced = valid & ((jb == 0) | (jb >= blk_q - 1))
    st = jnp.dot(kvc_ref[0, :, :G * d].astype(BF16), _block_diag_queries(qt_ref, A_HEADS, tq, d),
                 preferred_element_type=F32)
    scores, outs, pgs, pgrp = [], [], [], []
    for g in range(G):
        ps = []
        for h in range(hg):
            e = g * hg + h
            x = jnp.where(ok, st[:, e * tq:(e + 1) * tq], NEG)
            p = jnp.where(ok, jnp.exp2(x - jnp.max(x, axis=0, keepdims=True)), 0.0)
            den = jnp.sum(p, axis=0, keepdims=True)
            ps.append(p / jnp.where(den > 0.0, den, 1.0))
        pgs.append(jnp.concatenate(ps, axis=1).astype(BF16))
        pgrp.append(functools.reduce(jnp.add, ps))
    for g in range(G):
        outs.append(jnp.dot(vct_ref[0, g * d:(g + 1) * d, :], pgs[g], preferred_element_type=F32))
    for g in range(G):
        p_slc = _dot_01x(at_ref[...], pgrp[g])
        score = jnp.where(forced, BIG, jnp.where(valid, p_slc, -BIG))
        scores.append(jnp.where(jb < n_slc, score, -jnp.inf))
    score = jnp.concatenate(scores, axis=1)
    jbf = jb.astype(F32)
    sel = jnp.zeros(score.shape, F32)
    for _ in range(min(SLC_TOPK, n_slc)):
        mx = jnp.max(score, axis=0, keepdims=True)
        jm = jnp.min(jnp.where(score == mx, jbf, float(nbp)), axis=0, keepdims=True)
        hit = jbf == jm
        sel = jnp.where(hit, 1.0, sel)
        score = jnp.where(hit, -jnp.inf, score)
    for g in range(G):
        okg = valid & (sel[:, g * tq:(g + 1) * tq] > 0.5)
        sel_ref[:, g * nbp:(g + 1) * nbp] = jnp.where(okg, 0.0, SEL_OFF).T.astype(BF16)
    ot = jnp.concatenate([outs[g][:, h * tq:(h + 1) * tq] for g in range(G) for h in range(hg)], axis=0)
    o_ref[...] = ot.T


def cmp_attend_t(tt, kvc, vct, at01, nb, T, n_slc):
    aq = A_HEADS * A_HEAD_DIM
    nbp = at01.shape[0]
    tq = _pick(T, (256, 128))
    nq = T // tq
    return pl.pallas_call(
        functools.partial(_cmp_t_kernel, tq=tq, n_slc=n_slc, nbp=nbp),
        out_shape=(jax.ShapeDtypeStruct((nb * T, aq), F32), jax.ShapeDtypeStruct((nb * T, A_GROUPS * nbp), BF16)),
        grid=(nb, nq),
        in_specs=[pl.BlockSpec((aq, tq), lambda b, i: (0, b * nq + i)),
                  pl.BlockSpec((1,) + kvc.shape[1:], lambda b, i: (b, 0, 0)),
                  pl.BlockSpec((1,) + vct.shape[1:], lambda b, i: (b, 0, 0)),
                  pl.BlockSpec(at01.shape, lambda b, i: (0, 0))],
        out_specs=(pl.BlockSpec((tq, aq), lambda b, i: (b * nq + i, 0)),
                   pl.BlockSpec((tq, A_GROUPS * nbp), lambda b, i: (b * nq + i, 0))),
        compiler_params=_cparams(2), name="cmp_attend_t")(tt, kvc, vct, at01)


def _window_kernel(q_ref, *refs, tq, nband):
    kts, vs = refs[:nband], refs[nband:2 * nband]
    o_ref = refs[2 * nband]
    G, hg, d = A_GROUPS, A_HEADS // A_GROUPS, A_HEAD_DIM
    qi = pl.program_id(1)
    ncol = nband * tq
    row = lax.broadcasted_iota(jnp.int32, (tq, ncol), 0)
    col = lax.broadcasted_iota(jnp.int32, (tq, ncol), 1)
    pos_k = (qi - (nband - 1)) * tq + col
    dpos = qi * tq + row - pos_k
    ok = ((dpos >= 0) & (dpos < WINDOW) & (pos_k >= 0))[None]
    ss = []
    for g in range(G):
        qs = jnp.concatenate([q_ref[:, (g * hg + h) * d:(g * hg + h + 1) * d] for h in range(hg)], axis=0)
        qs = (qs * (d ** -0.5 * LOG2E)).astype(BF16)
        kt = jnp.concatenate([k[g * d:(g + 1) * d, :] for k in kts], axis=1)
        ss.append(jnp.dot(qs, kt, preferred_element_type=F32).reshape(hg, tq, ncol))
    ps, dens = [], []
    for s in ss:
        s = jnp.where(ok, s, NEG)
        p = jnp.where(ok, jnp.exp2(s - jnp.max(s, axis=-1, keepdims=True)), 0.0)
        dens.append(jnp.sum(p, axis=-1, keepdims=True))
        ps.append(p.reshape(hg * tq, ncol).astype(BF16))
    for g in range(G):
        v = jnp.concatenate([x[:, g * d:(g + 1) * d] for x in vs], axis=0)
        o = jnp.dot(ps[g], v, preferred_element_type=F32).reshape(hg, tq, d)
        o = o / jnp.where(dens[g] > 0.0, dens[g], 1.0)
        for h in range(hg):
            o_ref[:, (g * hg + h) * d:(g * hg + h + 1) * d] = o[h]


def window_attend(z, kt, kt_blk, zb, v_blk, nb, T):
    G, d = A_GROUPS, A_HEAD_DIM
    aq = A_HEADS * d
    tq = 128
    assert T % tq == 0 and WINDOW % tq == 0
    nq = T // tq
    nband = WINDOW // tq + 1
    band = lambda i, blk, tr: (lambda b, q: (blk, b * nq + jnp.maximum(q - (nband - 1) + i, 0)) if tr
                               else (b * nq + jnp.maximum(q - (nband - 1) + i, 0), blk))
    in_specs = [pl.BlockSpec((tq, aq), lambda b, q: (b * nq + q, EV_Q))]
    in_specs += [pl.BlockSpec((G * d, tq), band(i, kt_blk, True)) for i in range(nband)]
    in_specs += [pl.BlockSpec((tq, G * d), band(i, v_blk, False)) for i in range(nband)]
    return pl.pallas_call(
        functools.partial(_window_kernel, tq=tq, nband=nband),
        out_shape=jax.ShapeDtypeStruct((nb * T, aq), F32),
        grid=(nb, nq), in_specs=in_specs,
        out_specs=pl.BlockSpec((tq, aq), lambda b, q: (b * nq + q, 0)),
        compiler_params=_cparams(2), name="window_attend")(z, *([kt] * nband), *([zb] * nband))


def _seg_mask(C):
    nblk = C // SUB
    nseg = max(SUB * nblk * (nblk - 1) // 2, SUB)
    pm = np.zeros((C, nseg), np.float32)
    for i in range(1, nblk):
        off = SUB * i * (i - 1) // 2
        pm[SUB * i:SUB * (i + 1), off:off + SUB * i] = 1.0
    return pm


def _rec_kernel(*refs, mode, C, nsub, H, pos_off, has_s0, layer):
    refs = list(refs)
    a_ref, b_ref, v_ref, gate_ref = refs[:4]
    refs = refs[4:]
    aux_ref = refs.pop(0)
    pm_ref = refs.pop(0)
    s0_ref = refs.pop(0) if has_s0 else None
    o_ref, st_ref, st_scr = refs
    c_id = pl.program_id(1)
    K = 128

    @pl.when(c_id == 0)
    def _():
        if has_s0:
            st_scr[...] = s0_ref[0]
        else:
            st_scr[...] = jnp.zeros(st_scr.shape, F32)

    rows = lax.broadcasted_iota(jnp.int32, (C, 1), 0)
    ltri = jnp.where(rows >= lax.broadcasted_iota(jnp.int32, (1, C), 1), 1.0, 0.0).astype(BF16)
    srow = lax.broadcasted_iota(jnp.int32, (SUB, 1), 0)
    if mode == "hgrn":
        x = aux_ref[...]
        ex = jnp.exp(x - jnp.max(x, axis=0, keepdims=True))
        sm = ex / jnp.sum(ex, axis=0, keepdims=True)
        lb_all = jnp.zeros((1, H * K), F32)
        for i in range(1, layer + 1):
            lb_all = lb_all + sm[i:i + 1]
    nblk = C // SUB
    hs = lambda x, h: x[:, h * K:(h + 1) * K]
    pre = []
    for sc in range(nsub):
        rsl = slice(sc * C, (sc + 1) * C)
        a_all, b_all = a_ref[rsl, :], b_ref[rsl, :]
        if mode == "ret":
            cos, sin_s = aux_ref[rsl, :K], aux_ref[rsl, K:]
            rope = lambda x, cos=cos, sin_s=sin_s: x * cos + pltpu.roll(x, K // 2, 1) * sin_s
            qs = [rope(hs(a_all, h)) for h in range(H)]
            ks = [rope(hs(b_all, h)) * B_DK ** -0.5 for h in range(H)]
            gs = [jnp.full((C, K), math.log1p(-2.0 ** (-5.0 - h)), F32) for h in range(H)]
        else:
            qs, ks, gs = [], [], []
            for h in range(H):
                ah, zf, lb = hs(a_all, h), hs(b_all, h), hs(lb_all, h)
                f = lb + (1.0 - lb) * _sigmoid(zf)
                qs.append(ah * _sigmoid(ah))
                ks.append(1.0 - f)
                gs.append(jnp.log(jnp.maximum(f, F_FLOOR)))
        pre.append((qs, ks, gs))
    bss = [_dot_01x(ltri, jnp.concatenate(pre[sc][2], axis=-1)) for sc in range(nsub)]
    mid = []
    for sc in range(nsub):
        rsl = slice(sc * C, (sc + 1) * C)
        v_all = v_ref[rsl, :]
        per_head = []
        for h in range(H):
            q, k, bsum, v = pre[sc][0][h], pre[sc][1][h], hs(bss[sc], h), hs(v_all, h)
            qt = kt = vcat = None
            if nblk > 1:
                rho = jnp.concatenate(
                    [jnp.zeros((SUB, K), F32)] +
                    [jnp.broadcast_to(bsum[SUB * i - 1:SUB * i], (SUB, K)) for i in range(1, nblk)], axis=0)
                qt = q * jnp.exp(bsum - rho)
                kt = jnp.concatenate(
                    [k[:SUB * i] * jnp.exp(bsum[SUB * i - 1:SUB * i] - bsum[:SUB * i]) for i in range(1, nblk)], axis=0)
                vcat = jnp.concatenate([v[:SUB * i] for i in range(1, nblk)], axis=0).astype(BF16)
            diag = []
            for i in range(nblk):
                sl = slice(SUB * i, SUB * (i + 1))
                q8, k8, b8, v8 = q[sl], k[sl], bsum[sl], v[sl]
                od = jnp.zeros((SUB, K), F32)
                for s in range(SUB):
                    causal = srow >= s
                    dec = jnp.exp(jnp.where(causal, b8 - b8[s:s + 1], 0.0))
                    att = jnp.sum(q8 * k8[s:s + 1] * dec, axis=-1, keepdims=True)
                    od += jnp.where(causal, att, 0.0) * v8[s:s + 1]
                diag.append(od)
            od = jnp.concatenate(diag, axis=0) if nblk > 1 else diag[0]
            bend = bsum[C - 1:C]
            per_head.append(dict(qt=qt, kt=kt, vcat=vcat, od=od, qe=(q * jnp.exp(bsum)).astype(BF16),
                                 kst=(k * jnp.exp(bend - bsum)).astype(BF16), vb=v.astype(BF16), dec=jnp.exp(bend)))
        mid.append(per_head)
    if nblk > 1:
        ps = [[(_dot3_nt(m["qt"], m["kt"]) * pm_ref[...]).astype(BF16) for m in per_head] for per_head in mid]
        for sc in range(nsub):
            for h in range(H):
                mid[sc][h]["od"] = mid[sc][h]["od"] + jnp.dot(ps[sc][h], mid[sc][h]["vcat"], preferred_element_type=F32)
    sts = [st_scr[h] for h in range(H)]
    for sc in range(nsub):
        rsl = slice(sc * C, (sc + 1) * C)
        gate_all = gate_ref[rsl, :]
        for h in range(H):
            m = mid[sc][h]
            o = m["od"] + lax.dot_general(m["qe"], sts[h].astype(BF16), NT, preferred_element_type=F32)
            sts[h] = sts[h] * m["dec"] + lax.dot_general(m["vb"], m["kst"], TN, preferred_element_type=F32)
            gate = hs(gate_all, h)
            o = o * lax.rsqrt(jnp.mean(o * o, axis=-1, keepdims=True) + EPS) * (gate * _sigmoid(gate))
            o_ref[rsl, h * K:(h + 1) * K] = o.astype(o_ref.dtype)
    for h in range(H):
        st_scr[h] = sts[h]

    @pl.when(c_id == pl.num_programs(1) - 1)
    def _():
        st_ref[0] = st_scr[...]


def _rope_table_kernel(fs_ref, o_ref, *, pos_off):
    t = o_ref.shape[0]
    pos = (pos_off + pl.program_id(0) * t + lax.broadcasted_iota(jnp.int32, (t, 1), 0)).astype(F32)
    ang = pos * fs_ref[0:1, :]
    o_ref[...] = jnp.concatenate([jnp.cos(ang), jnp.sin(ang) * fs_ref[1:2, :]], axis=-1)


def rope_table(fs, T, pos_off):
    t = _pick(T, (512, 256, 128, 64, 32, 16, 8))
    return pl.pallas_call(
        functools.partial(_rope_table_kernel, pos_off=pos_off),
        out_shape=jax.ShapeDtypeStruct((T, 2 * fs.shape[1]), F32),
        grid=(T // t,),
        in_specs=[pl.BlockSpec(fs.shape, lambda i: (0, 0))],
        out_specs=pl.BlockSpec((t, 2 * fs.shape[1]), lambda i: (i, 0)),
        compiler_params=_cparams(1), name="rope_table")(fs)


def recurrence(z, blks, aux, *, mode, nb, T, pos_off, s0t=None, layer=0):
    H, K = 4, 128
    C = CHUNK if T % CHUNK == 0 else T
    nsub = _pick(T // C, (4, 2, 1))
    CB = nsub * C
    nc = T // CB
    pm = _seg_mask(C)
    nseg = pm.shape[1]
    row = lambda blk: (lambda b, c: (b * nc + c, blk))
    in_specs = [pl.BlockSpec((CB, H * K), row(blks[0])), pl.BlockSpec((CB, H * K), row(blks[1])),
                pl.BlockSpec((CB, H * K), row(blks[2])), pl.BlockSpec((CB, H * K), row(blks[3])),
                pl.BlockSpec((CB, aux.shape[1]), lambda b, c: (c, 0)) if mode == "ret"
                else pl.BlockSpec(aux.shape, lambda b, c: (0, 0)),
                pl.BlockSpec((C, nseg), lambda b, c: (0, 0))]
    args = [z, z, z, z, aux, jnp.asarray(pm)]
    if s0t is not None:
        in_specs.append(pl.BlockSpec((1, H, K, K), lambda b, c: (b, 0, 0, 0)))
        args.append(s0t)
    return pl.pallas_call(
        functools.partial(_rec_kernel, mode=mode, C=C, nsub=nsub, H=H, pos_off=pos_off, has_s0=s0t is not None,
                          layer=layer),
        out_shape=(jax.ShapeDtypeStruct((nb * T, H * K), BF16), jax.ShapeDtypeStruct((nb, H, K, K), F32)),
        grid=(nb, nc),
        in_specs=in_specs,
        out_specs=(pl.BlockSpec((CB, H * K), lambda b, c: (b * nc + c, 0)),
                   pl.BlockSpec((1, H, K, K), lambda b, c: (b, 0, 0, 0))),
        scratch_shapes=[pltpu.VMEM((H, K, K), F32)],
        compiler_params=_cparams(2), name="recurrence_" + mode)(*args)


def _page_specs(shape, layer, n_pages, per_step):
    def spec(i):
        return pl.BlockSpec((1, 1) + shape, lambda b, j, pt: (layer, pt[b * n_pages + j * per_step + i], 0, 0))
    return [spec(i) for i in range(per_step)]


def _online_update_all(items):
    staged = []
    for s, mk, m_ref, l_ref, acc_ref, pv_fn in items:
        hg, rows, n = s.shape
        if mk is not None:
            s = jnp.where(mk, s, NEG)
        m_old = m_ref[...]
        m_new = jnp.maximum(m_old, jnp.max(s, axis=-1, keepdims=True))
        alpha = jnp.exp(m_old - m_new)
        p = jnp.exp(s - m_new)
        if mk is not None:
            p = jnp.where(mk, p, 0.0)
        l_ref[...] = alpha * l_ref[...] + jnp.sum(p, axis=-1, keepdims=True)
        m_ref[...] = m_new
        staged.append((p.reshape(hg * rows, n).astype(BF16), alpha, (hg, rows)))
    for (pb, alpha, (hg, rows)), (_, _, _, _, acc_ref, pv_fn) in zip(staged, items):
        pv = pv_fn(pb)
        acc_ref[...] = alpha * acc_ref[...] + pv.reshape(hg, rows, pv.shape[-1])


def _diff_decode_kernel(pt_ref, q_ref, new_ref, *refs, per_step, H, dh):
    pages = refs[:per_step]
    o_ref, qbd_scr, m_scr, l_scr, acc_scr = refs[per_step:]
    j = pl.program_id(1)
    Ts = q_ref.shape[0]
    dv = 2 * dh
    rows_pp = 2 * H

    @pl.when(j == 0)
    def _():
        m_scr[...] = jnp.full(m_scr.shape, NEG, F32)
        l_scr[...] = jnp.zeros(l_scr.shape, F32)
        acc_scr[...] = jnp.zeros(acc_scr.shape, F32)
        lane = lax.broadcasted_iota(jnp.int32, (Ts, dv), 1)
        for h in range(H):
            q = q_ref[:, h * dv:(h + 1) * dv] * dh ** -0.5
            qbd_scr[h] = jnp.concatenate([jnp.where(lane < dh, q, 0.0), jnp.where(lane >= dh, q, 0.0)],
                                         axis=0).astype(BF16)

    items = []
    for h in range(H):
        k = jnp.concatenate([pg[0, 0, pl.ds(h, PAGE_SIZE, stride=rows_pp), :] for pg in pages], axis=0).astype(BF16)
        v = jnp.concatenate([pg[0, 0, pl.ds(H + h, PAGE_SIZE, stride=rows_pp), :] for pg in pages], axis=0).astype(BF16)
        s = lax.dot_general(qbd_scr[h], k, NT, preferred_element_type=F32)[None]
        items.append((s, None, m_scr.at[h], l_scr.at[h], acc_scr.at[h],
                      lambda p, v=v: jnp.dot(p, v, preferred_element_type=F32)))
    _online_update_all(items)

    @pl.when(j == pl.num_programs(1) - 1)
    def _():
        r = lax.broadcasted_iota(jnp.int32, (2 * Ts, Ts), 0)
        cidx = lax.broadcasted_iota(jnp.int32, (2 * Ts, Ts), 1)
        causal = (cidx <= jnp.where(r >= Ts, r - Ts, r))[None]
        last = []
        for h in range(H):
            kn = new_ref[:, h * dv:(h + 1) * dv].astype(BF16)
            vn = new_ref[:, (H + h) * dv:(H + h + 1) * dv].astype(BF16)
            s = lax.dot_general(qbd_scr[h], kn, NT, preferred_element_type=F32)[None]
            last.append((s, causal, m_scr.at[h], l_scr.at[h], acc_scr.at[h],
                         lambda p, vn=vn: jnp.dot(p, vn, preferred_element_type=F32)))
        _online_update_all(last)
        for h in range(H):
            l = l_scr[h]
            o = acc_scr[h] / jnp.where(l > 0.0, l, 1.0)
            o_ref[:, (2 * h) * dv:(2 * h + 1) * dv] = o[0, :Ts]
            o_ref[:, (2 * h + 1) * dv:(2 * h + 2) * dv] = o[0, Ts:]


def diff_decode(z, cache_rows, layer, page_table, nbs, Ts):
    H, dh = C_HEADS, C_HEAD_DIM
    cw = H * 2 * dh
    n_pages = page_table.shape[1]
    per_step = _pick(n_pages, (16, 8, 4, 2, 1))
    in_specs = [pl.BlockSpec((Ts, cw), lambda b, j, pt: (b, 0)),
                pl.BlockSpec((Ts, 2 * cw), lambda b, j, pt: (b, 0))]
    in_specs += _page_specs(cache_rows.shape[2:], layer, n_pages, per_step)
    grid_spec = pltpu.PrefetchScalarGridSpec(
        num_scalar_prefetch=1, grid=(nbs, n_pages // per_step), in_specs=in_specs,
        out_specs=pl.BlockSpec((Ts, 2 * cw), lambda b, j, pt: (b, 0)),
        scratch_shapes=[pltpu.VMEM((H, 2 * Ts, 2 * dh), BF16), pltpu.VMEM((H, 1, 2 * Ts, 1), F32),
                        pltpu.VMEM((H, 1, 2 * Ts, 1), F32), pltpu.VMEM((H, 1, 2 * Ts, 2 * dh), F32)])
    kv_new = z[:, cw:3 * cw]
    return pl.pallas_call(
        functools.partial(_diff_decode_kernel, per_step=per_step, H=H, dh=dh),
        out_shape=jax.ShapeDtypeStruct((nbs * Ts, 2 * cw), F32),
        grid_spec=grid_spec, compiler_params=_cparams(2),
        name="diff_decode")(page_table.reshape(-1), z, kv_new, *([cache_rows] * per_step))


def _slc_decode_kernel(pt_ref, q_ref, new_ref, sel_ref, *refs, per_step, nbp, past):
    pages = refs[:per_step]
    o_ref, qs_scr, m_scr, l_scr, acc_scr = refs[per_step:]
    G, hg, d = A_GROUPS, A_HEADS // A_GROUPS, A_HEAD_DIM
    j = pl.program_id(1)
    Ts = q_ref.shape[0]
    ntok = per_step * PAGE_SIZE

    @pl.when(j == 0)
    def _():
        m_scr[...] = jnp.full(m_scr.shape, NEG, F32)
        l_scr[...] = jnp.zeros(l_scr.shape, F32)
        acc_scr[...] = jnp.zeros(acc_scr.shape, F32)
        for g in range(G):
            qs = jnp.concatenate([q_ref[:, (g * hg + h) * d:(g * hg + h + 1) * d] for h in range(hg)], axis=0)
            qs_scr[g] = (qs * d ** -0.5).astype(BF16)

    tok = j * ntok + lax.broadcasted_iota(jnp.int32, (1, ntok), 1)
    jb = lax.broadcasted_iota(jnp.int32, (nbp, 1), 0)
    e01 = jnp.where(jb == lax.shift_right_logical(tok, int(math.log2(SLC_LEN))), 1.0, 0.0).astype(BF16)
    items = []
    for g in range(G):
        kt = jnp.concatenate([pg[0, 0, g * d:(g + 1) * d, :] for pg in pages], axis=1).astype(BF16)
        vt = jnp.concatenate([pg[0, 0, (G + g) * d:(G + g + 1) * d, :] for pg in pages], axis=1).astype(BF16)
        s = jnp.dot(qs_scr[g], kt, preferred_element_type=F32).reshape(hg, Ts, ntok)
        st = jnp.dot(sel_ref[:, g * nbp:(g + 1) * nbp], e01, preferred_element_type=F32)
        items.append((s, (st > -1.0)[None], m_scr.at[g], l_scr.at[g], acc_scr.at[g],
                      lambda p, vt=vt: lax.dot_general(p, vt, NT, preferred_element_type=F32)))
    _online_update_all(items)

    @pl.when(j == pl.num_programs(1) - 1)
    def _():
        causal = (lax.broadcasted_iota(jnp.int32, (Ts, Ts), 1) <= lax.broadcasted_iota(jnp.int32, (Ts, Ts), 0))[None]
        last = []
        for g in range(G):
            kn = new_ref[:, g * d:(g + 1) * d].astype(BF16)
            vn = new_ref[:, (G + g) * d:(G + g + 1) * d].astype(BF16)
            s = lax.dot_general(qs_scr[g], kn, NT, preferred_element_type=F32).reshape(hg, Ts, Ts)
            last.append((s, causal, m_scr.at[g], l_scr.at[g], acc_scr.at[g],
                         lambda p, vn=vn: jnp.dot(p, vn, preferred_element_type=F32)))
        _online_update_all(last)
        for g in range(G):
            l = l_scr[g]
            o = acc_scr[g] / jnp.where(l > 0.0, l, 1.0)
            for h in range(hg):
                o_ref[:, (g * hg + h) * d:(g * hg + h + 1) * d] = o[h]


def slc_decode(z, cache_t, layer, page_table, sel, nbs, Ts, nbp, past):
    G, d = A_GROUPS, A_HEAD_DIM
    aq = A_HEADS * d
    n_pages = page_table.shape[1]
    per_step = _pick(n_pages, (16, 8, 4, 2, 1))
    assert past % SLC_LEN == 0 and Ts <= SLC_LEN
    in_specs = [pl.BlockSpec((Ts, aq), lambda b, j, pt: (b, EV_Q)),
                pl.BlockSpec((Ts, 2 * G * d), lambda b, j, pt: (b, EV_SLC // (2 * G * d))),
                pl.BlockSpec((Ts, G * nbp), lambda b, j, pt: (b, 0))]
    in_specs += _page_specs(cache_t.shape[2:], layer, n_pages, per_step)
    hg = A_HEADS // G
    grid_spec = pltpu.PrefetchScalarGridSpec(
        num_scalar_prefetch=1, grid=(nbs, n_pages // per_step), in_specs=in_specs,
        out_specs=pl.BlockSpec((Ts, aq), lambda b, j, pt: (b, 0)),
        scratch_shapes=[pltpu.VMEM((G, hg * Ts, d), BF16), pltpu.VMEM((G, hg, Ts, 1), F32),
                        pltpu.VMEM((G, hg, Ts, 1), F32), pltpu.VMEM((G, hg, Ts, d), F32)])
    return pl.pallas_call(
        functools.partial(_slc_decode_kernel, per_step=per_step, nbp=nbp, past=past),
        out_shape=jax.ShapeDtypeStruct((nbs * Ts, aq), F32),
        grid_spec=grid_spec, compiler_params=_cparams(2),
        name="slc_decode")(page_table.reshape(-1), z, z, sel, *([cache_t] * per_step))


def _compress_paged_kernel(pt_ref, w_ref, pe_ref, *refs, per_step):
    pages = refs[:per_step]
    o_ref, ab_scr = refs[per_step:]
    j = pl.program_id(1)
    cpp = PAGE_SIZE // CMP_STRIDE
    r = lax.broadcasted_iota(jnp.int32, (PAGE_SIZE, PAGE_SIZE), 0)
    t = lax.broadcasted_iota(jnp.int32, (PAGE_SIZE, PAGE_SIZE), 1)
    assert cpp & (cpp - 1) == 0
    perm = jnp.where(t == (r & (cpp - 1)) * CMP_STRIDE + lax.shift_right_logical(r, int(math.log2(cpp))),
                     1.0, 0.0).astype(BF16)
    xs = [lax.dot_general(perm, pg[0, 0].astype(BF16), NT, preferred_element_type=F32) for pg in pages]
    C = xs[0].shape[1]
    ab = jnp.zeros((per_step * cpp, w_ref.shape[1]), F32)
    for l in range(CMP_STRIDE):
        xl = jnp.concatenate([x[l * cpp:(l + 1) * cpp] for x in xs], axis=0).astype(BF16)
        ab += jnp.dot(xl, w_ref[l * C:(l + 1) * C, :], preferred_element_type=F32)
    n = per_step * cpp
    ab_scr[pl.ds(pl.multiple_of(j * n, n), n), :] = ab

    @pl.when(j == pl.num_programs(1) - 1)
    def _():
        cab = jnp.dot(pe_ref[...].astype(BF16), w_ref[...], preferred_element_type=F32)
        half = ab_scr.shape[1] // 2
        const = cab[0:1, :half] + cab[1:2, half:]
        a, b = ab_scr[:, :half], ab_scr[:, half:]
        o_ref[0] = a + pltpu.roll(b, a.shape[0] - 1, 0) + const


def compress_paged(cache_t, layer, page_table, wbig, pe2):
    nbs, n_pages = page_table.shape
    per_step = _pick(n_pages, (16, 8, 4, 2, 1))
    cpp = PAGE_SIZE // CMP_STRIDE
    n_chunk = n_pages * cpp
    cout = wbig.shape[1] // 2
    in_specs = [pl.BlockSpec(wbig.shape, lambda b, j, pt: (0, 0)), pl.BlockSpec(pe2.shape, lambda b, j, pt: (0, 0))]
    in_specs += _page_specs(cache_t.shape[2:], layer, n_pages, per_step)
    grid_spec = pltpu.PrefetchScalarGridSpec(
        num_scalar_prefetch=1, grid=(nbs, n_pages // per_step), in_specs=in_specs,
        out_specs=pl.BlockSpec((1, n_chunk, cout), lambda b, j, pt: (b, 0, 0)),
        scratch_shapes=[pltpu.VMEM((n_chunk, 2 * cout), F32)])
    return pl.pallas_call(
        functools.partial(_compress_paged_kernel, per_step=per_step),
        out_shape=jax.ShapeDtypeStruct((nbs, n_chunk, cout), F32),
        grid_spec=grid_spec, compiler_params=_cparams(2),
        name="compress_paged")(page_table.reshape(-1), wbig, pe2, *([cache_t] * per_step))


def _even_w_in(w):
    aq, akv = A_HEADS * A_HEAD_DIM, A_GROUPS * A_HEAD_DIM
    splits = np.cumsum([aq] + [akv] * 6 + [3 * A_HEADS] + [512] * 4)[:-1]
    q, kc, vc, ks, vs, kw, vw, gt, rq, rk, rv, rg = jnp.split(w, [int(s) for s in splits], axis=1)
    gt = jnp.pad(gt, ((0, 0), (0, 128 - gt.shape[1])))
    return jnp.concatenate([q, rq, rk, rv, rg, kc, vc, ks, vs, kw, vw, gt], axis=1).astype(BF16)


EV_Q, EV_RQ, EV_RK, EV_RV, EV_RG = 0, 1, 2, 3, 4
EV_KS, EV_VS, EV_KW, EV_VW, EV_GT = 22, 23, 24, 25, 26
EV_CMP, EV_SLC, EV_WIN = 2560, 2816, 3072


def _compress_w(w_cmp, pe):
    G, d = A_GROUPS, A_HEAD_DIM
    r = CMP_LEN // CMP_STRIDE
    assert r == 2
    w = w_cmp.reshape(2, r, CMP_STRIDE, d, d)
    eye_kv = jnp.eye(2, dtype=F32)
    eye_g = jnp.eye(G, dtype=F32)
    big = jnp.einsum("khlde,kq,gp->lqpdhkge", w, eye_kv, eye_g)
    big = big.reshape(CMP_STRIDE * 2 * G * d, r * 2 * G * d).astype(BF16)
    pe_r = pe.reshape(2, r, CMP_STRIDE, d)
    rows = jnp.broadcast_to(pe_r.transpose(1, 2, 0, 3)[:, :, :, None, :], (r, CMP_STRIDE, 2, G, d))
    rows = rows.reshape(r, CMP_STRIDE * 2 * G * d)
    return big, jnp.pad(rows, ((0, 8 - r), (0, 0)))


def _slc_sum_matrix(n_rows, n_cmp, nbp):
    a = np.zeros((n_rows, nbp), np.float32)
    per, left = SLC_LEN // CMP_STRIDE, CMP_LEN // CMP_STRIDE - 1
    for j in range(nbp):
        for n in range(per * j - left, per * j + per):
            if 0 <= n < n_cmp:
                a[n, j] = 1.0
    return jnp.asarray(a, BF16)


def _gate_expand():
    e = np.zeros((128, 3 * A_HEADS * A_HEAD_DIM), np.float32)
    for h in range(A_HEADS):
        for i in range(3):
            e[3 * h + i, i * A_HEADS * A_HEAD_DIM + h * A_HEAD_DIM:i * A_HEADS * A_HEAD_DIM + (h + 1) * A_HEAD_DIM] = 1.0
    return jnp.asarray(e, BF16)


def _rope_aux():
    half = B_DK // 2
    freqs = ROPE_BASE ** (-jnp.arange(half, dtype=F32) / half)
    sign = jnp.concatenate([-jnp.ones((half,), F32), jnp.ones((half,), F32)])
    return jnp.stack([jnp.concatenate([freqs, freqs]), sign])


def _slc_shapes(n_slc, n_cmp_pad):
    nbp = -(-n_slc // 128) * 128
    return nbp, _slc_sum_matrix(n_cmp_pad, n_cmp_pad - 1, nbp)


def _nsa_common(nb, T, q_off):
    return dict(nb=nb, Tq=T, tq=_pick(T, (128, 64, 32, 16, 8)), G=A_GROUPS, hg=A_HEADS // A_GROUPS, dk=A_HEAD_DIM,
                dv=A_HEAD_DIM, g_per_v=1, q_off=q_off)


def _even_layer(xp, xs, p, a, dims, caches):
    B, T, nbs, Ts, past = dims
    g = p["norm_g"]
    w_in = _even_w_in(p["w_in_a"][a])
    wbig, pe2 = _compress_w(p["cmp_w"][a], p["cmp_pos"][a])
    w_out = p["w_out_a"][a].astype(BF16)
    aq = A_HEADS * A_HEAD_DIM
    wa, wb = w_out[:aq], w_out[aq:]
    e01 = _gate_expand()
    akv2 = 2 * A_GROUPS * A_HEAD_DIM

    akv = A_GROUPS * A_HEAD_DIM
    wt = jnp.concatenate([w_in[:, :aq], w_in[:, EV_SLC:EV_SLC + akv], w_in[:, EV_WIN:EV_WIN + akv]], axis=1).T
    wscale = jnp.concatenate([jnp.full((aq, 1), A_HEAD_DIM ** -0.5 * LOG2E, F32), jnp.ones((2 * akv, 1), F32)])
    zp, zpb, ttp = in_proj(xp, g[0:1], w_in, wt, wscale)
    kv_cmp_p = zp[:, EV_CMP:EV_CMP + akv2]
    kv_slc_p = zp[:, EV_SLC:EV_SLC + akv2]
    kv_win_p = zp[:, EV_WIN:EV_WIN + akv2]
    kvc_p, vct_p = compress(kv_cmp_p.reshape(B, T // CMP_STRIDE, CMP_STRIDE * akv2), wbig, pe2)
    n_slc = -(-T // SLC_LEN)
    nbp, a01 = _slc_shapes(n_slc, kvc_p.shape[1])
    o_cmp, sel = cmp_attend_t(ttp, kvc_p, vct_p, a01.T, B, T, n_slc)
    blk01 = np.zeros((nbp, T), np.float32)
    blk01[np.arange(T) // SLC_LEN, np.arange(T)] = 1.0
    o_slc = attend(zp, EV_Q, ttp, aq // akv, zpb, EV_VS, nb=B, T=T, tq=_pick(T, (512, 256, 128, 64, 32, 16, 8)),
                   tk=_pick(T, (1024, 512, 256, 128)), G=A_GROUPS, hg=A_HEADS // A_GROUPS, dk=A_HEAD_DIM,
                   dv=A_HEAD_DIM, g_per_v=1, mode="causal", sel=sel, e01=jnp.asarray(blk01, BF16))
    o_win = window_attend(zp, ttp, (aq + akv) // akv, zpb, EV_VW, B, T)
    ob_p, st_p = recurrence(zp, (EV_RQ, EV_RK, EV_RV, EV_RG), p["rope_p"], mode="ret", nb=B, T=T, pos_off=0)
    xp = even_out(o_cmp, o_slc, o_win, zp, EV_GT, ob_p, e01, wa, wb, g[1:2], xp)

    cache_cmp, cache_slc, win_buf, s0, page_table = caches
    zs = rms_matmul(xs, g[0:1], w_in, F32)
    kv_cmp_s = zs[:, EV_CMP:EV_CMP + akv2]
    kv_slc_s = zs[:, EV_SLC:EV_SLC + akv2]
    kv_win_s = zs[:, EV_WIN:EV_WIN + akv2]
    assert Ts < CMP_STRIDE and past % SLC_LEN == 0
    kvc_s = compress_paged(cache_cmp, a, page_table, wbig, pe2)
    n_slc = -(-(past + Ts) // SLC_LEN)
    nbp, a01 = _slc_shapes(n_slc, kvc_s.shape[1])
    o_cmp, sel = cmp_attend(zs, EV_Q, kvc_s, a01, nbs, Ts, past, n_slc)
    o_slc = slc_decode(zs, cache_slc, a, page_table, sel, nbs, Ts, nbp, past)
    band = jnp.concatenate([win_buf, kv_win_s.reshape(nbs, Ts, akv2)], axis=1)
    n_buf = win_buf.shape[1]
    bt = n_buf + Ts
    o_win = flash(zs, EV_Q, band.reshape(nbs * bt, akv2), 0, band.reshape(nbs * bt, akv2), 1, Tk=bt, tk=bt,
                  mode="window", k_off=past - n_buf, **_nsa_common(nbs, Ts, past))
    ob_s, st_s = recurrence(zs, (EV_RQ, EV_RK, EV_RV, EV_RG), p["rope_s"], mode="ret", nb=nbs, T=Ts, pos_off=past,
                            s0t=jnp.swapaxes(s0, -1, -2))
    xs = even_out(o_cmp, o_slc, o_win, zs, EV_GT, ob_s, e01, wa, wb, g[1:2], xs)

    shp = lambda x, n, t: x.reshape(n, t, 2, A_GROUPS, A_HEAD_DIM)
    wkeep = min(WINDOW, T)
    outs = (shp(kv_cmp_p, B, T), shp(kv_cmp_s, nbs, Ts), shp(kv_slc_p, B, T), shp(kv_slc_s, nbs, Ts),
            shp(kv_win_p, B, T)[:, T - wkeep:], shp(band[:, Ts:], nbs, n_buf),
            jnp.swapaxes(st_p, -1, -2), jnp.swapaxes(st_s, -1, -2).astype(s0.dtype))
    return xp, xs, outs


def _odd_layer(xp, xs, p, o, layer, dims, caches):
    B, T, nbs, Ts, past = dims
    g = p["norm_g"]
    w_in = p["w_in_c"][o].astype(BF16)
    w_out = p["w_out_c"][o].astype(BF16)
    cw = C_HEADS * 2 * C_HEAD_DIM
    wc, wd = w_out[:cw], w_out[cw:]
    lam_init = 0.8 - 0.6 * math.exp(-0.3 * layer)
    lam_p = p["diff_lam"][o]
    lb_logits = p["hgrn_lb"]

    zp, zpb, ktp = in_proj(xp, g[0:1], w_in, w_in[:, cw:2 * cw].T, jnp.ones((cw, 1), F32))
    o_diff = attend(zp, 0, ktp, 0, zpb, 2, nb=B, T=T, tq=_pick(T, (512, 256, 128, 64, 32, 16, 8)),
                    tk=_pick(T, (512, 256, 128)), G=2 * C_HEADS, hg=1, dk=C_HEAD_DIM, dv=2 * C_HEAD_DIM, g_per_v=2,
                    mode="causal")
    od_p, st_p = recurrence(zp, (3, 4, 5, 6), lb_logits, mode="hgrn", nb=B, T=T, pos_off=0, layer=o)
    xp = odd_out(o_diff, od_p, lam_p, lam_init, wc, wd, g[1:2], xp)

    cache_diff, s0, page_table = caches
    zs = rms_matmul(xs, g[0:1], w_in, F32)
    kv_s = zs[:, cw:3 * cw]
    o_diff = diff_decode(zs, cache_diff, o, page_table, nbs, Ts)
    od_s, st_s = recurrence(zs, (3, 4, 5, 6), lb_logits, mode="hgrn", nb=nbs, T=Ts, pos_off=past,
                            s0t=jnp.swapaxes(s0, -1, -2), layer=o)
    xs = odd_out(o_diff, od_s, lam_p, lam_init, wc, wd, g[1:2], xs)

    shp = lambda x, n, t: x.reshape(n, t, 2, C_HEADS, 2 * C_HEAD_DIM)
    outs = (shp(zp[:, cw:3 * cw], B, T), shp(kv_s, nbs, Ts),
            jnp.swapaxes(st_p, -1, -2), jnp.swapaxes(st_s, -1, -2).astype(s0.dtype))
    return xp, xs, outs


def _tail_layers(xp, xs, p, layer, dims, mem_prompt, cache_mem):
    B, T, nbs, Ts, _ = dims
    g = p["norm_g"][layer]
    D = xp.shape[1]
    w_q = p["w_xq"][layer].astype(BF16)
    w_kv = p["w_xkv"][layer].astype(BF16)
    w_o = p["w_xo"][layer].astype(BF16)
    w_up = p["w_up"][layer].astype(BF16)
    w_down = p["w_down"][layer].astype(BF16)
    n_mem = mem_prompt.shape[1]
    kv_mem = rms_matmul(mem_prompt.reshape(B * n_mem, D), g[0:1], w_kv, F32, norm=False).reshape(B, n_mem, 2 * D)
    qp = rms_matmul(xp, g[2:3], w_q, BF16)
    xp = proj_res(xattn(qp, kv_mem, T), w_o, g[3:4], xp)
    qs = rms_matmul(xs, g[2:3], w_q, BF16)
    xs = proj_res(xattn(qs, cache_mem, Ts, layer=layer), w_o, g[3:4], xs)
    xp = mlp(xp, g[4:5], g[5:6], w_up, w_down)
    xs = mlp(xs, g[4:5], g[5:6], w_up, w_down)
    return xp, xs, kv_mem.reshape(B, n_mem, 2, X_HEADS, D // X_HEADS)


def kernel(x_prompt, x_sample, cache_nsa_cmp_kv, cache_nsa_slc_kv, cache_nsa_win_kv, state_ret, cache_diff_kv, state_hgrn, cache_mem_kv, page_table, mem_prompt, norm_g, w_in_a, cmp_pos, cmp_w, w_out_a, w_in_c, diff_lam, hgrn_lb, w_out_c, w_xq, w_xkv, w_xo, w_up, w_down):
    B, T, D = x_prompt.shape
    nbs, Ts, _ = x_sample.shape
    depth = norm_g.shape[0]
    n_pages = page_table.shape[1]
    past = n_pages * PAGE_SIZE
    dims = (B, T, nbs, Ts, past)
    p = dict(w_in_a=w_in_a, cmp_pos=cmp_pos, cmp_w=cmp_w, w_out_a=w_out_a, w_in_c=w_in_c, diff_lam=diff_lam,
             hgrn_lb=hgrn_lb.astype(F32), w_out_c=w_out_c, w_xq=w_xq, w_xkv=w_xkv, w_xo=w_xo, w_up=w_up, w_down=w_down)
    p["rope_p"] = rope_table(_rope_aux(), T, 0)
    p["rope_s"] = rope_table(_rope_aux(), Ts, past)
    xp = x_prompt.reshape(B * T, D)
    xs = x_sample.reshape(nbs * Ts, D)
    feat_major = lambda c: c.transpose(0, 1, 3, 4, 5, 2).reshape(c.shape[0], c.shape[1], -1, c.shape[2])
    rows_of = lambda c: c.reshape(c.shape[0], c.shape[1], -1, c.shape[-1])
    c_cmp, c_slc, c_diff = feat_major(cache_nsa_cmp_kv), feat_major(cache_nsa_slc_kv), rows_of(cache_diff_kv)
    mshape = cache_mem_kv.shape
    c_mem = cache_mem_kv.reshape(mshape[:5] + (mshape[5] // 128, 128)).transpose(0, 1, 2, 3, 5, 4, 6)
    c_mem = c_mem.reshape(mshape[0], mshape[1], -1, 128)
    win_all = cache_nsa_win_kv.reshape(cache_nsa_win_kv.shape[0], nbs, cache_nsa_win_kv.shape[2], -1)
    ev, od, mem = [], [], []
    for layer in range(depth):
        pl_ = dict(p, norm_g=norm_g[layer])
        if layer % 2 == 0:
            a = layer // 2
            xp, xs, outs = _even_layer(xp, xs, pl_, a, dims, (c_cmp, c_slc, win_all[a], state_ret[a], page_table))
            ev.append(outs)
        else:
            o = layer // 2
            xp, xs, outs = _odd_layer(xp, xs, pl_, o, layer, dims, (c_diff, state_hgrn[o], page_table))
            od.append(outs)
        xp, xs, kvm = _tail_layers(xp, xs, dict(p, norm_g=norm_g), layer, dims, mem_prompt, c_mem)
        mem.append(kvm)
    stack = lambda lst, i: jnp.stack([t[i] for t in lst])
    return (xp.reshape(B, T, D), xs.reshape(nbs, Ts, D),
            stack(ev, 0), stack(ev, 1), stack(ev, 2), stack(ev, 3), stack(ev, 4), stack(ev, 5), stack(ev, 6), stack(ev, 7),
            stack(od, 0), stack(od, 1), stack(od, 2), stack(od, 3), jnp.stack(mem))
```

```python
import functools
import math

import numpy as np
import jax
import jax.numpy as jnp
from jax import lax
from jax.experimental import pallas as pl
from jax.experimental.pallas import tpu as pltpu

F32 = jnp.float32
BF16 = jnp.bfloat16

EPS = 1e-6
NEG = -1e30
BIG = 1e9
A_HEADS, A_GROUPS, A_HEAD_DIM = 8, 2, 64
CMP_LEN, CMP_STRIDE, SLC_LEN, SLC_TOPK, WINDOW = 32, 16, 64, 16, 512
B_HEADS, B_DK = 4, 128
ROPE_BASE = 10000.0
CHUNK = 64
C_HEADS, C_HEAD_DIM = 4, 64
D_HEADS = 4
F_FLOOR = 1e-6
X_HEADS = 4
PAGE_SIZE = 128
SUB = 8
LANES = 128

VMEM_LIMIT = 56 * 1024 * 1024

NT = (((1,), (1,)), ((), ()))
TN = (((0,), (0,)), ((), ()))


def _cparams(n_grid):
    return pltpu.CompilerParams(dimension_semantics=("arbitrary",) * n_grid, vmem_limit_bytes=VMEM_LIMIT)


def _pick(n, cands):
    for c in cands:
        if n % c == 0:
            return c
    return n


def _split3(x):
    hi = x.astype(BF16)
    r = x - hi.astype(F32)
    mid = r.astype(BF16)
    lo = (r - mid.astype(F32)).astype(BF16)
    return hi, mid, lo


def _dot_x01(x, m01):
    return sum(jnp.dot(p, m01, preferred_element_type=F32) for p in _split3(x))


def _dot_01x(m01, x):
    return sum(jnp.dot(m01, p, preferred_element_type=F32) for p in _split3(x))


def _dot3_nt(a, b):
    ah = a.astype(BF16)
    al = (a - ah.astype(F32)).astype(BF16)
    bh = b.astype(BF16)
    bl = (b - bh.astype(F32)).astype(BF16)
    d = lambda x, y: lax.dot_general(x, y, NT, preferred_element_type=F32)
    return d(ah, bh) + d(ah, bl) + d(al, bh)


def _sigmoid(x):
    return 1.0 / (1.0 + jnp.exp(-x))


def _rms(x, g):
    return x * lax.rsqrt(jnp.mean(x * x, axis=-1, keepdims=True) + EPS) * g


def _rms_matmul_kernel(x_ref, g_ref, w_ref, o_ref, *, norm):
    x = x_ref[...]
    if norm:
        x = _rms(x, g_ref[...])
    o_ref[...] = jnp.dot(x.astype(BF16), w_ref[...], preferred_element_type=F32).astype(o_ref.dtype)


def rms_matmul(x, g, w, out_dtype, norm=True):
    R, D = x.shape
    N = w.shape[1]
    tm = _pick(R, (512, 256, 128, 64, 32, 16, 8))
    tn = _pick(N, (1792, 1152, 1024, 896, 768, 640, 512, 384, 256, 128))
    return pl.pallas_call(
        functools.partial(_rms_matmul_kernel, norm=norm),
        out_shape=jax.ShapeDtypeStruct((R, N), out_dtype),
        grid=(R // tm, N // tn),
        in_specs=[pl.BlockSpec((tm, D), lambda i, j: (i, 0)),
                  pl.BlockSpec((1, D), lambda i, j: (0, 0)),
                  pl.BlockSpec((D, tn), lambda i, j: (0, j))],
        out_specs=pl.BlockSpec((tm, tn), lambda i, j: (i, j)),
        compiler_params=_cparams(2), name="rms_matmul")(x, g, w)


def _in_proj_kernel(x_ref, g_ref, w_ref, wt_ref, ws_ref, z_ref, zb_ref, kt_ref, *rest, pieces, rows):
    xn_scr = rest[-1]
    j = pl.program_id(1)

    @pl.when(j == 0)
    def _():
        xn = _rms(x_ref[...], g_ref[...]).astype(BF16)
        xn_scr[...] = xn
        t = lax.dot_general(wt_ref[...], xn, NT, preferred_element_type=F32)
        kt_ref[...] = (t * ws_ref[...]).astype(kt_ref.dtype)

    z = jnp.dot(xn_scr[...], w_ref[...], preferred_element_type=F32)
    z_ref[...] = z
    for jv, off, width, dst in pieces:
        @pl.when(j == jv)
        def _(off=off, width=width, dst=dst):
            zb_ref[:, dst:dst + width] = z[:, off:off + width].astype(BF16)
    if rows:
        kv_ref, (jv, off, n) = rest[0], rows

        @pl.when(j == jv)
        def _():
            tm = z.shape[0]
            for q in range(n):
                kv_ref[pl.ds(q, tm, stride=n), :] = z[:, off + q * LANES:off + (q + 1) * LANES]


def in_proj(x, g, w, wt, wscale, vcols, row_cols=None):
    R, D = x.shape
    N = w.shape[1]
    Fk = wt.shape[0]
    tm = _pick(R, (1024, 512, 256, 128, 64, 32, 16, 8))
    tn = _pick(N, (1792, 1152, 1024, 896, 768, 640, 512, 384, 256, 128))
    pieces, wv = [], 0
    for start, width in vcols:
        assert start % tn + width <= tn
        pieces.append((start // tn, start % tn, width, wv))
        wv += width
    out_shape = [jax.ShapeDtypeStruct((R, N), F32), jax.ShapeDtypeStruct((R, wv), BF16), jax.ShapeDtypeStruct((Fk, R), BF16)]
    out_specs = [pl.BlockSpec((tm, tn), lambda i, j: (i, j)), pl.BlockSpec((tm, wv), lambda i, j: (i, 0)),
                 pl.BlockSpec((Fk, tm), lambda i, j: (0, i))]
    rows = None
    if row_cols is not None:
        start, n = row_cols
        assert start % tn + n * LANES <= tn
        rows = (start // tn, start % tn, n)
        out_shape.append(jax.ShapeDtypeStruct((R * n, LANES), F32))
        out_specs.append(pl.BlockSpec((tm * n, LANES), lambda i, j: (i, 0)))
    return pl.pallas_call(
        functools.partial(_in_proj_kernel, pieces=tuple(pieces), rows=rows),
        out_shape=tuple(out_shape),
        grid=(R // tm, N // tn),
        in_specs=[pl.BlockSpec((tm, D), lambda i, j: (i, 0)),
                  pl.BlockSpec((1, D), lambda i, j: (0, 0)),
                  pl.BlockSpec((D, tn), lambda i, j: (0, j)),
                  pl.BlockSpec((Fk, D), lambda i, j: (0, 0)),
                  pl.BlockSpec((Fk, 1), lambda i, j: (0, 0))],
        out_specs=tuple(out_specs),
        scratch_shapes=[pltpu.VMEM((tm, D), BF16)],
        compiler_params=_cparams(2), name="in_proj")(x, g, w, wt, wscale)


def _mlp_kernel(x_ref, g4_ref, g5_ref, wu_ref, wd_ref, o_ref, xn_scr, acc_scr):
    j = pl.program_id(1)

    @pl.when(j == 0)
    def _():
        xn_scr[...] = _rms(x_ref[...], g4_ref[...]).astype(BF16)
        acc_scr[...] = jnp.zeros_like(acc_scr)

    h = jnp.dot(xn_scr[...], wu_ref[...], preferred_element_type=F32)
    h = jnp.square(jnp.maximum(h, 0.0))
    acc_scr[...] += jnp.dot(h.astype(BF16), wd_ref[...], preferred_element_type=F32)

    @pl.when(j == pl.num_programs(1) - 1)
    def _():
        o_ref[...] = x_ref[...] + _rms(acc_scr[...], g5_ref[...])


def mlp(x, g4, g5, w_up, w_down):
    R, D = x.shape
    F = w_up.shape[1]
    tm = _pick(R, (1024, 512, 256, 128, 64, 32, 16, 8))
    tf = _pick(F, (1024, 512, 256, 128))
    return pl.pallas_call(
        _mlp_kernel,
        out_shape=jax.ShapeDtypeStruct((R, D), F32),
        grid=(R // tm, F // tf),
        in_specs=[pl.BlockSpec((tm, D), lambda i, j: (i, 0)),
                  pl.BlockSpec((1, D), lambda i, j: (0, 0)),
                  pl.BlockSpec((1, D), lambda i, j: (0, 0)),
                  pl.BlockSpec((D, tf), lambda i, j: (0, j)),
                  pl.BlockSpec((tf, D), lambda i, j: (j, 0))],
        out_specs=pl.BlockSpec((tm, D), lambda i, j: (i, 0)),
        scratch_shapes=[pltpu.VMEM((tm, D), BF16), pltpu.VMEM((tm, D), F32)],
        compiler_params=_cparams(2), name="mlp")(x, g4, g5, w_up, w_down)


def _proj_res_kernel(a_ref, w_ref, g_ref, x_ref, o_ref):
    y = jnp.dot(a_ref[...], w_ref[...], preferred_element_type=F32)
    o_ref[...] = x_ref[...] + _rms(y, g_ref[...])


def proj_res(a, w, g, x):
    R, D = x.shape
    K = a.shape[1]
    tm = _pick(R, (512, 256, 128, 64, 32, 16, 8))
    return pl.pallas_call(
        _proj_res_kernel,
        out_shape=jax.ShapeDtypeStruct((R, D), F32),
        grid=(R // tm,),
        in_specs=[pl.BlockSpec((tm, K), lambda i: (i, 0)),
                  pl.BlockSpec((K, D), lambda i: (0, 0)),
                  pl.BlockSpec((1, D), lambda i: (0, 0)),
                  pl.BlockSpec((tm, D), lambda i: (i, 0))],
        out_specs=pl.BlockSpec((tm, D), lambda i: (i, 0)),
        compiler_params=_cparams(1), name="proj_res")(a, w, g, x)


def _even_out_kernel(oc_ref, os_ref, ow_ref, gt_ref, ob_ref, e_ref, wa_ref, wb_ref, g_ref, x_ref, o_ref):
    gates = _sigmoid(gt_ref[...])
    ge = _dot_x01(gates, e_ref[...])
    aq = oc_ref.shape[1]
    oa = ge[:, :aq] * oc_ref[...] + ge[:, aq:2 * aq] * os_ref[...] + ge[:, 2 * aq:] * ow_ref[...]
    y = jnp.dot(oa.astype(BF16), wa_ref[...], preferred_element_type=F32)
    y += jnp.dot(ob_ref[...], wb_ref[...], preferred_element_type=F32)
    o_ref[...] = x_ref[...] + _rms(y, g_ref[...])


def even_out(o_cmp, o_slc, o_win, z, gt_blk, o_b, e01, wa, wb, g, x):
    R, D = x.shape
    aq = o_cmp.shape[1]
    tm = _pick(R, (256, 128, 64, 32, 16, 8))
    row = lambda i: (i, 0)
    fix = lambda i: (0, 0)
    return pl.pallas_call(
        _even_out_kernel,
        out_shape=jax.ShapeDtypeStruct((R, D), F32),
        grid=(R // tm,),
        in_specs=[pl.BlockSpec((tm, aq), row), pl.BlockSpec((tm, aq), row), pl.BlockSpec((tm, aq), row),
                  pl.BlockSpec((tm, 128), lambda i: (i, gt_blk)),
                  pl.BlockSpec((tm, o_b.shape[1]), row),
                  pl.BlockSpec(e01.shape, fix), pl.BlockSpec(wa.shape, fix), pl.BlockSpec(wb.shape, fix),
                  pl.BlockSpec((1, D), fix), pl.BlockSpec((tm, D), row)],
        out_specs=pl.BlockSpec((tm, D), row),
        compiler_params=_cparams(1), name="even_out")(o_cmp, o_slc, o_win, z, o_b, e01, wa, wb, g, x)


def _odd_out_kernel(oc_ref, od_ref, lam_ref, wc_ref, wd_ref, g_ref, x_ref, o_ref, *, lam_init, heads, dv):
    lp = lam_ref[...]
    lam = (jnp.exp(jnp.sum(lp[0:1] * lp[1:2], axis=-1, keepdims=True))
           - jnp.exp(jnp.sum(lp[2:3] * lp[3:4], axis=-1, keepdims=True)) + lam_init)
    parts = []
    for h in range(heads):
        o1 = oc_ref[:, (2 * h) * dv:(2 * h + 1) * dv]
        o2 = oc_ref[:, (2 * h + 1) * dv:(2 * h + 2) * dv]
        o = o1 - lam * o2
        o = o * lax.rsqrt(jnp.mean(o * o, axis=-1, keepdims=True) + EPS) * (1.0 - lam_init)
        parts.append(o.astype(BF16))
    oc = jnp.concatenate(parts, axis=-1)
    y = jnp.dot(oc, wc_ref[...], preferred_element_type=F32)
    y += jnp.dot(od_ref[...], wd_ref[...], preferred_element_type=F32)
    o_ref[...] = x_ref[...] + _rms(y, g_ref[...])


def odd_out(o_diff, o_d, lam_p, lam_init, wc, wd, g, x):
    R, D = x.shape
    tm = _pick(R, (256, 128, 64, 32, 16, 8))
    row = lambda i: (i, 0)
    fix = lambda i: (0, 0)
    return pl.pallas_call(
        functools.partial(_odd_out_kernel, lam_init=lam_init, heads=C_HEADS, dv=2 * C_HEAD_DIM),
        out_shape=jax.ShapeDtypeStruct((R, D), F32),
        grid=(R // tm,),
        in_specs=[pl.BlockSpec((tm, o_diff.shape[1]), row), pl.BlockSpec((tm, o_d.shape[1]), row),
                  pl.BlockSpec(lam_p.shape, fix), pl.BlockSpec(wc.shape, fix), pl.BlockSpec(wd.shape, fix),
                  pl.BlockSpec((1, D), fix), pl.BlockSpec((tm, D), row)],
        out_specs=pl.BlockSpec((tm, D), row),
        compiler_params=_cparams(1), name="odd_out")(o_diff, o_d, lam_p, wc, wd, g, x)


def _xattn_kernel(q_ref, kv_ref, o_ref, *, heads, scale, rows):
    dm = q_ref.shape[1]
    hd = dm // heads
    ss, vhs = [], []
    for h in range(heads):
        qh = q_ref[:, h * hd:(h + 1) * hd]
        if rows:
            nh = hd // 128
            per_tok = 2 * nh * heads
            n_mem = kv_ref.shape[2] // per_tok
            row = lambda slot: jnp.concatenate(
                [kv_ref[0, 0, pl.ds((slot * nh + i) * heads + h, n_mem, stride=per_tok), :] for i in range(nh)],
                axis=-1).astype(BF16)
            kh, vh = row(0), row(1)
        else:
            kh = kv_ref[0, :, h * hd:(h + 1) * hd].astype(BF16)
            vh = kv_ref[0, :, dm + h * hd:dm + (h + 1) * hd].astype(BF16)
        ss.append(lax.dot_general(qh, kh, NT, preferred_element_type=F32) * scale)
        vhs.append(vh)
    ps = []
    for s in ss:
        e = jnp.exp(s - jnp.max(s, axis=-1, keepdims=True))
        ps.append((e / jnp.sum(e, axis=-1, keepdims=True)).astype(BF16))
    for h in range(heads):
        o = jnp.dot(ps[h], vhs[h], preferred_element_type=F32)
        o_ref[:, h * hd:(h + 1) * hd] = o.astype(o_ref.dtype)


def xattn(q, kv, rows_per_batch, layer=None):
    R, D = q.shape
    tq = _pick(rows_per_batch, (512, 256, 128, 64, 32, 16, 8))
    per = rows_per_batch // tq
    if layer is None:
        kv_spec = pl.BlockSpec((1,) + kv.shape[1:], lambda i: (i // per, 0, 0))
    else:
        kv_spec = pl.BlockSpec((1, 1) + kv.shape[2:], lambda i: (layer, i // per, 0, 0))
    return pl.pallas_call(
        functools.partial(_xattn_kernel, heads=X_HEADS, scale=(D // X_HEADS) ** -0.5, rows=layer is not None),
        out_shape=jax.ShapeDtypeStruct((R, D), BF16),
        grid=(R // tq,),
        in_specs=[pl.BlockSpec((tq, D), lambda i: (i, 0)), kv_spec],
        out_specs=pl.BlockSpec((tq, D), lambda i: (i, 0)),
        compiler_params=_cparams(1), name="xattn")(q, kv)


def _compress_kernel(x_ref, w_ref, pe_ref, o_ref, vt_ref):
    x = x_ref[0].astype(BF16)
    ab = jnp.dot(x, w_ref[...], preferred_element_type=F32)
    cab = jnp.dot(pe_ref[...].astype(BF16), w_ref[...], preferred_element_type=F32)
    half = ab.shape[1] // 2
    a, b = ab[:, :half], ab[:, half:]
    const = cab[0:1, :half] + cab[1:2, half:]
    n = a.shape[0]
    out = a + pltpu.roll(b, n - 1, 0) + const
    o_ref[0] = out
    vt_ref[0] = out[:, half // 2:].T.astype(vt_ref.dtype)


def compress(xc, wbig, pe2):
    nb, n_chunk, kdim = xc.shape
    cout = wbig.shape[1] // 2
    return pl.pallas_call(
        _compress_kernel,
        out_shape=(jax.ShapeDtypeStruct((nb, n_chunk, cout), F32), jax.ShapeDtypeStruct((nb, cout // 2, n_chunk), BF16)),
        grid=(nb,),
        in_specs=[pl.BlockSpec((1, n_chunk, kdim), lambda b: (b, 0, 0)),
                  pl.BlockSpec(wbig.shape, lambda b: (0, 0)),
                  pl.BlockSpec(pe2.shape, lambda b: (0, 0))],
        out_specs=(pl.BlockSpec((1, n_chunk, cout), lambda b: (b, 0, 0)),
                   pl.BlockSpec((1, cout // 2, n_chunk), lambda b: (b, 0, 0))),
        compiler_params=_cparams(1), name="compress")(xc, wbig, pe2)


def _cmp_kernel(q_ref, kv_ref, a_ref, o_ref, sel_ref, *, tq, q_off, n_slc, nbp, bpb):
    G, hg, d = A_GROUPS, A_HEADS // A_GROUPS, A_HEAD_DIM
    qi = pl.program_id(1)
    ncmp = kv_ref.shape[1]
    pos_q = q_off + qi * tq + lax.broadcasted_iota(jnp.int32, (tq, 1), 0)
    cmp_end = lax.broadcasted_iota(jnp.int32, (1, ncmp), 1) * CMP_STRIDE + (CMP_LEN - 1)
    mask3 = (cmp_end <= pos_q)[None]
    jb = lax.broadcasted_iota(jnp.int32, (1, nbp), 1)
    blk_q = lax.shift_right_logical(pos_q, int(math.log2(SLC_LEN)))
    valid = jb <= blk_q
    forced = valid & ((jb == 0) | (jb >= blk_q - 1))
    k_top = min(SLC_TOPK, n_slc)
    scores = []
    for bi in range(bpb):
        rs = slice(bi * tq, (bi + 1) * tq)
        for g in range(G):
            kc = kv_ref[bi, :, g * d:(g + 1) * d].astype(BF16)
            vc = kv_ref[bi, :, (G + g) * d:(G + g + 1) * d].astype(BF16)
            qs = jnp.concatenate([q_ref[rs, (g * hg + h) * d:(g * hg + h + 1) * d] for h in range(hg)], axis=0)
            qs = (qs * d ** -0.5).astype(BF16)
            s = lax.dot_general(qs, kc, NT, preferred_element_type=F32).reshape(hg, tq, ncmp)
            s = jnp.where(mask3, s, NEG)
            e = jnp.where(mask3, jnp.exp(s - jnp.max(s, axis=-1, keepdims=True)), 0.0)
            den = jnp.sum(e, axis=-1, keepdims=True)
            p = e / jnp.where(den > 0.0, den, 1.0)
            o = jnp.dot(p.reshape(hg * tq, ncmp).astype(BF16), vc, preferred_element_type=F32)
            for h in range(hg):
                o_ref[rs, (g * hg + h) * d:(g * hg + h + 1) * d] = o[h * tq:(h + 1) * tq]
            p_slc = _dot_x01(jnp.sum(p, axis=0), a_ref[...])
            score = jnp.where(forced, BIG, jnp.where(valid, p_slc, -BIG))
            scores.append(jnp.where(jb < n_slc, score, -jnp.inf))
    score = jnp.concatenate(scores, axis=0)
    sel = jnp.zeros(score.shape, F32)
    jbf = jb.astype(F32)
    for _ in range(k_top):
        mx = jnp.max(score, axis=-1, keepdims=True)
        jm = jnp.min(jnp.where(score == mx, jbf, float(nbp)), axis=-1, keepdims=True)
        hit = jbf == jm
        sel = jnp.where(hit, 1.0, sel)
        score = jnp.where(hit, -jnp.inf, score)
    for bi in range(bpb):
        for g in range(G):
            r0 = (bi * G + g) * tq
            ok = valid & (sel[r0:r0 + tq] > 0.5)
            sel_ref[bi * tq:(bi + 1) * tq, g * nbp:(g + 1) * nbp] = jnp.where(ok, 0.0, SEL_OFF).astype(BF16)


def cmp_attend(z, q_blk, kvc, a01, nb, Tq, q_off, n_slc):
    aq = A_HEADS * A_HEAD_DIM
    nbp = a01.shape[1]
    tq = _pick(Tq, (256, 128, 64, 32, 16, 8))
    nq = Tq // tq
    bpb = _pick(nb, (8, 4, 2, 1)) if nq == 1 and tq <= 16 else 1
    ncmp = kvc.shape[1]
    return pl.pallas_call(
        functools.partial(_cmp_kernel, tq=tq, q_off=q_off, n_slc=n_slc, nbp=nbp, bpb=bpb),
        out_shape=(jax.ShapeDtypeStruct((nb * Tq, aq), F32),
                   jax.ShapeDtypeStruct((nb * Tq, A_GROUPS * nbp), BF16)),
        grid=(nb // bpb, nq),
        in_specs=[pl.BlockSpec((bpb * tq, aq), lambda b, i: (b * nq + i, q_blk)),
                  pl.BlockSpec((bpb, ncmp, kvc.shape[2]), lambda b, i: (b, 0, 0)),
                  pl.BlockSpec(a01.shape, lambda b, i: (0, 0))],
        out_specs=(pl.BlockSpec((bpb * tq, aq), lambda b, i: (b * nq + i, 0)),
                   pl.BlockSpec((bpb * tq, A_GROUPS * nbp), lambda b, i: (b * nq + i, 0))),
        compiler_params=_cparams(2), name="cmp_attend")(z, kvc, a01)


class _FlashCfg:
    def __init__(self, **kw):
        self.__dict__.update(kw)


def _tile_range(c, qi):
    q_lo = c.q_off + qi * c.tq
    last = (q_lo + c.tq - 1 - c.k_off) // c.tk
    if c.mode == "window":
        first = (q_lo - (WINDOW - 1) - c.k_off) // c.tk
    else:
        first = 0 * qi
    return first, last


def _flash_kernel(*refs, c):
    if c.sel:
        q_ref, k_ref, v_ref, sel_ref, o_ref, qs_scr, m_scr, l_scr, acc_scr = refs
    else:
        q_ref, k_ref, v_ref, o_ref, qs_scr, m_scr, l_scr, acc_scr = refs
        sel_ref = None
    qi, j = pl.program_id(1), pl.program_id(2)
    tq, tk, G, hg, dk, dv = c.tq, c.tk, c.G, c.hg, c.dk, c.dv

    @pl.when(j == 0)
    def _():
        m_scr[...] = jnp.full(m_scr.shape, NEG, F32)
        l_scr[...] = jnp.zeros(l_scr.shape, F32)
        acc_scr[...] = jnp.zeros(acc_scr.shape, F32)
        for g in range(G):
            qs = jnp.concatenate([q_ref[:, (g * hg + h) * dk:(g * hg + h + 1) * dk] for h in range(hg)], axis=0)
            qs_scr[g] = (qs * dk ** -0.5).astype(BF16)

    first, last = _tile_range(c, qi)
    jabs = first + j

    @pl.when((jabs >= 0) & (jabs <= jnp.minimum(last, c.nk - 1)))
    def _():
        pos_q = c.q_off + qi * tq + lax.broadcasted_iota(jnp.int32, (tq, 1), 0)
        tok = jabs * tk + lax.broadcasted_iota(jnp.int32, (1, tk), 1)
        dpos = pos_q - (c.k_off + tok)
        mask = dpos >= 0
        if c.mode == "window":
            mask = mask & (dpos < WINDOW)
        if c.sel:
            jb = lax.broadcasted_iota(jnp.int32, (c.nbp, 1), 0)
            e01 = jnp.where(jb == lax.shift_right_logical(tok, int(math.log2(SLC_LEN))), 1.0, 0.0).astype(BF16)
        for g in range(G):
            kg = k_ref[:, g * dk:(g + 1) * dk].astype(BF16)
            vi = g // c.g_per_v
            vg = v_ref[:, vi * dv:(vi + 1) * dv].astype(BF16)
            s = lax.dot_general(qs_scr[g], kg, NT, preferred_element_type=F32).reshape(hg, tq, tk)
            mk = mask
            if c.sel:
                st = jnp.dot(sel_ref[:, g * c.nbp:(g + 1) * c.nbp], e01, preferred_element_type=F32)
                mk = mk & (st > 0.5)
            mk = mk[None]
            s = jnp.where(mk, s, NEG)
            m_old = m_scr[g]
            m_new = jnp.maximum(m_old, jnp.max(s, axis=-1, keepdims=True))
            alpha = jnp.exp(m_old - m_new)
            p = jnp.where(mk, jnp.exp(s - m_new), 0.0)
            l_scr[g] = alpha * l_scr[g] + jnp.sum(p, axis=-1, keepdims=True)
            pv = jnp.dot(p.reshape(hg * tq, tk).astype(BF16), vg, preferred_element_type=F32)
            acc_scr[g] = alpha * acc_scr[g] + pv.reshape(hg, tq, dv)
            m_scr[g] = m_new

    @pl.when(j == pl.num_programs(2) - 1)
    def _():
        for g in range(G):
            l = l_scr[g]
            o = acc_scr[g] / jnp.where(l > 0.0, l, 1.0)
            for h in range(hg):
                o_ref[:, (g * hg + h) * dv:(g * hg + h + 1) * dv] = o[h].astype(o_ref.dtype)


def flash(q2d, q_blk, k2d, k_blk, v2d, v_blk, *, nb, Tq, Tk, tq, tk, G, hg, dk, dv, g_per_v, mode,
          q_off, k_off, sel=None, nbp=0):
    nq, nk = Tq // tq, Tk // tk
    assert Tq % tq == 0 and Tk % tk == 0
    c = _FlashCfg(tq=tq, tk=tk, G=G, hg=hg, dk=dk, dv=dv, g_per_v=g_per_v, mode=mode, q_off=q_off, k_off=k_off,
                  nk=nk, sel=sel is not None, nbp=nbp)
    steps = max(min(_tile_range(c, i)[1], nk - 1) - _tile_range(c, i)[0] + 1 for i in range(nq))
    qw, kw, vw = G * hg * dk, G * dk, (G // g_per_v) * dv

    def kv_map(blk):
        def f(b, i, j):
            first, last = _tile_range(c, i)
            return (b * nk + jnp.clip(first + j, 0, jnp.minimum(last, nk - 1)), blk)
        return f

    in_specs = [pl.BlockSpec((tq, qw), lambda b, i, j: (b * nq + i, q_blk)),
                pl.BlockSpec((tk, kw), kv_map(k_blk)),
                pl.BlockSpec((tk, vw), kv_map(v_blk))]
    args = [q2d, k2d, v2d]
    if sel is not None:
        in_specs.append(pl.BlockSpec((tq, G * nbp), lambda b, i, j: (b * nq + i, 0)))
        args.append(sel)
    return pl.pallas_call(
        functools.partial(_flash_kernel, c=c),
        out_shape=jax.ShapeDtypeStruct((nb * Tq, G * hg * dv), F32),
        grid=(nb, nq, steps),
        in_specs=in_specs,
        out_specs=pl.BlockSpec((tq, G * hg * dv), lambda b, i, j: (b * nq + i, 0)),
        scratch_shapes=[pltpu.VMEM((G, hg * tq, dk), BF16), pltpu.VMEM((G, hg, tq, 1), F32),
                        pltpu.VMEM((G, hg, tq, 1), F32), pltpu.VMEM((G, hg, tq, dv), F32)],
        compiler_params=_cparams(3), name="flash_" + mode + ("_sel" if sel is not None else ""))(*args)


LOG2E = 1.4426950408889634
SEL_OFF = -2.0 ** 30


def _attn_kernel(tab_ref, q_ref, kt_ref, v_ref, *refs, c):
    if c.sel:
        sel_ref, e01_ref = refs[:2]
        refs = refs[2:]
    o_ref, qs_scr, s_scr, p_scr, m_scr, l_scr, a_scr, acc_scr = refs
    n = pl.program_id(1)
    qi, kj, flags = tab_ref[0, n], tab_ref[1, n], tab_ref[2, n]
    tq, tk, G, hg, dk, dv = c.tq, c.tk, c.G, c.hg, c.dk, c.dv
    rows = hg * tq
    ncb = tk // LANES

    @pl.when((flags & 1) != 0)
    def _():
        m_scr[...] = jnp.full(m_scr.shape, NEG, F32)
        l_scr[...] = jnp.zeros(l_scr.shape, F32)
        acc_scr[...] = jnp.zeros(acc_scr.shape, F32)
        for g in range(G):
            qs = jnp.concatenate([q_ref[:, (g * hg + h) * dk:(g * hg + h + 1) * dk] for h in range(hg)], axis=0)
            qs_scr[g] = (qs * (dk ** -0.5 * LOG2E)).astype(BF16)

    def step(masked):
        if masked:
            dq = qi * tq - kj * tk
            ok = ((lax.broadcasted_iota(jnp.int32, (tq, tk), 0) - lax.broadcasted_iota(jnp.int32, (tq, tk), 1)) + dq >= 0)[None]
        for g in range(G):
            s = jnp.dot(qs_scr[g], kt_ref[g * dk:(g + 1) * dk, :], preferred_element_type=F32)
            if c.sel or masked:
                s = s.reshape(hg, tq, tk)
                if c.sel:
                    s = s + jnp.dot(sel_ref[:, g * c.nbp:(g + 1) * c.nbp], e01_ref[...], preferred_element_type=F32)[None]
                if masked:
                    s = jnp.where(ok, s, NEG)
                s = s.reshape(rows, tk)
            s_scr[g] = s
        for g in range(G):
            cols = [s_scr[g, :, cb * LANES:(cb + 1) * LANES] for cb in range(ncb)]
            mx = cols[0]
            for x in cols[1:]:
                mx = jnp.maximum(mx, x)
            m_old = m_scr[g]
            m_new = jnp.maximum(m_old, jnp.broadcast_to(jnp.max(mx, axis=-1, keepdims=True), (rows, LANES)))
            alpha = jnp.exp2(m_old - m_new)
            psum = None
            for cb in range(ncb):
                p = jnp.exp2(cols[cb] - m_new)
                psum = p if psum is None else psum + p
                p_scr[g, :, cb * LANES:(cb + 1) * LANES] = p.astype(BF16)
            l_scr[g] = alpha * l_scr[g] + psum
            m_scr[g] = m_new
            a_scr[g] = alpha
        for g in range(G):
            vi = g // c.g_per_v
            pv = jnp.dot(p_scr[g], v_ref[:, vi * dv:(vi + 1) * dv], preferred_element_type=F32)
            acc_scr[g] = a_scr[g, :, :dv] * acc_scr[g] + pv

    @pl.when((flags & 4) != 0)
    def _():
        step(True)

    @pl.when((flags & 4) == 0)
    def _():
        step(False)

    @pl.when((flags & 2) != 0)
    def _():
        for g in range(G):
            l = jnp.sum(l_scr[g], axis=-1, keepdims=True)
            o = acc_scr[g] / jnp.where(l > 0.0, l, 1.0)
            for h in range(hg):
                o_ref[:, (g * hg + h) * dv:(g * hg + h + 1) * dv] = o[h * tq:(h + 1) * tq].astype(o_ref.dtype)


def attend(z, q_blk, kt, kt_blk, zb, v_blk, *, nb, T, tq, tk, G, hg, dk, dv, g_per_v, mode, sel=None, e01=None):
    nq, nk = T // tq, T // tk
    assert mode == "causal"
    c = _FlashCfg(tq=tq, tk=tk, G=G, hg=hg, dk=dk, dv=dv, g_per_v=g_per_v, mode=mode, q_off=0, k_off=0, nk=nk,
                  sel=sel is not None, nbp=0 if sel is None else sel.shape[1] // G)
    tab = []
    for i in range(nq):
        first, last = _tile_range(c, i)
        first, last = max(first, 0), min(last, nk - 1)
        for j in range(first, last + 1):
            lo, hi = i * tq - (j * tk + tk - 1), i * tq + tq - 1 - j * tk
            masked = lo < 0 or (mode == "window" and hi >= WINDOW)
            tab.append((i, j, (j == first) * 1 + (j == last) * 2 + masked * 4))
    tab = jnp.asarray(np.array(tab, np.int32).T)
    npairs = tab.shape[1]
    rows = hg * tq
    in_specs = [pl.BlockSpec((tq, G * hg * dk), lambda b, n, t: (b * nq + t[0, n], q_blk)),
                pl.BlockSpec((G * dk, tk), lambda b, n, t: (kt_blk, b * nk + t[1, n])),
                pl.BlockSpec((tk, (G // g_per_v) * dv), lambda b, n, t: (b * nk + t[1, n], v_blk))]
    args = [z, kt, zb]
    if sel is not None:
        in_specs += [pl.BlockSpec((tq, sel.shape[1]), lambda b, n, t: (b * nq + t[0, n], 0)),
                     pl.BlockSpec((e01.shape[0], tk), lambda b, n, t: (0, t[1, n]))]
        args += [sel, e01]
    grid_spec = pltpu.PrefetchScalarGridSpec(
        num_scalar_prefetch=1, grid=(nb, npairs), in_specs=in_specs,
        out_specs=pl.BlockSpec((tq, G * hg * dv), lambda b, n, t: (b * nq + t[0, n], 0)),
        scratch_shapes=[pltpu.VMEM((G, rows, dk), BF16), pltpu.VMEM((G, rows, tk), F32),
                        pltpu.VMEM((G, rows, tk), BF16), pltpu.VMEM((G, rows, LANES), F32),
                        pltpu.VMEM((G, rows, LANES), F32), pltpu.VMEM((G, rows, LANES), F32),
                        pltpu.VMEM((G, rows, dv), F32)])
    return pl.pallas_call(
        functools.partial(_attn_kernel, c=c),
        out_shape=jax.ShapeDtypeStruct((nb * T, G * hg * dv), F32),
        grid_spec=grid_spec, compiler_params=_cparams(2),
        name="attend_" + mode + ("_sel" if sel is not None else ""))(tab, *args)


def _block_diag_queries(qt_ref, heads, tq, dk):
    zero = jnp.zeros((dk, tq), BF16)
    half = heads // 2
    top = jnp.concatenate([qt_ref[i * dk:(i + 1) * dk, :] if i < half else zero for i in range(heads)], axis=1)
    bot = jnp.concatenate([zero if i < half else qt_ref[i * dk:(i + 1) * dk, :] for i in range(heads)], axis=1)
    return jnp.concatenate([top, bot], axis=0)


def _cmp_t_kernel(qt_ref, kvc_ref, vct_ref, at_ref, o_ref, sel_ref, *, tq, n_slc, nbp):
    G, hg, d = A_GROUPS, A_HEADS // A_GROUPS, A_HEAD_DIM
    assert G == 2
    qi = pl.program_id(1)
    ncmp = kvc_ref.shape[1]
    pos_q = qi * tq + lax.broadcasted_iota(jnp.int32, (1, tq), 1)
    cmp_end = lax.broadcasted_iota(jnp.int32, (ncmp, 1), 0) * CMP_STRIDE + (CMP_LEN - 1)
    ok = cmp_end <= pos_q
    jb = lax.broadcasted_iota(jnp.int32, (nbp, 1), 0)
    blk_q = lax.shift_right_logical(pos_q, int(math.log2(SLC_LEN)))
    valid = jb <= blk_q
    forced = valid & ((jb == 0) | (jb >= blk_q - 1))
    st = jnp.dot(kvc_ref[0, :, :G * d].astype(BF16), _block_diag_queries(qt_ref, A_HEADS, tq, d),
                 preferred_element_type=F32)
    scores, outs, pgs, pgrp = [], [], [], []
    for g in range(G):
        ps = []
        for h in range(hg):
            e = g * hg + h
            x = jnp.where(ok, st[:, e * tq:(e + 1) * tq], NEG)
            p = jnp.where(ok, jnp.exp2(x - jnp.max(x, axis=0, keepdims=True)), 0.0)
            den = jnp.sum(p, axis=0, keepdims=True)
            ps.append(p / jnp.where(den > 0.0, den, 1.0))
        pgs.append(jnp.concatenate(ps, axis=1).astype(BF16))
        pgrp.append(functools.reduce(jnp.add, ps))
    for g in range(G):
        outs.append(jnp.dot(vct_ref[0, g * d:(g + 1) * d, :], pgs[g], preferred_element_type=F32))
    for g in range(G):
        p_slc = _dot_01x(at_ref[...], pgrp[g])
        score = jnp.where(forced, BIG, jnp.where(valid, p_slc, -BIG))
        scores.append(jnp.where(jb < n_slc, score, -jnp.inf))
    score = jnp.concatenate(scores, axis=1)
    jbf = jb.astype(F32)
    sel = jnp.zeros(score.shape, F32)
    for _ in range(min(SLC_TOPK, n_slc)):
        mx = jnp.max(score, axis=0, keepdims=True)
        jm = jnp.min(jnp.where(score == mx, jbf, float(nbp)), axis=0, keepdims=True)
        hit = jbf == jm
        sel = jnp.where(hit, 1.0, sel)
        score = jnp.where(hit, -jnp.inf, score)
    for g in range(G):
        okg = valid & (sel[:, g * tq:(g + 1) * tq] > 0.5)
        sel_ref[:, g * nbp:(g + 1) * nbp] = jnp.where(okg, 0.0, SEL_OFF).T.astype(BF16)
    ot = jnp.concatenate([outs[g][:, h * tq:(h + 1) * tq] for g in range(G) for h in range(hg)], axis=0)
    o_ref[...] = ot.T


def cmp_attend_t(tt, kvc, vct, at01, nb, T, n_slc):
    aq = A_HEADS * A_HEAD_DIM
    nbp = at01.shape[0]
    tq = _pick(T, (256, 128))
    nq = T // tq
    return pl.pallas_call(
        functools.partial(_cmp_t_kernel, tq=tq, n_slc=n_slc, nbp=nbp),
        out_shape=(jax.ShapeDtypeStruct((nb * T, aq), F32), jax.ShapeDtypeStruct((nb * T, A_GROUPS * nbp), BF16)),
        grid=(nb, nq),
        in_specs=[pl.BlockSpec((aq, tq), lambda b, i: (0, b * nq + i)),
                  pl.BlockSpec((1,) + kvc.shape[1:], lambda b, i: (b, 0, 0)),
                  pl.BlockSpec((1,) + vct.shape[1:], lambda b, i: (b, 0, 0)),
                  pl.BlockSpec(at01.shape, lambda b, i: (0, 0))],
        out_specs=(pl.BlockSpec((tq, aq), lambda b, i: (b * nq + i, 0)),
                   pl.BlockSpec((tq, A_GROUPS * nbp), lambda b, i: (b * nq + i, 0))),
        compiler_params=_cparams(2), name="cmp_attend_t")(tt, kvc, vct, at01)


def _window_kernel(q_ref, *refs, tq, nband):
    kts, vs = refs[:nband], refs[nband:2 * nband]
    o_ref = refs[2 * nband]
    G, hg, d = A_GROUPS, A_HEADS // A_GROUPS, A_HEAD_DIM
    qi = pl.program_id(1)
    ncol = nband * tq
    row = lax.broadcasted_iota(jnp.int32, (tq, ncol), 0)
    col = lax.broadcasted_iota(jnp.int32, (tq, ncol), 1)
    pos_k = (qi - (nband - 1)) * tq + col
    dpos = qi * tq + row - pos_k
    ok = ((dpos >= 0) & (dpos < WINDOW) & (pos_k >= 0))[None]
    ss = []
    for g in range(G):
        qs = jnp.concatenate([q_ref[:, (g * hg + h) * d:(g * hg + h + 1) * d] for h in range(hg)], axis=0)
        qs = (qs * (d ** -0.5 * LOG2E)).astype(BF16)
        kt = jnp.concatenate([k[g * d:(g + 1) * d, :] for k in kts], axis=1)
        ss.append(jnp.dot(qs, kt, preferred_element_type=F32).reshape(hg, tq, ncol))
    ps, dens = [], []
    for s in ss:
        s = jnp.where(ok, s, NEG)
        p = jnp.where(ok, jnp.exp2(s - jnp.max(s, axis=-1, keepdims=True)), 0.0)
        dens.append(jnp.sum(p, axis=-1, keepdims=True))
        ps.append(p.reshape(hg * tq, ncol).astype(BF16))
    for g in range(G):
        v = jnp.concatenate([x[:, g * d:(g + 1) * d] for x in vs], axis=0)
        o = jnp.dot(ps[g], v, preferred_element_type=F32).reshape(hg, tq, d)
        o = o / jnp.where(dens[g] > 0.0, dens[g], 1.0)
        for h in range(hg):
            o_ref[:, (g * hg + h) * d:(g * hg + h + 1) * d] = o[h]


def window_attend(z, kt, kt_blk, zb, v_blk, nb, T):
    G, d = A_GROUPS, A_HEAD_DIM
    aq = A_HEADS * d
    tq = 128
    assert T % tq == 0 and WINDOW % tq == 0
    nq = T // tq
    nband = WINDOW // tq + 1
    band = lambda i, blk, tr: (lambda b, q: (blk, b * nq + jnp.maximum(q - (nband - 1) + i, 0)) if tr
                               else (b * nq + jnp.maximum(q - (nband - 1) + i, 0), blk))
    in_specs = [pl.BlockSpec((tq, aq), lambda b, q: (b * nq + q, EV_Q))]
    in_specs += [pl.BlockSpec((G * d, tq), band(i, kt_blk, True)) for i in range(nband)]
    in_specs += [pl.BlockSpec((tq, G * d), band(i, v_blk, False)) for i in range(nband)]
    return pl.pallas_call(
        functools.partial(_window_kernel, tq=tq, nband=nband),
        out_shape=jax.ShapeDtypeStruct((nb * T, aq), F32),
        grid=(nb, nq), in_specs=in_specs,
        out_specs=pl.BlockSpec((tq, aq), lambda b, q: (b * nq + q, 0)),
        compiler_params=_cparams(2), name="window_attend")(z, *([kt] * nband), *([zb] * nband))


def _seg_mask(C):
    nblk = C // SUB
    nseg = max(SUB * nblk * (nblk - 1) // 2, SUB)
    pm = np.zeros((C, nseg), np.float32)
    for i in range(1, nblk):
        off = SUB * i * (i - 1) // 2
        pm[SUB * i:SUB * (i + 1), off:off + SUB * i] = 1.0
    return pm


def _rec_kernel(*refs, mode, C, nsub, H, pos_off, has_s0, layer):
    refs = list(refs)
    a_ref, b_ref, v_ref, gate_ref = refs[:4]
    refs = refs[4:]
    aux_ref = refs.pop(0)
    pm_ref = refs.pop(0)
    s0_ref = refs.pop(0) if has_s0 else None
    o_ref, st_ref, st_scr = refs
    c_id = pl.program_id(1)
    K = 128

    @pl.when(c_id == 0)
    def _():
        if has_s0:
            st_scr[...] = s0_ref[0]
        else:
            st_scr[...] = jnp.zeros(st_scr.shape, F32)

    rows = lax.broadcasted_iota(jnp.int32, (C, 1), 0)
    ltri = jnp.where(rows >= lax.broadcasted_iota(jnp.int32, (1, C), 1), 1.0, 0.0).astype(BF16)
    srow = lax.broadcasted_iota(jnp.int32, (SUB, 1), 0)
    if mode == "hgrn":
        x = aux_ref[...]
        ex = jnp.exp(x - jnp.max(x, axis=0, keepdims=True))
        sm = ex / jnp.sum(ex, axis=0, keepdims=True)
        lb_all = jnp.zeros((1, H * K), F32)
        for i in range(1, layer + 1):
            lb_all = lb_all + sm[i:i + 1]
    nblk = C // SUB
    hs = lambda x, h: x[:, h * K:(h + 1) * K]
    pre = []
    for sc in range(nsub):
        rsl = slice(sc * C, (sc + 1) * C)
        a_all, b_all = a_ref[rsl, :], b_ref[rsl, :]
        if mode == "ret":
            cos, sin_s = aux_ref[rsl, :K], aux_ref[rsl, K:]
            rope = lambda x, cos=cos, sin_s=sin_s: x * cos + pltpu.roll(x, K // 2, 1) * sin_s
            qs = [rope(hs(a_all, h)) for h in range(H)]
            ks = [rope(hs(b_all, h)) * B_DK ** -0.5 for h in range(H)]
            gs = [jnp.full((C, K), math.log1p(-2.0 ** (-5.0 - h)), F32) for h in range(H)]
        else:
            qs, ks, gs = [], [], []
            for h in range(H):
                ah, zf, lb = hs(a_all, h), hs(b_all, h), hs(lb_all, h)
                f = lb + (1.0 - lb) * _sigmoid(zf)
                qs.append(ah * _sigmoid(ah))
                ks.append(1.0 - f)
                gs.append(jnp.log(jnp.maximum(f, F_FLOOR)))
        pre.append((qs, ks, gs))
    bss = [_dot_01x(ltri, jnp.concatenate(pre[sc][2], axis=-1)) for sc in range(nsub)]
    mid = []
    for sc in range(nsub):
        rsl = slice(sc * C, (sc + 1) * C)
        v_all = v_ref[rsl, :]
        per_head = []
        for h in range(H):
            q, k, bsum, v = pre[sc][0][h], pre[sc][1][h], hs(bss[sc], h), hs(v_all, h)
            qt = kt = vcat = None
            if nblk > 1:
                rho = jnp.concatenate(
                    [jnp.zeros((SUB, K), F32)] +
                    [jnp.broadcast_to(bsum[SUB * i - 1:SUB * i], (SUB, K)) for i in range(1, nblk)], axis=0)
                qt = q * jnp.exp(bsum - rho)
                kt = jnp.concatenate(
                    [k[:SUB * i] * jnp.exp(bsum[SUB * i - 1:SUB * i] - bsum[:SUB * i]) for i in range(1, nblk)], axis=0)
                vcat = jnp.concatenate([v[:SUB * i] for i in range(1, nblk)], axis=0).astype(BF16)
            diag = []
            for i in range(nblk):
                sl = slice(SUB * i, SUB * (i + 1))
                q8, k8, b8, v8 = q[sl], k[sl], bsum[sl], v[sl]
                od = jnp.zeros((SUB, K), F32)
                for s in range(SUB):
                    causal = srow >= s
                    dec = jnp.exp(jnp.where(causal, b8 - b8[s:s + 1], 0.0))
                    att = jnp.sum(q8 * k8[s:s + 1] * dec, axis=-1, keepdims=True)
                    od += jnp.where(causal, att, 0.0) * v8[s:s + 1]
                diag.append(od)
            od = jnp.concatenate(diag, axis=0) if nblk > 1 else diag[0]
            bend = bsum[C - 1:C]
            per_head.append(dict(qt=qt, kt=kt, vcat=vcat, od=od, qe=(q * jnp.exp(bsum)).astype(BF16),
                                 kst=(k * jnp.exp(bend - bsum)).astype(BF16), vb=v.astype(BF16), dec=jnp.exp(bend)))
        mid.append(per_head)
    if nblk > 1:
        ps = [[(_dot3_nt(m["qt"], m["kt"]) * pm_ref[...]).astype(BF16) for m in per_head] for per_head in mid]
        for sc in range(nsub):
            for h in range(H):
                mid[sc][h]["od"] = mid[sc][h]["od"] + jnp.dot(ps[sc][h], mid[sc][h]["vcat"], preferred_element_type=F32)
    sts = [st_scr[h] for h in range(H)]
    for sc in range(nsub):
        rsl = slice(sc * C, (sc + 1) * C)
        gate_all = gate_ref[rsl, :]
        for h in range(H):
            m = mid[sc][h]
            o = m["od"] + lax.dot_general(m["qe"], sts[h].astype(BF16), NT, preferred_element_type=F32)
            sts[h] = sts[h] * m["dec"] + lax.dot_general(m["vb"], m["kst"], TN, preferred_element_type=F32)
            gate = hs(gate_all, h)
            o = o * lax.rsqrt(jnp.mean(o * o, axis=-1, keepdims=True) + EPS) * (gate * _sigmoid(gate))
            o_ref[rsl, h * K:(h + 1) * K] = o.astype(o_ref.dtype)
    for h in range(H):
        st_scr[h] = sts[h]

    @pl.when(c_id == pl.num_programs(1) - 1)
    def _():
        st_ref[0] = st_scr[...]


def _rope_table_kernel(fs_ref, o_ref, *, pos_off):
    t = o_ref.shape[0]
    pos = (pos_off + pl.program_id(0) * t + lax.broadcasted_iota(jnp.int32, (t, 1), 0)).astype(F32)
    ang = pos * fs_ref[0:1, :]
    o_ref[...] = jnp.concatenate([jnp.cos(ang), jnp.sin(ang) * fs_ref[1:2, :]], axis=-1)


def rope_table(fs, T, pos_off):
    t = _pick(T, (512, 256, 128, 64, 32, 16, 8))
    return pl.pallas_call(
        functools.partial(_rope_table_kernel, pos_off=pos_off),
        out_shape=jax.ShapeDtypeStruct((T, 2 * fs.shape[1]), F32),
        grid=(T // t,),
        in_specs=[pl.BlockSpec(fs.shape, lambda i: (0, 0))],
        out_specs=pl.BlockSpec((t, 2 * fs.shape[1]), lambda i: (i, 0)),
        compiler_params=_cparams(1), name="rope_table")(fs)


def recurrence(z, blks, aux, *, mode, nb, T, pos_off, s0t=None, layer=0):
    H, K = 4, 128
    C = CHUNK if T % CHUNK == 0 else T
    nsub = _pick(T // C, (4, 2, 1))
    CB = nsub * C
    nc = T // CB
    pm = _seg_mask(C)
    nseg = pm.shape[1]
    row = lambda blk: (lambda b, c: (b * nc + c, blk))
    in_specs = [pl.BlockSpec((CB, H * K), row(blks[0])), pl.BlockSpec((CB, H * K), row(blks[1])),
                pl.BlockSpec((CB, H * K), row(blks[2])), pl.BlockSpec((CB, H * K), row(blks[3])),
                pl.BlockSpec((CB, aux.shape[1]), lambda b, c: (c, 0)) if mode == "ret"
                else pl.BlockSpec(aux.shape, lambda b, c: (0, 0)),
                pl.BlockSpec((C, nseg), lambda b, c: (0, 0))]
    args = [z, z, z, z, aux, jnp.asarray(pm)]
    if s0t is not None:
        in_specs.append(pl.BlockSpec((1, H, K, K), lambda b, c: (b, 0, 0, 0)))
        args.append(s0t)
    return pl.pallas_call(
        functools.partial(_rec_kernel, mode=mode, C=C, nsub=nsub, H=H, pos_off=pos_off, has_s0=s0t is not None,
                          layer=layer),
        out_shape=(jax.ShapeDtypeStruct((nb * T, H * K), BF16), jax.ShapeDtypeStruct((nb, H, K, K), F32)),
        grid=(nb, nc),
        in_specs=in_specs,
        out_specs=(pl.BlockSpec((CB, H * K), lambda b, c: (b * nc + c, 0)),
                   pl.BlockSpec((1, H, K, K), lambda b, c: (b, 0, 0, 0))),
        scratch_shapes=[pltpu.VMEM((H, K, K), F32)],
        compiler_params=_cparams(2), name="recurrence_" + mode)(*args)


def _page_specs(shape, layer, n_pages, per_step):
    def spec(i):
        return pl.BlockSpec((1, 1) + shape, lambda b, j, pt: (layer, pt[b * n_pages + j * per_step + i], 0, 0))
    return [spec(i) for i in range(per_step)]


def _online_update_all(items):
    staged = []
    for s, mk, m_ref, l_ref, acc_ref, pv_fn in items:
        hg, rows, n = s.shape
        if mk is not None:
            s = jnp.where(mk, s, NEG)
        m_old = m_ref[...]
        m_new = jnp.maximum(m_old, jnp.max(s, axis=-1, keepdims=True))
        alpha = jnp.exp(m_old - m_new)
        p = jnp.exp(s - m_new)
        if mk is not None:
            p = jnp.where(mk, p, 0.0)
        l_ref[...] = alpha * l_ref[...] + jnp.sum(p, axis=-1, keepdims=True)
        m_ref[...] = m_new
        staged.append((p.reshape(hg * rows, n).astype(BF16), alpha, (hg, rows)))
    for (pb, alpha, (hg, rows)), (_, _, _, _, acc_ref, pv_fn) in zip(staged, items):
        pv = pv_fn(pb)
        acc_ref[...] = alpha * acc_ref[...] + pv.reshape(hg, rows, pv.shape[-1])


def _diff_decode_kernel(pt_ref, q_ref, new_ref, *refs, per_step, H, dh):
    pages = refs[:per_step]
    o_ref, qbd_scr, m_scr, l_scr, acc_scr = refs[per_step:]
    j = pl.program_id(1)
    Ts = q_ref.shape[0]
    dv = 2 * dh
    rows_pp = 2 * H

    @pl.when(j == 0)
    def _():
        m_scr[...] = jnp.full(m_scr.shape, NEG, F32)
        l_scr[...] = jnp.zeros(l_scr.shape, F32)
        acc_scr[...] = jnp.zeros(acc_scr.shape, F32)
        lane = lax.broadcasted_iota(jnp.int32, (Ts, dv), 1)
        for h in range(H):
            q = q_ref[:, h * dv:(h + 1) * dv] * dh ** -0.5
            qbd_scr[h] = jnp.concatenate([jnp.where(lane < dh, q, 0.0), jnp.where(lane >= dh, q, 0.0)],
                                         axis=0).astype(BF16)

    items = []
    for h in range(H):
        k = jnp.concatenate([pg[0, 0, pl.ds(h, PAGE_SIZE, stride=rows_pp), :] for pg in pages], axis=0).astype(BF16)
        v = jnp.concatenate([pg[0, 0, pl.ds(H + h, PAGE_SIZE, stride=rows_pp), :] for pg in pages], axis=0).astype(BF16)
        s = lax.dot_general(qbd_scr[h], k, NT, preferred_element_type=F32)[None]
        items.append((s, None, m_scr.at[h], l_scr.at[h], acc_scr.at[h],
                      lambda p, v=v: jnp.dot(p, v, preferred_element_type=F32)))
    _online_update_all(items)

    @pl.when(j == pl.num_programs(1) - 1)
    def _():
        r = lax.broadcasted_iota(jnp.int32, (2 * Ts, Ts), 0)
        cidx = lax.broadcasted_iota(jnp.int32, (2 * Ts, Ts), 1)
        causal = (cidx <= jnp.where(r >= Ts, r - Ts, r))[None]
        last = []
        for h in range(H):
            kn = new_ref[:, h * dv:(h + 1) * dv].astype(BF16)
            vn = new_ref[:, (H + h) * dv:(H + h + 1) * dv].astype(BF16)
            s = lax.dot_general(qbd_scr[h], kn, NT, preferred_element_type=F32)[None]
            last.append((s, causal, m_scr.at[h], l_scr.at[h], acc_scr.at[h],
                         lambda p, vn=vn: jnp.dot(p, vn, preferred_element_type=F32)))
        _online_update_all(last)
        for h in range(H):
            l = l_scr[h]
            o = acc_scr[h] / jnp.where(l > 0.0, l, 1.0)
            o_ref[:, (2 * h) * dv:(2 * h + 1) * dv] = o[0, :Ts]
            o_ref[:, (2 * h + 1) * dv:(2 * h + 2) * dv] = o[0, Ts:]


def diff_decode(z, cache_rows, layer, page_table, nbs, Ts):
    H, dh = C_HEADS, C_HEAD_DIM
    cw = H * 2 * dh
    n_pages = page_table.shape[1]
    per_step = _pick(n_pages, (16, 8, 4, 2, 1))
    in_specs = [pl.BlockSpec((Ts, cw), lambda b, j, pt: (b, 0)),
                pl.BlockSpec((Ts, 2 * cw), lambda b, j, pt: (b, 0))]
    in_specs += _page_specs(cache_rows.shape[2:], layer, n_pages, per_step)
    grid_spec = pltpu.PrefetchScalarGridSpec(
        num_scalar_prefetch=1, grid=(nbs, n_pages // per_step), in_specs=in_specs,
        out_specs=pl.BlockSpec((Ts, 2 * cw), lambda b, j, pt: (b, 0)),
        scratch_shapes=[pltpu.VMEM((H, 2 * Ts, 2 * dh), BF16), pltpu.VMEM((H, 1, 2 * Ts, 1), F32),
                        pltpu.VMEM((H, 1, 2 * Ts, 1), F32), pltpu.VMEM((H, 1, 2 * Ts, 2 * dh), F32)])
    kv_new = z[:, cw:3 * cw]
    return pl.pallas_call(
        functools.partial(_diff_decode_kernel, per_step=per_step, H=H, dh=dh),
        out_shape=jax.ShapeDtypeStruct((nbs * Ts, 2 * cw), F32),
        grid_spec=grid_spec, compiler_params=_cparams(2),
        name="diff_decode")(page_table.reshape(-1), z, kv_new, *([cache_rows] * per_step))


def _slc_decode_kernel(pt_ref, q_ref, new_ref, sel_ref, *refs, per_step, nbp, past):
    pages = refs[:per_step]
    o_ref, qs_scr, m_scr, l_scr, acc_scr = refs[per_step:]
    G, hg, d = A_GROUPS, A_HEADS // A_GROUPS, A_HEAD_DIM
    j = pl.program_id(1)
    Ts = q_ref.shape[0]
    ntok = per_step * PAGE_SIZE

    @pl.when(j == 0)
    def _():
        m_scr[...] = jnp.full(m_scr.shape, NEG, F32)
        l_scr[...] = jnp.zeros(l_scr.shape, F32)
        acc_scr[...] = jnp.zeros(acc_scr.shape, F32)
        for g in range(G):
            qs = jnp.concatenate([q_ref[:, (g * hg + h) * d:(g * hg + h + 1) * d] for h in range(hg)], axis=0)
            qs_scr[g] = (qs * d ** -0.5).astype(BF16)

    tok = j * ntok + lax.broadcasted_iota(jnp.int32, (1, ntok), 1)
    jb = lax.broadcasted_iota(jnp.int32, (nbp, 1), 0)
    e01 = jnp.where(jb == lax.shift_right_logical(tok, int(math.log2(SLC_LEN))), 1.0, 0.0).astype(BF16)
    items = []
    for g in range(G):
        kt = jnp.concatenate([pg[0, 0, g * d:(g + 1) * d, :] for pg in pages], axis=1).astype(BF16)
        vt = jnp.concatenate([pg[0, 0, (G + g) * d:(G + g + 1) * d, :] for pg in pages], axis=1).astype(BF16)
        s = jnp.dot(qs_scr[g], kt, preferred_element_type=F32).reshape(hg, Ts, ntok)
        st = jnp.dot(sel_ref[:, g * nbp:(g + 1) * nbp], e01, preferred_element_type=F32)
        items.append((s, (st > -1.0)[None], m_scr.at[g], l_scr.at[g], acc_scr.at[g],
                      lambda p, vt=vt: lax.dot_general(p, vt, NT, preferred_element_type=F32)))
    _online_update_all(items)

    @pl.when(j == pl.num_programs(1) - 1)
    def _():
        causal = (lax.broadcasted_iota(jnp.int32, (Ts, Ts), 1) <= lax.broadcasted_iota(jnp.int32, (Ts, Ts), 0))[None]
        last = []
        for g in range(G):
            kn = new_ref[:, g * d:(g + 1) * d].astype(BF16)
            vn = new_ref[:, (G + g) * d:(G + g + 1) * d].astype(BF16)
            s = lax.dot_general(qs_scr[g], kn, NT, preferred_element_type=F32).reshape(hg, Ts, Ts)
            last.append((s, causal, m_scr.at[g], l_scr.at[g], acc_scr.at[g],
                         lambda p, vn=vn: jnp.dot(p, vn, preferred_element_type=F32)))
        _online_update_all(last)
        for g in range(G):
            l = l_scr[g]
            o = acc_scr[g] / jnp.where(l > 0.0, l, 1.0)
            for h in range(hg):
                o_ref[:, (g * hg + h) * d:(g * hg + h + 1) * d] = o[h]


def slc_decode(z, cache_t, layer, page_table, sel, nbs, Ts, nbp, past):
    G, d = A_GROUPS, A_HEAD_DIM
    aq = A_HEADS * d
    n_pages = page_table.shape[1]
    per_step = _pick(n_pages, (16, 8, 4, 2, 1))
    assert past % SLC_LEN == 0 and Ts <= SLC_LEN
    in_specs = [pl.BlockSpec((Ts, aq), lambda b, j, pt: (b, EV_Q)),
                pl.BlockSpec((Ts, 2 * G * d), lambda b, j, pt: (b, EV_SLC // (2 * G * d))),
                pl.BlockSpec((Ts, G * nbp), lambda b, j, pt: (b, 0))]
    in_specs += _page_specs(cache_t.shape[2:], layer, n_pages, per_step)
    hg = A_HEADS // G
    grid_spec = pltpu.PrefetchScalarGridSpec(
        num_scalar_prefetch=1, grid=(nbs, n_pages // per_step), in_specs=in_specs,
        out_specs=pl.BlockSpec((Ts, aq), lambda b, j, pt: (b, 0)),
        scratch_shapes=[pltpu.VMEM((G, hg * Ts, d), BF16), pltpu.VMEM((G, hg, Ts, 1), F32),
                        pltpu.VMEM((G, hg, Ts, 1), F32), pltpu.VMEM((G, hg, Ts, d), F32)])
    return pl.pallas_call(
        functools.partial(_slc_decode_kernel, per_step=per_step, nbp=nbp, past=past),
        out_shape=jax.ShapeDtypeStruct((nbs * Ts, aq), F32),
        grid_spec=grid_spec, compiler_params=_cparams(2),
        name="slc_decode")(page_table.reshape(-1), z, z, sel, *([cache_t] * per_step))


def _compress_paged_kernel(pt_ref, w_ref, pe_ref, *refs, per_step):
    pages = refs[:per_step]
    o_ref, ab_scr = refs[per_step:]
    j = pl.program_id(1)
    cpp = PAGE_SIZE // CMP_STRIDE
    r = lax.broadcasted_iota(jnp.int32, (PAGE_SIZE, PAGE_SIZE), 0)
    t = lax.broadcasted_iota(jnp.int32, (PAGE_SIZE, PAGE_SIZE), 1)
    assert cpp & (cpp - 1) == 0
    perm = jnp.where(t == (r & (cpp - 1)) * CMP_STRIDE + lax.shift_right_logical(r, int(math.log2(cpp))),
                     1.0, 0.0).astype(BF16)
    xs = [lax.dot_general(perm, pg[0, 0].astype(BF16), NT, preferred_element_type=F32) for pg in pages]
    xcat = jnp.concatenate([jnp.concatenate([x[l * cpp:(l + 1) * cpp] for x in xs], axis=0).astype(BF16)
                            for l in range(CMP_STRIDE)], axis=1)
    ab = jnp.dot(xcat, w_ref[...], preferred_element_type=F32)
    n = per_step * cpp
    ab_scr[pl.ds(pl.multiple_of(j * n, n), n), :] = ab

    @pl.when(j == pl.num_programs(1) - 1)
    def _():
        cab = jnp.dot(pe_ref[...].astype(BF16), w_ref[...], preferred_element_type=F32)
        half = ab_scr.shape[1] // 2
        const = cab[0:1, :half] + cab[1:2, half:]
        a, b = ab_scr[:, :half], ab_scr[:, half:]
        o_ref[0] = a + pltpu.roll(b, a.shape[0] - 1, 0) + const


def compress_paged(cache_t, layer, page_table, wbig, pe2):
    nbs, n_pages = page_table.shape
    per_step = _pick(n_pages, (16, 8, 4, 2, 1))
    cpp = PAGE_SIZE // CMP_STRIDE
    n_chunk = n_pages * cpp
    cout = wbig.shape[1] // 2
    in_specs = [pl.BlockSpec(wbig.shape, lambda b, j, pt: (0, 0)), pl.BlockSpec(pe2.shape, lambda b, j, pt: (0, 0))]
    in_specs += _page_specs(cache_t.shape[2:], layer, n_pages, per_step)
    grid_spec = pltpu.PrefetchScalarGridSpec(
        num_scalar_prefetch=1, grid=(nbs, n_pages // per_step), in_specs=in_specs,
        out_specs=pl.BlockSpec((1, n_chunk, cout), lambda b, j, pt: (b, 0, 0)),
        scratch_shapes=[pltpu.VMEM((n_chunk, 2 * cout), F32)])
    return pl.pallas_call(
        functools.partial(_compress_paged_kernel, per_step=per_step),
        out_shape=jax.ShapeDtypeStruct((nbs, n_chunk, cout), F32),
        grid_spec=grid_spec, compiler_params=_cparams(2),
        name="compress_paged")(page_table.reshape(-1), wbig, pe2, *([cache_t] * per_step))


def _even_w_in(w):
    aq, akv = A_HEADS * A_HEAD_DIM, A_GROUPS * A_HEAD_DIM
    splits = np.cumsum([aq] + [akv] * 6 + [3 * A_HEADS] + [512] * 4)[:-1]
    q, kc, vc, ks, vs, kw, vw, gt, rq, rk, rv, rg = jnp.split(w, [int(s) for s in splits], axis=1)
    gt = jnp.pad(gt, ((0, 0), (0, 128 - gt.shape[1])))
    return jnp.concatenate([q, rq, rk, rv, rg, kc, vc, ks, vs, kw, vw, gt], axis=1).astype(BF16)


EV_Q, EV_RQ, EV_RK, EV_RV, EV_RG = 0, 1, 2, 3, 4
EV_GT = 26
EV_CMP, EV_SLC, EV_WIN = 2560, 2816, 3072


def _compress_w(w_cmp, pe):
    G, d = A_GROUPS, A_HEAD_DIM
    r = CMP_LEN // CMP_STRIDE
    assert r == 2
    w = w_cmp.reshape(2, r, CMP_STRIDE, d, d)
    eye_kv = jnp.eye(2, dtype=F32)
    eye_g = jnp.eye(G, dtype=F32)
    big = jnp.einsum("khlde,kq,gp->lqpdhkge", w, eye_kv, eye_g)
    big = big.reshape(CMP_STRIDE * 2 * G * d, r * 2 * G * d).astype(BF16)
    pe_r = pe.reshape(2, r, CMP_STRIDE, d)
    rows = jnp.broadcast_to(pe_r.transpose(1, 2, 0, 3)[:, :, :, None, :], (r, CMP_STRIDE, 2, G, d))
    rows = rows.reshape(r, CMP_STRIDE * 2 * G * d)
    return big, jnp.pad(rows, ((0, 8 - r), (0, 0)))


def _slc_sum_matrix(n_rows, n_cmp, nbp):
    a = np.zeros((n_rows, nbp), np.float32)
    per, left = SLC_LEN // CMP_STRIDE, CMP_LEN // CMP_STRIDE - 1
    for j in range(nbp):
        for n in range(per * j - left, per * j + per):
            if 0 <= n < n_cmp:
                a[n, j] = 1.0
    return jnp.asarray(a, BF16)


def _gate_expand():
    e = np.zeros((128, 3 * A_HEADS * A_HEAD_DIM), np.float32)
    for h in range(A_HEADS):
        for i in range(3):
            e[3 * h + i, i * A_HEADS * A_HEAD_DIM + h * A_HEAD_DIM:i * A_HEADS * A_HEAD_DIM + (h + 1) * A_HEAD_DIM] = 1.0
    return jnp.asarray(e, BF16)


def _rope_aux():
    half = B_DK // 2
    freqs = ROPE_BASE ** (-jnp.arange(half, dtype=F32) / half)
    sign = jnp.concatenate([-jnp.ones((half,), F32), jnp.ones((half,), F32)])
    return jnp.stack([jnp.concatenate([freqs, freqs]), sign])


def _slc_shapes(n_slc, n_cmp_pad):
    nbp = -(-n_slc // 128) * 128
    return nbp, _slc_sum_matrix(n_cmp_pad, n_cmp_pad - 1, nbp)


def _nsa_common(nb, T, q_off):
    return dict(nb=nb, Tq=T, tq=_pick(T, (128, 64, 32, 16, 8)), G=A_GROUPS, hg=A_HEADS // A_GROUPS, dk=A_HEAD_DIM,
                dv=A_HEAD_DIM, g_per_v=1, q_off=q_off)


def _even_layer(xp, xs, p, a, dims, caches):
    B, T, nbs, Ts, past = dims
    g = p["norm_g"]
    w_in = _even_w_in(p["w_in_a"][a])
    wbig, pe2 = _compress_w(p["cmp_w"][a], p["cmp_pos"][a])
    w_out = p["w_out_a"][a].astype(BF16)
    aq = A_HEADS * A_HEAD_DIM
    wa, wb = w_out[:aq], w_out[aq:]
    e01 = _gate_expand()
    akv2 = 2 * A_GROUPS * A_HEAD_DIM

    akv = A_GROUPS * A_HEAD_DIM
    wt = jnp.concatenate([w_in[:, :aq], w_in[:, EV_SLC:EV_SLC + akv], w_in[:, EV_WIN:EV_WIN + akv]], axis=1).T
    wscale = jnp.concatenate([jnp.full((aq, 1), A_HEAD_DIM ** -0.5 * LOG2E, F32), jnp.ones((2 * akv, 1), F32)])
    zp, zpb, ttp = in_proj(xp, g[0:1], w_in, wt, wscale,
                           [(EV_SLC + akv, akv), (EV_WIN + akv, akv)])
    kv_cmp_p = zp[:, EV_CMP:EV_CMP + akv2]
    kv_slc_p = zp[:, EV_SLC:EV_SLC + akv2]
    kv_win_p = zp[:, EV_WIN:EV_WIN + akv2]
    kvc_p, vct_p = compress(kv_cmp_p.reshape(B, T // CMP_STRIDE, CMP_STRIDE * akv2), wbig, pe2)
    n_slc = -(-T // SLC_LEN)
    nbp, a01 = _slc_shapes(n_slc, kvc_p.shape[1])
    o_cmp, sel = cmp_attend_t(ttp, kvc_p, vct_p, a01.T, B, T, n_slc)
    blk01 = np.zeros((nbp, T), np.float32)
    blk01[np.arange(T) // SLC_LEN, np.arange(T)] = 1.0
    o_slc = attend(zp, EV_Q, ttp, aq // akv, zpb, 0, nb=B, T=T, tq=_pick(T, (512, 256, 128, 64, 32, 16, 8)),
                   tk=_pick(T, (1024, 512, 256, 128)), G=A_GROUPS, hg=A_HEADS // A_GROUPS, dk=A_HEAD_DIM,
                   dv=A_HEAD_DIM, g_per_v=1, mode="causal", sel=sel, e01=jnp.asarray(blk01, BF16))
    o_win = window_attend(zp, ttp, (aq + akv) // akv, zpb, 1, B, T)
    ob_p, st_p = recurrence(zp, (EV_RQ, EV_RK, EV_RV, EV_RG), p["rope_p"], mode="ret", nb=B, T=T, pos_off=0)
    xp = even_out(o_cmp, o_slc, o_win, zp, EV_GT, ob_p, e01, wa, wb, g[1:2], xp)

    cache_cmp, cache_slc, win_buf, s0, page_table = caches
    zs = rms_matmul(xs, g[0:1], w_in, F32)
    kv_cmp_s = zs[:, EV_CMP:EV_CMP + akv2]
    kv_slc_s = zs[:, EV_SLC:EV_SLC + akv2]
    kv_win_s = zs[:, EV_WIN:EV_WIN + akv2]
    assert Ts < CMP_STRIDE and past % SLC_LEN == 0
    kvc_s = compress_paged(cache_cmp, a, page_table, wbig, pe2)
    n_slc = -(-(past + Ts) // SLC_LEN)
    nbp, a01 = _slc_shapes(n_slc, kvc_s.shape[1])
    o_cmp, sel = cmp_attend(zs, EV_Q, kvc_s, a01, nbs, Ts, past, n_slc)
    o_slc = slc_decode(zs, cache_slc, a, page_table, sel, nbs, Ts, nbp, past)
    band = jnp.concatenate([win_buf, kv_win_s.reshape(nbs, Ts, akv2)], axis=1)
    n_buf = win_buf.shape[1]
    bt = n_buf + Ts
    o_win = flash(zs, EV_Q, band.reshape(nbs * bt, akv2), 0, band.reshape(nbs * bt, akv2), 1, Tk=bt, tk=bt,
                  mode="window", k_off=past - n_buf, **_nsa_common(nbs, Ts, past))
    ob_s, st_s = recurrence(zs, (EV_RQ, EV_RK, EV_RV, EV_RG), p["rope_s"], mode="ret", nb=nbs, T=Ts, pos_off=past,
                            s0t=jnp.swapaxes(s0, -1, -2))
    xs = even_out(o_cmp, o_slc, o_win, zs, EV_GT, ob_s, e01, wa, wb, g[1:2], xs)

    shp = lambda x, n, t: x.reshape(n, t, 2, A_GROUPS, A_HEAD_DIM)
    wkeep = min(WINDOW, T)
    outs = (shp(kv_cmp_p, B, T), shp(kv_cmp_s, nbs, Ts), shp(kv_slc_p, B, T), shp(kv_slc_s, nbs, Ts),
            shp(kv_win_p, B, T)[:, T - wkeep:], shp(band[:, Ts:], nbs, n_buf),
            jnp.swapaxes(st_p, -1, -2), jnp.swapaxes(st_s, -1, -2).astype(s0.dtype))
    return xp, xs, outs


def _odd_layer(xp, xs, p, o, layer, dims, caches):
    B, T, nbs, Ts, past = dims
    g = p["norm_g"]
    w_in = p["w_in_c"][o].astype(BF16)
    w_out = p["w_out_c"][o].astype(BF16)
    cw = C_HEADS * 2 * C_HEAD_DIM
    wc, wd = w_out[:cw], w_out[cw:]
    lam_init = 0.8 - 0.6 * math.exp(-0.3 * layer)
    lam_p = p["diff_lam"][o]
    lb_logits = p["hgrn_lb"]

    zp, zpb, ktp, kv_rows = in_proj(xp, g[0:1], w_in, w_in[:, cw:2 * cw].T, jnp.ones((cw, 1), F32),
                                    [(2 * cw, cw)], row_cols=(cw, 2 * cw // LANES))
    o_diff = attend(zp, 0, ktp, 0, zpb, 0, nb=B, T=T, tq=_pick(T, (512, 256, 128, 64, 32, 16, 8)),
                    tk=_pick(T, (512, 256, 128)), G=2 * C_HEADS, hg=1, dk=C_HEAD_DIM, dv=2 * C_HEAD_DIM, g_per_v=2,
                    mode="causal")
    od_p, st_p = recurrence(zp, (3, 4, 5, 6), lb_logits, mode="hgrn", nb=B, T=T, pos_off=0, layer=o)
    xp = odd_out(o_diff, od_p, lam_p, lam_init, wc, wd, g[1:2], xp)

    cache_diff, s0, page_table = caches
    zs = rms_matmul(xs, g[0:1], w_in, F32)
    kv_s = zs[:, cw:3 * cw]
    o_diff = diff_decode(zs, cache_diff, o, page_table, nbs, Ts)
    od_s, st_s = recurrence(zs, (3, 4, 5, 6), lb_logits, mode="hgrn", nb=nbs, T=Ts, pos_off=past,
                            s0t=jnp.swapaxes(s0, -1, -2), layer=o)
    xs = odd_out(o_diff, od_s, lam_p, lam_init, wc, wd, g[1:2], xs)

    shp = lambda x, n, t: x.reshape(n, t, 2, C_HEADS, 2 * C_HEAD_DIM)
    outs = (shp(kv_rows, B, T), shp(kv_s, nbs, Ts),
            jnp.swapaxes(st_p, -1, -2), jnp.swapaxes(st_s, -1, -2).astype(s0.dtype))
    return xp, xs, outs


def _tail_layers(xp, xs, p, layer, dims, mem_prompt, cache_mem):
    B, T, nbs, Ts, _ = dims
    g = p["norm_g"][layer]
    D = xp.shape[1]
    w_q = p["w_xq"][layer].astype(BF16)
    w_kv = p["w_xkv"][layer].astype(BF16)
    w_o = p["w_xo"][layer].astype(BF16)
    w_up = p["w_up"][layer].astype(BF16)
    w_down = p["w_down"][layer].astype(BF16)
    n_mem = mem_prompt.shape[1]
    kv_mem = rms_matmul(mem_prompt.reshape(B * n_mem, D), g[0:1], w_kv, F32, norm=False).reshape(B, n_mem, 2 * D)
    qp = rms_matmul(xp, g[2:3], w_q, BF16)
    xp = proj_res(xattn(qp, kv_mem, T), w_o, g[3:4], xp)
    qs = rms_matmul(xs, g[2:3], w_q, BF16)
    xs = proj_res(xattn(qs, cache_mem, Ts, layer=layer), w_o, g[3:4], xs)
    xp = mlp(xp, g[4:5], g[5:6], w_up, w_down)
    xs = mlp(xs, g[4:5], g[5:6], w_up, w_down)
    return xp, xs, kv_mem.reshape(B, n_mem, 2, X_HEADS, D // X_HEADS)


def kernel(x_prompt, x_sample, cache_nsa_cmp_kv, cache_nsa_slc_kv, cache_nsa_win_kv, state_ret, cache_diff_kv, state_hgrn, cache_mem_kv, page_table, mem_prompt, norm_g, w_in_a, cmp_pos, cmp_w, w_out_a, w_in_c, diff_lam, hgrn_lb, w_out_c, w_xq, w_xkv, w_xo, w_up, w_down):
    B, T, D = x_prompt.shape
    nbs, Ts, _ = x_sample.shape
    depth = norm_g.shape[0]
    n_pages = page_table.shape[1]
    past = n_pages * PAGE_SIZE
    dims = (B, T, nbs, Ts, past)
    p = dict(w_in_a=w_in_a, cmp_pos=cmp_pos, cmp_w=cmp_w, w_out_a=w_out_a, w_in_c=w_in_c, diff_lam=diff_lam,
             hgrn_lb=hgrn_lb.astype(F32), w_out_c=w_out_c, w_xq=w_xq, w_xkv=w_xkv, w_xo=w_xo, w_up=w_up, w_down=w_down)
    p["rope_p"] = rope_table(_rope_aux(), T, 0)
    p["rope_s"] = rope_table(_rope_aux(), Ts, past)
    xp = x_prompt.reshape(B * T, D)
    xs = x_sample.reshape(nbs * Ts, D)
    feat_major = lambda c: c.transpose(0, 1, 3, 4, 5, 2).reshape(c.shape[0], c.shape[1], -1, c.shape[2])
    rows_of = lambda c: c.reshape(c.shape[0], c.shape[1], -1, c.shape[-1])
    c_cmp, c_slc, c_diff = feat_major(cache_nsa_cmp_kv), feat_major(cache_nsa_slc_kv), rows_of(cache_diff_kv)
    mshape = cache_mem_kv.shape
    c_mem = cache_mem_kv.reshape(mshape[:5] + (mshape[5] // 128, 128)).transpose(0, 1, 2, 3, 5, 4, 6)
    c_mem = c_mem.reshape(mshape[0], mshape[1], -1, 128)
    win_all = cache_nsa_win_kv.reshape(cache_nsa_win_kv.shape[0], nbs, cache_nsa_win_kv.shape[2], -1)
    ev, od, mem = [], [], []
    for layer in range(depth):
        pl_ = dict(p, norm_g=norm_g[layer])
        if layer % 2 == 0:
            a = layer // 2
            xp, xs, outs = _even_layer(xp, xs, pl_, a, dims, (c_cmp, c_slc, win_all[a], state_ret[a], page_table))
            ev.append(outs)
        else:
            o = layer // 2
            xp, xs, outs = _odd_layer(xp, xs, pl_, o, layer, dims, (c_diff, state_hgrn[o], page_table))
            od.append(outs)
        xp, xs, kvm = _tail_layers(xp, xs, dict(p, norm_g=norm_g), layer, dims, mem_prompt, c_mem)
        mem.append(kvm)
    stack = lambda lst, i: jnp.stack([t[i] for t in lst])
    return (xp.reshape(B, T, D), xs.reshape(nbs, Ts, D),
            stack(ev, 0), stack(ev, 1), stack(ev, 2), stack(ev, 3), stack(ev, 4), stack(ev, 5), stack(ev, 6), stack(ev, 7),
            stack(od, 0), stack(od, 1), stack(od, 2), stack(od, 3), jnp.stack(mem))
```
